```python
import jax, jax.numpy as jnp
from jax import lax
import numpy as np

D_MODEL = 1024
BATCH = 8
SEQ = 2048
DEPTH = 2
DEC_BATCH = 128
DEC_SEQ = 1
PAST_LEN = 16384
PAGE_SIZE = 128

N_HEADS = 8
HEAD_K = D_MODEL // N_HEADS
HEAD_V = D_MODEL // N_HEADS
CHUNK = 32
CONV_W = 3
D_FF = 7 * D_MODEL // 2
N_EXPERTS = 8
TOP_K = 2
N_EVEN = (DEPTH + 1) // 2
N_ODD = DEPTH // 2
EPS = 1e-6

kernel_name = "hgrn2_shortconv_moe_hybrid_step"

F32 = jnp.float32


def rmsnorm(x, g):
    xf = x.astype(F32)
    y = xf * lax.rsqrt(jnp.mean(xf * xf, axis=-1, keepdims=True) + EPS) * g.astype(F32)
    return y.astype(x.dtype)


def hgrn2_features(xn, w_in, lb):
    bs, t, _ = xn.shape
    q, fl, v, gate = jnp.split(xn @ w_in, 4, axis=-1)
    q = jax.nn.silu(q.astype(F32)).reshape(bs, t, N_HEADS, HEAD_K)
    f = lb + (1.0 - lb) * jax.nn.sigmoid(fl.astype(F32))
    logf = jnp.log(f).reshape(bs, t, N_HEADS, HEAD_K)
    k = (1.0 - f).reshape(bs, t, N_HEADS, HEAD_K)
    v = v.astype(F32).reshape(bs, t, N_HEADS, HEAD_V)
    return q, k, v, logf, gate


def hgrn2_chunked(q, k, v, logf):
    bs, t = q.shape[:2]
    n = t // CHUNK

    def to_chunks(a):
        return a.reshape(bs, n, CHUNK, N_HEADS, a.shape[-1]).swapaxes(0, 1)

    G = jnp.cumsum(to_chunks(logf), axis=2)
    mask = jnp.tril(jnp.ones((CHUNK, CHUNK), dtype=bool))

    def step(S, xs):
        qc, kc, vc, Gc = xs
        g_last = Gc[:, -1]
        q_in = qc * jnp.exp(Gc)
        k_in = kc * jnp.exp(-Gc)
        A = jnp.where(mask, jnp.einsum('bthk,bshk->bhts', q_in, k_in), 0.0)
        o = jnp.einsum('bhts,bshv->bthv', A, vc) + jnp.einsum('bthk,bhkv->bthv', q_in, S)
        k_dec = kc * jnp.exp(g_last[:, None] - Gc)
        S = jnp.exp(g_last)[..., None] * S + jnp.einsum('bshk,bshv->bhkv', k_dec, vc)
        return S, o

    S0 = jnp.zeros((bs, N_HEADS, HEAD_K, HEAD_V), F32)
    S, o = lax.scan(step, S0, (to_chunks(q), to_chunks(k), to_chunks(v), G))
    o = o.swapaxes(0, 1).reshape(bs, t, N_HEADS, HEAD_V)
    return o, S


def hgrn2_recurrent(q, k, v, logf, S0):
    def step(S, xs):
        qt, kt, vt, lft = xs
        S = jnp.exp(lft)[..., None] * S + kt[..., None] * vt[..., None, :]
        return S, jnp.einsum('bhk,bhkv->bhv', qt, S)

    S, o = lax.scan(step, S0.astype(F32), tuple(a.swapaxes(0, 1) for a in (q, k, v, logf)))
    return o.swapaxes(0, 1), S


def hgrn2_out(o, gate, g_norm, w_out, dtype):
    bs, t = o.shape[:2]
    o = o * lax.rsqrt(jnp.mean(o * o, axis=-1, keepdims=True) + EPS)
    o = o.reshape(bs, t, D_MODEL) * g_norm.astype(F32) * jax.nn.silu(gate.astype(F32))
    return o.astype(dtype) @ w_out


def short_conv_mix(xn, buf, w_in, conv_w, w_out):
    b, c, h = jnp.split(xn @ w_in, 3, axis=-1)
    u = c * h
    u_ext = jnp.concatenate([buf.astype(u.dtype), u], axis=1)
    t = u.shape[1]
    y = conv_w[0] * u_ext[:, 0:t]
    for j in range(1, CONV_W):
        y = y + conv_w[j] * u_ext[:, j:j + t]
    return (b * y) @ w_out, u_ext[:, -(CONV_W - 1):]


def swiglu(x, w1, w3, w2):
    return (jax.nn.silu(x @ w1) * (x @ w3)) @ w2


def moe_swiglu(x, router, w1, w3, w2):
    bs, t, d = x.shape
    xt = x.reshape(-1, d)
    logits = (xt @ router).astype(F32)
    top_v, top_i = lax.top_k(logits, TOP_K)
    gates = jax.nn.softmax(top_v, axis=-1)
    combine = jnp.sum(jax.nn.one_hot(top_i, N_EXPERTS, dtype=F32) * gates[..., None], axis=1)
    out = jnp.zeros(xt.shape, F32)
    for e in range(N_EXPERTS):
        out = out + combine[:, e:e + 1] * swiglu(xt, w1[e], w3[e], w2[e]).astype(F32)
    return out.astype(x.dtype).reshape(bs, t, d)


def setup_inputs(seed: int = 0) -> dict:
    key = jax.random.key(seed)
    ks = jax.random.split(key, 20)
    n = lambda k, s, sc: jax.random.normal(k, s, F32) * sc
    D = D_MODEL
    return {
        "x_prompt": n(ks[0], (BATCH, SEQ, D), 1.0),
        "x_sample": n(ks[1], (DEC_BATCH, DEC_SEQ, D), 1.0),
        "state_hgrn": n(ks[2], (DEC_BATCH, N_EVEN, N_HEADS, HEAD_K, HEAD_V), 0.5),
        "state_conv": n(ks[3], (DEC_BATCH, N_ODD, CONV_W - 1, D), 1.0),
        "norm_mix": 1.0 + n(ks[4], (DEPTH, D), 0.02),
        "norm_ffn": 1.0 + n(ks[5], (DEPTH, D), 0.02),
        "norm_final": 1.0 + n(ks[6], (D,), 0.02),
        "hgrn_w_in": n(ks[7], (N_EVEN, D, 4 * D), D ** -0.5),
        "hgrn_lb": n(ks[8], (DEPTH + 1, D), 0.1),
        "hgrn_g_norm": 1.0 + n(ks[9], (N_EVEN, D), 0.02),
        "hgrn_w_out": n(ks[10], (N_EVEN, D, D), D ** -0.5),
        "conv_w_in": n(ks[11], (N_ODD, D, 3 * D), D ** -0.5),
        "conv_w": n(ks[12], (N_ODD, CONV_W, D), CONV_W ** -0.5),
        "conv_w_out": n(ks[13], (N_ODD, D, D), D ** -0.5),
        "ffn_w1": n(ks[14], (N_EVEN, D, D_FF), D ** -0.5),
        "ffn_w3": n(ks[15], (N_EVEN, D, D_FF), D ** -0.5),
        "ffn_w2": n(ks[16], (N_EVEN, D_FF, D), D_FF ** -0.5),
        "moe_router": n(ks[17], (N_ODD, D, N_EXPERTS), D ** -0.5),
        "moe_w1": n(ks[18], (N_ODD, N_EXPERTS, D, D_FF), D ** -0.5),
        "moe_w3": n(jax.random.fold_in(ks[19], 0), (N_ODD, N_EXPERTS, D, D_FF), D ** -0.5),
        "moe_w2": n(jax.random.fold_in(ks[19], 1), (N_ODD, N_EXPERTS, D_FF, D), D_FF ** -0.5),
    }


def reference(x_prompt, x_sample, state_hgrn, state_conv, norm_mix, norm_ffn, norm_final,
              hgrn_w_in, hgrn_lb, hgrn_g_norm, hgrn_w_out, conv_w_in, conv_w, conv_w_out,
              ffn_w1, ffn_w3, ffn_w2, moe_router, moe_w1, moe_w3, moe_w2):
    lb_all = jnp.cumsum(jax.nn.softmax(hgrn_lb.astype(F32), axis=0), axis=0)
    xp, xs = x_prompt, x_sample
    hgrn_p, hgrn_s, conv_p, conv_s = [], [], [], []
    for i in range(DEPTH):
        j = i // 2
        if i % 2 == 0:
            xn = rmsnorm(xp, norm_mix[i])
            q, k, v, lf, gate = hgrn2_features(xn, hgrn_w_in[j], lb_all[i])
            o, S = hgrn2_chunked(q, k, v, lf)
            xp = xp + hgrn2_out(o, gate, hgrn_g_norm[j], hgrn_w_out[j], xp.dtype)
            hgrn_p.append(S.astype(xp.dtype))

            xn = rmsnorm(xs, norm_mix[i])
            q, k, v, lf, gate = hgrn2_features(xn, hgrn_w_in[j], lb_all[i])
            o, S = hgrn2_recurrent(q, k, v, lf, state_hgrn[:, j])
            xs = xs + hgrn2_out(o, gate, hgrn_g_norm[j], hgrn_w_out[j], xs.dtype)
            hgrn_s.append(S.astype(state_hgrn.dtype))

            xp = xp + swiglu(rmsnorm(xp, norm_ffn[i]), ffn_w1[j], ffn_w3[j], ffn_w2[j])
            xs = xs + swiglu(rmsnorm(xs, norm_ffn[i]), ffn_w1[j], ffn_w3[j], ffn_w2[j])
        else:
            zero_buf = jnp.zeros((xp.shape[0], CONV_W - 1, D_MODEL), xp.dtype)
            y, buf = short_conv_mix(rmsnorm(xp, norm_mix[i]), zero_buf, conv_w_in[j], conv_w[j], conv_w_out[j])
            xp = xp + y
            conv_p.append(buf)
            y, buf = short_conv_mix(rmsnorm(xs, norm_mix[i]), state_conv[:, j], conv_w_in[j], conv_w[j], conv_w_out[j])
            xs = xs + y
            conv_s.append(buf.astype(state_conv.dtype))

            xp = xp + moe_swiglu(rmsnorm(xp, norm_ffn[i]), moe_router[j], moe_w1[j], moe_w3[j], moe_w2[j])
            xs = xs + moe_swiglu(rmsnorm(xs, norm_ffn[i]), moe_router[j], moe_w1[j], moe_w3[j], moe_w2[j])
    y_prompt = rmsnorm(xp, norm_final)
    y_sample = rmsnorm(xs, norm_final)
    state_hgrn_prompt = jnp.stack(hgrn_p, axis=1)
    state_hgrn_sample = jnp.stack(hgrn_s, axis=1)
    state_conv_prompt = jnp.stack(conv_p, axis=1)
    state_conv_sample = jnp.stack(conv_s, axis=1)
    return (y_prompt, y_sample, state_hgrn_prompt, state_hgrn_sample, state_conv_prompt, state_conv_sample)
```

```python
import functools

import jax
import jax.numpy as jnp
from jax import lax
from jax.experimental import pallas as pl
from jax.experimental.pallas import tpu as pltpu

F32 = jnp.float32
BF16 = jnp.bfloat16
I32 = jnp.int32

EPS = 1e-6
HEAD = 128
CHUNK = 32
CONV_W = 3
TOP_K = 2
LANES = 128
V7X_VMEM_LIMIT = 56 * 1024 * 1024

ARB = "arbitrary"


def _params(n_axes):
    return pltpu.CompilerParams(dimension_semantics=(ARB,) * n_axes, vmem_limit_bytes=V7X_VMEM_LIMIT)


def _pick_tile(n, candidates):
    for c in candidates:
        if n % c == 0:
            return c
    raise ValueError(f"no tile in {candidates} divides {n}")


def _rmsnorm_bf16(x, g):
    ms = jnp.mean(x * x, axis=-1, keepdims=True)
    return (x * lax.rsqrt(ms + EPS) * g).astype(BF16)


def _proj_body(x_ref, g_ref, w_ref, o_ref, xn_ref):
    @pl.when(pl.program_id(1) == 0)
    def _():
        xn_ref[...] = _rmsnorm_bf16(x_ref[...], g_ref[...])

    o_ref[...] = jnp.dot(xn_ref[...], w_ref[...].astype(BF16), preferred_element_type=F32)


def _proj(x, g, w, tm, tn):
    n, d = x.shape
    dout = w.shape[1]
    return pl.pallas_call(
        _proj_body,
        grid=(n // tm, dout // tn),
        in_specs=[pl.BlockSpec((tm, d), lambda i, j: (i, 0)),
                  pl.BlockSpec((1, d), lambda i, j: (0, 0)),
                  pl.BlockSpec((d, tn), lambda i, j: (0, j))],
        out_specs=pl.BlockSpec((tm, tn), lambda i, j: (i, j)),
        out_shape=jax.ShapeDtypeStruct((n, dout), F32),
        scratch_shapes=[pltpu.VMEM((tm, d), BF16)],
        compiler_params=_params(2),
        name="proj",
    )(x, g.reshape(1, d), w)


def _forget_lower_bound(lb_ref, layer):
    lbw = lb_ref[...]
    e = jnp.exp(lbw - jnp.max(lbw, axis=0, keepdims=True))
    sm = e / jnp.sum(e, axis=0, keepdims=True)
    return jnp.sum(sm[:layer + 1], axis=0, keepdims=True)


def _head_rmsnorm(o):
    return o * lax.rsqrt(jnp.mean(o * o, axis=-1, keepdims=True) + EPS)


def _hgrn_prompt_body(layer, n_heads, tb, q_ref, f_ref, v_ref, gt_ref, lb_ref, gn_ref,
                      o_ref, s_ref, st_scr):
    t = pl.program_id(1)

    @pl.when(t == 0)
    def _():
        st_scr[...] = jnp.zeros_like(st_scr)

    lb = _forget_lower_bound(lb_ref, layer)
    gn = gn_ref[...]
    row = lax.broadcasted_iota(I32, (CHUNK, CHUNK), 0)
    col = lax.broadcasted_iota(I32, (CHUNK, CHUNK), 1)
    causal = row >= col
    tri = causal.astype(F32)

    def chunk_step(c, carry):
        r0 = pl.multiple_of(c * CHUNK, CHUNK)
        rows = pl.ds(r0, CHUNK)
        q = jax.nn.silu(q_ref[rows, :])
        f = lb + (1.0 - lb) * jax.nn.sigmoid(f_ref[rows, :])
        logf = jnp.log(f)
        k = 1.0 - f
        v = v_ref[rows, :]
        G = jnp.dot(tri, logf, preferred_element_type=F32, precision=lax.Precision.HIGHEST)
        g_last = G[CHUNK - 1:CHUNK, :]
        q_in = (q * jnp.exp(G)).astype(BF16)
        k_in = (k * jnp.exp(-G)).astype(BF16)
        k_dec = (k * jnp.exp(g_last - G)).astype(BF16)
        decay = jnp.exp(g_last)
        v16 = v.astype(BF16)
        gate = jax.nn.silu(gt_ref[rows, :])
        for h in range(n_heads):
            hs = slice(h * HEAD, (h + 1) * HEAD)
            a = lax.dot_general(q_in[:, hs], k_in[:, hs], (((1,), (1,)), ((), ())),
                                preferred_element_type=F32)
            a = jnp.where(causal, a, 0.0).astype(BF16)
            st = st_scr[h]
            o = jnp.dot(a, v16[:, hs], preferred_element_type=F32)
            o = o + lax.dot_general(q_in[:, hs], st.astype(BF16), (((1,), (1,)), ((), ())),
                                    preferred_element_type=F32)
            upd = lax.dot_general(v16[:, hs], k_dec[:, hs], (((0,), (0,)), ((), ())),
                                  preferred_element_type=F32)
            st_scr[h] = st * decay[:, hs] + upd
            o = _head_rmsnorm(o) * gn[:, hs] * gate[:, hs]
            o_ref[rows, hs] = o.astype(BF16)
        return carry

    lax.fori_loop(0, tb // CHUNK, chunk_step, 0)

    @pl.when(t == pl.num_programs(1) - 1)
    def _():
        for h in range(n_heads):
            s_ref[0, 0, h] = st_scr[h].T


def _hgrn_prompt(p, hgrn_lb, g_norm, layer, batch, seq, tb):
    d = g_norm.shape[0]
    n_heads = d // HEAD
    n_t = seq // tb
    blk = lambda kk: pl.BlockSpec((tb, d), lambda b, t, kk=kk: (b * n_t + t, kk))
    return pl.pallas_call(
        functools.partial(_hgrn_prompt_body, layer, n_heads, tb),
        grid=(batch, n_t),
        in_specs=[blk(0), blk(1), blk(2), blk(3),
                  pl.BlockSpec(hgrn_lb.shape, lambda b, t: (0, 0)),
                  pl.BlockSpec((1, d), lambda b, t: (0, 0))],
        out_specs=[pl.BlockSpec((tb, d), lambda b, t: (b * n_t + t, 0)),
                   pl.BlockSpec((1, 1, n_heads, HEAD, HEAD), lambda b, t: (b, 0, 0, 0, 0))],
        out_shape=[jax.ShapeDtypeStruct((batch * seq, d), BF16),
                   jax.ShapeDtypeStruct((batch, 1, n_heads, HEAD, HEAD), F32)],
        scratch_shapes=[pltpu.VMEM((n_heads, HEAD, HEAD), F32)],
        compiler_params=_params(2),
        name="hgrn_prompt",
    )(p, p, p, p, hgrn_lb, g_norm.reshape(1, d))


def _hgrn_sample_body(layer, tg, q_ref, f_ref, v_ref, gt_ref, lb_ref, gn_ref, s0_ref,
                      o_ref, s_ref, o_scr):
    lb_h = _forget_lower_bound(lb_ref, layer)
    gn_h = gn_ref[...]

    q = jax.nn.silu(q_ref[...])
    f = lb_h + (1.0 - lb_h) * jax.nn.sigmoid(f_ref[...])
    decay = f
    k = 1.0 - f
    v = v_ref[...]
    q_t, d_t, k_t = q.T, decay.T, k.T
    for j in range(tg):
        s = d_t[:, j:j + 1] * s0_ref[j, 0, 0] + k_t[:, j:j + 1] * v[j:j + 1, :]
        s_ref[j, 0, 0] = s
        o_scr[j:j + 1, :] = jnp.sum(q_t[:, j:j + 1] * s, axis=0, keepdims=True)
    o = _head_rmsnorm(o_scr[...]) * gn_h * jax.nn.silu(gt_ref[...])
    o_ref[...] = o.astype(BF16)


def _hgrn_sample(p, hgrn_lb, g_norm, state, layer, row0, tg):
    nb = state.shape[0]
    d = g_norm.shape[0]
    n_heads = d // HEAD
    rb0 = row0 // tg
    blk = lambda kk: pl.BlockSpec((tg, HEAD), lambda g, h, kk=kk: (rb0 + g, kk * n_heads + h))
    st_spec = pl.BlockSpec((tg, 1, 1, HEAD, HEAD), lambda g, h: (g, 0, h, 0, 0))
    return pl.pallas_call(
        functools.partial(_hgrn_sample_body, layer, tg),
        grid=(nb // tg, n_heads),
        in_specs=[blk(0), blk(1), blk(2), blk(3),
                  pl.BlockSpec((hgrn_lb.shape[0], HEAD), lambda g, h: (0, h)),
                  pl.BlockSpec((1, HEAD), lambda g, h: (0, h)),
                  st_spec],
        out_specs=[pl.BlockSpec((tg, HEAD), lambda g, h: (g, h)), st_spec],
        out_shape=[jax.ShapeDtypeStruct((nb, d), BF16),
                   jax.ShapeDtypeStruct(state.shape, F32)],
        scratch_shapes=[pltpu.VMEM((tg, HEAD), F32)],
        compiler_params=_params(2),
        name="hgrn_sample",
    )(p, p, p, p, hgrn_lb, g_norm.reshape(1, d), state)


_CARRY = 8


def _conv_prompt_body(tb, b_ref, c_ref, h_ref, w_ref, z_ref, buf_ref, u_scr):
    t = pl.program_id(1)

    @pl.when(t == 0)
    def _():
        u_scr[0:_CARRY, :] = jnp.zeros((_CARRY, u_scr.shape[1]), F32)

    u = c_ref[...] * h_ref[...]
    u_scr[_CARRY:_CARRY + tb, :] = u
    w = w_ref[...]
    y = w[0:1, :] * u_scr[_CARRY - 2:_CARRY - 2 + tb, :]
    y = y + w[1:2, :] * u_scr[_CARRY - 1:_CARRY - 1 + tb, :]
    y = y + w[2:3, :] * u
    z_ref[...] = (b_ref[...] * y).astype(BF16)
    u_scr[0:_CARRY, :] = u[tb - _CARRY:tb, :]

    @pl.when(t == pl.num_programs(1) - 1)
    def _():
        buf_ref[0] = u[tb - (CONV_W - 1):tb, :]


def _conv_prompt(pc, conv_w, batch, seq, tb):
    d = conv_w.shape[1]
    n_t = seq // tb
    blk = lambda kk: pl.BlockSpec((tb, d), lambda b, t, kk=kk: (b * n_t + t, kk))
    return pl.pallas_call(
        functools.partial(_conv_prompt_body, tb),
        grid=(batch, n_t),
        in_specs=[blk(0), blk(1), blk(2), pl.BlockSpec(conv_w.shape, lambda b, t: (0, 0))],
        out_specs=[pl.BlockSpec((tb, d), lambda b, t: (b * n_t + t, 0)),
                   pl.BlockSpec((1, CONV_W - 1, d), lambda b, t: (b, 0, 0))],
        out_shape=[jax.ShapeDtypeStruct((batch * seq, d), BF16),
                   jax.ShapeDtypeStruct((batch, CONV_W - 1, d), F32)],
        scratch_shapes=[pltpu.VMEM((_CARRY + tb, d), F32)],
        compiler_params=_params(2),
        name="conv_prompt",
    )(pc, pc, pc, conv_w)


def _conv_sample_body(d, b_ref, c_ref, h_ref, w_ref, st_ref, z_ref, buf_ref):
    u = c_ref[...] * h_ref[...]
    w = w_ref[...]
    buf0 = st_ref[:, 0:d]
    buf1 = st_ref[:, d:2 * d]
    y = w[0:1, :] * buf0
    y = y + w[1:2, :] * buf1
    y = y + w[2:3, :] * u
    z_ref[...] = (b_ref[...] * y).astype(BF16)
    buf_ref[:, 0:d] = buf1
    buf_ref[:, d:2 * d] = u


def _conv_sample(pc, conv_w, state2d, row0):
    nb = state2d.shape[0]
    d = conv_w.shape[1]
    rb0 = row0 // nb
    blk = lambda kk: pl.BlockSpec((nb, d), lambda i, kk=kk: (rb0, kk))
    return pl.pallas_call(
        functools.partial(_conv_sample_body, d),
        grid=(1,),
        in_specs=[blk(0), blk(1), blk(2), pl.BlockSpec(conv_w.shape, lambda i: (0, 0)),
                  pl.BlockSpec(state2d.shape, lambda i: (0, 0))],
        out_specs=[pl.BlockSpec((nb, d), lambda i: (0, 0)),
                   pl.BlockSpec(state2d.shape, lambda i: (0, 0))],
        out_shape=[jax.ShapeDtypeStruct((nb, d), BF16),
                   jax.ShapeDtypeStruct(state2d.shape, F32)],
        compiler_params=_params(1),
        name="conv_sample",
    )(pc, pc, pc, conv_w, state2d)


def _outproj_body(x_ref, z_ref, w_ref, o_ref, w16_ref):
    @pl.when(pl.program_id(0) == 0)
    def _():
        w16_ref[...] = w_ref[...].astype(BF16)

    o_ref[...] = x_ref[...] + jnp.dot(z_ref[...], w16_ref[...], preferred_element_type=F32)


def _outproj(x, z, w, tm):
    n, d = x.shape
    return pl.pallas_call(
        _outproj_body,
        grid=(n // tm,),
        in_specs=[pl.BlockSpec((tm, d), lambda i: (i, 0)),
                  pl.BlockSpec((tm, d), lambda i: (i, 0)),
                  pl.BlockSpec((d, d), lambda i: (0, 0))],
        out_specs=pl.BlockSpec((tm, d), lambda i: (i, 0)),
        out_shape=jax.ShapeDtypeStruct((n, d), F32),
        scratch_shapes=[pltpu.VMEM((d, d), BF16)],
        compiler_params=_params(1),
        name="outproj",
    )(x, z, w)


def _swiglu_partial(xn, w1, w3, w2):
    h1 = jnp.dot(xn, w1.astype(BF16), preferred_element_type=F32)
    h3 = jnp.dot(xn, w3.astype(BF16), preferred_element_type=F32)
    h = (jax.nn.silu(h1) * h3).astype(BF16)
    return jnp.dot(h, w2.astype(BF16), preferred_element_type=F32)


def _ffn_body(x_ref, g_ref, w1_ref, w3_ref, w2_ref, o_ref, xn_ref, acc_ref):
    f = pl.program_id(1)

    @pl.when(f == 0)
    def _():
        xn_ref[...] = _rmsnorm_bf16(x_ref[...], g_ref[...])
        acc_ref[...] = jnp.zeros_like(acc_ref)

    acc_ref[...] += _swiglu_partial(xn_ref[...], w1_ref[...], w3_ref[...], w2_ref[...])

    @pl.when(f == pl.num_programs(1) - 1)
    def _():
        o_ref[...] = x_ref[...] + acc_ref[...]


def _ffn(x, g, w1, w3, w2, tm, tf):
    n, d = x.shape
    dff = w1.shape[1]
    return pl.pallas_call(
        _ffn_body,
        grid=(n // tm, dff // tf),
        in_specs=[pl.BlockSpec((tm, d), lambda i, f: (i, 0)),
                  pl.BlockSpec((1, d), lambda i, f: (0, 0)),
                  pl.BlockSpec((d, tf), lambda i, f: (0, f)),
                  pl.BlockSpec((d, tf), lambda i, f: (0, f)),
                  pl.BlockSpec((tf, d), lambda i, f: (f, 0))],
        out_specs=pl.BlockSpec((tm, d), lambda i, f: (i, 0)),
        out_shape=jax.ShapeDtypeStruct((n, d), F32),
        scratch_shapes=[pltpu.VMEM((tm, d), BF16), pltpu.VMEM((tm, d), F32)],
        compiler_params=_params(2),
        name="ffn",
    )(x, g.reshape(1, d), w1, w3, w2)


def _router_body(n_experts, x_ref, g_ref, rw_ref, route_ref, cnt_ref, cnt_scr):
    i = pl.program_id(0)
    tm = x_ref.shape[0]

    @pl.when(i == 0)
    def _():
        cnt_scr[...] = jnp.zeros_like(cnt_scr)

    xn = _rmsnorm_bf16(x_ref[...], g_ref[...])
    logits = jnp.dot(xn, rw_ref[...].astype(BF16), preferred_element_type=F32)
    lane = lax.broadcasted_iota(I32, (tm, LANES), 1).astype(F32)
    neg = jnp.float32(-jnp.inf)
    logits = jnp.where(lane < n_experts, logits, neg)
    m1 = jnp.max(logits, axis=-1, keepdims=True)
    i1 = jnp.min(jnp.where(logits == m1, lane, float(LANES)), axis=-1, keepdims=True)
    i1 = jnp.minimum(i1, float(n_experts - 1))
    rest = jnp.where(lane == i1, neg, logits)
    m2 = jnp.max(rest, axis=-1, keepdims=True)
    i2 = jnp.min(jnp.where(rest == m2, lane, float(LANES)), axis=-1, keepdims=True)
    i2 = jnp.minimum(i2, float(n_experts - 1))
    e2 = jnp.exp(m2 - m1)
    den = 1.0 + e2
    g1 = 1.0 / den
    g2 = e2 / den

    sel1 = lane == i1
    sel2 = lane == i2
    onehot = jnp.logical_or(sel1, sel2)
    row = lax.broadcasted_iota(I32, (tm, tm), 0)
    col = lax.broadcasted_iota(I32, (tm, tm), 1)
    before = (row > col).astype(BF16)
    rank = jnp.dot(before, onehot.astype(BF16), preferred_element_type=F32) + cnt_scr[...]
    r1 = jnp.sum(jnp.where(sel1, rank, 0.0), axis=-1, keepdims=True)
    r2 = jnp.sum(jnp.where(sel2, rank, 0.0), axis=-1, keepdims=True)
    cnt_scr[...] += jnp.sum(onehot.astype(F32), axis=0, keepdims=True)

    out = jnp.zeros((tm, LANES), F32)
    for slot, val in enumerate((i1, i2, g1, g2, r1, r2)):
        out = jnp.where(lane == slot, val, out)
    route_ref[...] = out
    cnt_ref[...] = cnt_scr[...]


def _router(x, g, rw_pad, n_experts, tm):
    n, d = x.shape
    return pl.pallas_call(
        functools.partial(_router_body, n_experts),
        grid=(n // tm,),
        in_specs=[pl.BlockSpec((tm, d), lambda i: (i, 0)),
                  pl.BlockSpec((1, d), lambda i: (0, 0)),
                  pl.BlockSpec((d, LANES), lambda i: (0, 0))],
        out_specs=[pl.BlockSpec((tm, LANES), lambda i: (i, 0)),
                   pl.BlockSpec((1, LANES), lambda i: (0, 0))],
        out_shape=[jax.ShapeDtypeStruct((n, LANES), F32),
                   jax.ShapeDtypeStruct((1, LANES), F32)],
        scratch_shapes=[pltpu.VMEM((1, LANES), F32)],
        compiler_params=_params(1),
        name="router",
    )(x, g.reshape(1, d), rw_pad)


def _start_row_gather(src_hbm, idx_ref, n_rows, dst, sem):
    def body(r, carry):
        pltpu.make_async_copy(src_hbm.at[pl.ds(idx_ref[0, 0, r], 1)], dst.at[pl.ds(r, 1)], sem).start()
        return carry

    lax.fori_loop(0, n_rows, body, 0, unroll=8)


def _wait_row_gather(src_hbm, n_rows, dst, sem):
    pltpu.make_async_copy(src_hbm.at[pl.ds(0, n_rows)], dst, sem).wait()


def _moe_body(te_ref, nu_ref, idx_cur, idx_nxt, x_hbm, gate_ref, g_ref, w1_ref, w3_ref, w2_ref,
              y_ref, xbuf, sem, xn_ref, acc_ref):
    del te_ref
    i = pl.program_id(0)
    f = pl.program_id(1)
    n_tiles = pl.num_programs(0)
    tm = xbuf.shape[1]
    slot = lax.rem(i, 2)
    used = i < nu_ref[0]

    @pl.when(f == 0)
    def _():
        @pl.when(i == 0)
        def _():
            _start_row_gather(x_hbm, idx_cur, tm, xbuf.at[0], sem.at[0])

        @pl.when(i + 1 < n_tiles)
        def _():
            _start_row_gather(x_hbm, idx_nxt, tm, xbuf.at[1 - slot], sem.at[1 - slot])

        _wait_row_gather(x_hbm, tm, xbuf.at[slot], sem.at[slot])
        xn_ref[...] = _rmsnorm_bf16(xbuf[slot], g_ref[...])
        acc_ref[...] = jnp.zeros_like(acc_ref)

    @pl.when(used)
    def _():
        acc_ref[...] += _swiglu_partial(xn_ref[...], w1_ref[0], w3_ref[0], w2_ref[0])

    @pl.when(f == pl.num_programs(1) - 1)
    def _():
        y_ref[...] = acc_ref[...] * gate_ref[...]


def _moe(x, g, w1, w3, w2, tile_expert, n_used, inv3, gate_sorted, tm, tf):
    n_tiles = inv3.shape[0]
    d = x.shape[1]
    dff = w1.shape[2]
    n_f = dff // tf

    def w_col(i, f, te, nu):
        return (te[i], 0, jnp.where(i < nu[0], f, n_f - 1))

    def w_row(i, f, te, nu):
        return (te[i], jnp.where(i < nu[0], f, n_f - 1), 0)

    grid_spec = pltpu.PrefetchScalarGridSpec(
        num_scalar_prefetch=2,
        grid=(n_tiles, n_f),
        in_specs=[pl.BlockSpec((1, 1, tm), lambda i, f, te, nu: (i, 0, 0), memory_space=pltpu.SMEM),
                  pl.BlockSpec((1, 1, tm), lambda i, f, te, nu: (jnp.minimum(i + 1, n_tiles - 1), 0, 0),
                               memory_space=pltpu.SMEM),
                  pl.BlockSpec(memory_space=pl.ANY),
                  pl.BlockSpec((tm, 1), lambda i, f, te, nu: (i, 0)),
                  pl.BlockSpec((1, d), lambda i, f, te, nu: (0, 0)),
                  pl.BlockSpec((1, d, tf), w_col),
                  pl.BlockSpec((1, d, tf), w_col),
                  pl.BlockSpec((1, tf, d), w_row)],
        out_specs=pl.BlockSpec((tm, d), lambda i, f, te, nu: (i, 0)),
        scratch_shapes=[pltpu.VMEM((2, tm, d), F32),
                        pltpu.SemaphoreType.DMA((2,)),
                        pltpu.VMEM((tm, d), BF16),
                        pltpu.VMEM((tm, d), F32)],
    )
    return pl.pallas_call(
        _moe_body,
        grid_spec=grid_spec,
        out_shape=jax.ShapeDtypeStruct((n_tiles * tm, d), F32),
        compiler_params=_params(2),
        name="moe",
    )(tile_expert, n_used, inv3, inv3, x, gate_sorted, g.reshape(1, d), w1, w3, w2)


def _combine_body(pos_cur, pos_nxt, x_ref, g_ref, y_hbm, o_ref, ybuf, sem):
    i = pl.program_id(0)
    n_tiles = pl.num_programs(0)
    rows = ybuf.shape[1]
    slot = lax.rem(i, 2)

    @pl.when(i == 0)
    def _():
        _start_row_gather(y_hbm, pos_cur, rows, ybuf.at[0], sem.at[0])

    @pl.when(i + 1 < n_tiles)
    def _():
        _start_row_gather(y_hbm, pos_nxt, rows, ybuf.at[1 - slot], sem.at[1 - slot])

    _wait_row_gather(y_hbm, rows, ybuf.at[slot], sem.at[slot])
    tc = rows // TOP_K
    moe = ybuf[slot, 0:tc, :] + ybuf[slot, tc:rows, :]
    x = x_ref[...] + moe
    ms = jnp.mean(x * x, axis=-1, keepdims=True)
    o_ref[...] = x * lax.rsqrt(ms + EPS) * g_ref[...]


def _combine(x, g, y_sorted, pos3, row0, n_rows, tc):
    d = x.shape[1]
    n_tiles = n_rows // tc
    rb0 = row0 // tc
    grid_spec = pltpu.PrefetchScalarGridSpec(
        num_scalar_prefetch=0,
        grid=(n_tiles,),
        in_specs=[pl.BlockSpec((1, 1, TOP_K * tc), lambda i: (i, 0, 0), memory_space=pltpu.SMEM),
                  pl.BlockSpec((1, 1, TOP_K * tc), lambda i: (jnp.minimum(i + 1, n_tiles - 1), 0, 0),
                               memory_space=pltpu.SMEM),
                  pl.BlockSpec((tc, d), lambda i: (rb0 + i, 0)),
                  pl.BlockSpec((1, d), lambda i: (0, 0)),
                  pl.BlockSpec(memory_space=pl.ANY)],
        out_specs=pl.BlockSpec((tc, d), lambda i: (i, 0)),
        scratch_shapes=[pltpu.VMEM((2, TOP_K * tc, d), F32), pltpu.SemaphoreType.DMA((2,))],
    )
    return pl.pallas_call(
        _combine_body,
        grid_spec=grid_spec,
        out_shape=jax.ShapeDtypeStruct((n_rows, d), F32),
        compiler_params=_params(1),
        name="combine",
    )(pos3, pos3, x, g.reshape(1, d), y_sorted)


def _tile_positions(pos, row0, n_rows, tc):
    p = pos[row0:row0 + n_rows].reshape(n_rows // tc, tc, TOP_K)
    return jnp.swapaxes(p, 1, 2).reshape(n_rows // tc, 1, TOP_K * tc)


def kernel(x_prompt, x_sample, state_hgrn, state_conv, norm_mix, norm_ffn, norm_final, hgrn_w_in, hgrn_lb, hgrn_g_norm, hgrn_w_out, conv_w_in, conv_w, conv_w_out, ffn_w1, ffn_w3, ffn_w2, moe_router, moe_w1, moe_w3, moe_w2):
    batch, seq, d = x_prompt.shape
    n_dec = x_sample.shape[0]
    n_prompt = batch * seq
    n = n_prompt + n_dec
    n_experts = moe_router.shape[-1]
    assert x_sample.shape[1] == 1 and d % HEAD == 0 and seq % CHUNK == 0
    assert norm_mix.shape[0] == 2, "one HGRN2 layer followed by one short-conv layer"

    tm = _pick_tile(n, (688, 384, 128, 16))
    tb = _pick_tile(seq, (256, 128, 64, 32))
    tg = _pick_tile(n_dec, (16,))
    tn = _pick_tile(d, (1024, 512, 256, 128))
    tf = _pick_tile(ffn_w1.shape[-1], (512, 256, 128))
    tm_moe = _pick_tile(n_prompt, (512, 128, 16))
    tc = _pick_tile(n_prompt, (512, 128, 16))
    assert n_prompt % tg == 0 and n_prompt % n_dec == 0 and n_prompt % tc == 0

    x = jnp.concatenate([x_prompt.reshape(n_prompt, d), x_sample.reshape(n_dec, d)], axis=0)

    p = _proj(x, norm_mix[0], hgrn_w_in[0], tm, tn)
    o_p, s_prompt = _hgrn_prompt(p, hgrn_lb, hgrn_g_norm[0], 0, batch, seq, tb)
    o_s, s_sample = _hgrn_sample(p, hgrn_lb, hgrn_g_norm[0], state_hgrn, 0, n_prompt, tg)
    x = _outproj(x, jnp.concatenate([o_p, o_s], axis=0), hgrn_w_out[0], tm)
    x = _ffn(x, norm_ffn[0], ffn_w1[0], ffn_w3[0], ffn_w2[0], tm, tf)

    pc = _proj(x, norm_mix[1], conv_w_in[0], tm, tn)
    z_p, c_prompt = _conv_prompt(pc, conv_w[0], batch, seq, tb)
    z_s, c_sample = _conv_sample(pc, conv_w[0], state_conv.reshape(n_dec, (CONV_W - 1) * d), n_prompt)
    x = _outproj(x, jnp.concatenate([z_p, z_s], axis=0), conv_w_out[0], tm)

    rw_pad = jnp.pad(moe_router[0], ((0, 0), (0, LANES - n_experts)))
    route, cnt = _router(x, norm_ffn[1], rw_pad, n_experts, tm)
    ids = route[:, 0:TOP_K].astype(I32)
    gates = route[:, TOP_K:2 * TOP_K]
    ranks = route[:, 2 * TOP_K:3 * TOP_K].astype(I32)
    counts = cnt[0, :n_experts].astype(I32)

    n_tiles = (TOP_K * n + n_experts * (tm_moe - 1)) // tm_moe
    padded = ((counts + tm_moe - 1) // tm_moe) * tm_moe
    ends = jnp.cumsum(padded)
    starts = ends - padded
    cstart = jnp.cumsum(counts) - counts
    pos = starts[ids] + ranks
    order = jnp.argsort(ids.reshape(-1), stable=True).astype(I32)
    slot_pos = jnp.arange(n_tiles * tm_moe, dtype=I32)
    slot_e = jnp.minimum(jnp.searchsorted(ends, slot_pos, side="right"), n_experts - 1).astype(I32)
    slot_rank = slot_pos - starts[slot_e]
    slot_valid = slot_rank < counts[slot_e]
    slot_asg = order[jnp.clip(cstart[slot_e] + slot_rank, 0, TOP_K * n - 1)]
    inv = jnp.where(slot_valid, slot_asg // TOP_K, 0).astype(I32)
    gate_sorted = jnp.where(slot_valid, gates.reshape(-1)[slot_asg], 0.0).reshape(-1, 1)
    tile_expert = slot_e[::tm_moe]
    n_used = (ends[-1] // tm_moe).astype(I32).reshape(1)

    y_sorted = _moe(x, norm_ffn[1], moe_w1[0], moe_w3[0], moe_w2[0], tile_expert, n_used,
                    inv.reshape(n_tiles, 1, tm_moe), gate_sorted, tm_moe, tf)

    y_prompt = _combine(x, norm_final, y_sorted, _tile_positions(pos, 0, n_prompt, tc), 0, n_prompt, tc)
    y_sample = _combine(x, norm_final, y_sorted, _tile_positions(pos, n_prompt, n_dec, n_dec),
                        n_prompt, n_dec, n_dec)

    return (y_prompt.reshape(batch, seq, d),
            y_sample.reshape(n_dec, 1, d),
            s_prompt,
            s_sample,
            c_prompt.reshape(batch, 1, CONV_W - 1, d),
            c_sample.reshape(n_dec, 1, CONV_W - 1, d))
```

```python
import functools

import jax
import jax.numpy as jnp
from jax import lax
from jax.experimental import pallas as pl
from jax.experimental.pallas import tpu as pltpu

F32 = jnp.float32
BF16 = jnp.bfloat16
I32 = jnp.int32

EPS = 1e-6
HEAD = 128
CHUNK = 32
CONV_W = 3
TOP_K = 2
LANES = 128
V7X_VMEM_LIMIT = 56 * 1024 * 1024

ARB = "arbitrary"


def _params(n_axes):
    return pltpu.CompilerParams(dimension_semantics=(ARB,) * n_axes, vmem_limit_bytes=V7X_VMEM_LIMIT)


def _pick_tile(n, candidates):
    for c in candidates:
        if n % c == 0:
            return c
    raise ValueError(f"no tile in {candidates} divides {n}")


def _rmsnorm_bf16(x, g):
    ms = jnp.mean(x * x, axis=-1, keepdims=True)
    return (x * lax.rsqrt(ms + EPS) * g).astype(BF16)


def _proj_body(x_ref, g_ref, w_ref, o_ref, xn_ref):
    @pl.when(pl.program_id(1) == 0)
    def _():
        xn_ref[...] = _rmsnorm_bf16(x_ref[...], g_ref[...])

    o_ref[...] = jnp.dot(xn_ref[...], w_ref[...].astype(BF16), preferred_element_type=F32)


def _proj(x, g, w, tm, tn):
    n, d = x.shape
    dout = w.shape[1]
    return pl.pallas_call(
        _proj_body,
        grid=(n // tm, dout // tn),
        in_specs=[pl.BlockSpec((tm, d), lambda i, j: (i, 0)),
                  pl.BlockSpec((1, d), lambda i, j: (0, 0)),
                  pl.BlockSpec((d, tn), lambda i, j: (0, j))],
        out_specs=pl.BlockSpec((tm, tn), lambda i, j: (i, j)),
        out_shape=jax.ShapeDtypeStruct((n, dout), F32),
        scratch_shapes=[pltpu.VMEM((tm, d), BF16)],
        compiler_params=_params(2),
        name="proj",
    )(x, g.reshape(1, d), w)


def _forget_lower_bound(lb_ref, layer):
    lbw = lb_ref[...]
    e = jnp.exp(lbw - jnp.max(lbw, axis=0, keepdims=True))
    sm = e / jnp.sum(e, axis=0, keepdims=True)
    return jnp.sum(sm[:layer + 1], axis=0, keepdims=True)


def _head_rmsnorm(o):
    return o * lax.rsqrt(jnp.mean(o * o, axis=-1, keepdims=True) + EPS)


def _hgrn_prompt_body(layer, n_heads, tb, q_ref, f_ref, v_ref, gt_ref, lb_ref, gn_ref,
                      o_ref, s_ref, st_scr):
    t = pl.program_id(1)

    @pl.when(t == 0)
    def _():
        st_scr[...] = jnp.zeros_like(st_scr)

    lb = _forget_lower_bound(lb_ref, layer)
    gn = gn_ref[...]
    row = lax.broadcasted_iota(I32, (CHUNK, CHUNK), 0)
    col = lax.broadcasted_iota(I32, (CHUNK, CHUNK), 1)
    causal = row >= col
    tri = causal.astype(F32)

    def chunk_step(c, carry):
        r0 = pl.multiple_of(c * CHUNK, CHUNK)
        rows = pl.ds(r0, CHUNK)
        q = jax.nn.silu(q_ref[rows, :])
        f = lb + (1.0 - lb) * jax.nn.sigmoid(f_ref[rows, :])
        logf = jnp.log(f)
        k = 1.0 - f
        v = v_ref[rows, :]
        G = jnp.dot(tri, logf, preferred_element_type=F32, precision=lax.Precision.HIGHEST)
        g_last = G[CHUNK - 1:CHUNK, :]
        q_in = (q * jnp.exp(G)).astype(BF16)
        k_in = (k * jnp.exp(-G)).astype(BF16)
        k_dec = (k * jnp.exp(g_last - G)).astype(BF16)
        decay = jnp.exp(g_last)
        v16 = v.astype(BF16)
        gate = jax.nn.silu(gt_ref[rows, :])
        for h in range(n_heads):
            hs = slice(h * HEAD, (h + 1) * HEAD)
            a = lax.dot_general(q_in[:, hs], k_in[:, hs], (((1,), (1,)), ((), ())),
                                preferred_element_type=F32)
            a = jnp.where(causal, a, 0.0).astype(BF16)
            st = st_scr[h]
            o = jnp.dot(a, v16[:, hs], preferred_element_type=F32)
            o = o + lax.dot_general(q_in[:, hs], st.astype(BF16), (((1,), (1,)), ((), ())),
                                    preferred_element_type=F32)
            upd = lax.dot_general(v16[:, hs], k_dec[:, hs], (((0,), (0,)), ((), ())),
                                  preferred_element_type=F32)
            st_scr[h] = st * decay[:, hs] + upd
            o = _head_rmsnorm(o) * gn[:, hs] * gate[:, hs]
            o_ref[rows, hs] = o.astype(BF16)
        return carry

    lax.fori_loop(0, tb // CHUNK, chunk_step, 0)

    @pl.when(t == pl.num_programs(1) - 1)
    def _():
        for h in range(n_heads):
            s_ref[0, 0, h] = st_scr[h].T


def _hgrn_prompt(p, hgrn_lb, g_norm, layer, batch, seq, tb):
    d = g_norm.shape[0]
    n_heads = d // HEAD
    n_t = seq // tb
    blk = lambda kk: pl.BlockSpec((tb, d), lambda b, t, kk=kk: (b * n_t + t, kk))
    return pl.pallas_call(
        functools.partial(_hgrn_prompt_body, layer, n_heads, tb),
        grid=(batch, n_t),
        in_specs=[blk(0), blk(1), blk(2), blk(3),
                  pl.BlockSpec(hgrn_lb.shape, lambda b, t: (0, 0)),
                  pl.BlockSpec((1, d), lambda b, t: (0, 0))],
        out_specs=[pl.BlockSpec((tb, d), lambda b, t: (b * n_t + t, 0)),
                   pl.BlockSpec((1, 1, n_heads, HEAD, HEAD), lambda b, t: (b, 0, 0, 0, 0))],
        out_shape=[jax.ShapeDtypeStruct((batch * seq, d), BF16),
                   jax.ShapeDtypeStruct((batch, 1, n_heads, HEAD, HEAD), F32)],
        scratch_shapes=[pltpu.VMEM((n_heads, HEAD, HEAD), F32)],
        compiler_params=_params(2),
        name="hgrn_prompt",
    )(p, p, p, p, hgrn_lb, g_norm.reshape(1, d))


def _hgrn_sample_body(layer, tg, q_ref, f_ref, v_ref, gt_ref, lb_ref, gn_ref, s0_ref,
                      o_ref, s_ref, o_scr):
    lb_h = _forget_lower_bound(lb_ref, layer)
    gn_h = gn_ref[...]

    q = jax.nn.silu(q_ref[...])
    f = lb_h + (1.0 - lb_h) * jax.nn.sigmoid(f_ref[...])
    decay = f
    k = 1.0 - f
    v = v_ref[...]
    q_t, d_t, k_t = q.T, decay.T, k.T
    for j in range(tg):
        s = d_t[:, j:j + 1] * s0_ref[j, 0, 0] + k_t[:, j:j + 1] * v[j:j + 1, :]
        s_ref[j, 0, 0] = s
        o_scr[j:j + 1, :] = jnp.sum(q_t[:, j:j + 1] * s, axis=0, keepdims=True)
    o = _head_rmsnorm(o_scr[...]) * gn_h * jax.nn.silu(gt_ref[...])
    o_ref[...] = o.astype(BF16)


def _hgrn_sample(p, hgrn_lb, g_norm, state, layer, row0, tg):
    nb = state.shape[0]
    d = g_norm.shape[0]
    n_heads = d // HEAD
    rb0 = row0 // tg
    blk = lambda kk: pl.BlockSpec((tg, HEAD), lambda g, h, kk=kk: (rb0 + g, kk * n_heads + h))
    st_spec = pl.BlockSpec((tg, 1, 1, HEAD, HEAD), lambda g, h: (g, 0, h, 0, 0))
    return pl.pallas_call(
        functools.partial(_hgrn_sample_body, layer, tg),
        grid=(nb // tg, n_heads),
        in_specs=[blk(0), blk(1), blk(2), blk(3),
                  pl.BlockSpec((hgrn_lb.shape[0], HEAD), lambda g, h: (0, h)),
                  pl.BlockSpec((1, HEAD), lambda g, h: (0, h)),
                  st_spec],
        out_specs=[pl.BlockSpec((tg, HEAD), lambda g, h: (g, h)), st_spec],
        out_shape=[jax.ShapeDtypeStruct((nb, d), BF16),
                   jax.ShapeDtypeStruct(state.shape, F32)],
        scratch_shapes=[pltpu.VMEM((tg, HEAD), F32)],
        compiler_params=_params(2),
        name="hgrn_sample",
    )(p, p, p, p, hgrn_lb, g_norm.reshape(1, d), state)


_CARRY = 8


def _conv_prompt_body(tb, b_ref, c_ref, h_ref, w_ref, z_ref, buf_ref, u_scr):
    t = pl.program_id(1)

    @pl.when(t == 0)
    def _():
        u_scr[0:_CARRY, :] = jnp.zeros((_CARRY, u_scr.shape[1]), F32)

    u = c_ref[...] * h_ref[...]
    u_scr[_CARRY:_CARRY + tb, :] = u
    w = w_ref[...]
    y = w[0:1, :] * u_scr[_CARRY - 2:_CARRY - 2 + tb, :]
    y = y + w[1:2, :] * u_scr[_CARRY - 1:_CARRY - 1 + tb, :]
    y = y + w[2:3, :] * u
    z_ref[...] = (b_ref[...] * y).astype(BF16)
    u_scr[0:_CARRY, :] = u[tb - _CARRY:tb, :]

    @pl.when(t == pl.num_programs(1) - 1)
    def _():
        buf_ref[0] = u[tb - (CONV_W - 1):tb, :]


def _conv_prompt(pc, conv_w, batch, seq, tb):
    d = conv_w.shape[1]
    n_t = seq // tb
    blk = lambda kk: pl.BlockSpec((tb, d), lambda b, t, kk=kk: (b * n_t + t, kk))
    return pl.pallas_call(
        functools.partial(_conv_prompt_body, tb),
        grid=(batch, n_t),
        in_specs=[blk(0), blk(1), blk(2), pl.BlockSpec(conv_w.shape, lambda b, t: (0, 0))],
        out_specs=[pl.BlockSpec((tb, d), lambda b, t: (b * n_t + t, 0)),
                   pl.BlockSpec((1, CONV_W - 1, d), lambda b, t: (b, 0, 0))],
        out_shape=[jax.ShapeDtypeStruct((batch * seq, d), BF16),
                   jax.ShapeDtypeStruct((batch, CONV_W - 1, d), F32)],
        scratch_shapes=[pltpu.VMEM((_CARRY + tb, d), F32)],
        compiler_params=_params(2),
        name="conv_prompt",
    )(pc, pc, pc, conv_w)


def _conv_sample_body(d, b_ref, c_ref, h_ref, w_ref, st_ref, z_ref, buf_ref):
    u = c_ref[...] * h_ref[...]
    w = w_ref[...]
    buf0 = st_ref[:, 0:d]
    buf1 = st_ref[:, d:2 * d]
    y = w[0:1, :] * buf0
    y = y + w[1:2, :] * buf1
    y = y + w[2:3, :] * u
    z_ref[...] = (b_ref[...] * y).astype(BF16)
    buf_ref[:, 0:d] = buf1
    buf_ref[:, d:2 * d] = u


def _conv_sample(pc, conv_w, state2d, row0):
    nb = state2d.shape[0]
    d = conv_w.shape[1]
    rb0 = row0 // nb
    blk = lambda kk: pl.BlockSpec((nb, d), lambda i, kk=kk: (rb0, kk))
    return pl.pallas_call(
        functools.partial(_conv_sample_body, d),
        grid=(1,),
        in_specs=[blk(0), blk(1), blk(2), pl.BlockSpec(conv_w.shape, lambda i: (0, 0)),
                  pl.BlockSpec(state2d.shape, lambda i: (0, 0))],
        out_specs=[pl.BlockSpec((nb, d), lambda i: (0, 0)),
                   pl.BlockSpec(state2d.shape, lambda i: (0, 0))],
        out_shape=[jax.ShapeDtypeStruct((nb, d), BF16),
                   jax.ShapeDtypeStruct(state2d.shape, F32)],
        compiler_params=_params(1),
        name="conv_sample",
    )(pc, pc, pc, conv_w, state2d)


def _outproj_body(x_ref, z_ref, w_ref, o_ref, w16_ref):
    @pl.when(pl.program_id(0) == 0)
    def _():
        w16_ref[...] = w_ref[...].astype(BF16)

    o_ref[...] = x_ref[...] + jnp.dot(z_ref[...], w16_ref[...], preferred_element_type=F32)


def _outproj(x, z, w, tm):
    n, d = x.shape
    return pl.pallas_call(
        _outproj_body,
        grid=(n // tm,),
        in_specs=[pl.BlockSpec((tm, d), lambda i: (i, 0)),
                  pl.BlockSpec((tm, d), lambda i: (i, 0)),
                  pl.BlockSpec((d, d), lambda i: (0, 0))],
        out_specs=pl.BlockSpec((tm, d), lambda i: (i, 0)),
        out_shape=jax.ShapeDtypeStruct((n, d), F32),
        scratch_shapes=[pltpu.VMEM((d, d), BF16)],
        compiler_params=_params(1),
        name="outproj",
    )(x, z, w)


def _swiglu_partial(xn, w1, w3, w2):
    h1 = jnp.dot(xn, w1.astype(BF16), preferred_element_type=F32)
    h3 = jnp.dot(xn, w3.astype(BF16), preferred_element_type=F32)
    h = (jax.nn.silu(h1) * h3).astype(BF16)
    return jnp.dot(h, w2.astype(BF16), preferred_element_type=F32)


def _ffn_body(x_ref, g_ref, w1_ref, w3_ref, w2_ref, o_ref, xn_ref, acc_ref):
    f = pl.program_id(1)

    @pl.when(f == 0)
    def _():
        xn_ref[...] = _rmsnorm_bf16(x_ref[...], g_ref[...])
        acc_ref[...] = jnp.zeros_like(acc_ref)

    acc_ref[...] += _swiglu_partial(xn_ref[...], w1_ref[...], w3_ref[...], w2_ref[...])

    @pl.when(f == pl.num_programs(1) - 1)
    def _():
        o_ref[...] = x_ref[...] + acc_ref[...]


def _ffn(x, g, w1, w3, w2, tm, tf):
    n, d = x.shape
    dff = w1.shape[1]
    return pl.pallas_call(
        _ffn_body,
        grid=(n // tm, dff // tf),
        in_specs=[pl.BlockSpec((tm, d), lambda i, f: (i, 0)),
                  pl.BlockSpec((1, d), lambda i, f: (0, 0)),
                  pl.BlockSpec((d, tf), lambda i, f: (0, f)),
                  pl.BlockSpec((d, tf), lambda i, f: (0, f)),
                  pl.BlockSpec((tf, d), lambda i, f: (f, 0))],
        out_specs=pl.BlockSpec((tm, d), lambda i, f: (i, 0)),
        out_shape=jax.ShapeDtypeStruct((n, d), F32),
        scratch_shapes=[pltpu.VMEM((tm, d), BF16), pltpu.VMEM((tm, d), F32)],
        compiler_params=_params(2),
        name="ffn",
    )(x, g.reshape(1, d), w1, w3, w2)


def _router_body(n_experts, x_ref, g_ref, rw_ref, route_ref, cnt_ref, cnt_scr):
    i = pl.program_id(0)
    tm = x_ref.shape[0]

    @pl.when(i == 0)
    def _():
        cnt_scr[...] = jnp.zeros_like(cnt_scr)

    xn = _rmsnorm_bf16(x_ref[...], g_ref[...])
    logits = jnp.dot(xn, rw_ref[...].astype(BF16), preferred_element_type=F32)
    lane = lax.broadcasted_iota(I32, (tm, LANES), 1).astype(F32)
    neg = jnp.float32(-jnp.inf)
    logits = jnp.where(lane < n_experts, logits, neg)
    m1 = jnp.max(logits, axis=-1, keepdims=True)
    i1 = jnp.min(jnp.where(logits == m1, lane, float(LANES)), axis=-1, keepdims=True)
    i1 = jnp.minimum(i1, float(n_experts - 1))
    rest = jnp.where(lane == i1, neg, logits)
    m2 = jnp.max(rest, axis=-1, keepdims=True)
    i2 = jnp.min(jnp.where(rest == m2, lane, float(LANES)), axis=-1, keepdims=True)
    i2 = jnp.minimum(i2, float(n_experts - 1))
    e2 = jnp.exp(m2 - m1)
    den = 1.0 + e2
    g1 = 1.0 / den
    g2 = e2 / den

    sel1 = lane == i1
    sel2 = lane == i2
    onehot = jnp.logical_or(sel1, sel2)
    row = lax.broadcasted_iota(I32, (tm, tm), 0)
    col = lax.broadcasted_iota(I32, (tm, tm), 1)
    before = (row > col).astype(BF16)
    rank = jnp.dot(before, onehot.astype(BF16), preferred_element_type=F32) + cnt_scr[...]
    r1 = jnp.sum(jnp.where(sel1, rank, 0.0), axis=-1, keepdims=True)
    r2 = jnp.sum(jnp.where(sel2, rank, 0.0), axis=-1, keepdims=True)
    cnt_scr[...] += jnp.sum(onehot.astype(F32), axis=0, keepdims=True)

    out = jnp.zeros((tm, LANES), F32)
    for slot, val in enumerate((i1, i2, g1, g2, r1, r2)):
        out = jnp.where(lane == slot, val, out)
    route_ref[...] = out
    cnt_ref[...] = cnt_scr[...]


def _router(x, g, rw_pad, n_experts, tm):
    n, d = x.shape
    return pl.pallas_call(
        functools.partial(_router_body, n_experts),
        grid=(n // tm,),
        in_specs=[pl.BlockSpec((tm, d), lambda i: (i, 0)),
                  pl.BlockSpec((1, d), lambda i: (0, 0)),
                  pl.BlockSpec((d, LANES), lambda i: (0, 0))],
        out_specs=[pl.BlockSpec((tm, LANES), lambda i: (i, 0)),
                   pl.BlockSpec((1, LANES), lambda i: (0, 0))],
        out_shape=[jax.ShapeDtypeStruct((n, LANES), F32),
                   jax.ShapeDtypeStruct((1, LANES), F32)],
        scratch_shapes=[pltpu.VMEM((1, LANES), F32)],
        compiler_params=_params(1),
        name="router",
    )(x, g.reshape(1, d), rw_pad)


def _start_row_gather(src_hbm, idx_ref, n_rows, dst, sem):
    def body(r, carry):
        pltpu.make_async_copy(src_hbm.at[pl.ds(idx_ref[0, 0, r], 1)], dst.at[pl.ds(r, 1)], sem).start()
        return carry

    lax.fori_loop(0, n_rows, body, 0, unroll=8)


def _wait_row_gather(src_hbm, n_rows, dst, sem):
    pltpu.make_async_copy(src_hbm.at[pl.ds(0, n_rows)], dst, sem).wait()


def _moe_body(n_f, tf, te_ref, nu_ref, idx_cur, idx_nxt, x_hbm, gate_ref, g_ref, w1_hbm, w3_hbm, w2_hbm,
              y_ref, xbuf, sem, xn_ref, acc_ref, c1, c3, c2, s1, s3, s2, wsem):
    i = pl.program_id(0)
    f = pl.program_id(1)
    n_tiles = pl.num_programs(0)
    tm = xbuf.shape[1]
    slot = lax.rem(i, 2)
    used = i < nu_ref[0]
    e = te_ref[i]
    first_of_expert = jnp.logical_or(i == 0, e != te_ref[jnp.maximum(i - 1, 0)])

    @pl.when(f == 0)
    def _():
        @pl.when(i == 0)
        def _():
            _start_row_gather(x_hbm, idx_cur, tm, xbuf.at[0], sem.at[0])

        @pl.when(i + 1 < n_tiles)
        def _():
            _start_row_gather(x_hbm, idx_nxt, tm, xbuf.at[1 - slot], sem.at[1 - slot])

        _wait_row_gather(x_hbm, tm, xbuf.at[slot], sem.at[slot])
        xn_ref[...] = _rmsnorm_bf16(xbuf[slot], g_ref[...])
        acc_ref[...] = jnp.zeros_like(acc_ref)

    def chunk_copies(ff, ws):
        cols = pl.ds(pl.multiple_of(ff * tf, tf), tf)
        return (pltpu.make_async_copy(w1_hbm.at[e, :, cols], s1.at[ws], wsem.at[ws, 0]),
                pltpu.make_async_copy(w3_hbm.at[e, :, cols], s3.at[ws], wsem.at[ws, 1]),
                pltpu.make_async_copy(w2_hbm.at[e, cols, :], s2.at[ws], wsem.at[ws, 2]))

    @pl.when(jnp.logical_and(used, first_of_expert))
    def _():
        ws = lax.rem(f, 2)

        @pl.when(f == 0)
        def _():
            for c in chunk_copies(0, 0):
                c.start()

        @pl.when(f + 1 < n_f)
        def _():
            for c in chunk_copies(f + 1, 1 - ws):
                c.start()

        for c in chunk_copies(f, ws):
            c.wait()
        c1[f] = s1[ws].astype(BF16)
        c3[f] = s3[ws].astype(BF16)
        c2[f] = s2[ws].astype(BF16)

    @pl.when(used)
    def _():
        acc_ref[...] += _swiglu_partial(xn_ref[...], c1[f], c3[f], c2[f])

    @pl.when(f == n_f - 1)
    def _():
        y_ref[...] = acc_ref[...] * gate_ref[...]


def _moe(x, g, w1, w3, w2, tile_expert, n_used, inv3, gate_sorted, tm, tf):
    n_tiles = inv3.shape[0]
    d = x.shape[1]
    dff = w1.shape[2]
    n_f = dff // tf
    grid_spec = pltpu.PrefetchScalarGridSpec(
        num_scalar_prefetch=2,
        grid=(n_tiles, n_f),
        in_specs=[pl.BlockSpec((1, 1, tm), lambda i, f, te, nu: (i, 0, 0), memory_space=pltpu.SMEM),
                  pl.BlockSpec((1, 1, tm), lambda i, f, te, nu: (jnp.minimum(i + 1, n_tiles - 1), 0, 0),
                               memory_space=pltpu.SMEM),
                  pl.BlockSpec(memory_space=pl.ANY),
                  pl.BlockSpec((tm, 1), lambda i, f, te, nu: (i, 0)),
                  pl.BlockSpec((1, d), lambda i, f, te, nu: (0, 0)),
                  pl.BlockSpec(memory_space=pl.ANY),
                  pl.BlockSpec(memory_space=pl.ANY),
                  pl.BlockSpec(memory_space=pl.ANY)],
        out_specs=pl.BlockSpec((tm, d), lambda i, f, te, nu: (i, 0)),
        scratch_shapes=[pltpu.VMEM((2, tm, d), F32),
                        pltpu.SemaphoreType.DMA((2,)),
                        pltpu.VMEM((tm, d), BF16),
                        pltpu.VMEM((tm, d), F32),
                        pltpu.VMEM((n_f, d, tf), BF16),
                        pltpu.VMEM((n_f, d, tf), BF16),
                        pltpu.VMEM((n_f, tf, d), BF16),
                        pltpu.VMEM((2, d, tf), F32),
                        pltpu.VMEM((2, d, tf), F32),
                        pltpu.VMEM((2, tf, d), F32),
                        pltpu.SemaphoreType.DMA((2, 3))],
    )
    return pl.pallas_call(
        functools.partial(_moe_body, n_f, tf),
        grid_spec=grid_spec,
        out_shape=jax.ShapeDtypeStruct((n_tiles * tm, d), F32),
        compiler_params=_params(2),
        name="moe",
    )(tile_expert, n_used, inv3, inv3, x, gate_sorted, g.reshape(1, d), w1, w3, w2)


def _combine_body(pos_cur, pos_nxt, x_ref, g_ref, y_hbm, o_ref, ybuf, sem):
    i = pl.program_id(0)
    n_tiles = pl.num_programs(0)
    rows = ybuf.shape[1]
    slot = lax.rem(i, 2)

    @pl.when(i == 0)
    def _():
        _start_row_gather(y_hbm, pos_cur, rows, ybuf.at[0], sem.at[0])

    @pl.when(i + 1 < n_tiles)
    def _():
        _start_row_gather(y_hbm, pos_nxt, rows, ybuf.at[1 - slot], sem.at[1 - slot])

    _wait_row_gather(y_hbm, rows, ybuf.at[slot], sem.at[slot])
    tc = rows // TOP_K
    moe = ybuf[slot, 0:tc, :] + ybuf[slot, tc:rows, :]
    x = x_ref[...] + moe
    ms = jnp.mean(x * x, axis=-1, keepdims=True)
    o_ref[...] = x * lax.rsqrt(ms + EPS) * g_ref[...]


def _combine(x, g, y_sorted, pos3, row0, n_rows, tc):
    d = x.shape[1]
    n_tiles = n_rows // tc
    rb0 = row0 // tc
    grid_spec = pltpu.PrefetchScalarGridSpec(
        num_scalar_prefetch=0,
        grid=(n_tiles,),
        in_specs=[pl.BlockSpec((1, 1, TOP_K * tc), lambda i: (i, 0, 0), memory_space=pltpu.SMEM),
                  pl.BlockSpec((1, 1, TOP_K * tc), lambda i: (jnp.minimum(i + 1, n_tiles - 1), 0, 0),
                               memory_space=pltpu.SMEM),
                  pl.BlockSpec((tc, d), lambda i: (rb0 + i, 0)),
                  pl.BlockSpec((1, d), lambda i: (0, 0)),
                  pl.BlockSpec(memory_space=pl.ANY)],
        out_specs=pl.BlockSpec((tc, d), lambda i: (i, 0)),
        scratch_shapes=[pltpu.VMEM((2, TOP_K * tc, d), F32), pltpu.SemaphoreType.DMA((2,))],
    )
    return pl.pallas_call(
        _combine_body,
        grid_spec=grid_spec,
        out_shape=jax.ShapeDtypeStruct((n_rows, d), F32),
        compiler_params=_params(1),
        name="combine",
    )(pos3, pos3, x, g.reshape(1, d), y_sorted)


def _tile_positions(pos, row0, n_rows, tc):
    p = pos[row0:row0 + n_rows].reshape(n_rows // tc, tc, TOP_K)
    return jnp.swapaxes(p, 1, 2).reshape(n_rows // tc, 1, TOP_K * tc)


def kernel(x_prompt, x_sample, state_hgrn, state_conv, norm_mix, norm_ffn, norm_final, hgrn_w_in, hgrn_lb, hgrn_g_norm, hgrn_w_out, conv_w_in, conv_w, conv_w_out, ffn_w1, ffn_w3, ffn_w2, moe_router, moe_w1, moe_w3, moe_w2):
    batch, seq, d = x_prompt.shape
    n_dec = x_sample.shape[0]
    n_prompt = batch * seq
    n = n_prompt + n_dec
    n_experts = moe_router.shape[-1]
    assert x_sample.shape[1] == 1 and d % HEAD == 0 and seq % CHUNK == 0
    assert norm_mix.shape[0] == 2, "one HGRN2 layer followed by one short-conv layer"

    tm = _pick_tile(n, (688, 384, 128, 16))
    tb = _pick_tile(seq, (256, 128, 64, 32))
    tg = _pick_tile(n_dec, (16,))
    tn = _pick_tile(d, (1024, 512, 256, 128))
    tf = _pick_tile(ffn_w1.shape[-1], (512, 256, 128))
    tm_moe = _pick_tile(n_prompt, (512, 128, 16))
    tc = _pick_tile(n_prompt, (512, 128, 16))
    assert n_prompt % tg == 0 and n_prompt % n_dec == 0 and n_prompt % tc == 0

    x = jnp.concatenate([x_prompt.reshape(n_prompt, d), x_sample.reshape(n_dec, d)], axis=0)

    p = _proj(x, norm_mix[0], hgrn_w_in[0], tm, tn)
    o_p, s_prompt = _hgrn_prompt(p, hgrn_lb, hgrn_g_norm[0], 0, batch, seq, tb)
    o_s, s_sample = _hgrn_sample(p, hgrn_lb, hgrn_g_norm[0], state_hgrn, 0, n_prompt, tg)
    x = _outproj(x, jnp.concatenate([o_p, o_s], axis=0), hgrn_w_out[0], tm)
    x = _ffn(x, norm_ffn[0], ffn_w1[0], ffn_w3[0], ffn_w2[0], tm, tf)

    pc = _proj(x, norm_mix[1], conv_w_in[0], tm, tn)
    z_p, c_prompt = _conv_prompt(pc, conv_w[0], batch, seq, tb)
    z_s, c_sample = _conv_sample(pc, conv_w[0], state_conv.reshape(n_dec, (CONV_W - 1) * d), n_prompt)
    x = _outproj(x, jnp.concatenate([z_p, z_s], axis=0), conv_w_out[0], tm)

    rw_pad = jnp.pad(moe_router[0], ((0, 0), (0, LANES - n_experts)))
    route, cnt = _router(x, norm_ffn[1], rw_pad, n_experts, tm)
    ids = route[:, 0:TOP_K].astype(I32)
    gates = route[:, TOP_K:2 * TOP_K]
    ranks = route[:, 2 * TOP_K:3 * TOP_K].astype(I32)
    counts = cnt[0, :n_experts].astype(I32)

    n_tiles = (TOP_K * n + n_experts * (tm_moe - 1)) // tm_moe
    padded = ((counts + tm_moe - 1) // tm_moe) * tm_moe
    ends = jnp.cumsum(padded)
    starts = ends - padded
    cstart = jnp.cumsum(counts) - counts
    pos = starts[ids] + ranks
    order = jnp.argsort(ids.reshape(-1), stable=True).astype(I32)
    slot_pos = jnp.arange(n_tiles * tm_moe, dtype=I32)
    slot_e = jnp.minimum(jnp.searchsorted(ends, slot_pos, side="right"), n_experts - 1).astype(I32)
    slot_rank = slot_pos - starts[slot_e]
    slot_valid = slot_rank < counts[slot_e]
    slot_asg = order[jnp.clip(cstart[slot_e] + slot_rank, 0, TOP_K * n - 1)]
    inv = jnp.where(slot_valid, slot_asg // TOP_K, 0).astype(I32)
    gate_sorted = jnp.where(slot_valid, gates.reshape(-1)[slot_asg], 0.0).reshape(-1, 1)
    tile_expert = slot_e[::tm_moe]
    n_used = (ends[-1] // tm_moe).astype(I32).reshape(1)

    y_sorted = _moe(x, norm_ffn[1], moe_w1[0], moe_w3[0], moe_w2[0], tile_expert, n_used,
                    inv.reshape(n_tiles, 1, tm_moe), gate_sorted, tm_moe, tf)

    y_prompt = _combine(x, norm_final, y_sorted, _tile_positions(pos, 0, n_prompt, tc), 0, n_prompt, tc)
    y_sample = _combine(x, norm_final, y_sorted, _tile_positions(pos, n_prompt, n_dec, n_dec),
                        n_prompt, n_dec, n_dec)

    return (y_prompt.reshape(batch, seq, d),
            y_sample.reshape(n_dec, 1, d),
            s_prompt,
            s_sample,
            c_prompt.reshape(batch, 1, CONV_W - 1, d),
            c_sample.reshape(n_dec, 1, CONV_W - 1, d))
```

```python
import functools

import jax
import jax.numpy as jnp
from jax import lax
from jax.experimental import pallas as pl
from jax.experimental.pallas import tpu as pltpu

F32 = jnp.float32
BF16 = jnp.bfloat16
I32 = jnp.int32

EPS = 1e-6
HEAD = 128
CHUNK = 32
CONV_W = 3
TOP_K = 2
LANES = 128
V7X_VMEM_LIMIT = 56 * 1024 * 1024

ARB = "arbitrary"


def _params(n_axes):
    return pltpu.CompilerParams(dimension_semantics=(ARB,) * n_axes, vmem_limit_bytes=V7X_VMEM_LIMIT)


def _pick_tile(n, candidates):
    for c in candidates:
        if n % c == 0:
            return c
    raise ValueError(f"no tile in {candidates} divides {n}")


def _rmsnorm_bf16(x, g):
    ms = jnp.mean(x * x, axis=-1, keepdims=True)
    return (x * lax.rsqrt(ms + EPS) * g).astype(BF16)


def _proj_body(f32_col, x_ref, g_ref, w_ref, *refs):
    o16_ref, xn_ref = refs[0], refs[-1]
    j = pl.program_id(1)

    @pl.when(j == 0)
    def _():
        xn_ref[...] = _rmsnorm_bf16(x_ref[...], g_ref[...])

    def result():
        return jnp.dot(xn_ref[...], w_ref[...].astype(BF16), preferred_element_type=F32)

    if f32_col is None:
        o16_ref[...] = result().astype(BF16)
    else:
        o32_ref = refs[1]

        @pl.when(j == f32_col)
        def _():
            o32_ref[...] = result()

        @pl.when(j != f32_col)
        def _():
            o16_ref[...] = result().astype(BF16)


def _proj(x, g, w, tm, f32_col=None):
    n, d = x.shape
    n_col = w.shape[1] // d
    if f32_col is None:
        col16 = lambda i, j: (i, j)
        n16 = n_col
    else:
        assert 0 < f32_col < n_col
        col16 = lambda i, j: (i, jnp.where(j >= f32_col, j - 1, j))
        n16 = n_col - 1
    out_specs = [pl.BlockSpec((tm, d), col16)]
    out_shape = [jax.ShapeDtypeStruct((n, n16 * d), BF16)]
    if f32_col is not None:
        out_specs.append(pl.BlockSpec((tm, d), lambda i, j: (i, 0)))
        out_shape.append(jax.ShapeDtypeStruct((n, d), F32))
    return pl.pallas_call(
        functools.partial(_proj_body, f32_col),
        grid=(n // tm, n_col),
        in_specs=[pl.BlockSpec((tm, d), lambda i, j: (i, 0)),
                  pl.BlockSpec((1, d), lambda i, j: (0, 0)),
                  pl.BlockSpec((d, d), lambda i, j: (0, j))],
        out_specs=out_specs,
        out_shape=out_shape,
        scratch_shapes=[pltpu.VMEM((tm, d), BF16)],
        compiler_params=_params(2),
        name="proj",
    )(x, g.reshape(1, d), w)


def _forget_lower_bound(lb_ref, layer):
    lbw = lb_ref[...]
    e = jnp.exp(lbw - jnp.max(lbw, axis=0, keepdims=True))
    sm = e / jnp.sum(e, axis=0, keepdims=True)
    return jnp.sum(sm[:layer + 1], axis=0, keepdims=True)


def _head_rmsnorm(o):
    return o * lax.rsqrt(jnp.mean(o * o, axis=-1, keepdims=True) + EPS)


def _sigmoid(x):
    return 0.5 * (jnp.tanh(0.5 * x) + 1.0)


def _silu(x):
    return x * _sigmoid(x)


def _split3_bf16(x):
    hi = x.astype(BF16)
    r1 = x - hi.astype(F32)
    mid = r1.astype(BF16)
    lo = (r1 - mid.astype(F32)).astype(BF16)
    return hi, mid, lo


def _hgrn_prompt_body(layer, n_heads, tb, q_ref, f_ref, v_ref, gt_ref, lb_ref, gn_ref,
                      o_ref, s_ref, st_scr, qin_scr, kin_scr, kdec_scr, g_scr, sprev_scr):
    t = pl.program_id(1)
    nc = tb // CHUNK
    d = n_heads * HEAD

    @pl.when(t == 0)
    def _():
        st_scr[...] = jnp.zeros_like(st_scr)

    lb = _forget_lower_bound(lb_ref, layer)
    gn = gn_ref[...]
    row = lax.broadcasted_iota(I32, (tb, tb), 0)
    col = lax.broadcasted_iota(I32, (tb, tb), 1)
    same_chunk = (row // CHUNK) == (col // CHUNK)
    causal = jnp.logical_and(same_chunk, row >= col)

    f = lb + (1.0 - lb) * _sigmoid(f_ref[...])
    logf = jnp.log(f)
    k = 1.0 - f
    tri = causal.astype(BF16)
    G = None
    for part in _split3_bf16(logf):
        term = jnp.dot(tri, part, preferred_element_type=F32)
        G = term if G is None else G + term
    g_scr[...] = G
    g_last = [g_scr[(c + 1) * CHUNK - 1:(c + 1) * CHUNK, :] for c in range(nc)]
    decay = [jnp.exp(g) for g in g_last]
    decay_rows = jnp.concatenate([jnp.broadcast_to(dc, (CHUNK, d)) for dc in decay], axis=0)
    k_in = k * jnp.exp(-G)
    qin_scr[...] = (_silu(q_ref[...].astype(F32)) * jnp.exp(G)).astype(BF16)
    kin_scr[...] = k_in.astype(BF16)
    kdec_scr[...] = (k_in * decay_rows).astype(BF16)

    lane_chunk = lax.broadcasted_iota(I32, (HEAD, tb), 1) // CHUNK
    row_chunk = lax.broadcasted_iota(I32, (tb, HEAD), 0) // CHUNK
    zero16 = jnp.zeros((), BF16)
    grp = 2 if nc % 2 == 0 else 1

    for h in range(n_heads):
        hs = slice(h * HEAD, (h + 1) * HEAD)
        qh = qin_scr[:, hs]
        kd = kdec_scr[:, hs]
        vh = v_ref[:, hs]
        a = lax.dot_general(qh, kin_scr[:, hs], (((1,), (1,)), ((), ())), preferred_element_type=F32)
        a = jnp.where(causal, a, 0.0).astype(BF16)
        o = jnp.dot(a, vh, preferred_element_type=F32)
        v_t = vh.T
        v_blocks = jnp.concatenate([jnp.where(lane_chunk == c, v_t, zero16) for c in range(nc)], axis=0)
        ut = jnp.dot(v_blocks, kd, preferred_element_type=F32)
        st = st_scr[h]
        for c in range(nc):
            sprev_scr[h, :, c * HEAD:(c + 1) * HEAD] = st.astype(BF16)
            st = st * decay[c][:, hs] + ut[c * HEAD:(c + 1) * HEAD, :]
        st_scr[h] = st
        inter = []
        for g0 in range(0, nc, grp):
            rows = slice(g0 * CHUNK, (g0 + grp) * CHUNK)
            q_blocks = jnp.concatenate(
                [jnp.where(row_chunk[rows] == g0 + j, qh[rows], zero16) for j in range(grp)], axis=1)
            inter.append(lax.dot_general(q_blocks, sprev_scr[h, :, g0 * HEAD:(g0 + grp) * HEAD],
                                         (((1,), (1,)), ((), ())), preferred_element_type=F32))
        o = o + jnp.concatenate(inter, axis=0)
        o = _head_rmsnorm(o) * gn[:, hs] * _silu(gt_ref[:, hs].astype(F32))
        o_ref[:, hs] = o.astype(BF16)

    @pl.when(t == pl.num_programs(1) - 1)
    def _():
        for h in range(n_heads):
            s_ref[0, 0, h] = st_scr[h].T


def _hgrn_prompt(p16, p32, hgrn_lb, g_norm, layer, batch, seq, tb):
    d = g_norm.shape[0]
    n_heads = d // HEAD
    n_t = seq // tb
    blk = lambda kk: pl.BlockSpec((tb, d), lambda b, t, kk=kk: (b * n_t + t, kk))
    return pl.pallas_call(
        functools.partial(_hgrn_prompt_body, layer, n_heads, tb),
        grid=(batch, n_t),
        in_specs=[blk(0), blk(0), blk(1), blk(2),
                  pl.BlockSpec(hgrn_lb.shape, lambda b, t: (0, 0)),
                  pl.BlockSpec((1, d), lambda b, t: (0, 0))],
        out_specs=[pl.BlockSpec((tb, d), lambda b, t: (b * n_t + t, 0)),
                   pl.BlockSpec((1, 1, n_heads, HEAD, HEAD), lambda b, t: (b, 0, 0, 0, 0))],
        out_shape=[jax.ShapeDtypeStruct((batch * seq, d), BF16),
                   jax.ShapeDtypeStruct((batch, 1, n_heads, HEAD, HEAD), F32)],
        scratch_shapes=[pltpu.VMEM((n_heads, HEAD, HEAD), F32),
                        pltpu.VMEM((tb, d), BF16), pltpu.VMEM((tb, d), BF16), pltpu.VMEM((tb, d), BF16),
                        pltpu.VMEM((tb, d), F32),
                        pltpu.VMEM((n_heads, HEAD, (tb // CHUNK) * HEAD), BF16)],
        compiler_params=_params(2),
        name="hgrn_prompt",
    )(p16, p32, p16, p16, hgrn_lb, g_norm.reshape(1, d))


def _hgrn_sample_body(layer, tg, q_ref, f_ref, v_ref, gt_ref, lb_ref, gn_ref, s0_ref,
                      o_ref, s_ref, o_scr):
    lb_h = _forget_lower_bound(lb_ref, layer)
    gn_h = gn_ref[...]

    q = jax.nn.silu(q_ref[...].astype(F32))
    f = lb_h + (1.0 - lb_h) * jax.nn.sigmoid(f_ref[...])
    decay = f
    k = 1.0 - f
    v = v_ref[...].astype(F32)
    q_t, d_t, k_t = q.T, decay.T, k.T
    for j in range(tg):
        s = d_t[:, j:j + 1] * s0_ref[j, 0, 0] + k_t[:, j:j + 1] * v[j:j + 1, :]
        s_ref[j, 0, 0] = s
        o_scr[j:j + 1, :] = jnp.sum(q_t[:, j:j + 1] * s, axis=0, keepdims=True)
    o = _head_rmsnorm(o_scr[...]) * gn_h * jax.nn.silu(gt_ref[...].astype(F32))
    o_ref[...] = o.astype(BF16)


def _hgrn_sample(p16, p32, hgrn_lb, g_norm, state, layer, row0, tg):
    nb = state.shape[0]
    d = g_norm.shape[0]
    n_heads = d // HEAD
    rb0 = row0 // tg
    blk = lambda kk: pl.BlockSpec((tg, HEAD), lambda g, h, kk=kk: (rb0 + g, kk * n_heads + h))
    st_spec = pl.BlockSpec((tg, 1, 1, HEAD, HEAD), lambda g, h: (g, 0, h, 0, 0))
    return pl.pallas_call(
        functools.partial(_hgrn_sample_body, layer, tg),
        grid=(nb // tg, n_heads),
        in_specs=[blk(0), blk(0), blk(1), blk(2),
                  pl.BlockSpec((hgrn_lb.shape[0], HEAD), lambda g, h: (0, h)),
                  pl.BlockSpec((1, HEAD), lambda g, h: (0, h)),
                  st_spec],
        out_specs=[pl.BlockSpec((tg, HEAD), lambda g, h: (g, h)), st_spec],
        out_shape=[jax.ShapeDtypeStruct((nb, d), BF16),
                   jax.ShapeDtypeStruct(state.shape, F32)],
        scratch_shapes=[pltpu.VMEM((tg, HEAD), F32)],
        compiler_params=_params(2),
        name="hgrn_sample",
    )(p16, p32, p16, p16, hgrn_lb, g_norm.reshape(1, d), state)


_CARRY = 8


def _conv_prompt_body(tb, b_ref, c_ref, h_ref, w_ref, z_ref, buf_ref, u_scr):
    t = pl.program_id(1)

    @pl.when(t == 0)
    def _():
        u_scr[0:_CARRY, :] = jnp.zeros((_CARRY, u_scr.shape[1]), F32)

    u = c_ref[...].astype(F32) * h_ref[...].astype(F32)
    u_scr[_CARRY:_CARRY + tb, :] = u
    w = w_ref[...]
    y = w[0:1, :] * u_scr[_CARRY - 2:_CARRY - 2 + tb, :]
    y = y + w[1:2, :] * u_scr[_CARRY - 1:_CARRY - 1 + tb, :]
    y = y + w[2:3, :] * u
    z_ref[...] = (b_ref[...].astype(F32) * y).astype(BF16)
    u_scr[0:_CARRY, :] = u[tb - _CARRY:tb, :]

    @pl.when(t == pl.num_programs(1) - 1)
    def _():
        buf_ref[0] = u[tb - (CONV_W - 1):tb, :]


def _conv_prompt(pc, conv_w, batch, seq, tb):
    d = conv_w.shape[1]
    n_t = seq // tb
    blk = lambda kk: pl.BlockSpec((tb, d), lambda b, t, kk=kk: (b * n_t + t, kk))
    return pl.pallas_call(
        functools.partial(_conv_prompt_body, tb),
        grid=(batch, n_t),
        in_specs=[blk(0), blk(1), blk(2), pl.BlockSpec(conv_w.shape, lambda b, t: (0, 0))],
        out_specs=[pl.BlockSpec((tb, d), lambda b, t: (b * n_t + t, 0)),
                   pl.BlockSpec((1, CONV_W - 1, d), lambda b, t: (b, 0, 0))],
        out_shape=[jax.ShapeDtypeStruct((batch * seq, d), BF16),
                   jax.ShapeDtypeStruct((batch, CONV_W - 1, d), F32)],
        scratch_shapes=[pltpu.VMEM((_CARRY + tb, d), F32)],
        compiler_params=_params(2),
        name="conv_prompt",
    )(pc, pc, pc, conv_w)


def _conv_sample_body(d, b_ref, c_ref, h_ref, w_ref, st_ref, z_ref, buf_ref):
    u = c_ref[...].astype(F32) * h_ref[...].astype(F32)
    w = w_ref[...]
    buf0 = st_ref[:, 0:d]
    buf1 = st_ref[:, d:2 * d]
    y = w[0:1, :] * buf0
    y = y + w[1:2, :] * buf1
    y = y + w[2:3, :] * u
    z_ref[...] = (b_ref[...].astype(F32) * y).astype(BF16)
    buf_ref[:, 0:d] = buf1
    buf_ref[:, d:2 * d] = u


def _conv_sample(pc, conv_w, state2d, row0):
    nb = state2d.shape[0]
    d = conv_w.shape[1]
    rb0 = row0 // nb
    blk = lambda kk: pl.BlockSpec((nb, d), lambda i, kk=kk: (rb0, kk))
    return pl.pallas_call(
        functools.partial(_conv_sample_body, d),
        grid=(1,),
        in_specs=[blk(0), blk(1), blk(2), pl.BlockSpec(conv_w.shape, lambda i: (0, 0)),
                  pl.BlockSpec(state2d.shape, lambda i: (0, 0))],
        out_specs=[pl.BlockSpec((nb, d), lambda i: (0, 0)),
                   pl.BlockSpec(state2d.shape, lambda i: (0, 0))],
        out_shape=[jax.ShapeDtypeStruct((nb, d), BF16),
                   jax.ShapeDtypeStruct(state2d.shape, F32)],
        compiler_params=_params(1),
        name="conv_sample",
    )(pc, pc, pc, conv_w, state2d)


def _outproj_body(x_ref, z_ref, w_ref, o_ref, w16_ref):
    @pl.when(pl.program_id(0) == 0)
    def _():
        w16_ref[...] = w_ref[...].astype(BF16)

    o_ref[...] = x_ref[...] + jnp.dot(z_ref[...], w16_ref[...], preferred_element_type=F32)


def _outproj(x, z, w, tm):
    n, d = x.shape
    return pl.pallas_call(
        _outproj_body,
        grid=(n // tm,),
        in_specs=[pl.BlockSpec((tm, d), lambda i: (i, 0)),
                  pl.BlockSpec((tm, d), lambda i: (i, 0)),
                  pl.BlockSpec((d, d), lambda i: (0, 0))],
        out_specs=pl.BlockSpec((tm, d), lambda i: (i, 0)),
        out_shape=jax.ShapeDtypeStruct((n, d), F32),
        scratch_shapes=[pltpu.VMEM((d, d), BF16)],
        compiler_params=_params(1),
        name="outproj",
    )(x, z, w)


def _swiglu_partial(xn, w1, w3, w2):
    h1 = jnp.dot(xn, w1.astype(BF16), preferred_element_type=F32)
    h3 = jnp.dot(xn, w3.astype(BF16), preferred_element_type=F32)
    h = (jax.nn.silu(h1) * h3).astype(BF16)
    return jnp.dot(h, w2.astype(BF16), preferred_element_type=F32)


def _ffn_body(x_ref, g_ref, w1_ref, w3_ref, w2_ref, o_ref, xn_ref, acc_ref):
    f = pl.program_id(1)

    @pl.when(f == 0)
    def _():
        xn_ref[...] = _rmsnorm_bf16(x_ref[...], g_ref[...])
        acc_ref[...] = jnp.zeros_like(acc_ref)

    acc_ref[...] += _swiglu_partial(xn_ref[...], w1_ref[...], w3_ref[...], w2_ref[...])

    @pl.when(f == pl.num_programs(1) - 1)
    def _():
        o_ref[...] = x_ref[...] + acc_ref[...]


def _ffn(x, g, w1, w3, w2, tm, tf):
    n, d = x.shape
    dff = w1.shape[1]
    return pl.pallas_call(
        _ffn_body,
        grid=(n // tm, dff // tf),
        in_specs=[pl.BlockSpec((tm, d), lambda i, f: (i, 0)),
                  pl.BlockSpec((1, d), lambda i, f: (0, 0)),
                  pl.BlockSpec((d, tf), lambda i, f: (0, f)),
                  pl.BlockSpec((d, tf), lambda i, f: (0, f)),
                  pl.BlockSpec((tf, d), lambda i, f: (f, 0))],
        out_specs=pl.BlockSpec((tm, d), lambda i, f: (i, 0)),
        out_shape=jax.ShapeDtypeStruct((n, d), F32),
        scratch_shapes=[pltpu.VMEM((tm, d), BF16), pltpu.VMEM((tm, d), F32)],
        compiler_params=_params(2),
        name="ffn",
    )(x, g.reshape(1, d), w1, w3, w2)


def _router_body(n_experts, x_ref, g_ref, rw_ref, route_ref, cnt_ref, cnt_scr):
    i = pl.program_id(0)
    tm = x_ref.shape[0]

    @pl.when(i == 0)
    def _():
        cnt_scr[...] = jnp.zeros_like(cnt_scr)

    xn = _rmsnorm_bf16(x_ref[...], g_ref[...])
    logits = jnp.dot(xn, rw_ref[...].astype(BF16), preferred_element_type=F32)
    lane = lax.broadcasted_iota(I32, (tm, LANES), 1).astype(F32)
    neg = jnp.float32(-jnp.inf)
    logits = jnp.where(lane < n_experts, logits, neg)
    m1 = jnp.max(logits, axis=-1, keepdims=True)
    i1 = jnp.min(jnp.where(logits == m1, lane, float(LANES)), axis=-1, keepdims=True)
    i1 = jnp.minimum(i1, float(n_experts - 1))
    rest = jnp.where(lane == i1, neg, logits)
    m2 = jnp.max(rest, axis=-1, keepdims=True)
    i2 = jnp.min(jnp.where(rest == m2, lane, float(LANES)), axis=-1, keepdims=True)
    i2 = jnp.minimum(i2, float(n_experts - 1))
    e2 = jnp.exp(m2 - m1)
    den = 1.0 + e2
    g1 = 1.0 / den
    g2 = e2 / den

    sel1 = lane == i1
    sel2 = lane == i2
    onehot = jnp.logical_or(sel1, sel2)
    row = lax.broadcasted_iota(I32, (tm, tm), 0)
    col = lax.broadcasted_iota(I32, (tm, tm), 1)
    before = (row > col).astype(BF16)
    rank = jnp.dot(before, onehot.astype(BF16), preferred_element_type=F32) + cnt_scr[...]
    r1 = jnp.sum(jnp.where(sel1, rank, 0.0), axis=-1, keepdims=True)
    r2 = jnp.sum(jnp.where(sel2, rank, 0.0), axis=-1, keepdims=True)
    cnt_scr[...] += jnp.sum(onehot.astype(F32), axis=0, keepdims=True)

    out = jnp.zeros((tm, LANES), F32)
    for slot, val in enumerate((i1, i2, g1, g2, r1, r2)):
        out = jnp.where(lane == slot, val, out)
    route_ref[...] = out
    cnt_ref[...] = cnt_scr[...]


def _router(x, g, rw_pad, n_experts, tm):
    n, d = x.shape
    return pl.pallas_call(
        functools.partial(_router_body, n_experts),
        grid=(n // tm,),
        in_specs=[pl.BlockSpec((tm, d), lambda i: (i, 0)),
                  pl.BlockSpec((1, d), lambda i: (0, 0)),
                  pl.BlockSpec((d, LANES), lambda i: (0, 0))],
        out_specs=[pl.BlockSpec((tm, LANES), lambda i: (i, 0)),
                   pl.BlockSpec((1, LANES), lambda i: (0, 0))],
        out_shape=[jax.ShapeDtypeStruct((n, LANES), F32),
                   jax.ShapeDtypeStruct((1, LANES), F32)],
        scratch_shapes=[pltpu.VMEM((1, LANES), F32)],
        compiler_params=_params(1),
        name="router",
    )(x, g.reshape(1, d), rw_pad)


def _start_row_gather(src_hbm, idx_ref, n_rows, dst, sem):
    def body(r, carry):
        pltpu.make_async_copy(src_hbm.at[pl.ds(idx_ref[0, 0, r], 1)], dst.at[pl.ds(r, 1)], sem).start()
        return carry

    lax.fori_loop(0, n_rows, body, 0, unroll=8)


def _wait_row_gather(src_hbm, n_rows, dst, sem):
    pltpu.make_async_copy(src_hbm.at[pl.ds(0, n_rows)], dst, sem).wait()


def _moe_body(n_f, tf, te_ref, nu_ref, idx_cur, idx_nxt, x_hbm, gate_ref, g_ref, w1_hbm, w3_hbm, w2_hbm,
              y_ref, xbuf, sem, xn_ref, acc_ref, c1, c3, c2, s1, s3, s2, wsem):
    i = pl.program_id(0)
    f = pl.program_id(1)
    n_tiles = pl.num_programs(0)
    tm = xbuf.shape[1]
    slot = lax.rem(i, 2)
    used = i < nu_ref[0]
    e = te_ref[i]
    first_of_expert = jnp.logical_or(i == 0, e != te_ref[jnp.maximum(i - 1, 0)])

    @pl.when(f == 0)
    def _():
        @pl.when(i == 0)
        def _():
            _start_row_gather(x_hbm, idx_cur, tm, xbuf.at[0], sem.at[0])

        @pl.when(i + 1 < n_tiles)
        def _():
            _start_row_gather(x_hbm, idx_nxt, tm, xbuf.at[1 - slot], sem.at[1 - slot])

        _wait_row_gather(x_hbm, tm, xbuf.at[slot], sem.at[slot])
        xn_ref[...] = _rmsnorm_bf16(xbuf[slot], g_ref[...])
        acc_ref[...] = jnp.zeros_like(acc_ref)

    def chunk_copies(ff, ws):
        cols = pl.ds(pl.multiple_of(ff * tf, tf), tf)
        return (pltpu.make_async_copy(w1_hbm.at[e, :, cols], s1.at[ws], wsem.at[ws, 0]),
                pltpu.make_async_copy(w3_hbm.at[e, :, cols], s3.at[ws], wsem.at[ws, 1]),
                pltpu.make_async_copy(w2_hbm.at[e, cols, :], s2.at[ws], wsem.at[ws, 2]))

    @pl.when(jnp.logical_and(used, first_of_expert))
    def _():
        ws = lax.rem(f, 2)

        @pl.when(f == 0)
        def _():
            for c in chunk_copies(0, 0):
                c.start()

        @pl.when(f + 1 < n_f)
        def _():
            for c in chunk_copies(f + 1, 1 - ws):
                c.start()

        for c in chunk_copies(f, ws):
            c.wait()
        c1[f] = s1[ws].astype(BF16)
        c3[f] = s3[ws].astype(BF16)
        c2[f] = s2[ws].astype(BF16)

    @pl.when(used)
    def _():
        acc_ref[...] += _swiglu_partial(xn_ref[...], c1[f], c3[f], c2[f])

    @pl.when(f == n_f - 1)
    def _():
        y_ref[...] = acc_ref[...] * gate_ref[...]


def _moe(x, g, w1, w3, w2, tile_expert, n_used, inv3, gate_sorted, tm, tf):
    n_tiles = inv3.shape[0]
    d = x.shape[1]
    dff = w1.shape[2]
    n_f = dff // tf
    grid_spec = pltpu.PrefetchScalarGridSpec(
        num_scalar_prefetch=2,
        grid=(n_tiles, n_f),
        in_specs=[pl.BlockSpec((1, 1, tm), lambda i, f, te, nu: (i, 0, 0), memory_space=pltpu.SMEM),
                  pl.BlockSpec((1, 1, tm), lambda i, f, te, nu: (jnp.minimum(i + 1, n_tiles - 1), 0, 0),
                               memory_space=pltpu.SMEM),
                  pl.BlockSpec(memory_space=pl.ANY),
                  pl.BlockSpec((tm, 1), lambda i, f, te, nu: (i, 0)),
                  pl.BlockSpec((1, d), lambda i, f, te, nu: (0, 0)),
                  pl.BlockSpec(memory_space=pl.ANY),
                  pl.BlockSpec(memory_space=pl.ANY),
                  pl.BlockSpec(memory_space=pl.ANY)],
        out_specs=pl.BlockSpec((tm, d), lambda i, f, te, nu: (i, 0)),
        scratch_shapes=[pltpu.VMEM((2, tm, d), F32),
                        pltpu.SemaphoreType.DMA((2,)),
                        pltpu.VMEM((tm, d), BF16),
                        pltpu.VMEM((tm, d), F32),
                        pltpu.VMEM((n_f, d, tf), BF16),
                        pltpu.VMEM((n_f, d, tf), BF16),
                        pltpu.VMEM((n_f, tf, d), BF16),
                        pltpu.VMEM((2, d, tf), F32),
                        pltpu.VMEM((2, d, tf), F32),
                        pltpu.VMEM((2, tf, d), F32),
                        pltpu.SemaphoreType.DMA((2, 3))],
    )
    return pl.pallas_call(
        functools.partial(_moe_body, n_f, tf),
        grid_spec=grid_spec,
        out_shape=jax.ShapeDtypeStruct((n_tiles * tm, d), F32),
        compiler_params=_params(2),
        name="moe",
    )(tile_expert, n_used, inv3, inv3, x, gate_sorted, g.reshape(1, d), w1, w3, w2)


def _combine_body(pos_cur, pos_nxt, x_ref, g_ref, y_hbm, o_ref, ybuf, sem):
    i = pl.program_id(0)
    n_tiles = pl.num_programs(0)
    rows = ybuf.shape[1]
    slot = lax.rem(i, 2)

    @pl.when(i == 0)
    def _():
        _start_row_gather(y_hbm, pos_cur, rows, ybuf.at[0], sem.at[0])

    @pl.when(i + 1 < n_tiles)
    def _():
        _start_row_gather(y_hbm, pos_nxt, rows, ybuf.at[1 - slot], sem.at[1 - slot])

    _wait_row_gather(y_hbm, rows, ybuf.at[slot], sem.at[slot])
    tc = rows // TOP_K
    moe = ybuf[slot, 0:tc, :] + ybuf[slot, tc:rows, :]
    x = x_ref[...] + moe
    ms = jnp.mean(x * x, axis=-1, keepdims=True)
    o_ref[...] = x * lax.rsqrt(ms + EPS) * g_ref[...]


def _combine(x, g, y_sorted, pos3, row0, n_rows, tc):
    d = x.shape[1]
    n_tiles = n_rows // tc
    rb0 = row0 // tc
    grid_spec = pltpu.PrefetchScalarGridSpec(
        num_scalar_prefetch=0,
        grid=(n_tiles,),
        in_specs=[pl.BlockSpec((1, 1, TOP_K * tc), lambda i: (i, 0, 0), memory_space=pltpu.SMEM),
                  pl.BlockSpec((1, 1, TOP_K * tc), lambda i: (jnp.minimum(i + 1, n_tiles - 1), 0, 0),
                               memory_space=pltpu.SMEM),
                  pl.BlockSpec((tc, d), lambda i: (rb0 + i, 0)),
                  pl.BlockSpec((1, d), lambda i: (0, 0)),
                  pl.BlockSpec(memory_space=pl.ANY)],
        out_specs=pl.BlockSpec((tc, d), lambda i: (i, 0)),
        scratch_shapes=[pltpu.VMEM((2, TOP_K * tc, d), F32), pltpu.SemaphoreType.DMA((2,))],
    )
    return pl.pallas_call(
        _combine_body,
        grid_spec=grid_spec,
        out_shape=jax.ShapeDtypeStruct((n_rows, d), F32),
        compiler_params=_params(1),
        name="combine",
    )(pos3, pos3, x, g.reshape(1, d), y_sorted)


def _tile_positions(pos, row0, n_rows, tc):
    p = pos[row0:row0 + n_rows].reshape(n_rows // tc, tc, TOP_K)
    return jnp.swapaxes(p, 1, 2).reshape(n_rows // tc, 1, TOP_K * tc)


def kernel(x_prompt, x_sample, state_hgrn, state_conv, norm_mix, norm_ffn, norm_final, hgrn_w_in, hgrn_lb, hgrn_g_norm, hgrn_w_out, conv_w_in, conv_w, conv_w_out, ffn_w1, ffn_w3, ffn_w2, moe_router, moe_w1, moe_w3, moe_w2):
    batch, seq, d = x_prompt.shape
    n_dec = x_sample.shape[0]
    n_prompt = batch * seq
    n = n_prompt + n_dec
    n_experts = moe_router.shape[-1]
    assert x_sample.shape[1] == 1 and d % HEAD == 0 and seq % CHUNK == 0
    assert norm_mix.shape[0] == 2, "one HGRN2 layer followed by one short-conv layer"

    tm = _pick_tile(n, (688, 384, 128, 16))
    tm_proj = _pick_tile(n, (1376, 688, 384, 128, 16))
    tb = _pick_tile(seq, (256, 128, 64, 32))
    tg = _pick_tile(n_dec, (16,))
    tf = _pick_tile(ffn_w1.shape[-1], (512, 256, 128))
    tm_moe = _pick_tile(n_prompt, (512, 128, 16))
    tc = _pick_tile(n_prompt, (512, 128, 16))
    assert n_prompt % tg == 0 and n_prompt % n_dec == 0 and n_prompt % tc == 0

    x = jnp.concatenate([x_prompt.reshape(n_prompt, d), x_sample.reshape(n_dec, d)], axis=0)

    p16, p32 = _proj(x, norm_mix[0], hgrn_w_in[0], tm_proj, f32_col=1)
    o_p, s_prompt = _hgrn_prompt(p16, p32, hgrn_lb, hgrn_g_norm[0], 0, batch, seq, tb)
    o_s, s_sample = _hgrn_sample(p16, p32, hgrn_lb, hgrn_g_norm[0], state_hgrn, 0, n_prompt, tg)
    x = _outproj(x, jnp.concatenate([o_p, o_s], axis=0), hgrn_w_out[0], tm_proj)
    x = _ffn(x, norm_ffn[0], ffn_w1[0], ffn_w3[0], ffn_w2[0], tm, tf)

    (pc,) = _proj(x, norm_mix[1], conv_w_in[0], tm_proj)
    z_p, c_prompt = _conv_prompt(pc, conv_w[0], batch, seq, tb)
    z_s, c_sample = _conv_sample(pc, conv_w[0], state_conv.reshape(n_dec, (CONV_W - 1) * d), n_prompt)
    x = _outproj(x, jnp.concatenate([z_p, z_s], axis=0), conv_w_out[0], tm_proj)

    rw_pad = jnp.pad(moe_router[0], ((0, 0), (0, LANES - n_experts)))
    route, cnt = _router(x, norm_ffn[1], rw_pad, n_experts, tm)
    ids = route[:, 0:TOP_K].astype(I32)
    gates = route[:, TOP_K:2 * TOP_K]
    ranks = route[:, 2 * TOP_K:3 * TOP_K].astype(I32)
    counts = cnt[0, :n_experts].astype(I32)

    n_tiles = (TOP_K * n + n_experts * (tm_moe - 1)) // tm_moe
    padded = ((counts + tm_moe - 1) // tm_moe) * tm_moe
    ends = jnp.cumsum(padded)
    starts = ends - padded
    cstart = jnp.cumsum(counts) - counts
    pos = starts[ids] + ranks
    order = jnp.argsort(ids.reshape(-1), stable=True).astype(I32)
    slot_pos = jnp.arange(n_tiles * tm_moe, dtype=I32)
    slot_e = jnp.minimum(jnp.searchsorted(ends, slot_pos, side="right"), n_experts - 1).astype(I32)
    slot_rank = slot_pos - starts[slot_e]
    slot_valid = slot_rank < counts[slot_e]
    slot_asg = order[jnp.clip(cstart[slot_e] + slot_rank, 0, TOP_K * n - 1)]
    inv = jnp.where(slot_valid, slot_asg // TOP_K, 0).astype(I32)
    gate_sorted = jnp.where(slot_valid, gates.reshape(-1)[slot_asg], 0.0).reshape(-1, 1)
    tile_expert = slot_e[::tm_moe]
    n_used = (ends[-1] // tm_moe).astype(I32).reshape(1)

    y_sorted = _moe(x, norm_ffn[1], moe_w1[0], moe_w3[0], moe_w2[0], tile_expert, n_used,
                    inv.reshape(n_tiles, 1, tm_moe), gate_sorted, tm_moe, tf)

    y_prompt = _combine(x, norm_final, y_sorted, _tile_positions(pos, 0, n_prompt, tc), 0, n_prompt, tc)
    y_sample = _combine(x, norm_final, y_sorted, _tile_positions(pos, n_prompt, n_dec, n_dec),
                        n_prompt, n_dec, n_dec)

    return (y_prompt.reshape(batch, seq, d),
            y_sample.reshape(n_dec, 1, d),
            s_prompt,
            s_sample,
            c_prompt.reshape(batch, 1, CONV_W - 1, d),
            c_sample.reshape(n_dec, 1, CONV_W - 1, d))
```

```python
import functools

import jax
import jax.numpy as jnp
from jax import lax
from jax.experimental import pallas as pl
from jax.experimental.pallas import tpu as pltpu

F32 = jnp.float32
BF16 = jnp.bfloat16
I32 = jnp.int32

EPS = 1e-6
HEAD = 128
CHUNK = 32
CONV_W = 3
TOP_K = 2
LANES = 128
V7X_VMEM_LIMIT = 56 * 1024 * 1024

ARB = "arbitrary"


def _params(n_axes):
    return pltpu.CompilerParams(dimension_semantics=(ARB,) * n_axes, vmem_limit_bytes=V7X_VMEM_LIMIT)


def _pick_tile(n, candidates):
    for c in candidates:
        if n % c == 0:
            return c
    raise ValueError(f"no tile in {candidates} divides {n}")


def _rmsnorm_bf16(x, g):
    ms = jnp.mean(x * x, axis=-1, keepdims=True)
    return (x * lax.rsqrt(ms + EPS) * g).astype(BF16)


def _proj_body(f32_col, x_ref, g_ref, w_ref, *refs):
    o16_ref, xn_ref = refs[0], refs[-1]
    j = pl.program_id(1)

    @pl.when(j == 0)
    def _():
        xn_ref[...] = _rmsnorm_bf16(x_ref[...], g_ref[...])

    def result():
        return jnp.dot(xn_ref[...], w_ref[...].astype(BF16), preferred_element_type=F32)

    if f32_col is None:
        o16_ref[...] = result().astype(BF16)
    else:
        o32_ref = refs[1]

        @pl.when(j == f32_col)
        def _():
            o32_ref[...] = result()

        @pl.when(j != f32_col)
        def _():
            o16_ref[...] = result().astype(BF16)


def _proj(x, g, w, tm, f32_col=None):
    n, d = x.shape
    n_col = w.shape[1] // d
    if f32_col is None:
        col16 = lambda i, j: (i, j)
        n16 = n_col
    else:
        assert 0 < f32_col < n_col
        col16 = lambda i, j: (i, jnp.where(j >= f32_col, j - 1, j))
        n16 = n_col - 1
    out_specs = [pl.BlockSpec((tm, d), col16)]
    out_shape = [jax.ShapeDtypeStruct((n, n16 * d), BF16)]
    if f32_col is not None:
        out_specs.append(pl.BlockSpec((tm, d), lambda i, j: (i, 0)))
        out_shape.append(jax.ShapeDtypeStruct((n, d), F32))
    return pl.pallas_call(
        functools.partial(_proj_body, f32_col),
        grid=(n // tm, n_col),
        in_specs=[pl.BlockSpec((tm, d), lambda i, j: (i, 0)),
                  pl.BlockSpec((1, d), lambda i, j: (0, 0)),
                  pl.BlockSpec((d, d), lambda i, j: (0, j))],
        out_specs=out_specs,
        out_shape=out_shape,
        scratch_shapes=[pltpu.VMEM((tm, d), BF16)],
        compiler_params=_params(2),
        name="proj",
    )(x, g.reshape(1, d), w)


def _forget_lower_bound(lb_ref, layer):
    lbw = lb_ref[...]
    e = jnp.exp(lbw - jnp.max(lbw, axis=0, keepdims=True))
    sm = e / jnp.sum(e, axis=0, keepdims=True)
    return jnp.sum(sm[:layer + 1], axis=0, keepdims=True)


def _head_rmsnorm(o):
    return o * lax.rsqrt(jnp.mean(o * o, axis=-1, keepdims=True) + EPS)


def _sigmoid(x):
    return 0.5 * (jnp.tanh(0.5 * x) + 1.0)


def _silu(x):
    return x * _sigmoid(x)


def _split3_bf16(x):
    hi = x.astype(BF16)
    r1 = x - hi.astype(F32)
    mid = r1.astype(BF16)
    lo = (r1 - mid.astype(F32)).astype(BF16)
    return hi, mid, lo


def _hgrn_prompt_body(layer, n_heads, tb, q_ref, f_ref, v_ref, gt_ref, lb_ref, gn_ref,
                      o_ref, s_ref, st_scr, qin_scr, kin_scr, kdec_scr, g_scr, sprev_scr):
    t = pl.program_id(1)
    nc = tb // CHUNK
    d = n_heads * HEAD

    @pl.when(t == 0)
    def _():
        st_scr[...] = jnp.zeros_like(st_scr)

    lb = _forget_lower_bound(lb_ref, layer)
    gn = gn_ref[...]
    row = lax.broadcasted_iota(I32, (tb, tb), 0)
    col = lax.broadcasted_iota(I32, (tb, tb), 1)
    same_chunk = (row // CHUNK) == (col // CHUNK)
    causal = jnp.logical_and(same_chunk, row >= col)

    f = lb + (1.0 - lb) * _sigmoid(f_ref[...])
    logf = jnp.log(f)
    k = 1.0 - f
    tri = causal.astype(BF16)
    G = None
    for part in _split3_bf16(logf):
        term = jnp.dot(tri, part, preferred_element_type=F32)
        G = term if G is None else G + term
    g_scr[...] = G
    g_last = [g_scr[(c + 1) * CHUNK - 1:(c + 1) * CHUNK, :] for c in range(nc)]
    decay = [jnp.exp(g) for g in g_last]
    decay_rows = jnp.concatenate([jnp.broadcast_to(dc, (CHUNK, d)) for dc in decay], axis=0)
    k_in = k * jnp.exp(-G)
    qin_scr[...] = (_silu(q_ref[...].astype(F32)) * jnp.exp(G)).astype(BF16)
    kin_scr[...] = k_in.astype(BF16)
    kdec_scr[...] = (k_in * decay_rows).astype(BF16)

    lane_chunk = lax.broadcasted_iota(I32, (HEAD, tb), 1) // CHUNK
    row_chunk = lax.broadcasted_iota(I32, (tb, HEAD), 0) // CHUNK
    zero16 = jnp.zeros((), BF16)
    grp = 2 if nc % 2 == 0 else 1

    for h in range(n_heads):
        hs = slice(h * HEAD, (h + 1) * HEAD)
        qh = qin_scr[:, hs]
        kd = kdec_scr[:, hs]
        vh = v_ref[:, hs]
        a = lax.dot_general(qh, kin_scr[:, hs], (((1,), (1,)), ((), ())), preferred_element_type=F32)
        a = jnp.where(causal, a, 0.0).astype(BF16)
        o = jnp.dot(a, vh, preferred_element_type=F32)
        v_t = vh.T
        v_blocks = jnp.concatenate([jnp.where(lane_chunk == c, v_t, zero16) for c in range(nc)], axis=0)
        ut = jnp.dot(v_blocks, kd, preferred_element_type=F32)
        st = st_scr[h]
        for c in range(nc):
            sprev_scr[h, :, c * HEAD:(c + 1) * HEAD] = st.astype(BF16)
            st = st * decay[c][:, hs] + ut[c * HEAD:(c + 1) * HEAD, :]
        st_scr[h] = st
        inter = []
        for g0 in range(0, nc, grp):
            rows = slice(g0 * CHUNK, (g0 + grp) * CHUNK)
            q_blocks = jnp.concatenate(
                [jnp.where(row_chunk[rows] == g0 + j, qh[rows], zero16) for j in range(grp)], axis=1)
            inter.append(lax.dot_general(q_blocks, sprev_scr[h, :, g0 * HEAD:(g0 + grp) * HEAD],
                                         (((1,), (1,)), ((), ())), preferred_element_type=F32))
        o = o + jnp.concatenate(inter, axis=0)
        o = _head_rmsnorm(o) * gn[:, hs] * _silu(gt_ref[:, hs].astype(F32))
        o_ref[:, hs] = o.astype(BF16)

    @pl.when(t == pl.num_programs(1) - 1)
    def _():
        for h in range(n_heads):
            s_ref[0, 0, h] = st_scr[h].T


def _hgrn_prompt(p16, p32, hgrn_lb, g_norm, layer, batch, seq, tb):
    d = g_norm.shape[0]
    n_heads = d // HEAD
    n_t = seq // tb
    blk = lambda kk: pl.BlockSpec((tb, d), lambda b, t, kk=kk: (b * n_t + t, kk))
    return pl.pallas_call(
        functools.partial(_hgrn_prompt_body, layer, n_heads, tb),
        grid=(batch, n_t),
        in_specs=[blk(0), blk(0), blk(1), blk(2),
                  pl.BlockSpec(hgrn_lb.shape, lambda b, t: (0, 0)),
                  pl.BlockSpec((1, d), lambda b, t: (0, 0))],
        out_specs=[pl.BlockSpec((tb, d), lambda b, t: (b * n_t + t, 0)),
                   pl.BlockSpec((1, 1, n_heads, HEAD, HEAD), lambda b, t: (b, 0, 0, 0, 0))],
        out_shape=[jax.ShapeDtypeStruct((batch * seq, d), BF16),
                   jax.ShapeDtypeStruct((batch, 1, n_heads, HEAD, HEAD), F32)],
        scratch_shapes=[pltpu.VMEM((n_heads, HEAD, HEAD), F32),
                        pltpu.VMEM((tb, d), BF16), pltpu.VMEM((tb, d), BF16), pltpu.VMEM((tb, d), BF16),
                        pltpu.VMEM((tb, d), F32),
                        pltpu.VMEM((n_heads, HEAD, (tb // CHUNK) * HEAD), BF16)],
        compiler_params=_params(2),
        name="hgrn_prompt",
    )(p16, p32, p16, p16, hgrn_lb, g_norm.reshape(1, d))


def _hgrn_sample_body(layer, tg, q_ref, f_ref, v_ref, gt_ref, lb_ref, gn_ref, s0_ref,
                      o_ref, s_ref, o_scr):
    lb_h = _forget_lower_bound(lb_ref, layer)
    gn_h = gn_ref[...]

    q = jax.nn.silu(q_ref[...].astype(F32))
    f = lb_h + (1.0 - lb_h) * jax.nn.sigmoid(f_ref[...])
    decay = f
    k = 1.0 - f
    v = v_ref[...].astype(F32)
    q_t, d_t, k_t = q.T, decay.T, k.T
    for j in range(tg):
        s = d_t[:, j:j + 1] * s0_ref[j, 0, 0] + k_t[:, j:j + 1] * v[j:j + 1, :]
        s_ref[j, 0, 0] = s
        o_scr[j:j + 1, :] = jnp.sum(q_t[:, j:j + 1] * s, axis=0, keepdims=True)
    o = _head_rmsnorm(o_scr[...]) * gn_h * jax.nn.silu(gt_ref[...].astype(F32))
    o_ref[...] = o.astype(BF16)


def _hgrn_sample(p16, p32, hgrn_lb, g_norm, state, layer, row0, tg):
    nb = state.shape[0]
    d = g_norm.shape[0]
    n_heads = d // HEAD
    rb0 = row0 // tg
    blk = lambda kk: pl.BlockSpec((tg, HEAD), lambda g, h, kk=kk: (rb0 + g, kk * n_heads + h))
    st_spec = pl.BlockSpec((tg, 1, 1, HEAD, HEAD), lambda g, h: (g, 0, h, 0, 0))
    return pl.pallas_call(
        functools.partial(_hgrn_sample_body, layer, tg),
        grid=(nb // tg, n_heads),
        in_specs=[blk(0), blk(0), blk(1), blk(2),
                  pl.BlockSpec((hgrn_lb.shape[0], HEAD), lambda g, h: (0, h)),
                  pl.BlockSpec((1, HEAD), lambda g, h: (0, h)),
                  st_spec],
        out_specs=[pl.BlockSpec((tg, HEAD), lambda g, h: (g, h)), st_spec],
        out_shape=[jax.ShapeDtypeStruct((nb, d), BF16),
                   jax.ShapeDtypeStruct(state.shape, F32)],
        scratch_shapes=[pltpu.VMEM((tg, HEAD), F32)],
        compiler_params=_params(2),
        name="hgrn_sample",
    )(p16, p32, p16, p16, hgrn_lb, g_norm.reshape(1, d), state)


_CARRY = 8


def _conv_prompt_body(tb, b_ref, c_ref, h_ref, w_ref, z_ref, buf_ref, u_scr):
    t = pl.program_id(1)

    @pl.when(t == 0)
    def _():
        u_scr[0:_CARRY, :] = jnp.zeros((_CARRY, u_scr.shape[1]), F32)

    u = c_ref[...].astype(F32) * h_ref[...].astype(F32)
    u_scr[_CARRY:_CARRY + tb, :] = u
    w = w_ref[...]
    y = w[0:1, :] * u_scr[_CARRY - 2:_CARRY - 2 + tb, :]
    y = y + w[1:2, :] * u_scr[_CARRY - 1:_CARRY - 1 + tb, :]
    y = y + w[2:3, :] * u
    z_ref[...] = (b_ref[...].astype(F32) * y).astype(BF16)
    u_scr[0:_CARRY, :] = u[tb - _CARRY:tb, :]

    @pl.when(t == pl.num_programs(1) - 1)
    def _():
        buf_ref[0] = u[tb - (CONV_W - 1):tb, :]


def _conv_prompt(pc, conv_w, batch, seq, tb):
    d = conv_w.shape[1]
    n_t = seq // tb
    blk = lambda kk: pl.BlockSpec((tb, d), lambda b, t, kk=kk: (b * n_t + t, kk))
    return pl.pallas_call(
        functools.partial(_conv_prompt_body, tb),
        grid=(batch, n_t),
        in_specs=[blk(0), blk(1), blk(2), pl.BlockSpec(conv_w.shape, lambda b, t: (0, 0))],
        out_specs=[pl.BlockSpec((tb, d), lambda b, t: (b * n_t + t, 0)),
                   pl.BlockSpec((1, CONV_W - 1, d), lambda b, t: (b, 0, 0))],
        out_shape=[jax.ShapeDtypeStruct((batch * seq, d), BF16),
                   jax.ShapeDtypeStruct((batch, CONV_W - 1, d), F32)],
        scratch_shapes=[pltpu.VMEM((_CARRY + tb, d), F32)],
        compiler_params=_params(2),
        name="conv_prompt",
    )(pc, pc, pc, conv_w)


def _conv_sample_body(d, b_ref, c_ref, h_ref, w_ref, st_ref, z_ref, buf_ref):
    u = c_ref[...].astype(F32) * h_ref[...].astype(F32)
    w = w_ref[...]
    buf0 = st_ref[:, 0:d]
    buf1 = st_ref[:, d:2 * d]
    y = w[0:1, :] * buf0
    y = y + w[1:2, :] * buf1
    y = y + w[2:3, :] * u
    z_ref[...] = (b_ref[...].astype(F32) * y).astype(BF16)
    buf_ref[:, 0:d] = buf1
    buf_ref[:, d:2 * d] = u


def _conv_sample(pc, conv_w, state2d, row0):
    nb = state2d.shape[0]
    d = conv_w.shape[1]
    rb0 = row0 // nb
    blk = lambda kk: pl.BlockSpec((nb, d), lambda i, kk=kk: (rb0, kk))
    return pl.pallas_call(
        functools.partial(_conv_sample_body, d),
        grid=(1,),
        in_specs=[blk(0), blk(1), blk(2), pl.BlockSpec(conv_w.shape, lambda i: (0, 0)),
                  pl.BlockSpec(state2d.shape, lambda i: (0, 0))],
        out_specs=[pl.BlockSpec((nb, d), lambda i: (0, 0)),
                   pl.BlockSpec(state2d.shape, lambda i: (0, 0))],
        out_shape=[jax.ShapeDtypeStruct((nb, d), BF16),
                   jax.ShapeDtypeStruct(state2d.shape, F32)],
        compiler_params=_params(1),
        name="conv_sample",
    )(pc, pc, pc, conv_w, state2d)


def _outproj_body(x_ref, z_ref, w_ref, o_ref, w16_ref):
    @pl.when(pl.program_id(0) == 0)
    def _():
        w16_ref[...] = w_ref[...].astype(BF16)

    o_ref[...] = x_ref[...] + jnp.dot(z_ref[...], w16_ref[...], preferred_element_type=F32)


def _outproj(x, z, w, tm):
    n, d = x.shape
    return pl.pallas_call(
        _outproj_body,
        grid=(n // tm,),
        in_specs=[pl.BlockSpec((tm, d), lambda i: (i, 0)),
                  pl.BlockSpec((tm, d), lambda i: (i, 0)),
                  pl.BlockSpec((d, d), lambda i: (0, 0))],
        out_specs=pl.BlockSpec((tm, d), lambda i: (i, 0)),
        out_shape=jax.ShapeDtypeStruct((n, d), F32),
        scratch_shapes=[pltpu.VMEM((d, d), BF16)],
        compiler_params=_params(1),
        name="outproj",
    )(x, z, w)


def _swiglu_partial(xn, w1, w3, w2):
    h1 = jnp.dot(xn, w1.astype(BF16), preferred_element_type=F32)
    h3 = jnp.dot(xn, w3.astype(BF16), preferred_element_type=F32)
    h = (jax.nn.silu(h1) * h3).astype(BF16)
    return jnp.dot(h, w2.astype(BF16), preferred_element_type=F32)


def _ffn_body(x_ref, g_ref, w1_ref, w3_ref, w2_ref, o_ref, xn_ref, acc_ref):
    f = pl.program_id(1)

    @pl.when(f == 0)
    def _():
        xn_ref[...] = _rmsnorm_bf16(x_ref[...], g_ref[...])
        acc_ref[...] = jnp.zeros_like(acc_ref)

    acc_ref[...] += _swiglu_partial(xn_ref[...], w1_ref[...], w3_ref[...], w2_ref[...])

    @pl.when(f == pl.num_programs(1) - 1)
    def _():
        o_ref[...] = x_ref[...] + acc_ref[...]


def _ffn(x, g, w1, w3, w2, tm, tf):
    n, d = x.shape
    dff = w1.shape[1]
    return pl.pallas_call(
        _ffn_body,
        grid=(n // tm, dff // tf),
        in_specs=[pl.BlockSpec((tm, d), lambda i, f: (i, 0)),
                  pl.BlockSpec((1, d), lambda i, f: (0, 0)),
                  pl.BlockSpec((d, tf), lambda i, f: (0, f)),
                  pl.BlockSpec((d, tf), lambda i, f: (0, f)),
                  pl.BlockSpec((tf, d), lambda i, f: (f, 0))],
        out_specs=pl.BlockSpec((tm, d), lambda i, f: (i, 0)),
        out_shape=jax.ShapeDtypeStruct((n, d), F32),
        scratch_shapes=[pltpu.VMEM((tm, d), BF16), pltpu.VMEM((tm, d), F32)],
        compiler_params=_params(2),
        name="ffn",
    )(x, g.reshape(1, d), w1, w3, w2)


def _router_body(n_experts, x_ref, g_ref, rw_ref, route_ref, cnt_ref, cnt_scr):
    i = pl.program_id(0)
    tm = x_ref.shape[0]

    @pl.when(i == 0)
    def _():
        cnt_scr[...] = jnp.zeros_like(cnt_scr)

    xn = _rmsnorm_bf16(x_ref[...], g_ref[...])
    logits = jnp.dot(xn, rw_ref[...].astype(BF16), preferred_element_type=F32)
    lane = lax.broadcasted_iota(I32, (tm, LANES), 1).astype(F32)
    neg = jnp.float32(-jnp.inf)
    logits = jnp.where(lane < n_experts, logits, neg)
    m1 = jnp.max(logits, axis=-1, keepdims=True)
    i1 = jnp.min(jnp.where(logits == m1, lane, float(LANES)), axis=-1, keepdims=True)
    i1 = jnp.minimum(i1, float(n_experts - 1))
    rest = jnp.where(lane == i1, neg, logits)
    m2 = jnp.max(rest, axis=-1, keepdims=True)
    i2 = jnp.min(jnp.where(rest == m2, lane, float(LANES)), axis=-1, keepdims=True)
    i2 = jnp.minimum(i2, float(n_experts - 1))
    e2 = jnp.exp(m2 - m1)
    den = 1.0 + e2
    g1 = 1.0 / den
    g2 = e2 / den

    sel1 = lane == i1
    sel2 = lane == i2
    onehot = jnp.logical_or(sel1, sel2)
    row = lax.broadcasted_iota(I32, (tm, tm), 0)
    col = lax.broadcasted_iota(I32, (tm, tm), 1)
    before = (row > col).astype(BF16)
    rank = jnp.dot(before, onehot.astype(BF16), preferred_element_type=F32) + cnt_scr[...]
    r1 = jnp.sum(jnp.where(sel1, rank, 0.0), axis=-1, keepdims=True)
    r2 = jnp.sum(jnp.where(sel2, rank, 0.0), axis=-1, keepdims=True)
    cnt_scr[...] += jnp.sum(onehot.astype(F32), axis=0, keepdims=True)

    out = jnp.zeros((tm, LANES), F32)
    for slot, val in enumerate((i1, i2, g1, g2, r1, r2)):
        out = jnp.where(lane == slot, val, out)
    route_ref[...] = out
    cnt_ref[...] = cnt_scr[...]


def _router(x, g, rw_pad, n_experts, tm):
    n, d = x.shape
    return pl.pallas_call(
        functools.partial(_router_body, n_experts),
        grid=(n // tm,),
        in_specs=[pl.BlockSpec((tm, d), lambda i: (i, 0)),
                  pl.BlockSpec((1, d), lambda i: (0, 0)),
                  pl.BlockSpec((d, LANES), lambda i: (0, 0))],
        out_specs=[pl.BlockSpec((tm, LANES), lambda i: (i, 0)),
                   pl.BlockSpec((1, LANES), lambda i: (0, 0))],
        out_shape=[jax.ShapeDtypeStruct((n, LANES), F32),
                   jax.ShapeDtypeStruct((1, LANES), F32)],
        scratch_shapes=[pltpu.VMEM((1, LANES), F32)],
        compiler_params=_params(1),
        name="router",
    )(x, g.reshape(1, d), rw_pad)


SUBLANES = 8
MOE_TM = 448


def _start_row_group(src_hbm, idx_ref, g, dst, sem):
    for k in range(SUBLANES):
        pltpu.make_async_copy(src_hbm.at[pl.ds(idx_ref[0, 0, g * SUBLANES + k], 1)],
                              dst.at[g, pl.ds(k, 1)], sem).start()


def _start_row_gather(src_hbm, idx_ref, n_rows, dst, sem):
    def body(g, carry):
        _start_row_group(src_hbm, idx_ref, g, dst, sem)
        return carry

    lax.fori_loop(0, n_rows // SUBLANES, body, 0)


def _wait_row_gather(src_hbm, n_rows, dst, sem):
    pltpu.make_async_copy(src_hbm.at[pl.ds(0, n_rows)], dst.reshape(n_rows, dst.shape[-1]), sem).wait()


def _moe_body(n_f, tf, te_ref, nu_ref, idx_cur, idx_nxt, x_hbm, gate_ref, g_ref, w1_hbm, w3_hbm, w2_hbm,
              y_ref, xbuf, sem, xn_ref, acc_ref, c1, c3, c2, s1, s3, s2, wsem):
    i = pl.program_id(0)
    f = pl.program_id(1)
    n_tiles = pl.num_programs(0)
    tm = xbuf.shape[1] * SUBLANES
    rows_per_step = tm // n_f
    slot = lax.rem(i, 2)
    n_used = nu_ref[0]
    used = i < n_used
    e = te_ref[i]
    first_of_expert = jnp.logical_or(i == 0, e != te_ref[jnp.maximum(i - 1, 0)])

    @pl.when(f == 0)
    def _():
        @pl.when(i == 0)
        def _():
            _start_row_gather(x_hbm, idx_cur, tm, xbuf.at[0], sem.at[0])

        @pl.when(i <= n_used)
        def _():
            _wait_row_gather(x_hbm, tm, xbuf.at[slot], sem.at[slot])

        xn_ref[...] = _rmsnorm_bf16(xbuf[slot].reshape(tm, xbuf.shape[-1]), g_ref[...])
        acc_ref[...] = jnp.zeros_like(acc_ref)

    def chunk_copies(ff, ws):
        cols = pl.ds(pl.multiple_of(ff * tf, tf), tf)
        return (pltpu.make_async_copy(w1_hbm.at[e, :, cols], s1.at[ws], wsem.at[ws, 0]),
                pltpu.make_async_copy(w3_hbm.at[e, :, cols], s3.at[ws], wsem.at[ws, 1]),
                pltpu.make_async_copy(w2_hbm.at[e, cols, :], s2.at[ws], wsem.at[ws, 2]))

    @pl.when(jnp.logical_and(used, first_of_expert))
    def _():
        ws = lax.rem(f, 2)

        @pl.when(f == 0)
        def _():
            for c in chunk_copies(0, 0):
                c.start()

        @pl.when(f + 1 < n_f)
        def _():
            for c in chunk_copies(f + 1, 1 - ws):
                c.start()

        for c in chunk_copies(f, ws):
            c.wait()
        c1[f] = s1[ws].astype(BF16)
        c3[f] = s3[ws].astype(BF16)
        c2[f] = s2[ws].astype(BF16)

    @pl.when(used)
    def _():
        for g in range(rows_per_step // SUBLANES):
            _start_row_group(x_hbm, idx_nxt, f * (rows_per_step // SUBLANES) + g, xbuf.at[1 - slot], sem.at[1 - slot])
        acc_ref[...] += _swiglu_partial(xn_ref[...], c1[f], c3[f], c2[f])

    @pl.when(f == n_f - 1)
    def _():
        y_ref[...] = acc_ref[...] * gate_ref[...]

        @pl.when(jnp.logical_and(used, i == n_tiles - 1))
        def _():
            _wait_row_gather(x_hbm, tm, xbuf.at[1 - slot], sem.at[1 - slot])


def _moe(x, g, w1, w3, w2, tile_expert, n_used, inv3, gate_sorted, tm, tf):
    n_tiles = inv3.shape[0]
    d = x.shape[1]
    dff = w1.shape[2]
    n_f = dff // tf
    assert tm % (n_f * SUBLANES) == 0, "each hidden chunk fetches whole 8-row tiles of the next row tile"
    grid_spec = pltpu.PrefetchScalarGridSpec(
        num_scalar_prefetch=2,
        grid=(n_tiles, n_f),
        in_specs=[pl.BlockSpec((1, 1, tm), lambda i, f, te, nu: (i, 0, 0), memory_space=pltpu.SMEM),
                  pl.BlockSpec((1, 1, tm), lambda i, f, te, nu: (jnp.minimum(i + 1, n_tiles - 1), 0, 0),
                               memory_space=pltpu.SMEM),
                  pl.BlockSpec(memory_space=pl.ANY),
                  pl.BlockSpec((tm, 1), lambda i, f, te, nu: (i, 0)),
                  pl.BlockSpec((1, d), lambda i, f, te, nu: (0, 0)),
                  pl.BlockSpec(memory_space=pl.ANY),
                  pl.BlockSpec(memory_space=pl.ANY),
                  pl.BlockSpec(memory_space=pl.ANY)],
        out_specs=pl.BlockSpec((tm, d), lambda i, f, te, nu: (i, 0)),
        scratch_shapes=[pltpu.VMEM((2, tm // SUBLANES, SUBLANES, d), F32),
                        pltpu.SemaphoreType.DMA((2,)),
                        pltpu.VMEM((tm, d), BF16),
                        pltpu.VMEM((tm, d), F32),
                        pltpu.VMEM((n_f, d, tf), BF16),
                        pltpu.VMEM((n_f, d, tf), BF16),
                        pltpu.VMEM((n_f, tf, d), BF16),
                        pltpu.VMEM((2, d, tf), F32),
                        pltpu.VMEM((2, d, tf), F32),
                        pltpu.VMEM((2, tf, d), F32),
                        pltpu.SemaphoreType.DMA((2, 3))],
    )
    return pl.pallas_call(
        functools.partial(_moe_body, n_f, tf),
        grid_spec=grid_spec,
        out_shape=jax.ShapeDtypeStruct((n_tiles * tm, d), F32),
        compiler_params=_params(2),
        name="moe",
    )(tile_expert, n_used, inv3, inv3, x, gate_sorted, g.reshape(1, d), w1, w3, w2)


def _combine_body(pos_cur, pos_nxt, x_ref, g_ref, y_hbm, o_ref, ybuf, sem):
    i = pl.program_id(0)
    n_tiles = pl.num_programs(0)
    rows = ybuf.shape[1] * SUBLANES
    slot = lax.rem(i, 2)

    @pl.when(i == 0)
    def _():
        _start_row_gather(y_hbm, pos_cur, rows, ybuf.at[0], sem.at[0])

    @pl.when(i + 1 < n_tiles)
    def _():
        _start_row_gather(y_hbm, pos_nxt, rows, ybuf.at[1 - slot], sem.at[1 - slot])

    _wait_row_gather(y_hbm, rows, ybuf.at[slot], sem.at[slot])
    tc = rows // TOP_K
    y = ybuf[slot].reshape(rows, ybuf.shape[-1])
    moe = y[0:tc, :] + y[tc:rows, :]
    x = x_ref[...] + moe
    ms = jnp.mean(x * x, axis=-1, keepdims=True)
    o_ref[...] = x * lax.rsqrt(ms + EPS) * g_ref[...]


def _combine(x, g, y_sorted, pos3, row0, n_rows, tc):
    d = x.shape[1]
    n_tiles = n_rows // tc
    rb0 = row0 // tc
    grid_spec = pltpu.PrefetchScalarGridSpec(
        num_scalar_prefetch=0,
        grid=(n_tiles,),
        in_specs=[pl.BlockSpec((1, 1, TOP_K * tc), lambda i: (i, 0, 0), memory_space=pltpu.SMEM),
                  pl.BlockSpec((1, 1, TOP_K * tc), lambda i: (jnp.minimum(i + 1, n_tiles - 1), 0, 0),
                               memory_space=pltpu.SMEM),
                  pl.BlockSpec((tc, d), lambda i: (rb0 + i, 0)),
                  pl.BlockSpec((1, d), lambda i: (0, 0)),
                  pl.BlockSpec(memory_space=pl.ANY)],
        out_specs=pl.BlockSpec((tc, d), lambda i: (i, 0)),
        scratch_shapes=[pltpu.VMEM((2, TOP_K * tc // SUBLANES, SUBLANES, d), F32), pltpu.SemaphoreType.DMA((2,))],
    )
    return pl.pallas_call(
        _combine_body,
        grid_spec=grid_spec,
        out_shape=jax.ShapeDtypeStruct((n_rows, d), F32),
        compiler_params=_params(1),
        name="combine",
    )(pos3, pos3, x, g.reshape(1, d), y_sorted)


def _tile_positions(pos, row0, n_rows, tc):
    p = pos[row0:row0 + n_rows].reshape(n_rows // tc, tc, TOP_K)
    return jnp.swapaxes(p, 1, 2).reshape(n_rows // tc, 1, TOP_K * tc)


def _expert_layer(x, g_ffn, g_final, router_w, w1, w3, w2, n_prompt, n_dec, tm, tm_moe, tf, tc):
    n = n_prompt + n_dec
    n_experts = router_w.shape[-1]
    rw_pad = jnp.pad(router_w, ((0, 0), (0, LANES - n_experts)))
    route, cnt = _router(x, g_ffn, rw_pad, n_experts, tm)
    ids = route[:, 0:TOP_K].astype(I32)
    gates = route[:, TOP_K:2 * TOP_K]
    ranks = route[:, 2 * TOP_K:3 * TOP_K].astype(I32)
    counts = cnt[0, :n_experts].astype(I32)

    n_tiles = (TOP_K * n + n_experts * (tm_moe - 1)) // tm_moe
    padded = ((counts + tm_moe - 1) // tm_moe) * tm_moe
    ends = jnp.cumsum(padded)
    starts = ends - padded
    cstart = jnp.cumsum(counts) - counts
    pos = starts[ids] + ranks
    order = jnp.argsort(ids.reshape(-1), stable=True).astype(I32)
    slot_pos = jnp.arange(n_tiles * tm_moe, dtype=I32)
    slot_e = jnp.minimum(jnp.sum(slot_pos[:, None] >= ends[None, :], axis=1), n_experts - 1).astype(I32)
    slot_rank = slot_pos - starts[slot_e]
    slot_valid = slot_rank < counts[slot_e]
    slot_asg = order[jnp.clip(cstart[slot_e] + slot_rank, 0, TOP_K * n - 1)]
    inv = jnp.where(slot_valid, slot_asg // TOP_K, 0).astype(I32)
    gate_sorted = jnp.where(slot_valid, gates.reshape(-1)[slot_asg], 0.0).reshape(-1, 1)
    tile_expert = slot_e[::tm_moe]
    n_used = (ends[-1] // tm_moe).astype(I32).reshape(1)

    y_sorted = _moe(x, g_ffn, w1, w3, w2, tile_expert, n_used,
                    inv.reshape(n_tiles, 1, tm_moe), gate_sorted, tm_moe, tf)
    y_prompt = _combine(x, g_final, y_sorted, _tile_positions(pos, 0, n_prompt, tc), 0, n_prompt, tc)
    y_sample = _combine(x, g_final, y_sorted, _tile_positions(pos, n_prompt, n_dec, n_dec),
                        n_prompt, n_dec, n_dec)
    return y_prompt, y_sample


def kernel(x_prompt, x_sample, state_hgrn, state_conv, norm_mix, norm_ffn, norm_final, hgrn_w_in, hgrn_lb, hgrn_g_norm, hgrn_w_out, conv_w_in, conv_w, conv_w_out, ffn_w1, ffn_w3, ffn_w2, moe_router, moe_w1, moe_w3, moe_w2):
    batch, seq, d = x_prompt.shape
    n_dec = x_sample.shape[0]
    n_prompt = batch * seq
    n = n_prompt + n_dec
    n_experts = moe_router.shape[-1]
    assert x_sample.shape[1] == 1 and d % HEAD == 0 and seq % CHUNK == 0
    assert norm_mix.shape[0] == 2, "one HGRN2 layer followed by one short-conv layer"

    tm = _pick_tile(n, (688, 384, 128, 16))
    tm_proj = _pick_tile(n, (1376, 688, 384, 128, 16))
    tb = _pick_tile(seq, (256, 128, 64, 32))
    tg = _pick_tile(n_dec, (16,))
    tf = _pick_tile(ffn_w1.shape[-1], (512, 256, 128))
    tm_moe = _pick_tile(n_prompt, (512, 128, 16))
    tm_moe = MOE_TM if n_prompt % 128 == 0 and ffn_w1.shape[-1] % (7 * 512) == 0 else tm_moe
    tc = _pick_tile(n_prompt, (512, 128, 16))
    assert n_prompt % tg == 0 and n_prompt % n_dec == 0 and n_prompt % tc == 0

    x = jnp.concatenate([x_prompt.reshape(n_prompt, d), x_sample.reshape(n_dec, d)], axis=0)

    p16, p32 = _proj(x, norm_mix[0], hgrn_w_in[0], tm_proj, f32_col=1)
    o_p, s_prompt = _hgrn_prompt(p16, p32, hgrn_lb, hgrn_g_norm[0], 0, batch, seq, tb)
    o_s, s_sample = _hgrn_sample(p16, p32, hgrn_lb, hgrn_g_norm[0], state_hgrn, 0, n_prompt, tg)
    x = _outproj(x, jnp.concatenate([o_p, o_s], axis=0), hgrn_w_out[0], tm_proj)
    x = _ffn(x, norm_ffn[0], ffn_w1[0], ffn_w3[0], ffn_w2[0], tm, tf)

    (pc,) = _proj(x, norm_mix[1], conv_w_in[0], tm_proj)
    z_p, c_prompt = _conv_prompt(pc, conv_w[0], batch, seq, tb)
    z_s, c_sample = _conv_sample(pc, conv_w[0], state_conv.reshape(n_dec, (CONV_W - 1) * d), n_prompt)
    x = _outproj(x, jnp.concatenate([z_p, z_s], axis=0), conv_w_out[0], tm_proj)

    y_prompt, y_sample = _expert_layer(x, norm_ffn[1], norm_final, moe_router[0], moe_w1[0], moe_w3[0], moe_w2[0],
                                       n_prompt, n_dec, tm, tm_moe, tf, tc)

    return (y_prompt.reshape(batch, seq, d),
            y_sample.reshape(n_dec, 1, d),
            s_prompt,
            s_sample,
            c_prompt.reshape(batch, 1, CONV_W - 1, d),
            c_sample.reshape(n_dec, 1, CONV_W - 1, d))
```

```python
import functools

import jax
import jax.numpy as jnp
from jax import lax
from jax.experimental import pallas as pl
from jax.experimental.pallas import tpu as pltpu

F32 = jnp.float32
BF16 = jnp.bfloat16
I32 = jnp.int32

EPS = 1e-6
HEAD = 128
CHUNK = 32
CONV_W = 3
TOP_K = 2
LANES = 128
V7X_VMEM_LIMIT = 56 * 1024 * 1024

ARB = "arbitrary"


def _params(n_axes):
    return pltpu.CompilerParams(dimension_semantics=(ARB,) * n_axes, vmem_limit_bytes=V7X_VMEM_LIMIT)


def _pick_tile(n, candidates):
    for c in candidates:
        if n % c == 0:
            return c
    raise ValueError(f"no tile in {candidates} divides {n}")


def _rmsnorm_bf16(x, g):
    ms = jnp.mean(x * x, axis=-1, keepdims=True)
    return (x * lax.rsqrt(ms + EPS) * g).astype(BF16)


def _proj_body(f32_col, x_ref, g_ref, w_ref, *refs):
    o16_ref, xn_ref = refs[0], refs[-1]
    j = pl.program_id(1)

    @pl.when(j == 0)
    def _():
        xn_ref[...] = _rmsnorm_bf16(x_ref[...], g_ref[...])

    def result():
        return jnp.dot(xn_ref[...], w_ref[...].astype(BF16), preferred_element_type=F32)

    if f32_col is None:
        o16_ref[...] = result().astype(BF16)
    else:
        o32_ref = refs[1]

        @pl.when(j == f32_col)
        def _():
            o32_ref[...] = result()

        @pl.when(j != f32_col)
        def _():
            o16_ref[...] = result().astype(BF16)


def _proj(x, g, w, tm, f32_col=None):
    n, d = x.shape
    n_col = w.shape[1] // d
    if f32_col is None:
        col16 = lambda i, j: (i, j)
        n16 = n_col
    else:
        assert 0 < f32_col < n_col
        col16 = lambda i, j: (i, jnp.where(j >= f32_col, j - 1, j))
        n16 = n_col - 1
    out_specs = [pl.BlockSpec((tm, d), col16)]
    out_shape = [jax.ShapeDtypeStruct((n, n16 * d), BF16)]
    if f32_col is not None:
        out_specs.append(pl.BlockSpec((tm, d), lambda i, j: (i, 0)))
        out_shape.append(jax.ShapeDtypeStruct((n, d), F32))
    return pl.pallas_call(
        functools.partial(_proj_body, f32_col),
        grid=(n // tm, n_col),
        in_specs=[pl.BlockSpec((tm, d), lambda i, j: (i, 0)),
                  pl.BlockSpec((1, d), lambda i, j: (0, 0)),
                  pl.BlockSpec((d, d), lambda i, j: (0, j))],
        out_specs=out_specs,
        out_shape=out_shape,
        scratch_shapes=[pltpu.VMEM((tm, d), BF16)],
        compiler_params=_params(2),
        name="proj",
    )(x, g.reshape(1, d), w)


def _forget_lower_bound(lb_ref, layer):
    lbw = lb_ref[...]
    e = jnp.exp(lbw - jnp.max(lbw, axis=0, keepdims=True))
    sm = e / jnp.sum(e, axis=0, keepdims=True)
    return jnp.sum(sm[:layer + 1], axis=0, keepdims=True)


def _head_rmsnorm(o):
    return o * lax.rsqrt(jnp.mean(o * o, axis=-1, keepdims=True) + EPS)


def _sigmoid(x):
    return 0.5 * (jnp.tanh(0.5 * x) + 1.0)


def _silu(x):
    return x * _sigmoid(x)


def _split3_bf16(x):
    hi = x.astype(BF16)
    r1 = x - hi.astype(F32)
    mid = r1.astype(BF16)
    lo = (r1 - mid.astype(F32)).astype(BF16)
    return hi, mid, lo


def _hgrn_prompt_body(layer, n_heads, tb, q_ref, f_ref, v_ref, gt_ref, lb_ref, gn_ref,
                      o_ref, s_ref, st_scr, qin_scr, kin_scr, kdec_scr, g_scr, sprev_scr):
    t = pl.program_id(1)
    nc = tb // CHUNK
    d = n_heads * HEAD

    @pl.when(t == 0)
    def _():
        st_scr[...] = jnp.zeros_like(st_scr)

    lb = _forget_lower_bound(lb_ref, layer)
    gn = gn_ref[...]
    row = lax.broadcasted_iota(I32, (tb, tb), 0)
    col = lax.broadcasted_iota(I32, (tb, tb), 1)
    same_chunk = (row // CHUNK) == (col // CHUNK)
    causal = jnp.logical_and(same_chunk, row >= col)

    f = lb + (1.0 - lb) * _sigmoid(f_ref[...])
    logf = jnp.log(f)
    k = 1.0 - f
    tri = causal.astype(BF16)
    G = None
    for part in _split3_bf16(logf):
        term = jnp.dot(tri, part, preferred_element_type=F32)
        G = term if G is None else G + term
    g_scr[...] = G
    g_last = [g_scr[(c + 1) * CHUNK - 1:(c + 1) * CHUNK, :] for c in range(nc)]
    decay = [jnp.exp(g) for g in g_last]
    decay_rows = jnp.concatenate([jnp.broadcast_to(dc, (CHUNK, d)) for dc in decay], axis=0)
    k_in = k * jnp.exp(-G)
    qin_scr[...] = (_silu(q_ref[...].astype(F32)) * jnp.exp(G)).astype(BF16)
    kin_scr[...] = k_in.astype(BF16)
    kdec_scr[...] = (k_in * decay_rows).astype(BF16)

    lane_chunk = lax.broadcasted_iota(I32, (HEAD, tb), 1) // CHUNK
    row_chunk = lax.broadcasted_iota(I32, (tb, HEAD), 0) // CHUNK
    zero16 = jnp.zeros((), BF16)
    grp = 2 if nc % 2 == 0 else 1

    for h in range(n_heads):
        hs = slice(h * HEAD, (h + 1) * HEAD)
        qh = qin_scr[:, hs]
        kd = kdec_scr[:, hs]
        vh = v_ref[:, hs]
        a = lax.dot_general(qh, kin_scr[:, hs], (((1,), (1,)), ((), ())), preferred_element_type=F32)
        a = jnp.where(causal, a, 0.0).astype(BF16)
        o = jnp.dot(a, vh, preferred_element_type=F32)
        v_t = vh.T
        v_blocks = jnp.concatenate([jnp.where(lane_chunk == c, v_t, zero16) for c in range(nc)], axis=0)
        ut = jnp.dot(v_blocks, kd, preferred_element_type=F32)
        st = st_scr[h]
        for c in range(nc):
            sprev_scr[h, :, c * HEAD:(c + 1) * HEAD] = st.astype(BF16)
            st = st * decay[c][:, hs] + ut[c * HEAD:(c + 1) * HEAD, :]
        st_scr[h] = st
        inter = []
        for g0 in range(0, nc, grp):
            rows = slice(g0 * CHUNK, (g0 + grp) * CHUNK)
            q_blocks = jnp.concatenate(
                [jnp.where(row_chunk[rows] == g0 + j, qh[rows], zero16) for j in range(grp)], axis=1)
            inter.append(lax.dot_general(q_blocks, sprev_scr[h, :, g0 * HEAD:(g0 + grp) * HEAD],
                                         (((1,), (1,)), ((), ())), preferred_element_type=F32))
        o = o + jnp.concatenate(inter, axis=0)
        o = _head_rmsnorm(o) * gn[:, hs] * _silu(gt_ref[:, hs].astype(F32))
        o_ref[:, hs] = o.astype(BF16)

    @pl.when(t == pl.num_programs(1) - 1)
    def _():
        for h in range(n_heads):
            s_ref[0, 0, h] = st_scr[h].T


def _hgrn_prompt(p16, p32, hgrn_lb, g_norm, layer, batch, seq, tb):
    d = g_norm.shape[0]
    n_heads = d // HEAD
    n_t = seq // tb
    blk = lambda kk: pl.BlockSpec((tb, d), lambda b, t, kk=kk: (b * n_t + t, kk))
    return pl.pallas_call(
        functools.partial(_hgrn_prompt_body, layer, n_heads, tb),
        grid=(batch, n_t),
        in_specs=[blk(0), blk(0), blk(1), blk(2),
                  pl.BlockSpec(hgrn_lb.shape, lambda b, t: (0, 0)),
                  pl.BlockSpec((1, d), lambda b, t: (0, 0))],
        out_specs=[pl.BlockSpec((tb, d), lambda b, t: (b * n_t + t, 0)),
                   pl.BlockSpec((1, 1, n_heads, HEAD, HEAD), lambda b, t: (b, 0, 0, 0, 0))],
        out_shape=[jax.ShapeDtypeStruct((batch * seq, d), BF16),
                   jax.ShapeDtypeStruct((batch, 1, n_heads, HEAD, HEAD), F32)],
        scratch_shapes=[pltpu.VMEM((n_heads, HEAD, HEAD), F32),
                        pltpu.VMEM((tb, d), BF16), pltpu.VMEM((tb, d), BF16), pltpu.VMEM((tb, d), BF16),
                        pltpu.VMEM((tb, d), F32),
                        pltpu.VMEM((n_heads, HEAD, (tb // CHUNK) * HEAD), BF16)],
        compiler_params=_params(2),
        name="hgrn_prompt",
    )(p16, p32, p16, p16, hgrn_lb, g_norm.reshape(1, d))


def _hgrn_sample_body(layer, tg, q_ref, f_ref, v_ref, gt_ref, lb_ref, gn_ref, s0_ref,
                      o_ref, s_ref, o_scr):
    lb_h = _forget_lower_bound(lb_ref, layer)
    gn_h = gn_ref[...]

    q = jax.nn.silu(q_ref[...].astype(F32))
    f = lb_h + (1.0 - lb_h) * jax.nn.sigmoid(f_ref[...])
    decay = f
    k = 1.0 - f
    v = v_ref[...].astype(F32)
    q_t, d_t, k_t = q.T, decay.T, k.T
    for j in range(tg):
        s = d_t[:, j:j + 1] * s0_ref[j, 0, 0] + k_t[:, j:j + 1] * v[j:j + 1, :]
        s_ref[j, 0, 0] = s
        o_scr[j:j + 1, :] = jnp.sum(q_t[:, j:j + 1] * s, axis=0, keepdims=True)
    o = _head_rmsnorm(o_scr[...]) * gn_h * jax.nn.silu(gt_ref[...].astype(F32))
    o_ref[...] = o.astype(BF16)


def _hgrn_sample(p16, p32, hgrn_lb, g_norm, state, layer, row0, tg):
    nb = state.shape[0]
    d = g_norm.shape[0]
    n_heads = d // HEAD
    rb0 = row0 // tg
    blk = lambda kk: pl.BlockSpec((tg, HEAD), lambda g, h, kk=kk: (rb0 + g, kk * n_heads + h))
    st_spec = pl.BlockSpec((tg, 1, 1, HEAD, HEAD), lambda g, h: (g, 0, h, 0, 0))
    return pl.pallas_call(
        functools.partial(_hgrn_sample_body, layer, tg),
        grid=(nb // tg, n_heads),
        in_specs=[blk(0), blk(0), blk(1), blk(2),
                  pl.BlockSpec((hgrn_lb.shape[0], HEAD), lambda g, h: (0, h)),
                  pl.BlockSpec((1, HEAD), lambda g, h: (0, h)),
                  st_spec],
        out_specs=[pl.BlockSpec((tg, HEAD), lambda g, h: (g, h)), st_spec],
        out_shape=[jax.ShapeDtypeStruct((nb, d), BF16),
                   jax.ShapeDtypeStruct(state.shape, F32)],
        scratch_shapes=[pltpu.VMEM((tg, HEAD), F32)],
        compiler_params=_params(2),
        name="hgrn_sample",
    )(p16, p32, p16, p16, hgrn_lb, g_norm.reshape(1, d), state)


_CARRY = 8


def _conv_prompt_body(tb, b_ref, c_ref, h_ref, w_ref, z_ref, buf_ref, u_scr):
    t = pl.program_id(1)

    @pl.when(t == 0)
    def _():
        u_scr[0:_CARRY, :] = jnp.zeros((_CARRY, u_scr.shape[1]), F32)

    u = c_ref[...].astype(F32) * h_ref[...].astype(F32)
    u_scr[_CARRY:_CARRY + tb, :] = u
    w = w_ref[...]
    y = w[0:1, :] * u_scr[_CARRY - 2:_CARRY - 2 + tb, :]
    y = y + w[1:2, :] * u_scr[_CARRY - 1:_CARRY - 1 + tb, :]
    y = y + w[2:3, :] * u
    z_ref[...] = (b_ref[...].astype(F32) * y).astype(BF16)
    u_scr[0:_CARRY, :] = u[tb - _CARRY:tb, :]

    @pl.when(t == pl.num_programs(1) - 1)
    def _():
        buf_ref[0] = u[tb - (CONV_W - 1):tb, :]


def _conv_prompt(pc, conv_w, batch, seq, tb):
    d = conv_w.shape[1]
    n_t = seq // tb
    blk = lambda kk: pl.BlockSpec((tb, d), lambda b, t, kk=kk: (b * n_t + t, kk))
    return pl.pallas_call(
        functools.partial(_conv_prompt_body, tb),
        grid=(batch, n_t),
        in_specs=[blk(0), blk(1), blk(2), pl.BlockSpec(conv_w.shape, lambda b, t: (0, 0))],
        out_specs=[pl.BlockSpec((tb, d), lambda b, t: (b * n_t + t, 0)),
                   pl.BlockSpec((1, CONV_W - 1, d), lambda b, t: (b, 0, 0))],
        out_shape=[jax.ShapeDtypeStruct((batch * seq, d), BF16),
                   jax.ShapeDtypeStruct((batch, CONV_W - 1, d), F32)],
        scratch_shapes=[pltpu.VMEM((_CARRY + tb, d), F32)],
        compiler_params=_params(2),
        name="conv_prompt",
    )(pc, pc, pc, conv_w)


def _conv_sample_body(d, b_ref, c_ref, h_ref, w_ref, st_ref, z_ref, buf_ref):
    u = c_ref[...].astype(F32) * h_ref[...].astype(F32)
    w = w_ref[...]
    buf0 = st_ref[:, 0:d]
    buf1 = st_ref[:, d:2 * d]
    y = w[0:1, :] * buf0
    y = y + w[1:2, :] * buf1
    y = y + w[2:3, :] * u
    z_ref[...] = (b_ref[...].astype(F32) * y).astype(BF16)
    buf_ref[:, 0:d] = buf1
    buf_ref[:, d:2 * d] = u


def _conv_sample(pc, conv_w, state2d, row0):
    nb = state2d.shape[0]
    d = conv_w.shape[1]
    rb0 = row0 // nb
    blk = lambda kk: pl.BlockSpec((nb, d), lambda i, kk=kk: (rb0, kk))
    return pl.pallas_call(
        functools.partial(_conv_sample_body, d),
        grid=(1,),
        in_specs=[blk(0), blk(1), blk(2), pl.BlockSpec(conv_w.shape, lambda i: (0, 0)),
                  pl.BlockSpec(state2d.shape, lambda i: (0, 0))],
        out_specs=[pl.BlockSpec((nb, d), lambda i: (0, 0)),
                   pl.BlockSpec(state2d.shape, lambda i: (0, 0))],
        out_shape=[jax.ShapeDtypeStruct((nb, d), BF16),
                   jax.ShapeDtypeStruct(state2d.shape, F32)],
        compiler_params=_params(1),
        name="conv_sample",
    )(pc, pc, pc, conv_w, state2d)


def _outproj_body(x_ref, z_ref, w_ref, o_ref, w16_ref):
    @pl.when(pl.program_id(0) == 0)
    def _():
        w16_ref[...] = w_ref[...].astype(BF16)

    o_ref[...] = x_ref[...] + jnp.dot(z_ref[...], w16_ref[...], preferred_element_type=F32)


def _outproj(x, z, w, tm):
    n, d = x.shape
    return pl.pallas_call(
        _outproj_body,
        grid=(n // tm,),
        in_specs=[pl.BlockSpec((tm, d), lambda i: (i, 0)),
                  pl.BlockSpec((tm, d), lambda i: (i, 0)),
                  pl.BlockSpec((d, d), lambda i: (0, 0))],
        out_specs=pl.BlockSpec((tm, d), lambda i: (i, 0)),
        out_shape=jax.ShapeDtypeStruct((n, d), F32),
        scratch_shapes=[pltpu.VMEM((d, d), BF16)],
        compiler_params=_params(1),
        name="outproj",
    )(x, z, w)


def _swiglu_partial(xn, w1, w3, w2):
    h1 = jnp.dot(xn, w1.astype(BF16), preferred_element_type=F32)
    h3 = jnp.dot(xn, w3.astype(BF16), preferred_element_type=F32)
    h = (jax.nn.silu(h1) * h3).astype(BF16)
    return jnp.dot(h, w2.astype(BF16), preferred_element_type=F32)


def _ffn_body(x_ref, g_ref, w1_ref, w3_ref, w2_ref, o_ref, xn_ref, acc_ref):
    f = pl.program_id(1)

    @pl.when(f == 0)
    def _():
        xn_ref[...] = _rmsnorm_bf16(x_ref[...], g_ref[...])
        acc_ref[...] = jnp.zeros_like(acc_ref)

    acc_ref[...] += _swiglu_partial(xn_ref[...], w1_ref[...], w3_ref[...], w2_ref[...])

    @pl.when(f == pl.num_programs(1) - 1)
    def _():
        o_ref[...] = x_ref[...] + acc_ref[...]


def _ffn(x, g, w1, w3, w2, tm, tf):
    n, d = x.shape
    dff = w1.shape[1]
    return pl.pallas_call(
        _ffn_body,
        grid=(n // tm, dff // tf),
        in_specs=[pl.BlockSpec((tm, d), lambda i, f: (i, 0)),
                  pl.BlockSpec((1, d), lambda i, f: (0, 0)),
                  pl.BlockSpec((d, tf), lambda i, f: (0, f)),
                  pl.BlockSpec((d, tf), lambda i, f: (0, f)),
                  pl.BlockSpec((tf, d), lambda i, f: (f, 0))],
        out_specs=pl.BlockSpec((tm, d), lambda i, f: (i, 0)),
        out_shape=jax.ShapeDtypeStruct((n, d), F32),
        scratch_shapes=[pltpu.VMEM((tm, d), BF16), pltpu.VMEM((tm, d), F32)],
        compiler_params=_params(2),
        name="ffn",
    )(x, g.reshape(1, d), w1, w3, w2)


ROUTE_ROWS = 8


def _router_body(n_experts, x_ref, g_ref, rw_ref, route_ref, cnt_ref, cnt_scr):
    i = pl.program_id(0)
    tm = x_ref.shape[0]

    @pl.when(i == 0)
    def _():
        cnt_scr[...] = jnp.zeros_like(cnt_scr)

    xn = _rmsnorm_bf16(x_ref[...], g_ref[...])
    logits = jnp.dot(xn, rw_ref[...].astype(BF16), preferred_element_type=F32)
    lane = lax.broadcasted_iota(I32, (tm, LANES), 1).astype(F32)
    neg = jnp.float32(-jnp.inf)
    logits = jnp.where(lane < n_experts, logits, neg)
    m1 = jnp.max(logits, axis=-1, keepdims=True)
    i1 = jnp.min(jnp.where(logits == m1, lane, float(LANES)), axis=-1, keepdims=True)
    i1 = jnp.minimum(i1, float(n_experts - 1))
    rest = jnp.where(lane == i1, neg, logits)
    m2 = jnp.max(rest, axis=-1, keepdims=True)
    i2 = jnp.min(jnp.where(rest == m2, lane, float(LANES)), axis=-1, keepdims=True)
    i2 = jnp.minimum(i2, float(n_experts - 1))
    e2 = jnp.exp(m2 - m1)
    den = 1.0 + e2
    g1 = 1.0 / den
    g2 = e2 / den

    sel1 = lane == i1
    sel2 = lane == i2
    onehot = jnp.logical_or(sel1, sel2)
    row = lax.broadcasted_iota(I32, (tm, tm), 0)
    col = lax.broadcasted_iota(I32, (tm, tm), 1)
    before = (row > col).astype(BF16)
    rank = jnp.dot(before, onehot.astype(BF16), preferred_element_type=F32) + cnt_scr[...]
    r1 = jnp.sum(jnp.where(sel1, rank, 0.0), axis=-1, keepdims=True)
    r2 = jnp.sum(jnp.where(sel2, rank, 0.0), axis=-1, keepdims=True)
    cnt_scr[...] += jnp.sum(onehot.astype(F32), axis=0, keepdims=True)

    out = jnp.zeros((tm, LANES), F32)
    for slot, val in enumerate((i1, i2, g1, g2, r1, r2)):
        out = jnp.where(lane == slot, val, out)
    route_ref[...] = out.T[0:ROUTE_ROWS, :]
    cnt_ref[...] = cnt_scr[...]


def _router(x, g, rw_pad, n_experts, tm):
    n, d = x.shape
    return pl.pallas_call(
        functools.partial(_router_body, n_experts),
        grid=(n // tm,),
        in_specs=[pl.BlockSpec((tm, d), lambda i: (i, 0)),
                  pl.BlockSpec((1, d), lambda i: (0, 0)),
                  pl.BlockSpec((d, LANES), lambda i: (0, 0))],
        out_specs=[pl.BlockSpec((ROUTE_ROWS, tm), lambda i: (0, i)),
                   pl.BlockSpec((1, LANES), lambda i: (0, 0))],
        out_shape=[jax.ShapeDtypeStruct((ROUTE_ROWS, n), F32),
                   jax.ShapeDtypeStruct((1, LANES), F32)],
        scratch_shapes=[pltpu.VMEM((1, LANES), F32)],
        compiler_params=_params(1),
        name="router",
    )(x, g.reshape(1, d), rw_pad)


SUBLANES = 8
MOE_TM = 448


def _start_row_group(src_hbm, idx_ref, g, dst, sem):
    for k in range(SUBLANES):
        pltpu.make_async_copy(src_hbm.at[pl.ds(idx_ref[0, 0, g * SUBLANES + k], 1)],
                              dst.at[g, pl.ds(k, 1)], sem).start()


def _start_row_gather(src_hbm, idx_ref, n_rows, dst, sem):
    def body(g, carry):
        _start_row_group(src_hbm, idx_ref, g, dst, sem)
        return carry

    lax.fori_loop(0, n_rows // SUBLANES, body, 0)


def _wait_row_gather(src_hbm, n_rows, dst, sem):
    pltpu.make_async_copy(src_hbm.at[pl.ds(0, n_rows)], dst.reshape(n_rows, dst.shape[-1]), sem).wait()


def _moe_body(n_f, tf, te_ref, nu_ref, idx_cur, idx_nxt, x_hbm, gate_ref, g_ref, w1_hbm, w3_hbm, w2_hbm,
              y_ref, xbuf, sem, xn_ref, acc_ref, c1, c3, c2, s1, s3, s2, wsem):
    i = pl.program_id(0)
    f = pl.program_id(1)
    n_tiles = pl.num_programs(0)
    tm = xbuf.shape[1] * SUBLANES
    rows_per_step = tm // n_f
    slot = lax.rem(i, 2)
    n_used = nu_ref[0]
    used = i < n_used
    e = te_ref[i]
    first_of_expert = jnp.logical_or(i == 0, e != te_ref[jnp.maximum(i - 1, 0)])

    @pl.when(f == 0)
    def _():
        @pl.when(i == 0)
        def _():
            _start_row_gather(x_hbm, idx_cur, tm, xbuf.at[0], sem.at[0])

        @pl.when(i <= n_used)
        def _():
            _wait_row_gather(x_hbm, tm, xbuf.at[slot], sem.at[slot])

        xn_ref[...] = _rmsnorm_bf16(xbuf[slot].reshape(tm, xbuf.shape[-1]), g_ref[...])
        acc_ref[...] = jnp.zeros_like(acc_ref)

    def chunk_copies(ff, ws):
        cols = pl.ds(pl.multiple_of(ff * tf, tf), tf)
        return (pltpu.make_async_copy(w1_hbm.at[e, :, cols], s1.at[ws], wsem.at[ws, 0]),
                pltpu.make_async_copy(w3_hbm.at[e, :, cols], s3.at[ws], wsem.at[ws, 1]),
                pltpu.make_async_copy(w2_hbm.at[e, cols, :], s2.at[ws], wsem.at[ws, 2]))

    @pl.when(jnp.logical_and(used, first_of_expert))
    def _():
        ws = lax.rem(f, 2)

        @pl.when(f == 0)
        def _():
            for c in chunk_copies(0, 0):
                c.start()

        @pl.when(f + 1 < n_f)
        def _():
            for c in chunk_copies(f + 1, 1 - ws):
                c.start()

        for c in chunk_copies(f, ws):
            c.wait()
        c1[f] = s1[ws].astype(BF16)
        c3[f] = s3[ws].astype(BF16)
        c2[f] = s2[ws].astype(BF16)

    @pl.when(used)
    def _():
        for g in range(rows_per_step // SUBLANES):
            _start_row_group(x_hbm, idx_nxt, f * (rows_per_step // SUBLANES) + g, xbuf.at[1 - slot], sem.at[1 - slot])
        acc_ref[...] += _swiglu_partial(xn_ref[...], c1[f], c3[f], c2[f])

    @pl.when(f == n_f - 1)
    def _():
        r_id = lax.broadcasted_iota(I32, (tm, tm), 0)
        c_id = lax.broadcasted_iota(I32, (tm, tm), 1)
        gate_col = jnp.sum(jnp.where(r_id == c_id, gate_ref[0], 0.0), axis=1, keepdims=True)
        y_ref[...] = acc_ref[...] * gate_col

        @pl.when(jnp.logical_and(used, i == n_tiles - 1))
        def _():
            _wait_row_gather(x_hbm, tm, xbuf.at[1 - slot], sem.at[1 - slot])


def _moe(x, g, w1, w3, w2, tile_expert, n_used, inv3, gate_sorted, tm, tf):
    n_tiles = inv3.shape[0]
    d = x.shape[1]
    dff = w1.shape[2]
    n_f = dff // tf
    assert tm % (n_f * SUBLANES) == 0, "each hidden chunk fetches whole 8-row tiles of the next row tile"
    grid_spec = pltpu.PrefetchScalarGridSpec(
        num_scalar_prefetch=2,
        grid=(n_tiles, n_f),
        in_specs=[pl.BlockSpec((1, 1, tm), lambda i, f, te, nu: (i, 0, 0), memory_space=pltpu.SMEM),
                  pl.BlockSpec((1, 1, tm), lambda i, f, te, nu: (jnp.minimum(i + 1, n_tiles - 1), 0, 0),
                               memory_space=pltpu.SMEM),
                  pl.BlockSpec(memory_space=pl.ANY),
                  pl.BlockSpec((1, 1, tm), lambda i, f, te, nu: (i, 0, 0)),
                  pl.BlockSpec((1, d), lambda i, f, te, nu: (0, 0)),
                  pl.BlockSpec(memory_space=pl.ANY),
                  pl.BlockSpec(memory_space=pl.ANY),
                  pl.BlockSpec(memory_space=pl.ANY)],
        out_specs=pl.BlockSpec((tm, d), lambda i, f, te, nu: (i, 0)),
        scratch_shapes=[pltpu.VMEM((2, tm // SUBLANES, SUBLANES, d), F32),
                        pltpu.SemaphoreType.DMA((2,)),
                        pltpu.VMEM((tm, d), BF16),
                        pltpu.VMEM((tm, d), F32),
                        pltpu.VMEM((n_f, d, tf), BF16),
                        pltpu.VMEM((n_f, d, tf), BF16),
                        pltpu.VMEM((n_f, tf, d), BF16),
                        pltpu.VMEM((2, d, tf), F32),
                        pltpu.VMEM((2, d, tf), F32),
                        pltpu.VMEM((2, tf, d), F32),
                        pltpu.SemaphoreType.DMA((2, 3))],
    )
    return pl.pallas_call(
        functools.partial(_moe_body, n_f, tf),
        grid_spec=grid_spec,
        out_shape=jax.ShapeDtypeStruct((n_tiles * tm, d), F32),
        compiler_params=_params(2),
        name="moe",
    )(tile_expert, n_used, inv3, inv3, x, gate_sorted, g.reshape(1, d), w1, w3, w2)


def _combine_body(pos_cur, pos_nxt, x_ref, g_ref, y_hbm, o_ref, ybuf, sem):
    i = pl.program_id(0)
    n_tiles = pl.num_programs(0)
    rows = ybuf.shape[1] * SUBLANES
    slot = lax.rem(i, 2)

    @pl.when(i == 0)
    def _():
        _start_row_gather(y_hbm, pos_cur, rows, ybuf.at[0], sem.at[0])

    @pl.when(i + 1 < n_tiles)
    def _():
        _start_row_gather(y_hbm, pos_nxt, rows, ybuf.at[1 - slot], sem.at[1 - slot])

    _wait_row_gather(y_hbm, rows, ybuf.at[slot], sem.at[slot])
    tc = rows // TOP_K
    y = ybuf[slot].reshape(rows, ybuf.shape[-1])
    moe = y[0:tc, :] + y[tc:rows, :]
    x = x_ref[...] + moe
    ms = jnp.mean(x * x, axis=-1, keepdims=True)
    o_ref[...] = x * lax.rsqrt(ms + EPS) * g_ref[...]


def _combine(x, g, y_sorted, pos3, row0, n_rows, tc):
    d = x.shape[1]
    n_tiles = n_rows // tc
    rb0 = row0 // tc
    grid_spec = pltpu.PrefetchScalarGridSpec(
        num_scalar_prefetch=0,
        grid=(n_tiles,),
        in_specs=[pl.BlockSpec((1, 1, TOP_K * tc), lambda i: (i, 0, 0), memory_space=pltpu.SMEM),
                  pl.BlockSpec((1, 1, TOP_K * tc), lambda i: (jnp.minimum(i + 1, n_tiles - 1), 0, 0),
                               memory_space=pltpu.SMEM),
                  pl.BlockSpec((tc, d), lambda i: (rb0 + i, 0)),
                  pl.BlockSpec((1, d), lambda i: (0, 0)),
                  pl.BlockSpec(memory_space=pl.ANY)],
        out_specs=pl.BlockSpec((tc, d), lambda i: (i, 0)),
        scratch_shapes=[pltpu.VMEM((2, TOP_K * tc // SUBLANES, SUBLANES, d), F32), pltpu.SemaphoreType.DMA((2,))],
    )
    return pl.pallas_call(
        _combine_body,
        grid_spec=grid_spec,
        out_shape=jax.ShapeDtypeStruct((n_rows, d), F32),
        compiler_params=_params(1),
        name="combine",
    )(pos3, pos3, x, g.reshape(1, d), y_sorted)


def _lookup(table, idx):
    out = jnp.zeros(idx.shape, table.dtype)
    for e in range(table.shape[0]):
        out = jnp.where(idx == e, table[e], out)
    return out


def _expert_layer(x, g_ffn, g_final, router_w, w1, w3, w2, n_prompt, n_dec, tm_router, tm_moe, tf, tc):
    n = n_prompt + n_dec
    n_experts = router_w.shape[-1]
    rw_pad = jnp.pad(router_w, ((0, 0), (0, LANES - n_experts)))
    route, cnt = _router(x, g_ffn, rw_pad, n_experts, tm_router)
    ids = route[0:TOP_K].astype(I32)
    gates = route[TOP_K:2 * TOP_K]
    ranks = route[2 * TOP_K:3 * TOP_K].astype(I32)
    counts = cnt[0, :n_experts].astype(I32)

    n_tiles = (TOP_K * n + n_experts * (tm_moe - 1)) // tm_moe
    padded = ((counts + tm_moe - 1) // tm_moe) * tm_moe
    ends = jnp.cumsum(padded)
    starts = ends - padded
    cstart = jnp.cumsum(counts) - counts
    pos = _lookup(starts, ids) + ranks
    keys = ids * n + jnp.arange(n, dtype=I32)[None, :]
    order = jnp.argsort(keys.reshape(-1)).astype(I32)
    slot_pos = jnp.arange(n_tiles * tm_moe, dtype=I32)
    slot_e = jnp.minimum(jnp.sum(slot_pos[None, :] >= ends[:, None], axis=0), n_experts - 1).astype(I32)
    slot_rank = slot_pos - _lookup(starts, slot_e)
    slot_valid = slot_rank < _lookup(counts, slot_e)
    slot_asg = order[jnp.clip(_lookup(cstart, slot_e) + slot_rank, 0, TOP_K * n - 1)]
    inv = jnp.where(slot_valid, slot_asg % n, 0).astype(I32)
    gate_sorted = jnp.where(slot_valid, gates.reshape(-1)[slot_asg], 0.0)
    tile_expert = slot_e[::tm_moe]
    n_used = (ends[-1] // tm_moe).astype(I32).reshape(1)

    y_sorted = _moe(x, g_ffn, w1, w3, w2, tile_expert, n_used, inv.reshape(n_tiles, 1, tm_moe),
                    gate_sorted.reshape(n_tiles, 1, tm_moe), tm_moe, tf)

    def tile_positions(row0, n_rows, t):
        p = pos[:, row0:row0 + n_rows].reshape(TOP_K, n_rows // t, t)
        return jnp.swapaxes(p, 0, 1).reshape(n_rows // t, 1, TOP_K * t)

    y_prompt = _combine(x, g_final, y_sorted, tile_positions(0, n_prompt, tc), 0, n_prompt, tc)
    y_sample = _combine(x, g_final, y_sorted, tile_positions(n_prompt, n_dec, n_dec), n_prompt, n_dec, n_dec)
    return y_prompt, y_sample


def kernel(x_prompt, x_sample, state_hgrn, state_conv, norm_mix, norm_ffn, norm_final, hgrn_w_in, hgrn_lb, hgrn_g_norm, hgrn_w_out, conv_w_in, conv_w, conv_w_out, ffn_w1, ffn_w3, ffn_w2, moe_router, moe_w1, moe_w3, moe_w2):
    batch, seq, d = x_prompt.shape
    n_dec = x_sample.shape[0]
    n_prompt = batch * seq
    n = n_prompt + n_dec
    n_experts = moe_router.shape[-1]
    assert x_sample.shape[1] == 1 and d % HEAD == 0 and seq % CHUNK == 0
    assert norm_mix.shape[0] == 2, "one HGRN2 layer followed by one short-conv layer"

    tm = _pick_tile(n, (688, 384, 128, 16))
    tm_proj = _pick_tile(n, (1376, 688, 384, 128, 16))
    tb = _pick_tile(seq, (256, 128, 64, 32))
    tg = _pick_tile(n_dec, (16,))
    tf = _pick_tile(ffn_w1.shape[-1], (512, 256, 128))
    tm_moe = _pick_tile(n_prompt, (512, 128, 16))
    tm_moe = MOE_TM if n_prompt % 128 == 0 and ffn_w1.shape[-1] % (7 * 512) == 0 else tm_moe
    tc = _pick_tile(n_prompt, (512, 128, 16))
    tm_router = 3 * LANES if n % (3 * LANES) == 0 else n
    assert n_prompt % tg == 0 and n_prompt % n_dec == 0 and n_prompt % tc == 0

    x = jnp.concatenate([x_prompt.reshape(n_prompt, d), x_sample.reshape(n_dec, d)], axis=0)

    p16, p32 = _proj(x, norm_mix[0], hgrn_w_in[0], tm_proj, f32_col=1)
    o_p, s_prompt = _hgrn_prompt(p16, p32, hgrn_lb, hgrn_g_norm[0], 0, batch, seq, tb)
    o_s, s_sample = _hgrn_sample(p16, p32, hgrn_lb, hgrn_g_norm[0], state_hgrn, 0, n_prompt, tg)
    x = _outproj(x, jnp.concatenate([o_p, o_s], axis=0), hgrn_w_out[0], tm_proj)
    x = _ffn(x, norm_ffn[0], ffn_w1[0], ffn_w3[0], ffn_w2[0], tm, tf)

    (pc,) = _proj(x, norm_mix[1], conv_w_in[0], tm_proj)
    z_p, c_prompt = _conv_prompt(pc, conv_w[0], batch, seq, tb)
    z_s, c_sample = _conv_sample(pc, conv_w[0], state_conv.reshape(n_dec, (CONV_W - 1) * d), n_prompt)
    x = _outproj(x, jnp.concatenate([z_p, z_s], axis=0), conv_w_out[0], tm_proj)

    y_prompt, y_sample = _expert_layer(x, norm_ffn[1], norm_final, moe_router[0], moe_w1[0], moe_w3[0], moe_w2[0],
                                       n_prompt, n_dec, tm_router, tm_moe, tf, tc)

    return (y_prompt.reshape(batch, seq, d),
            y_sample.reshape(n_dec, 1, d),
            s_prompt,
            s_sample,
            c_prompt.reshape(batch, 1, CONV_W - 1, d),
            c_sample.reshape(n_dec, 1, CONV_W - 1, d))
```

```python
import functools

import jax
import jax.numpy as jnp
from jax import lax
from jax.experimental import pallas as pl
from jax.experimental.pallas import tpu as pltpu

F32 = jnp.float32
BF16 = jnp.bfloat16
I32 = jnp.int32

EPS = 1e-6
HEAD = 128
CHUNK = 32
CONV_W = 3
TOP_K = 2
LANES = 128
V7X_VMEM_LIMIT = 56 * 1024 * 1024

ARB = "arbitrary"


def _params(n_axes):
    return pltpu.CompilerParams(dimension_semantics=(ARB,) * n_axes, vmem_limit_bytes=V7X_VMEM_LIMIT)


def _pick_tile(n, candidates):
    for c in candidates:
        if n % c == 0:
            return c
    raise ValueError(f"no tile in {candidates} divides {n}")


def _rmsnorm_bf16(x, g):
    ms = jnp.mean(x * x, axis=-1, keepdims=True)
    return (x * lax.rsqrt(ms + EPS) * g).astype(BF16)


def _proj_body(f32_col, x_ref, g_ref, w_hbm, *refs):
    o16_ref = refs[0]
    xn_ref, w16_ref, stage_ref, wsem = refs[-4:]
    i = pl.program_id(0)
    j = pl.program_id(1)
    d = x_ref.shape[1]

    @pl.when(j == 0)
    def _():
        xn_ref[...] = _rmsnorm_bf16(x_ref[...], g_ref[...])

    @pl.when(i == 0)
    def _():
        cp = pltpu.make_async_copy(w_hbm.at[:, pl.ds(pl.multiple_of(j * d, d), d)], stage_ref, wsem)
        cp.start()
        cp.wait()
        w16_ref[j] = stage_ref[...].astype(BF16)

    def result():
        return jnp.dot(xn_ref[...], w16_ref[j], preferred_element_type=F32)

    if f32_col is None:
        o16_ref[...] = result().astype(BF16)
    else:
        o32_ref = refs[1]

        @pl.when(j == f32_col)
        def _():
            o32_ref[...] = result()

        @pl.when(j != f32_col)
        def _():
            o16_ref[...] = result().astype(BF16)


def _proj(x, g, w, tm, f32_col=None):
    n, d = x.shape
    n_col = w.shape[1] // d
    if f32_col is None:
        col16 = lambda i, j: (i, j)
        n16 = n_col
    else:
        assert 0 < f32_col < n_col
        col16 = lambda i, j: (i, jnp.where(j >= f32_col, j - 1, j))
        n16 = n_col - 1
    out_specs = [pl.BlockSpec((tm, d), col16)]
    out_shape = [jax.ShapeDtypeStruct((n, n16 * d), BF16)]
    if f32_col is not None:
        out_specs.append(pl.BlockSpec((tm, d), lambda i, j: (i, 0)))
        out_shape.append(jax.ShapeDtypeStruct((n, d), F32))
    return pl.pallas_call(
        functools.partial(_proj_body, f32_col),
        grid=(n // tm, n_col),
        in_specs=[pl.BlockSpec((tm, d), lambda i, j: (i, 0)),
                  pl.BlockSpec((1, d), lambda i, j: (0, 0)),
                  pl.BlockSpec(memory_space=pl.ANY)],
        out_specs=out_specs,
        out_shape=out_shape,
        scratch_shapes=[pltpu.VMEM((tm, d), BF16), pltpu.VMEM((n_col, d, d), BF16),
                        pltpu.VMEM((d, d), F32), pltpu.SemaphoreType.DMA(())],
        compiler_params=_params(2),
        name="proj",
    )(x, g.reshape(1, d), w)


def _forget_lower_bound(lb_ref, layer):
    lbw = lb_ref[...]
    e = jnp.exp(lbw - jnp.max(lbw, axis=0, keepdims=True))
    sm = e / jnp.sum(e, axis=0, keepdims=True)
    return jnp.sum(sm[:layer + 1], axis=0, keepdims=True)


def _head_rmsnorm(o):
    return o * lax.rsqrt(jnp.mean(o * o, axis=-1, keepdims=True) + EPS)


def _sigmoid(x):
    return 0.5 * (jnp.tanh(0.5 * x) + 1.0)


def _silu(x):
    return x * _sigmoid(x)


def _split3_bf16(x):
    hi = x.astype(BF16)
    r1 = x - hi.astype(F32)
    mid = r1.astype(BF16)
    lo = (r1 - mid.astype(F32)).astype(BF16)
    return hi, mid, lo


def _hgrn_prompt_body(layer, n_heads, tb, q_ref, f_ref, v_ref, gt_ref, lb_ref, gn_ref,
                      o_ref, s_ref, st_scr, qin_scr, kin_scr, kdec_scr, g_scr, sprev_scr):
    t = pl.program_id(1)
    nc = tb // CHUNK
    d = n_heads * HEAD

    @pl.when(t == 0)
    def _():
        st_scr[...] = jnp.zeros_like(st_scr)

    lb = _forget_lower_bound(lb_ref, layer)
    gn = gn_ref[...]
    row = lax.broadcasted_iota(I32, (tb, tb), 0)
    col = lax.broadcasted_iota(I32, (tb, tb), 1)
    same_chunk = (row // CHUNK) == (col // CHUNK)
    causal = jnp.logical_and(same_chunk, row >= col)

    f = lb + (1.0 - lb) * _sigmoid(f_ref[...])
    logf = jnp.log(f)
    k = 1.0 - f
    tri = causal.astype(BF16)
    G = None
    for part in _split3_bf16(logf):
        term = jnp.dot(tri, part, preferred_element_type=F32)
        G = term if G is None else G + term
    g_scr[...] = G
    g_last = [g_scr[(c + 1) * CHUNK - 1:(c + 1) * CHUNK, :] for c in range(nc)]
    decay = [jnp.exp(g) for g in g_last]
    decay_rows = jnp.concatenate([jnp.broadcast_to(dc, (CHUNK, d)) for dc in decay], axis=0)
    k_in = k * jnp.exp(-G)
    qin_scr[...] = (_silu(q_ref[...].astype(F32)) * jnp.exp(G)).astype(BF16)
    kin_scr[...] = k_in.astype(BF16)
    kdec_scr[...] = (k_in * decay_rows).astype(BF16)

    lane_chunk = lax.broadcasted_iota(I32, (HEAD, tb), 1) // CHUNK
    row_chunk = lax.broadcasted_iota(I32, (tb, HEAD), 0) // CHUNK
    zero16 = jnp.zeros((), BF16)
    grp = 2 if nc % 2 == 0 else 1

    for h in range(n_heads):
        hs = slice(h * HEAD, (h + 1) * HEAD)
        qh = qin_scr[:, hs]
        kd = kdec_scr[:, hs]
        vh = v_ref[:, hs]
        a = lax.dot_general(qh, kin_scr[:, hs], (((1,), (1,)), ((), ())), preferred_element_type=F32)
        a = jnp.where(causal, a, 0.0).astype(BF16)
        o = jnp.dot(a, vh, preferred_element_type=F32)
        v_t = vh.T
        v_blocks = jnp.concatenate([jnp.where(lane_chunk == c, v_t, zero16) for c in range(nc)], axis=0)
        ut = jnp.dot(v_blocks, kd, preferred_element_type=F32)
        st = st_scr[h]
        for c in range(nc):
            sprev_scr[h, :, c * HEAD:(c + 1) * HEAD] = st.astype(BF16)
            st = st * decay[c][:, hs] + ut[c * HEAD:(c + 1) * HEAD, :]
        st_scr[h] = st
        inter = []
        for g0 in range(0, nc, grp):
            rows = slice(g0 * CHUNK, (g0 + grp) * CHUNK)
            q_blocks = jnp.concatenate(
                [jnp.where(row_chunk[rows] == g0 + j, qh[rows], zero16) for j in range(grp)], axis=1)
            inter.append(lax.dot_general(q_blocks, sprev_scr[h, :, g0 * HEAD:(g0 + grp) * HEAD],
                                         (((1,), (1,)), ((), ())), preferred_element_type=F32))
        o = o + jnp.concatenate(inter, axis=0)
        o = _head_rmsnorm(o) * gn[:, hs] * _silu(gt_ref[:, hs].astype(F32))
        o_ref[:, hs] = o.astype(BF16)

    @pl.when(t == pl.num_programs(1) - 1)
    def _():
        for h in range(n_heads):
            s_ref[0, 0, h] = st_scr[h].T


def _hgrn_prompt(p16, p32, hgrn_lb, g_norm, layer, batch, seq, tb):
    d = g_norm.shape[0]
    n_heads = d // HEAD
    n_t = seq // tb
    blk = lambda kk: pl.BlockSpec((tb, d), lambda b, t, kk=kk: (b * n_t + t, kk))
    return pl.pallas_call(
        functools.partial(_hgrn_prompt_body, layer, n_heads, tb),
        grid=(batch, n_t),
        in_specs=[blk(0), blk(0), blk(1), blk(2),
                  pl.BlockSpec(hgrn_lb.shape, lambda b, t: (0, 0)),
                  pl.BlockSpec((1, d), lambda b, t: (0, 0))],
        out_specs=[pl.BlockSpec((tb, d), lambda b, t: (b * n_t + t, 0)),
                   pl.BlockSpec((1, 1, n_heads, HEAD, HEAD), lambda b, t: (b, 0, 0, 0, 0))],
        out_shape=[jax.ShapeDtypeStruct((batch * seq, d), BF16),
                   jax.ShapeDtypeStruct((batch, 1, n_heads, HEAD, HEAD), F32)],
        scratch_shapes=[pltpu.VMEM((n_heads, HEAD, HEAD), F32),
                        pltpu.VMEM((tb, d), BF16), pltpu.VMEM((tb, d), BF16), pltpu.VMEM((tb, d), BF16),
                        pltpu.VMEM((tb, d), F32),
                        pltpu.VMEM((n_heads, HEAD, (tb // CHUNK) * HEAD), BF16)],
        compiler_params=_params(2),
        name="hgrn_prompt",
    )(p16, p32, p16, p16, hgrn_lb, g_norm.reshape(1, d))


def _hgrn_sample_body(layer, tg, q_ref, f_ref, v_ref, gt_ref, lb_ref, gn_ref, s0_ref,
                      o_ref, s_ref, o_scr):
    lb_h = _forget_lower_bound(lb_ref, layer)
    gn_h = gn_ref[...]

    q = jax.nn.silu(q_ref[...].astype(F32))
    f = lb_h + (1.0 - lb_h) * jax.nn.sigmoid(f_ref[...])
    decay = f
    k = 1.0 - f
    v = v_ref[...].astype(F32)
    q_t, d_t, k_t = q.T, decay.T, k.T
    for j in range(tg):
        s = d_t[:, j:j + 1] * s0_ref[j, 0, 0] + k_t[:, j:j + 1] * v[j:j + 1, :]
        s_ref[j, 0, 0] = s
        o_scr[j:j + 1, :] = jnp.sum(q_t[:, j:j + 1] * s, axis=0, keepdims=True)
    o = _head_rmsnorm(o_scr[...]) * gn_h * jax.nn.silu(gt_ref[...].astype(F32))
    o_ref[...] = o.astype(BF16)


def _hgrn_sample(p16, p32, hgrn_lb, g_norm, state, layer, row0, tg):
    nb = state.shape[0]
    d = g_norm.shape[0]
    n_heads = d // HEAD
    rb0 = row0 // tg
    blk = lambda kk: pl.BlockSpec((tg, HEAD), lambda g, h, kk=kk: (rb0 + g, kk * n_heads + h))
    st_spec = pl.BlockSpec((tg, 1, 1, HEAD, HEAD), lambda g, h: (g, 0, h, 0, 0))
    return pl.pallas_call(
        functools.partial(_hgrn_sample_body, layer, tg),
        grid=(nb // tg, n_heads),
        in_specs=[blk(0), blk(0), blk(1), blk(2),
                  pl.BlockSpec((hgrn_lb.shape[0], HEAD), lambda g, h: (0, h)),
                  pl.BlockSpec((1, HEAD), lambda g, h: (0, h)),
                  st_spec],
        out_specs=[pl.BlockSpec((tg, HEAD), lambda g, h: (g, h)), st_spec],
        out_shape=[jax.ShapeDtypeStruct((nb, d), BF16),
                   jax.ShapeDtypeStruct(state.shape, F32)],
        scratch_shapes=[pltpu.VMEM((tg, HEAD), F32)],
        compiler_params=_params(2),
        name="hgrn_sample",
    )(p16, p32, p16, p16, hgrn_lb, g_norm.reshape(1, d), state)


_CARRY = 8


def _conv_prompt_body(tb, b_ref, c_ref, h_ref, w_ref, z_ref, buf_ref, u_scr):
    t = pl.program_id(1)

    @pl.when(t == 0)
    def _():
        u_scr[0:_CARRY, :] = jnp.zeros((_CARRY, u_scr.shape[1]), F32)

    u = c_ref[...].astype(F32) * h_ref[...].astype(F32)
    u_scr[_CARRY:_CARRY + tb, :] = u
    w = w_ref[...]
    y = w[0:1, :] * u_scr[_CARRY - 2:_CARRY - 2 + tb, :]
    y = y + w[1:2, :] * u_scr[_CARRY - 1:_CARRY - 1 + tb, :]
    y = y + w[2:3, :] * u
    z_ref[...] = (b_ref[...].astype(F32) * y).astype(BF16)
    u_scr[0:_CARRY, :] = u[tb - _CARRY:tb, :]

    @pl.when(t == pl.num_programs(1) - 1)
    def _():
        buf_ref[0] = u[tb - (CONV_W - 1):tb, :]


def _conv_prompt(pc, conv_w, batch, seq, tb):
    d = conv_w.shape[1]
    n_t = seq // tb
    blk = lambda kk: pl.BlockSpec((tb, d), lambda b, t, kk=kk: (b * n_t + t, kk))
    return pl.pallas_call(
        functools.partial(_conv_prompt_body, tb),
        grid=(batch, n_t),
        in_specs=[blk(0), blk(1), blk(2), pl.BlockSpec(conv_w.shape, lambda b, t: (0, 0))],
        out_specs=[pl.BlockSpec((tb, d), lambda b, t: (b * n_t + t, 0)),
                   pl.BlockSpec((1, CONV_W - 1, d), lambda b, t: (b, 0, 0))],
        out_shape=[jax.ShapeDtypeStruct((batch * seq, d), BF16),
                   jax.ShapeDtypeStruct((batch, CONV_W - 1, d), F32)],
        scratch_shapes=[pltpu.VMEM((_CARRY + tb, d), F32)],
        compiler_params=_params(2),
        name="conv_prompt",
    )(pc, pc, pc, conv_w)


def _conv_sample_body(d, b_ref, c_ref, h_ref, w_ref, st_ref, z_ref, buf_ref):
    u = c_ref[...].astype(F32) * h_ref[...].astype(F32)
    w = w_ref[...]
    buf0 = st_ref[:, 0:d]
    buf1 = st_ref[:, d:2 * d]
    y = w[0:1, :] * buf0
    y = y + w[1:2, :] * buf1
    y = y + w[2:3, :] * u
    z_ref[...] = (b_ref[...].astype(F32) * y).astype(BF16)
    buf_ref[:, 0:d] = buf1
    buf_ref[:, d:2 * d] = u


def _conv_sample(pc, conv_w, state2d, row0):
    nb = state2d.shape[0]
    d = conv_w.shape[1]
    rb0 = row0 // nb
    blk = lambda kk: pl.BlockSpec((nb, d), lambda i, kk=kk: (rb0, kk))
    return pl.pallas_call(
        functools.partial(_conv_sample_body, d),
        grid=(1,),
        in_specs=[blk(0), blk(1), blk(2), pl.BlockSpec(conv_w.shape, lambda i: (0, 0)),
                  pl.BlockSpec(state2d.shape, lambda i: (0, 0))],
        out_specs=[pl.BlockSpec((nb, d), lambda i: (0, 0)),
                   pl.BlockSpec(state2d.shape, lambda i: (0, 0))],
        out_shape=[jax.ShapeDtypeStruct((nb, d), BF16),
                   jax.ShapeDtypeStruct(state2d.shape, F32)],
        compiler_params=_params(1),
        name="conv_sample",
    )(pc, pc, pc, conv_w, state2d)


def _outproj_body(x_ref, z_ref, w_ref, o_ref, w16_ref):
    @pl.when(pl.program_id(0) == 0)
    def _():
        w16_ref[...] = w_ref[...].astype(BF16)

    o_ref[...] = x_ref[...] + jnp.dot(z_ref[...], w16_ref[...], preferred_element_type=F32)


def _outproj(x, z, w, tm):
    n, d = x.shape
    return pl.pallas_call(
        _outproj_body,
        grid=(n // tm,),
        in_specs=[pl.BlockSpec((tm, d), lambda i: (i, 0)),
                  pl.BlockSpec((tm, d), lambda i: (i, 0)),
                  pl.BlockSpec((d, d), lambda i: (0, 0))],
        out_specs=pl.BlockSpec((tm, d), lambda i: (i, 0)),
        out_shape=jax.ShapeDtypeStruct((n, d), F32),
        scratch_shapes=[pltpu.VMEM((d, d), BF16)],
        compiler_params=_params(1),
        name="outproj",
    )(x, z, w)


def _swiglu_partial(xn, w1, w3, w2):
    h1 = jnp.dot(xn, w1.astype(BF16), preferred_element_type=F32)
    h3 = jnp.dot(xn, w3.astype(BF16), preferred_element_type=F32)
    h = (jax.nn.silu(h1) * h3).astype(BF16)
    return jnp.dot(h, w2.astype(BF16), preferred_element_type=F32)


def _cache_weight_chunk(w1_hbm, w3_hbm, w2_hbm, f, n_f, tf, stages, caches, wsem):
    n_slots = stages[0].shape[0]

    def copies(ff, ws):
        cols = pl.ds(pl.multiple_of(ff * tf, tf), tf)
        srcs = (w1_hbm.at[:, cols], w3_hbm.at[:, cols], w2_hbm.at[cols, :])
        return [pltpu.make_async_copy(src, st.at[ws], wsem.at[ws, m]) for m, (src, st) in enumerate(zip(srcs, stages))]

    if n_slots == 1:
        ws = 0
        for c in copies(f, 0):
            c.start()
    else:
        ws = lax.rem(f, 2)

        @pl.when(f == 0)
        def _():
            for c in copies(0, 0):
                c.start()

        @pl.when(f + 1 < n_f)
        def _():
            for c in copies(f + 1, 1 - ws):
                c.start()

    for c in copies(f, ws):
        c.wait()
    for st, ca in zip(stages, caches):
        ca[f] = st[ws].astype(BF16)


def _ffn_body(n_f, tf, x_ref, g_ref, w1_hbm, w3_hbm, w2_hbm, o_ref, xn_ref, acc_ref, c1, c3, c2, s1, s3, s2, wsem):
    i = pl.program_id(0)
    f = pl.program_id(1)

    @pl.when(f == 0)
    def _():
        xn_ref[...] = _rmsnorm_bf16(x_ref[...], g_ref[...])
        acc_ref[...] = jnp.zeros_like(acc_ref)

    @pl.when(i == 0)
    def _():
        _cache_weight_chunk(w1_hbm, w3_hbm, w2_hbm, f, n_f, tf, (s1, s3, s2), (c1, c3, c2), wsem)

    acc_ref[...] += _swiglu_partial(xn_ref[...], c1[f], c3[f], c2[f])

    @pl.when(f == n_f - 1)
    def _():
        o_ref[...] = x_ref[...] + acc_ref[...]


def _ffn(x, g, w1, w3, w2, tm, tf):
    n, d = x.shape
    dff = w1.shape[1]
    n_f = dff // tf
    return pl.pallas_call(
        functools.partial(_ffn_body, n_f, tf),
        grid=(n // tm, n_f),
        in_specs=[pl.BlockSpec((tm, d), lambda i, f: (i, 0)),
                  pl.BlockSpec((1, d), lambda i, f: (0, 0)),
                  pl.BlockSpec(memory_space=pl.ANY),
                  pl.BlockSpec(memory_space=pl.ANY),
                  pl.BlockSpec(memory_space=pl.ANY)],
        out_specs=pl.BlockSpec((tm, d), lambda i, f: (i, 0)),
        out_shape=jax.ShapeDtypeStruct((n, d), F32),
        scratch_shapes=[pltpu.VMEM((tm, d), BF16), pltpu.VMEM((tm, d), F32),
                        pltpu.VMEM((n_f, d, tf), BF16), pltpu.VMEM((n_f, d, tf), BF16),
                        pltpu.VMEM((n_f, tf, d), BF16),
                        pltpu.VMEM((1, d, tf), F32), pltpu.VMEM((1, d, tf), F32), pltpu.VMEM((1, tf, d), F32),
                        pltpu.SemaphoreType.DMA((1, 3))],
        compiler_params=_params(2),
        name="ffn",
    )(x, g.reshape(1, d), w1, w3, w2)


ROUTE_ROWS = 8


def _router_body(n_experts, x_ref, g_ref, rw_ref, route_ref, cnt_ref, cnt_scr):
    i = pl.program_id(0)
    tm = x_ref.shape[0]

    @pl.when(i == 0)
    def _():
        cnt_scr[...] = jnp.zeros_like(cnt_scr)

    xn = _rmsnorm_bf16(x_ref[...], g_ref[...])
    logits = jnp.dot(xn, rw_ref[...].astype(BF16), preferred_element_type=F32)
    lane = lax.broadcasted_iota(I32, (tm, LANES), 1).astype(F32)
    neg = jnp.float32(-jnp.inf)
    logits = jnp.where(lane < n_experts, logits, neg)
    m1 = jnp.max(logits, axis=-1, keepdims=True)
    i1 = jnp.min(jnp.where(logits == m1, lane, float(LANES)), axis=-1, keepdims=True)
    i1 = jnp.minimum(i1, float(n_experts - 1))
    rest = jnp.where(lane == i1, neg, logits)
    m2 = jnp.max(rest, axis=-1, keepdims=True)
    i2 = jnp.min(jnp.where(rest == m2, lane, float(LANES)), axis=-1, keepdims=True)
    i2 = jnp.minimum(i2, float(n_experts - 1))
    e2 = jnp.exp(m2 - m1)
    den = 1.0 + e2
    g1 = 1.0 / den
    g2 = e2 / den

    sel1 = lane == i1
    sel2 = lane == i2
    onehot = jnp.logical_or(sel1, sel2)
    row = lax.broadcasted_iota(I32, (tm, tm), 0)
    col = lax.broadcasted_iota(I32, (tm, tm), 1)
    before = (row > col).astype(BF16)
    rank = jnp.dot(before, onehot.astype(BF16), preferred_element_type=F32) + cnt_scr[...]
    r1 = jnp.sum(jnp.where(sel1, rank, 0.0), axis=-1, keepdims=True)
    r2 = jnp.sum(jnp.where(sel2, rank, 0.0), axis=-1, keepdims=True)
    cnt_scr[...] += jnp.sum(onehot.astype(F32), axis=0, keepdims=True)

    out = jnp.zeros((tm, LANES), F32)
    for slot, val in enumerate((i1, i2, g1, g2, r1, r2)):
        out = jnp.where(lane == slot, val, out)
    route_ref[...] = out.T[0:ROUTE_ROWS, :]
    cnt_ref[...] = cnt_scr[...]


def _router(x, g, rw_pad, n_experts, tm):
    n, d = x.shape
    return pl.pallas_call(
        functools.partial(_router_body, n_experts),
        grid=(n // tm,),
        in_specs=[pl.BlockSpec((tm, d), lambda i: (i, 0)),
                  pl.BlockSpec((1, d), lambda i: (0, 0)),
                  pl.BlockSpec((d, LANES), lambda i: (0, 0))],
        out_specs=[pl.BlockSpec((ROUTE_ROWS, tm), lambda i: (0, i)),
                   pl.BlockSpec((1, LANES), lambda i: (0, 0))],
        out_shape=[jax.ShapeDtypeStruct((ROUTE_ROWS, n), F32),
                   jax.ShapeDtypeStruct((1, LANES), F32)],
        scratch_shapes=[pltpu.VMEM((1, LANES), F32)],
        compiler_params=_params(1),
        name="router",
    )(x, g.reshape(1, d), rw_pad)


SUBLANES = 8
MOE_TM = 448


def _start_row_group(src_hbm, idx_ref, g, dst, sem):
    for k in range(SUBLANES):
        pltpu.make_async_copy(src_hbm.at[pl.ds(idx_ref[0, 0, g * SUBLANES + k], 1)],
                              dst.at[g, pl.ds(k, 1)], sem).start()


def _start_row_gather(src_hbm, idx_ref, n_rows, dst, sem):
    def body(g, carry):
        _start_row_group(src_hbm, idx_ref, g, dst, sem)
        return carry

    lax.fori_loop(0, n_rows // SUBLANES, body, 0)


def _wait_row_gather(src_hbm, n_rows, dst, sem):
    pltpu.make_async_copy(src_hbm.at[pl.ds(0, n_rows)], dst.reshape(n_rows, dst.shape[-1]), sem).wait()


def _moe_body(n_f, tf, te_ref, nu_ref, idx_cur, idx_nxt, x_hbm, gate_ref, g_ref, w1_hbm, w3_hbm, w2_hbm,
              y_ref, xbuf, sem, xn_ref, acc_ref, c1, c3, c2, s1, s3, s2, wsem):
    i = pl.program_id(0)
    f = pl.program_id(1)
    n_tiles = pl.num_programs(0)
    tm = xbuf.shape[1] * SUBLANES
    rows_per_step = tm // n_f
    slot = lax.rem(i, 2)
    n_used = nu_ref[0]
    used = i < n_used
    e = te_ref[i]
    first_of_expert = jnp.logical_or(i == 0, e != te_ref[jnp.maximum(i - 1, 0)])

    @pl.when(f == 0)
    def _():
        @pl.when(i == 0)
        def _():
            _start_row_gather(x_hbm, idx_cur, tm, xbuf.at[0], sem.at[0])

        @pl.when(i <= n_used)
        def _():
            _wait_row_gather(x_hbm, tm, xbuf.at[slot], sem.at[slot])

        xn_ref[...] = _rmsnorm_bf16(xbuf[slot].reshape(tm, xbuf.shape[-1]), g_ref[...])
        acc_ref[...] = jnp.zeros_like(acc_ref)

    @pl.when(jnp.logical_and(used, first_of_expert))
    def _():
        _cache_weight_chunk(w1_hbm.at[e], w3_hbm.at[e], w2_hbm.at[e], f, n_f, tf, (s1, s3, s2), (c1, c3, c2), wsem)

    @pl.when(used)
    def _():
        for g in range(rows_per_step // SUBLANES):
            _start_row_group(x_hbm, idx_nxt, f * (rows_per_step // SUBLANES) + g, xbuf.at[1 - slot], sem.at[1 - slot])
        acc_ref[...] += _swiglu_partial(xn_ref[...], c1[f], c3[f], c2[f])

    @pl.when(f == n_f - 1)
    def _():
        r_id = lax.broadcasted_iota(I32, (tm, tm), 0)
        c_id = lax.broadcasted_iota(I32, (tm, tm), 1)
        gate_col = jnp.sum(jnp.where(r_id == c_id, gate_ref[0], 0.0), axis=1, keepdims=True)
        y_ref[...] = acc_ref[...] * gate_col

        @pl.when(jnp.logical_and(used, i == n_tiles - 1))
        def _():
            _wait_row_gather(x_hbm, tm, xbuf.at[1 - slot], sem.at[1 - slot])


def _moe(x, g, w1, w3, w2, tile_expert, n_used, inv3, gate_sorted, tm, tf):
    n_tiles = inv3.shape[0]
    d = x.shape[1]
    dff = w1.shape[2]
    n_f = dff // tf
    assert tm % (n_f * SUBLANES) == 0, "each hidden chunk fetches whole 8-row tiles of the next row tile"
    grid_spec = pltpu.PrefetchScalarGridSpec(
        num_scalar_prefetch=2,
        grid=(n_tiles, n_f),
        in_specs=[pl.BlockSpec((1, 1, tm), lambda i, f, te, nu: (i, 0, 0), memory_space=pltpu.SMEM),
                  pl.BlockSpec((1, 1, tm), lambda i, f, te, nu: (jnp.minimum(i + 1, n_tiles - 1), 0, 0),
                               memory_space=pltpu.SMEM),
                  pl.BlockSpec(memory_space=pl.ANY),
                  pl.BlockSpec((1, 1, tm), lambda i, f, te, nu: (i, 0, 0)),
                  pl.BlockSpec((1, d), lambda i, f, te, nu: (0, 0)),
                  pl.BlockSpec(memory_space=pl.ANY),
                  pl.BlockSpec(memory_space=pl.ANY),
                  pl.BlockSpec(memory_space=pl.ANY)],
        out_specs=pl.BlockSpec((tm, d), lambda i, f, te, nu: (i, 0)),
        scratch_shapes=[pltpu.VMEM((2, tm // SUBLANES, SUBLANES, d), F32),
                        pltpu.SemaphoreType.DMA((2,)),
                        pltpu.VMEM((tm, d), BF16),
                        pltpu.VMEM((tm, d), F32),
                        pltpu.VMEM((n_f, d, tf), BF16),
                        pltpu.VMEM((n_f, d, tf), BF16),
                        pltpu.VMEM((n_f, tf, d), BF16),
                        pltpu.VMEM((2, d, tf), F32),
                        pltpu.VMEM((2, d, tf), F32),
                        pltpu.VMEM((2, tf, d), F32),
                        pltpu.SemaphoreType.DMA((2, 3))],
    )
    return pl.pallas_call(
        functools.partial(_moe_body, n_f, tf),
        grid_spec=grid_spec,
        out_shape=jax.ShapeDtypeStruct((n_tiles * tm, d), F32),
        compiler_params=_params(2),
        name="moe",
    )(tile_expert, n_used, inv3, inv3, x, gate_sorted, g.reshape(1, d), w1, w3, w2)


def _combine_body(pos_cur, pos_nxt, x_ref, g_ref, y_hbm, o_ref, ybuf, sem):
    i = pl.program_id(0)
    n_tiles = pl.num_programs(0)
    rows = ybuf.shape[1] * SUBLANES
    slot = lax.rem(i, 2)

    @pl.when(i == 0)
    def _():
        _start_row_gather(y_hbm, pos_cur, rows, ybuf.at[0], sem.at[0])

    @pl.when(i + 1 < n_tiles)
    def _():
        _start_row_gather(y_hbm, pos_nxt, rows, ybuf.at[1 - slot], sem.at[1 - slot])

    _wait_row_gather(y_hbm, rows, ybuf.at[slot], sem.at[slot])
    tc = rows // TOP_K
    y = ybuf[slot].reshape(rows, ybuf.shape[-1])
    moe = y[0:tc, :] + y[tc:rows, :]
    x = x_ref[...] + moe
    ms = jnp.mean(x * x, axis=-1, keepdims=True)
    o_ref[...] = x * lax.rsqrt(ms + EPS) * g_ref[...]


def _combine(x, g, y_sorted, pos3, row0, n_rows, tc):
    d = x.shape[1]
    n_tiles = n_rows // tc
    rb0 = row0 // tc
    grid_spec = pltpu.PrefetchScalarGridSpec(
        num_scalar_prefetch=0,
        grid=(n_tiles,),
        in_specs=[pl.BlockSpec((1, 1, TOP_K * tc), lambda i: (i, 0, 0), memory_space=pltpu.SMEM),
                  pl.BlockSpec((1, 1, TOP_K * tc), lambda i: (jnp.minimum(i + 1, n_tiles - 1), 0, 0),
                               memory_space=pltpu.SMEM),
                  pl.BlockSpec((tc, d), lambda i: (rb0 + i, 0)),
                  pl.BlockSpec((1, d), lambda i: (0, 0)),
                  pl.BlockSpec(memory_space=pl.ANY)],
        out_specs=pl.BlockSpec((tc, d), lambda i: (i, 0)),
        scratch_shapes=[pltpu.VMEM((2, TOP_K * tc // SUBLANES, SUBLANES, d), F32), pltpu.SemaphoreType.DMA((2,))],
    )
    return pl.pallas_call(
        _combine_body,
        grid_spec=grid_spec,
        out_shape=jax.ShapeDtypeStruct((n_rows, d), F32),
        compiler_params=_params(1),
        name="combine",
    )(pos3, pos3, x, g.reshape(1, d), y_sorted)


def _lookup(table, idx):
    out = jnp.zeros(idx.shape, table.dtype)
    for e in range(table.shape[0]):
        out = jnp.where(idx == e, table[e], out)
    return out


def _expert_layer(x, g_ffn, g_final, router_w, w1, w3, w2, n_prompt, n_dec, tm_router, tm_moe, tf, tc):
    n = n_prompt + n_dec
    n_experts = router_w.shape[-1]
    rw_pad = jnp.pad(router_w, ((0, 0), (0, LANES - n_experts)))
    route, cnt = _router(x, g_ffn, rw_pad, n_experts, tm_router)
    ids = route[0:TOP_K].astype(I32)
    gates = route[TOP_K:2 * TOP_K]
    ranks = route[2 * TOP_K:3 * TOP_K].astype(I32)
    counts = cnt[0, :n_experts].astype(I32)

    n_tiles = (TOP_K * n + n_experts * (tm_moe - 1)) // tm_moe
    padded = ((counts + tm_moe - 1) // tm_moe) * tm_moe
    ends = jnp.cumsum(padded)
    starts = ends - padded
    cstart = jnp.cumsum(counts) - counts
    pos = _lookup(starts, ids) + ranks
    keys = ids * n + jnp.arange(n, dtype=I32)[None, :]
    order = jnp.argsort(keys.reshape(-1)).astype(I32)
    slot_pos = jnp.arange(n_tiles * tm_moe, dtype=I32)
    slot_e = jnp.minimum(jnp.sum(slot_pos[None, :] >= ends[:, None], axis=0), n_experts - 1).astype(I32)
    slot_rank = slot_pos - _lookup(starts, slot_e)
    slot_valid = slot_rank < _lookup(counts, slot_e)
    slot_asg = order[jnp.clip(_lookup(cstart, slot_e) + slot_rank, 0, TOP_K * n - 1)]
    inv = jnp.where(slot_valid, slot_asg % n, 0).astype(I32)
    gate_sorted = jnp.where(slot_valid, gates.reshape(-1)[slot_asg], 0.0)
    tile_expert = slot_e[::tm_moe]
    n_used = (ends[-1] // tm_moe).astype(I32).reshape(1)

    y_sorted = _moe(x, g_ffn, w1, w3, w2, tile_expert, n_used, inv.reshape(n_tiles, 1, tm_moe),
                    gate_sorted.reshape(n_tiles, 1, tm_moe), tm_moe, tf)

    def tile_positions(row0, n_rows, t):
        p = pos[:, row0:row0 + n_rows].reshape(TOP_K, n_rows // t, t)
        return jnp.swapaxes(p, 0, 1).reshape(n_rows // t, 1, TOP_K * t)

    y_prompt = _combine(x, g_final, y_sorted, tile_positions(0, n_prompt, tc), 0, n_prompt, tc)
    y_sample = _combine(x, g_final, y_sorted, tile_positions(n_prompt, n_dec, n_dec), n_prompt, n_dec, n_dec)
    return y_prompt, y_sample


def kernel(x_prompt, x_sample, state_hgrn, state_conv, norm_mix, norm_ffn, norm_final, hgrn_w_in, hgrn_lb, hgrn_g_norm, hgrn_w_out, conv_w_in, conv_w, conv_w_out, ffn_w1, ffn_w3, ffn_w2, moe_router, moe_w1, moe_w3, moe_w2):
    batch, seq, d = x_prompt.shape
    n_dec = x_sample.shape[0]
    n_prompt = batch * seq
    n = n_prompt + n_dec
    n_experts = moe_router.shape[-1]
    assert x_sample.shape[1] == 1 and d % HEAD == 0 and seq % CHUNK == 0
    assert norm_mix.shape[0] == 2, "one HGRN2 layer followed by one short-conv layer"

    tm = _pick_tile(n, (688, 384, 128, 16))
    tm_proj = _pick_tile(n, (1376, 688, 384, 128, 16))
    tb = _pick_tile(seq, (256, 128, 64, 32))
    tg = _pick_tile(n_dec, (16,))
    tf = _pick_tile(ffn_w1.shape[-1], (512, 256, 128))
    tm_moe = _pick_tile(n_prompt, (512, 128, 16))
    tm_moe = MOE_TM if n_prompt % 128 == 0 and ffn_w1.shape[-1] % (7 * 512) == 0 else tm_moe
    tc = _pick_tile(n_prompt, (512, 128, 16))
    tm_router = 3 * LANES if n % (3 * LANES) == 0 else n
    assert n_prompt % tg == 0 and n_prompt % n_dec == 0 and n_prompt % tc == 0

    x = jnp.concatenate([x_prompt.reshape(n_prompt, d), x_sample.reshape(n_dec, d)], axis=0)

    p16, p32 = _proj(x, norm_mix[0], hgrn_w_in[0], tm_proj, f32_col=1)
    o_p, s_prompt = _hgrn_prompt(p16, p32, hgrn_lb, hgrn_g_norm[0], 0, batch, seq, tb)
    o_s, s_sample = _hgrn_sample(p16, p32, hgrn_lb, hgrn_g_norm[0], state_hgrn, 0, n_prompt, tg)
    x = _outproj(x, jnp.concatenate([o_p, o_s], axis=0), hgrn_w_out[0], tm_proj)
    x = _ffn(x, norm_ffn[0], ffn_w1[0], ffn_w3[0], ffn_w2[0], tm, tf)

    (pc,) = _proj(x, norm_mix[1], conv_w_in[0], tm_proj)
    z_p, c_prompt = _conv_prompt(pc, conv_w[0], batch, seq, tb)
    z_s, c_sample = _conv_sample(pc, conv_w[0], state_conv.reshape(n_dec, (CONV_W - 1) * d), n_prompt)
    x = _outproj(x, jnp.concatenate([z_p, z_s], axis=0), conv_w_out[0], tm_proj)

    y_prompt, y_sample = _expert_layer(x, norm_ffn[1], norm_final, moe_router[0], moe_w1[0], moe_w3[0], moe_w2[0],
                                       n_prompt, n_dec, tm_router, tm_moe, tf, tc)

    return (y_prompt.reshape(batch, seq, d),
            y_sample.reshape(n_dec, 1, d),
            s_prompt,
            s_sample,
            c_prompt.reshape(batch, 1, CONV_W - 1, d),
            c_sample.reshape(n_dec, 1, CONV_W - 1, d))
```

```python
import functools

import jax
import jax.numpy as jnp
from jax import lax
from jax.experimental import pallas as pl
from jax.experimental.pallas import tpu as pltpu

F32 = jnp.float32
BF16 = jnp.bfloat16
I32 = jnp.int32

EPS = 1e-6
HEAD = 128
CHUNK = 32
CONV_W = 3
TOP_K = 2
LANES = 128
V7X_VMEM_LIMIT = 56 * 1024 * 1024

ARB = "arbitrary"


def _params(n_axes):
    return pltpu.CompilerParams(dimension_semantics=(ARB,) * n_axes, vmem_limit_bytes=V7X_VMEM_LIMIT)


def _pick_tile(n, candidates):
    for c in candidates:
        if n % c == 0:
            return c
    raise ValueError(f"no tile in {candidates} divides {n}")


def _rmsnorm_bf16(x, g):
    ms = jnp.mean(x * x, axis=-1, keepdims=True)
    return (x * lax.rsqrt(ms + EPS) * g).astype(BF16)


def _proj_body(f32_col, x_ref, g_ref, w_hbm, *refs):
    o16_ref = refs[0]
    xn_ref, w16_ref, stage_ref, wsem = refs[-4:]
    i = pl.program_id(0)
    j = pl.program_id(1)
    d = x_ref.shape[1]

    @pl.when(j == 0)
    def _():
        xn_ref[...] = _rmsnorm_bf16(x_ref[...], g_ref[...])

    @pl.when(i == 0)
    def _():
        cp = pltpu.make_async_copy(w_hbm.at[:, pl.ds(pl.multiple_of(j * d, d), d)], stage_ref, wsem)
        cp.start()
        cp.wait()
        w16_ref[j] = stage_ref[...].astype(BF16)

    def result():
        return jnp.dot(xn_ref[...], w16_ref[j], preferred_element_type=F32)

    if f32_col is None:
        o16_ref[...] = result().astype(BF16)
    else:
        o32_ref = refs[1]

        @pl.when(j == f32_col)
        def _():
            o32_ref[...] = result()

        @pl.when(j != f32_col)
        def _():
            o16_ref[...] = result().astype(BF16)


def _proj(x, g, w, tm, f32_col=None):
    n, d = x.shape
    n_col = w.shape[1] // d
    if f32_col is None:
        col16 = lambda i, j: (i, j)
        n16 = n_col
    else:
        assert 0 < f32_col < n_col
        col16 = lambda i, j: (i, jnp.where(j >= f32_col, j - 1, j))
        n16 = n_col - 1
    out_specs = [pl.BlockSpec((tm, d), col16)]
    out_shape = [jax.ShapeDtypeStruct((n, n16 * d), BF16)]
    if f32_col is not None:
        out_specs.append(pl.BlockSpec((tm, d), lambda i, j: (i, 0)))
        out_shape.append(jax.ShapeDtypeStruct((n, d), F32))
    return pl.pallas_call(
        functools.partial(_proj_body, f32_col),
        grid=(n // tm, n_col),
        in_specs=[pl.BlockSpec((tm, d), lambda i, j: (i, 0)),
                  pl.BlockSpec((1, d), lambda i, j: (0, 0)),
                  pl.BlockSpec(memory_space=pl.ANY)],
        out_specs=out_specs,
        out_shape=out_shape,
        scratch_shapes=[pltpu.VMEM((tm, d), BF16), pltpu.VMEM((n_col, d, d), BF16),
                        pltpu.VMEM((d, d), F32), pltpu.SemaphoreType.DMA(())],
        compiler_params=_params(2),
        name="proj",
    )(x, g.reshape(1, d), w)


def _forget_lower_bound(lb_ref, layer):
    lbw = lb_ref[...]
    e = jnp.exp(lbw - jnp.max(lbw, axis=0, keepdims=True))
    sm = e / jnp.sum(e, axis=0, keepdims=True)
    return jnp.sum(sm[:layer + 1], axis=0, keepdims=True)


def _head_rmsnorm(o):
    return o * lax.rsqrt(jnp.mean(o * o, axis=-1, keepdims=True) + EPS)


def _sigmoid(x):
    return 0.5 * (jnp.tanh(0.5 * x) + 1.0)


def _silu(x):
    h = 0.5 * x
    return h + h * jnp.tanh(h)


def _split3_bf16(x):
    hi = x.astype(BF16)
    r1 = x - hi.astype(F32)
    mid = r1.astype(BF16)
    lo = (r1 - mid.astype(F32)).astype(BF16)
    return hi, mid, lo


def _hgrn_prompt_body(layer, n_heads, tb, q_ref, f_ref, v_ref, gt_ref, lb_ref, gn_ref,
                      o_ref, s_ref, st_scr, qin_scr, kin_scr, kdec_scr, g_scr, sprev_scr):
    t = pl.program_id(1)
    nc = tb // CHUNK
    d = n_heads * HEAD

    @pl.when(t == 0)
    def _():
        st_scr[...] = jnp.zeros_like(st_scr)

    lb = _forget_lower_bound(lb_ref, layer)
    gn = gn_ref[...]
    row = lax.broadcasted_iota(I32, (tb, tb), 0)
    col = lax.broadcasted_iota(I32, (tb, tb), 1)
    same_chunk = (row // CHUNK) == (col // CHUNK)
    causal = jnp.logical_and(same_chunk, row >= col)

    f = lb + (1.0 - lb) * _sigmoid(f_ref[...])
    logf = jnp.log2(f)
    k = 1.0 - f
    tri = causal.astype(BF16)
    G = None
    for part in _split3_bf16(logf):
        term = jnp.dot(tri, part, preferred_element_type=F32)
        G = term if G is None else G + term
    g_scr[...] = G
    g_last = [g_scr[(c + 1) * CHUNK - 1:(c + 1) * CHUNK, :] for c in range(nc)]
    decay = [jnp.exp2(g) for g in g_last]
    decay_rows = jnp.concatenate([jnp.broadcast_to(dc, (CHUNK, d)) for dc in decay], axis=0)
    k_in = k * jnp.exp2(-G)
    qin_scr[...] = (_silu(q_ref[...].astype(F32)) * jnp.exp2(G)).astype(BF16)
    kin_scr[...] = k_in.astype(BF16)
    kdec_scr[...] = (k_in * decay_rows).astype(BF16)

    row_chunk = lax.broadcasted_iota(I32, (tb, HEAD), 0) // CHUNK
    zero16 = jnp.zeros((), BF16)
    grp = next(g for g in (4, 2, 1) if nc % g == 0)

    for h in range(n_heads):
        hs = slice(h * HEAD, (h + 1) * HEAD)
        qh = qin_scr[:, hs]
        kd = kdec_scr[:, hs]
        vh = v_ref[:, hs]
        a = lax.dot_general(qh, kin_scr[:, hs], (((1,), (1,)), ((), ())), preferred_element_type=F32)
        a = jnp.where(causal, a, 0.0).astype(BF16)
        o = jnp.dot(a, vh, preferred_element_type=F32)
        ut = jnp.concatenate(
            [lax.dot_general(vh[c * CHUNK:(c + 1) * CHUNK], kd[c * CHUNK:(c + 1) * CHUNK],
                             (((0,), (0,)), ((), ())), preferred_element_type=F32) for c in range(nc)], axis=0)
        st = st_scr[h]
        for c in range(nc):
            sprev_scr[h, :, c * HEAD:(c + 1) * HEAD] = st.astype(BF16)
            st = st * decay[c][:, hs] + ut[c * HEAD:(c + 1) * HEAD, :]
        st_scr[h] = st
        inter = []
        for g0 in range(0, nc, grp):
            rows = slice(g0 * CHUNK, (g0 + grp) * CHUNK)
            q_blocks = jnp.concatenate(
                [jnp.where(row_chunk[rows] == g0 + j, qh[rows], zero16) for j in range(grp)], axis=1)
            inter.append(lax.dot_general(q_blocks, sprev_scr[h, :, g0 * HEAD:(g0 + grp) * HEAD],
                                         (((1,), (1,)), ((), ())), preferred_element_type=F32))
        o = o + jnp.concatenate(inter, axis=0)
        o = _head_rmsnorm(o) * gn[:, hs] * _silu(gt_ref[:, hs].astype(F32))
        o_ref[:, hs] = o.astype(BF16)

    @pl.when(t == pl.num_programs(1) - 1)
    def _():
        for h in range(n_heads):
            s_ref[0, 0, h] = st_scr[h].T


def _hgrn_prompt(p16, p32, hgrn_lb, g_norm, layer, batch, seq, tb):
    d = g_norm.shape[0]
    n_heads = d // HEAD
    n_t = seq // tb
    blk = lambda kk: pl.BlockSpec((tb, d), lambda b, t, kk=kk: (b * n_t + t, kk))
    return pl.pallas_call(
        functools.partial(_hgrn_prompt_body, layer, n_heads, tb),
        grid=(batch, n_t),
        in_specs=[blk(0), blk(0), blk(1), blk(2),
                  pl.BlockSpec(hgrn_lb.shape, lambda b, t: (0, 0)),
                  pl.BlockSpec((1, d), lambda b, t: (0, 0))],
        out_specs=[pl.BlockSpec((tb, d), lambda b, t: (b * n_t + t, 0)),
                   pl.BlockSpec((1, 1, n_heads, HEAD, HEAD), lambda b, t: (b, 0, 0, 0, 0))],
        out_shape=[jax.ShapeDtypeStruct((batch * seq, d), BF16),
                   jax.ShapeDtypeStruct((batch, 1, n_heads, HEAD, HEAD), F32)],
        scratch_shapes=[pltpu.VMEM((n_heads, HEAD, HEAD), F32),
                        pltpu.VMEM((tb, d), BF16), pltpu.VMEM((tb, d), BF16), pltpu.VMEM((tb, d), BF16),
                        pltpu.VMEM((tb, d), F32),
                        pltpu.VMEM((n_heads, HEAD, (tb // CHUNK) * HEAD), BF16)],
        compiler_params=_params(2),
        name="hgrn_prompt",
    )(p16, p32, p16, p16, hgrn_lb, g_norm.reshape(1, d))


def _hgrn_sample_body(layer, tg, q_ref, f_ref, v_ref, gt_ref, lb_ref, gn_ref, s0_ref,
                      o_ref, s_ref, o_scr):
    lb_h = _forget_lower_bound(lb_ref, layer)
    gn_h = gn_ref[...]

    q = jax.nn.silu(q_ref[...].astype(F32))
    f = lb_h + (1.0 - lb_h) * jax.nn.sigmoid(f_ref[...])
    decay = f
    k = 1.0 - f
    v = v_ref[...].astype(F32)
    q_t, d_t, k_t = q.T, decay.T, k.T
    for j in range(tg):
        s = d_t[:, j:j + 1] * s0_ref[j, 0, 0] + k_t[:, j:j + 1] * v[j:j + 1, :]
        s_ref[j, 0, 0] = s
        o_scr[j:j + 1, :] = jnp.sum(q_t[:, j:j + 1] * s, axis=0, keepdims=True)
    o = _head_rmsnorm(o_scr[...]) * gn_h * jax.nn.silu(gt_ref[...].astype(F32))
    o_ref[...] = o.astype(BF16)


def _hgrn_sample(p16, p32, hgrn_lb, g_norm, state, layer, row0, tg):
    nb = state.shape[0]
    d = g_norm.shape[0]
    n_heads = d // HEAD
    rb0 = row0 // tg
    blk = lambda kk: pl.BlockSpec((tg, HEAD), lambda g, h, kk=kk: (rb0 + g, kk * n_heads + h))
    st_spec = pl.BlockSpec((tg, 1, 1, HEAD, HEAD), lambda g, h: (g, 0, h, 0, 0))
    return pl.pallas_call(
        functools.partial(_hgrn_sample_body, layer, tg),
        grid=(nb // tg, n_heads),
        in_specs=[blk(0), blk(0), blk(1), blk(2),
                  pl.BlockSpec((hgrn_lb.shape[0], HEAD), lambda g, h: (0, h)),
                  pl.BlockSpec((1, HEAD), lambda g, h: (0, h)),
                  st_spec],
        out_specs=[pl.BlockSpec((tg, HEAD), lambda g, h: (g, h)), st_spec],
        out_shape=[jax.ShapeDtypeStruct((nb, d), BF16),
                   jax.ShapeDtypeStruct(state.shape, F32)],
        scratch_shapes=[pltpu.VMEM((tg, HEAD), F32)],
        compiler_params=_params(2),
        name="hgrn_sample",
    )(p16, p32, p16, p16, hgrn_lb, g_norm.reshape(1, d), state)


_CARRY = 8


def _conv_prompt_body(tb, b_ref, c_ref, h_ref, w_ref, z_ref, buf_ref, u_scr):
    t = pl.program_id(1)

    @pl.when(t == 0)
    def _():
        u_scr[0:_CARRY, :] = jnp.zeros((_CARRY, u_scr.shape[1]), F32)

    u = c_ref[...].astype(F32) * h_ref[...].astype(F32)
    u_scr[_CARRY:_CARRY + tb, :] = u
    w = w_ref[...]
    y = w[0:1, :] * u_scr[_CARRY - 2:_CARRY - 2 + tb, :]
    y = y + w[1:2, :] * u_scr[_CARRY - 1:_CARRY - 1 + tb, :]
    y = y + w[2:3, :] * u
    z_ref[...] = (b_ref[...].astype(F32) * y).astype(BF16)
    u_scr[0:_CARRY, :] = u[tb - _CARRY:tb, :]

    @pl.when(t == pl.num_programs(1) - 1)
    def _():
        buf_ref[0] = u[tb - (CONV_W - 1):tb, :]


def _conv_prompt(pc, conv_w, batch, seq, tb):
    d = conv_w.shape[1]
    n_t = seq // tb
    blk = lambda kk: pl.BlockSpec((tb, d), lambda b, t, kk=kk: (b * n_t + t, kk))
    return pl.pallas_call(
        functools.partial(_conv_prompt_body, tb),
        grid=(batch, n_t),
        in_specs=[blk(0), blk(1), blk(2), pl.BlockSpec(conv_w.shape, lambda b, t: (0, 0))],
        out_specs=[pl.BlockSpec((tb, d), lambda b, t: (b * n_t + t, 0)),
                   pl.BlockSpec((1, CONV_W - 1, d), lambda b, t: (b, 0, 0))],
        out_shape=[jax.ShapeDtypeStruct((batch * seq, d), BF16),
                   jax.ShapeDtypeStruct((batch, CONV_W - 1, d), F32)],
        scratch_shapes=[pltpu.VMEM((_CARRY + tb, d), F32)],
        compiler_params=_params(2),
        name="conv_prompt",
    )(pc, pc, pc, conv_w)


def _conv_sample_body(d, b_ref, c_ref, h_ref, w_ref, st_ref, z_ref, buf_ref):
    u = c_ref[...].astype(F32) * h_ref[...].astype(F32)
    w = w_ref[...]
    buf0 = st_ref[:, 0:d]
    buf1 = st_ref[:, d:2 * d]
    y = w[0:1, :] * buf0
    y = y + w[1:2, :] * buf1
    y = y + w[2:3, :] * u
    z_ref[...] = (b_ref[...].astype(F32) * y).astype(BF16)
    buf_ref[:, 0:d] = buf1
    buf_ref[:, d:2 * d] = u


def _conv_sample(pc, conv_w, state2d, row0):
    nb = state2d.shape[0]
    d = conv_w.shape[1]
    rb0 = row0 // nb
    blk = lambda kk: pl.BlockSpec((nb, d), lambda i, kk=kk: (rb0, kk))
    return pl.pallas_call(
        functools.partial(_conv_sample_body, d),
        grid=(1,),
        in_specs=[blk(0), blk(1), blk(2), pl.BlockSpec(conv_w.shape, lambda i: (0, 0)),
                  pl.BlockSpec(state2d.shape, lambda i: (0, 0))],
        out_specs=[pl.BlockSpec((nb, d), lambda i: (0, 0)),
                   pl.BlockSpec(state2d.shape, lambda i: (0, 0))],
        out_shape=[jax.ShapeDtypeStruct((nb, d), BF16),
                   jax.ShapeDtypeStruct(state2d.shape, F32)],
        compiler_params=_params(1),
        name="conv_sample",
    )(pc, pc, pc, conv_w, state2d)


def _outproj_body(x_ref, z_ref, w_ref, o_ref, w16_ref):
    @pl.when(pl.program_id(0) == 0)
    def _():
        w16_ref[...] = w_ref[...].astype(BF16)

    o_ref[...] = x_ref[...] + jnp.dot(z_ref[...], w16_ref[...], preferred_element_type=F32)


def _outproj(x, z, w, tm):
    n, d = x.shape
    return pl.pallas_call(
        _outproj_body,
        grid=(n // tm,),
        in_specs=[pl.BlockSpec((tm, d), lambda i: (i, 0)),
                  pl.BlockSpec((tm, d), lambda i: (i, 0)),
                  pl.BlockSpec((d, d), lambda i: (0, 0))],
        out_specs=pl.BlockSpec((tm, d), lambda i: (i, 0)),
        out_shape=jax.ShapeDtypeStruct((n, d), F32),
        scratch_shapes=[pltpu.VMEM((d, d), BF16)],
        compiler_params=_params(1),
        name="outproj",
    )(x, z, w)


def _swiglu_partial(xn, w1, w3, w2):
    h1 = jnp.dot(xn, w1.astype(BF16), preferred_element_type=F32)
    h3 = jnp.dot(xn, w3.astype(BF16), preferred_element_type=F32)
    h = (jax.nn.silu(h1) * h3).astype(BF16)
    return jnp.dot(h, w2.astype(BF16), preferred_element_type=F32)


def _cache_weight_chunk(w1_hbm, w3_hbm, w2_hbm, f, n_f, tf, stages, caches, wsem):
    n_slots = stages[0].shape[0]

    def copies(ff, ws):
        cols = pl.ds(pl.multiple_of(ff * tf, tf), tf)
        srcs = (w1_hbm.at[:, cols], w3_hbm.at[:, cols], w2_hbm.at[cols, :])
        return [pltpu.make_async_copy(src, st.at[ws], wsem.at[ws, m]) for m, (src, st) in enumerate(zip(srcs, stages))]

    if n_slots == 1:
        ws = 0
        for c in copies(f, 0):
            c.start()
    else:
        ws = lax.rem(f, 2)

        @pl.when(f == 0)
        def _():
            for c in copies(0, 0):
                c.start()

        @pl.when(f + 1 < n_f)
        def _():
            for c in copies(f + 1, 1 - ws):
                c.start()

    for c in copies(f, ws):
        c.wait()
    for st, ca in zip(stages, caches):
        ca[f] = st[ws].astype(BF16)


def _ffn_body(n_f, tf, x_ref, g_ref, w1_hbm, w3_hbm, w2_hbm, o_ref, xn_ref, acc_ref, c1, c3, c2, s1, s3, s2, wsem):
    i = pl.program_id(0)
    f = pl.program_id(1)

    @pl.when(f == 0)
    def _():
        xn_ref[...] = _rmsnorm_bf16(x_ref[...], g_ref[...])
        acc_ref[...] = jnp.zeros_like(acc_ref)

    @pl.when(i == 0)
    def _():
        _cache_weight_chunk(w1_hbm, w3_hbm, w2_hbm, f, n_f, tf, (s1, s3, s2), (c1, c3, c2), wsem)

    acc_ref[...] += _swiglu_partial(xn_ref[...], c1[f], c3[f], c2[f])

    @pl.when(f == n_f - 1)
    def _():
        o_ref[...] = x_ref[...] + acc_ref[...]


def _ffn(x, g, w1, w3, w2, tm, tf):
    n, d = x.shape
    dff = w1.shape[1]
    n_f = dff // tf
    return pl.pallas_call(
        functools.partial(_ffn_body, n_f, tf),
        grid=(n // tm, n_f),
        in_specs=[pl.BlockSpec((tm, d), lambda i, f: (i, 0)),
                  pl.BlockSpec((1, d), lambda i, f: (0, 0)),
                  pl.BlockSpec(memory_space=pl.ANY),
                  pl.BlockSpec(memory_space=pl.ANY),
                  pl.BlockSpec(memory_space=pl.ANY)],
        out_specs=pl.BlockSpec((tm, d), lambda i, f: (i, 0)),
        out_shape=jax.ShapeDtypeStruct((n, d), F32),
        scratch_shapes=[pltpu.VMEM((tm, d), BF16), pltpu.VMEM((tm, d), F32),
                        pltpu.VMEM((n_f, d, tf), BF16), pltpu.VMEM((n_f, d, tf), BF16),
                        pltpu.VMEM((n_f, tf, d), BF16),
                        pltpu.VMEM((1, d, tf), F32), pltpu.VMEM((1, d, tf), F32), pltpu.VMEM((1, tf, d), F32),
                        pltpu.SemaphoreType.DMA((1, 3))],
        compiler_params=_params(2),
        name="ffn",
    )(x, g.reshape(1, d), w1, w3, w2)


ROUTE_ROWS = 8


def _router_body(n_experts, x_ref, g_ref, rw_ref, route_ref, cnt_ref, cnt_scr):
    i = pl.program_id(0)
    tm = x_ref.shape[0]

    @pl.when(i == 0)
    def _():
        cnt_scr[...] = jnp.zeros_like(cnt_scr)

    xn = _rmsnorm_bf16(x_ref[...], g_ref[...])
    logits = jnp.dot(xn, rw_ref[...].astype(BF16), preferred_element_type=F32)
    lane = lax.broadcasted_iota(I32, (tm, LANES), 1).astype(F32)
    neg = jnp.float32(-jnp.inf)
    logits = jnp.where(lane < n_experts, logits, neg)
    m1 = jnp.max(logits, axis=-1, keepdims=True)
    i1 = jnp.min(jnp.where(logits == m1, lane, float(LANES)), axis=-1, keepdims=True)
    i1 = jnp.minimum(i1, float(n_experts - 1))
    rest = jnp.where(lane == i1, neg, logits)
    m2 = jnp.max(rest, axis=-1, keepdims=True)
    i2 = jnp.min(jnp.where(rest == m2, lane, float(LANES)), axis=-1, keepdims=True)
    i2 = jnp.minimum(i2, float(n_experts - 1))
    e2 = jnp.exp(m2 - m1)
    den = 1.0 + e2
    g1 = 1.0 / den
    g2 = e2 / den

    sel1 = lane == i1
    sel2 = lane == i2
    onehot = jnp.logical_or(sel1, sel2)
    row = lax.broadcasted_iota(I32, (tm, tm), 0)
    col = lax.broadcasted_iota(I32, (tm, tm), 1)
    before = (row > col).astype(BF16)
    rank = jnp.dot(before, onehot.astype(BF16), preferred_element_type=F32) + cnt_scr[...]
    r1 = jnp.sum(jnp.where(sel1, rank, 0.0), axis=-1, keepdims=True)
    r2 = jnp.sum(jnp.where(sel2, rank, 0.0), axis=-1, keepdims=True)
    cnt_scr[...] += jnp.sum(onehot.astype(F32), axis=0, keepdims=True)

    out = jnp.zeros((tm, LANES), F32)
    for slot, val in enumerate((i1, i2, g1, g2, r1, r2)):
        out = jnp.where(lane == slot, val, out)
    route_ref[...] = out.T[0:ROUTE_ROWS, :]
    cnt_ref[...] = cnt_scr[...]


def _router(x, g, rw_pad, n_experts, tm):
    n, d = x.shape
    return pl.pallas_call(
        functools.partial(_router_body, n_experts),
        grid=(n // tm,),
        in_specs=[pl.BlockSpec((tm, d), lambda i: (i, 0)),
                  pl.BlockSpec((1, d), lambda i: (0, 0)),
                  pl.BlockSpec((d, LANES), lambda i: (0, 0))],
        out_specs=[pl.BlockSpec((ROUTE_ROWS, tm), lambda i: (0, i)),
                   pl.BlockSpec((1, LANES), lambda i: (0, 0))],
        out_shape=[jax.ShapeDtypeStruct((ROUTE_ROWS, n), F32),
                   jax.ShapeDtypeStruct((1, LANES), F32)],
        scratch_shapes=[pltpu.VMEM((1, LANES), F32)],
        compiler_params=_params(1),
        name="router",
    )(x, g.reshape(1, d), rw_pad)


SUBLANES = 8
MOE_TM = 448


def _start_row_group(src_hbm, idx_ref, g, dst, sem):
    for k in range(SUBLANES):
        pltpu.make_async_copy(src_hbm.at[pl.ds(idx_ref[0, 0, g * SUBLANES + k], 1)],
                              dst.at[g, pl.ds(k, 1)], sem).start()


def _start_row_gather(src_hbm, idx_ref, n_rows, dst, sem):
    def body(g, carry):
        _start_row_group(src_hbm, idx_ref, g, dst, sem)
        return carry

    lax.fori_loop(0, n_rows // SUBLANES, body, 0)


def _wait_row_gather(src_hbm, n_rows, dst, sem):
    pltpu.make_async_copy(src_hbm.at[pl.ds(0, n_rows)], dst.reshape(n_rows, dst.shape[-1]), sem).wait()


def _moe_body(n_f, tf, te_ref, nu_ref, idx_cur, idx_nxt, x_hbm, gate_ref, g_ref, w1_hbm, w3_hbm, w2_hbm,
              y_ref, xbuf, sem, xn_ref, acc_ref, c1, c3, c2, s1, s3, s2, wsem):
    i = pl.program_id(0)
    f = pl.program_id(1)
    n_tiles = pl.num_programs(0)
    tm = xbuf.shape[1] * SUBLANES
    rows_per_step = tm // n_f
    slot = lax.rem(i, 2)
    n_used = nu_ref[0]
    used = i < n_used
    e = te_ref[i]
    first_of_expert = jnp.logical_or(i == 0, e != te_ref[jnp.maximum(i - 1, 0)])

    @pl.when(f == 0)
    def _():
        @pl.when(i == 0)
        def _():
            _start_row_gather(x_hbm, idx_cur, tm, xbuf.at[0], sem.at[0])

        @pl.when(i <= n_used)
        def _():
            _wait_row_gather(x_hbm, tm, xbuf.at[slot], sem.at[slot])

        xn_ref[...] = _rmsnorm_bf16(xbuf[slot].reshape(tm, xbuf.shape[-1]), g_ref[...])
        acc_ref[...] = jnp.zeros_like(acc_ref)

    @pl.when(jnp.logical_and(used, first_of_expert))
    def _():
        _cache_weight_chunk(w1_hbm.at[e], w3_hbm.at[e], w2_hbm.at[e], f, n_f, tf, (s1, s3, s2), (c1, c3, c2), wsem)

    @pl.when(used)
    def _():
        for g in range(rows_per_step // SUBLANES):
            _start_row_group(x_hbm, idx_nxt, f * (rows_per_step // SUBLANES) + g, xbuf.at[1 - slot], sem.at[1 - slot])
        acc_ref[...] += _swiglu_partial(xn_ref[...], c1[f], c3[f], c2[f])

    @pl.when(f == n_f - 1)
    def _():
        r_id = lax.broadcasted_iota(I32, (tm, tm), 0)
        c_id = lax.broadcasted_iota(I32, (tm, tm), 1)
        gate_col = jnp.sum(jnp.where(r_id == c_id, gate_ref[0], 0.0), axis=1, keepdims=True)
        y_ref[...] = acc_ref[...] * gate_col

        @pl.when(jnp.logical_and(used, i == n_tiles - 1))
        def _():
            _wait_row_gather(x_hbm, tm, xbuf.at[1 - slot], sem.at[1 - slot])


def _moe(x, g, w1, w3, w2, tile_expert, n_used, inv3, gate_sorted, tm, tf):
    n_tiles = inv3.shape[0]
    d = x.shape[1]
    dff = w1.shape[2]
    n_f = dff // tf
    assert tm % (n_f * SUBLANES) == 0, "each hidden chunk fetches whole 8-row tiles of the next row tile"
    grid_spec = pltpu.PrefetchScalarGridSpec(
        num_scalar_prefetch=2,
        grid=(n_tiles, n_f),
        in_specs=[pl.BlockSpec((1, 1, tm), lambda i, f, te, nu: (i, 0, 0), memory_space=pltpu.SMEM),
                  pl.BlockSpec((1, 1, tm), lambda i, f, te, nu: (jnp.minimum(i + 1, n_tiles - 1), 0, 0),
                               memory_space=pltpu.SMEM),
                  pl.BlockSpec(memory_space=pl.ANY),
                  pl.BlockSpec((1, 1, tm), lambda i, f, te, nu: (i, 0, 0)),
                  pl.BlockSpec((1, d), lambda i, f, te, nu: (0, 0)),
                  pl.BlockSpec(memory_space=pl.ANY),
                  pl.BlockSpec(memory_space=pl.ANY),
                  pl.BlockSpec(memory_space=pl.ANY)],
        out_specs=pl.BlockSpec((tm, d), lambda i, f, te, nu: (i, 0)),
        scratch_shapes=[pltpu.VMEM((2, tm // SUBLANES, SUBLANES, d), F32),
                        pltpu.SemaphoreType.DMA((2,)),
                        pltpu.VMEM((tm, d), BF16),
                        pltpu.VMEM((tm, d), F32),
                        pltpu.VMEM((n_f, d, tf), BF16),
                        pltpu.VMEM((n_f, d, tf), BF16),
                        pltpu.VMEM((n_f, tf, d), BF16),
                        pltpu.VMEM((2, d, tf), F32),
                        pltpu.VMEM((2, d, tf), F32),
                        pltpu.VMEM((2, tf, d), F32),
                        pltpu.SemaphoreType.DMA((2, 3))],
    )
    return pl.pallas_call(
        functools.partial(_moe_body, n_f, tf),
        grid_spec=grid_spec,
        out_shape=jax.ShapeDtypeStruct((n_tiles * tm, d), F32),
        compiler_params=_params(2),
        name="moe",
    )(tile_expert, n_used, inv3, inv3, x, gate_sorted, g.reshape(1, d), w1, w3, w2)


def _combine_body(pos_cur, pos_nxt, x_ref, g_ref, y_hbm, o_ref, ybuf, sem):
    i = pl.program_id(0)
    n_tiles = pl.num_programs(0)
    rows = ybuf.shape[1] * SUBLANES
    slot = lax.rem(i, 2)

    @pl.when(i == 0)
    def _():
        _start_row_gather(y_hbm, pos_cur, rows, ybuf.at[0], sem.at[0])

    @pl.when(i + 1 < n_tiles)
    def _():
        _start_row_gather(y_hbm, pos_nxt, rows, ybuf.at[1 - slot], sem.at[1 - slot])

    _wait_row_gather(y_hbm, rows, ybuf.at[slot], sem.at[slot])
    tc = rows // TOP_K
    y = ybuf[slot].reshape(rows, ybuf.shape[-1])
    moe = y[0:tc, :] + y[tc:rows, :]
    x = x_ref[...] + moe
    ms = jnp.mean(x * x, axis=-1, keepdims=True)
    o_ref[...] = x * lax.rsqrt(ms + EPS) * g_ref[...]


def _combine(x, g, y_sorted, pos3, row0, n_rows, tc):
    d = x.shape[1]
    n_tiles = n_rows // tc
    rb0 = row0 // tc
    grid_spec = pltpu.PrefetchScalarGridSpec(
        num_scalar_prefetch=0,
        grid=(n_tiles,),
        in_specs=[pl.BlockSpec((1, 1, TOP_K * tc), lambda i: (i, 0, 0), memory_space=pltpu.SMEM),
                  pl.BlockSpec((1, 1, TOP_K * tc), lambda i: (jnp.minimum(i + 1, n_tiles - 1), 0, 0),
                               memory_space=pltpu.SMEM),
                  pl.BlockSpec((tc, d), lambda i: (rb0 + i, 0)),
                  pl.BlockSpec((1, d), lambda i: (0, 0)),
                  pl.BlockSpec(memory_space=pl.ANY)],
        out_specs=pl.BlockSpec((tc, d), lambda i: (i, 0)),
        scratch_shapes=[pltpu.VMEM((2, TOP_K * tc // SUBLANES, SUBLANES, d), F32), pltpu.SemaphoreType.DMA((2,))],
    )
    return pl.pallas_call(
        _combine_body,
        grid_spec=grid_spec,
        out_shape=jax.ShapeDtypeStruct((n_rows, d), F32),
        compiler_params=_params(1),
        name="combine",
    )(pos3, pos3, x, g.reshape(1, d), y_sorted)


def _lookup(table, idx):
    out = jnp.zeros(idx.shape, table.dtype)
    for e in range(table.shape[0]):
        out = jnp.where(idx == e, table[e], out)
    return out


def _expert_layer(x, g_ffn, g_final, router_w, w1, w3, w2, n_prompt, n_dec, tm_router, tm_moe, tf, tc):
    n = n_prompt + n_dec
    n_experts = router_w.shape[-1]
    rw_pad = jnp.pad(router_w, ((0, 0), (0, LANES - n_experts)))
    route, cnt = _router(x, g_ffn, rw_pad, n_experts, tm_router)
    ids = route[0:TOP_K].astype(I32)
    gates = route[TOP_K:2 * TOP_K]
    ranks = route[2 * TOP_K:3 * TOP_K].astype(I32)
    counts = cnt[0, :n_experts].astype(I32)

    n_tiles = (TOP_K * n + n_experts * (tm_moe - 1)) // tm_moe
    padded = ((counts + tm_moe - 1) // tm_moe) * tm_moe
    ends = jnp.cumsum(padded)
    starts = ends - padded
    cstart = jnp.cumsum(counts) - counts
    pos = _lookup(starts, ids) + ranks
    keys = ids * n + jnp.arange(n, dtype=I32)[None, :]
    order = jnp.argsort(keys.reshape(-1)).astype(I32)
    slot_pos = jnp.arange(n_tiles * tm_moe, dtype=I32)
    slot_e = jnp.minimum(jnp.sum(slot_pos[None, :] >= ends[:, None], axis=0), n_experts - 1).astype(I32)
    slot_rank = slot_pos - _lookup(starts, slot_e)
    slot_valid = slot_rank < _lookup(counts, slot_e)
    slot_asg = order[jnp.clip(_lookup(cstart, slot_e) + slot_rank, 0, TOP_K * n - 1)]
    inv = jnp.where(slot_valid, slot_asg % n, 0).astype(I32)
    gate_sorted = jnp.where(slot_valid, gates.reshape(-1)[slot_asg], 0.0)
    tile_expert = slot_e[::tm_moe]
    n_used = (ends[-1] // tm_moe).astype(I32).reshape(1)

    y_sorted = _moe(x, g_ffn, w1, w3, w2, tile_expert, n_used, inv.reshape(n_tiles, 1, tm_moe),
                    gate_sorted.reshape(n_tiles, 1, tm_moe), tm_moe, tf)

    def tile_positions(row0, n_rows, t):
        p = pos[:, row0:row0 + n_rows].reshape(TOP_K, n_rows // t, t)
        return jnp.swapaxes(p, 0, 1).reshape(n_rows // t, 1, TOP_K * t)

    y_prompt = _combine(x, g_final, y_sorted, tile_positions(0, n_prompt, tc), 0, n_prompt, tc)
    y_sample = _combine(x, g_final, y_sorted, tile_positions(n_prompt, n_dec, n_dec), n_prompt, n_dec, n_dec)
    return y_prompt, y_sample


def kernel(x_prompt, x_sample, state_hgrn, state_conv, norm_mix, norm_ffn, norm_final, hgrn_w_in, hgrn_lb, hgrn_g_norm, hgrn_w_out, conv_w_in, conv_w, conv_w_out, ffn_w1, ffn_w3, ffn_w2, moe_router, moe_w1, moe_w3, moe_w2):
    batch, seq, d = x_prompt.shape
    n_dec = x_sample.shape[0]
    n_prompt = batch * seq
    n = n_prompt + n_dec
    n_experts = moe_router.shape[-1]
    assert x_sample.shape[1] == 1 and d % HEAD == 0 and seq % CHUNK == 0
    assert norm_mix.shape[0] == 2, "one HGRN2 layer followed by one short-conv layer"

    tm = _pick_tile(n, (688, 384, 128, 16))
    tm_proj = _pick_tile(n, (1376, 688, 384, 128, 16))
    tb = _pick_tile(seq, (256, 128, 64, 32))
    tb_conv = _pick_tile(seq, (512, 256, 128, 64, 32))
    tg = _pick_tile(n_dec, (16,))
    tf = _pick_tile(ffn_w1.shape[-1], (512, 256, 128))
    tm_moe = _pick_tile(n_prompt, (512, 128, 16))
    tm_moe = MOE_TM if n_prompt % 128 == 0 and ffn_w1.shape[-1] % (7 * 512) == 0 else tm_moe
    tc = _pick_tile(n_prompt, (512, 128, 16))
    tm_router = 3 * LANES if n % (3 * LANES) == 0 else n
    assert n_prompt % tg == 0 and n_prompt % n_dec == 0 and n_prompt % tc == 0

    x = jnp.concatenate([x_prompt.reshape(n_prompt, d), x_sample.reshape(n_dec, d)], axis=0)

    p16, p32 = _proj(x, norm_mix[0], hgrn_w_in[0], tm_proj, f32_col=1)
    o_p, s_prompt = _hgrn_prompt(p16, p32, hgrn_lb, hgrn_g_norm[0], 0, batch, seq, tb)
    o_s, s_sample = _hgrn_sample(p16, p32, hgrn_lb, hgrn_g_norm[0], state_hgrn, 0, n_prompt, tg)
    x = _outproj(x, jnp.concatenate([o_p, o_s], axis=0), hgrn_w_out[0], tm_proj)
    x = _ffn(x, norm_ffn[0], ffn_w1[0], ffn_w3[0], ffn_w2[0], tm, tf)

    (pc,) = _proj(x, norm_mix[1], conv_w_in[0], tm_proj)
    z_p, c_prompt = _conv_prompt(pc, conv_w[0], batch, seq, tb_conv)
    z_s, c_sample = _conv_sample(pc, conv_w[0], state_conv.reshape(n_dec, (CONV_W - 1) * d), n_prompt)
    x = _outproj(x, jnp.concatenate([z_p, z_s], axis=0), conv_w_out[0], tm_proj)

    y_prompt, y_sample = _expert_layer(x, norm_ffn[1], norm_final, moe_router[0], moe_w1[0], moe_w3[0], moe_w2[0],
                                       n_prompt, n_dec, tm_router, tm_moe, tf, tc)

    return (y_prompt.reshape(batch, seq, d),
            y_sample.reshape(n_dec, 1, d),
            s_prompt,
            s_sample,
            c_prompt.reshape(batch, 1, CONV_W - 1, d),
            c_sample.reshape(n_dec, 1, CONV_W - 1, d))
```

```python
import functools

import jax
import jax.numpy as jnp
from jax import lax
from jax.experimental import pallas as pl
from jax.experimental.pallas import tpu as pltpu

F32 = jnp.float32
BF16 = jnp.bfloat16
I32 = jnp.int32

EPS = 1e-6
HEAD = 128
CHUNK = 32
CONV_W = 3
TOP_K = 2
LANES = 128
V7X_VMEM_LIMIT = 56 * 1024 * 1024

ARB = "arbitrary"


def _params(n_axes):
    return pltpu.CompilerParams(dimension_semantics=(ARB,) * n_axes, vmem_limit_bytes=V7X_VMEM_LIMIT)


def _pick_tile(n, candidates):
    for c in candidates:
        if n % c == 0:
            return c
    raise ValueError(f"no tile in {candidates} divides {n}")


def _rmsnorm_bf16(x, g):
    ms = jnp.mean(x * x, axis=-1, keepdims=True)
    return (x * lax.rsqrt(ms + EPS) * g).astype(BF16)


def _proj_body(f32_col, x_ref, g_ref, w_hbm, *refs):
    o16_ref = refs[0]
    xn_ref, w16_ref, stage_ref, wsem = refs[-4:]
    i = pl.program_id(0)
    j = pl.program_id(1)
    d = x_ref.shape[1]

    @pl.when(j == 0)
    def _():
        xn_ref[...] = _rmsnorm_bf16(x_ref[...], g_ref[...])

    @pl.when(i == 0)
    def _():
        cp = pltpu.make_async_copy(w_hbm.at[:, pl.ds(pl.multiple_of(j * d, d), d)], stage_ref, wsem)
        cp.start()
        cp.wait()
        w16_ref[j] = stage_ref[...].astype(BF16)

    def result():
        return jnp.dot(xn_ref[...], w16_ref[j], preferred_element_type=F32)

    if f32_col is None:
        o16_ref[...] = result().astype(BF16)
    else:
        o32_ref = refs[1]

        @pl.when(j == f32_col)
        def _():
            o32_ref[...] = result()

        @pl.when(j != f32_col)
        def _():
            o16_ref[...] = result().astype(BF16)


def _proj(x, g, w, tm, f32_col=None):
    n, d = x.shape
    n_col = w.shape[1] // d
    if f32_col is None:
        col16 = lambda i, j: (i, j)
        n16 = n_col
    else:
        assert 0 < f32_col < n_col
        col16 = lambda i, j: (i, jnp.where(j >= f32_col, j - 1, j))
        n16 = n_col - 1
    out_specs = [pl.BlockSpec((tm, d), col16)]
    out_shape = [jax.ShapeDtypeStruct((n, n16 * d), BF16)]
    if f32_col is not None:
        out_specs.append(pl.BlockSpec((tm, d), lambda i, j: (i, 0)))
        out_shape.append(jax.ShapeDtypeStruct((n, d), F32))
    return pl.pallas_call(
        functools.partial(_proj_body, f32_col),
        grid=(n // tm, n_col),
        in_specs=[pl.BlockSpec((tm, d), lambda i, j: (i, 0)),
                  pl.BlockSpec((1, d), lambda i, j: (0, 0)),
                  pl.BlockSpec(memory_space=pl.ANY)],
        out_specs=out_specs,
        out_shape=out_shape,
        scratch_shapes=[pltpu.VMEM((tm, d), BF16), pltpu.VMEM((n_col, d, d), BF16),
                        pltpu.VMEM((d, d), F32), pltpu.SemaphoreType.DMA(())],
        compiler_params=_params(2),
        name="proj",
    )(x, g.reshape(1, d), w)


def _forget_lower_bound(lb_ref, layer):
    lbw = lb_ref[...]
    e = jnp.exp(lbw - jnp.max(lbw, axis=0, keepdims=True))
    sm = e / jnp.sum(e, axis=0, keepdims=True)
    return jnp.sum(sm[:layer + 1], axis=0, keepdims=True)


def _head_rmsnorm(o):
    return o * lax.rsqrt(jnp.mean(o * o, axis=-1, keepdims=True) + EPS)


def _sigmoid(x):
    return 0.5 * (jnp.tanh(0.5 * x) + 1.0)


def _silu(x):
    h = 0.5 * x
    return h + h * jnp.tanh(h)


def _split3_bf16(x):
    hi = x.astype(BF16)
    r1 = x - hi.astype(F32)
    mid = r1.astype(BF16)
    lo = (r1 - mid.astype(F32)).astype(BF16)
    return hi, mid, lo


def _hgrn_prompt_body(layer, n_heads, tb, q_ref, f_ref, v_ref, gt_ref, lb_ref, gn_ref,
                      o_ref, s_ref, st_scr, qin_scr, kin_scr, kdec_scr, g_scr, sprev_scr):
    t = pl.program_id(1)
    nc = tb // CHUNK
    d = n_heads * HEAD

    @pl.when(t == 0)
    def _():
        st_scr[...] = jnp.zeros_like(st_scr)

    lb = _forget_lower_bound(lb_ref, layer)
    gn = gn_ref[...]
    row = lax.broadcasted_iota(I32, (tb, tb), 0)
    col = lax.broadcasted_iota(I32, (tb, tb), 1)
    same_chunk = (row // CHUNK) == (col // CHUNK)
    causal = jnp.logical_and(same_chunk, row >= col)

    f = lb + (1.0 - lb) * _sigmoid(f_ref[...])
    logf = jnp.log2(f)
    k = 1.0 - f
    tri = causal.astype(BF16)
    G = None
    for part in _split3_bf16(logf):
        term = jnp.dot(tri, part, preferred_element_type=F32)
        G = term if G is None else G + term
    g_scr[...] = G
    g_last = [g_scr[(c + 1) * CHUNK - 1:(c + 1) * CHUNK, :] for c in range(nc)]
    decay = [jnp.exp2(g) for g in g_last]
    decay_rows = jnp.concatenate([jnp.broadcast_to(dc, (CHUNK, d)) for dc in decay], axis=0)
    k_in = k * jnp.exp2(-G)
    qin_scr[...] = (_silu(q_ref[...].astype(F32)) * jnp.exp2(G)).astype(BF16)
    kin_scr[...] = k_in.astype(BF16)
    kdec_scr[...] = (k_in * decay_rows).astype(BF16)

    row_chunk = lax.broadcasted_iota(I32, (tb, HEAD), 0) // CHUNK
    zero16 = jnp.zeros((), BF16)
    grp = next(g for g in (4, 2, 1) if nc % g == 0)

    for h in range(n_heads):
        hs = slice(h * HEAD, (h + 1) * HEAD)
        qh = qin_scr[:, hs]
        kd = kdec_scr[:, hs]
        vh = v_ref[:, hs]
        a = lax.dot_general(qh, kin_scr[:, hs], (((1,), (1,)), ((), ())), preferred_element_type=F32)
        a = jnp.where(causal, a, 0.0).astype(BF16)
        o = jnp.dot(a, vh, preferred_element_type=F32)
        ut = jnp.concatenate(
            [lax.dot_general(vh[c * CHUNK:(c + 1) * CHUNK], kd[c * CHUNK:(c + 1) * CHUNK],
                             (((0,), (0,)), ((), ())), preferred_element_type=F32) for c in range(nc)], axis=0)
        st = st_scr[h]
        for c in range(nc):
            sprev_scr[h, :, c * HEAD:(c + 1) * HEAD] = st.astype(BF16)
            st = st * decay[c][:, hs] + ut[c * HEAD:(c + 1) * HEAD, :]
        st_scr[h] = st
        inter = []
        for g0 in range(0, nc, grp):
            rows = slice(g0 * CHUNK, (g0 + grp) * CHUNK)
            q_blocks = jnp.concatenate(
                [jnp.where(row_chunk[rows] == g0 + j, qh[rows], zero16) for j in range(grp)], axis=1)
            inter.append(lax.dot_general(q_blocks, sprev_scr[h, :, g0 * HEAD:(g0 + grp) * HEAD],
                                         (((1,), (1,)), ((), ())), preferred_element_type=F32))
        o = o + jnp.concatenate(inter, axis=0)
        o = _head_rmsnorm(o) * gn[:, hs] * _silu(gt_ref[:, hs].astype(F32))
        o_ref[:, hs] = o.astype(BF16)

    @pl.when(t == pl.num_programs(1) - 1)
    def _():
        for h in range(n_heads):
            s_ref[0, 0, h] = st_scr[h].T


def _hgrn_prompt(p16, p32, hgrn_lb, g_norm, layer, batch, seq, tb):
    d = g_norm.shape[0]
    n_heads = d // HEAD
    n_t = seq // tb
    blk = lambda kk: pl.BlockSpec((tb, d), lambda b, t, kk=kk: (b * n_t + t, kk))
    return pl.pallas_call(
        functools.partial(_hgrn_prompt_body, layer, n_heads, tb),
        grid=(batch, n_t),
        in_specs=[blk(0), blk(0), blk(1), blk(2),
                  pl.BlockSpec(hgrn_lb.shape, lambda b, t: (0, 0)),
                  pl.BlockSpec((1, d), lambda b, t: (0, 0))],
        out_specs=[pl.BlockSpec((tb, d), lambda b, t: (b * n_t + t, 0)),
                   pl.BlockSpec((1, 1, n_heads, HEAD, HEAD), lambda b, t: (b, 0, 0, 0, 0))],
        out_shape=[jax.ShapeDtypeStruct((batch * seq, d), BF16),
                   jax.ShapeDtypeStruct((batch, 1, n_heads, HEAD, HEAD), F32)],
        scratch_shapes=[pltpu.VMEM((n_heads, HEAD, HEAD), F32),
                        pltpu.VMEM((tb, d), BF16), pltpu.VMEM((tb, d), BF16), pltpu.VMEM((tb, d), BF16),
                        pltpu.VMEM((tb, d), F32),
                        pltpu.VMEM((n_heads, HEAD, (tb // CHUNK) * HEAD), BF16)],
        compiler_params=_params(2),
        name="hgrn_prompt",
    )(p16, p32, p16, p16, hgrn_lb, g_norm.reshape(1, d))


def _hgrn_sample_body(layer, tg, q_ref, f_ref, v_ref, gt_ref, lb_ref, gn_ref, s0_ref,
                      o_ref, s_ref, o_scr):
    lb_h = _forget_lower_bound(lb_ref, layer)
    gn_h = gn_ref[...]

    q = jax.nn.silu(q_ref[...].astype(F32))
    f = lb_h + (1.0 - lb_h) * jax.nn.sigmoid(f_ref[...])
    decay = f
    k = 1.0 - f
    v = v_ref[...].astype(F32)
    q_t, d_t, k_t = q.T, decay.T, k.T
    for j in range(tg):
        s = d_t[:, j:j + 1] * s0_ref[j, 0, 0] + k_t[:, j:j + 1] * v[j:j + 1, :]
        s_ref[j, 0, 0] = s
        o_scr[j:j + 1, :] = jnp.sum(q_t[:, j:j + 1] * s, axis=0, keepdims=True)
    o = _head_rmsnorm(o_scr[...]) * gn_h * jax.nn.silu(gt_ref[...].astype(F32))
    o_ref[...] = o.astype(BF16)


def _hgrn_sample(p16, p32, hgrn_lb, g_norm, state, layer, row0, tg):
    nb = state.shape[0]
    d = g_norm.shape[0]
    n_heads = d // HEAD
    rb0 = row0 // tg
    blk = lambda kk: pl.BlockSpec((tg, HEAD), lambda g, h, kk=kk: (rb0 + g, kk * n_heads + h))
    st_spec = pl.BlockSpec((tg, 1, 1, HEAD, HEAD), lambda g, h: (g, 0, h, 0, 0))
    return pl.pallas_call(
        functools.partial(_hgrn_sample_body, layer, tg),
        grid=(nb // tg, n_heads),
        in_specs=[blk(0), blk(0), blk(1), blk(2),
                  pl.BlockSpec((hgrn_lb.shape[0], HEAD), lambda g, h: (0, h)),
                  pl.BlockSpec((1, HEAD), lambda g, h: (0, h)),
                  st_spec],
        out_specs=[pl.BlockSpec((tg, HEAD), lambda g, h: (g, h)), st_spec],
        out_shape=[jax.ShapeDtypeStruct((nb, d), BF16),
                   jax.ShapeDtypeStruct(state.shape, F32)],
        scratch_shapes=[pltpu.VMEM((tg, HEAD), F32)],
        compiler_params=_params(2),
        name="hgrn_sample",
    )(p16, p32, p16, p16, hgrn_lb, g_norm.reshape(1, d), state)


_CARRY = 8


def _conv_prompt_body(tb, b_ref, c_ref, h_ref, w_ref, z_ref, buf_ref, u_scr):
    t = pl.program_id(1)

    @pl.when(t == 0)
    def _():
        u_scr[0:_CARRY, :] = jnp.zeros((_CARRY, u_scr.shape[1]), F32)

    u = c_ref[...].astype(F32) * h_ref[...].astype(F32)
    u_scr[_CARRY:_CARRY + tb, :] = u
    w = w_ref[...]
    y = w[0:1, :] * u_scr[_CARRY - 2:_CARRY - 2 + tb, :]
    y = y + w[1:2, :] * u_scr[_CARRY - 1:_CARRY - 1 + tb, :]
    y = y + w[2:3, :] * u
    z_ref[...] = (b_ref[...].astype(F32) * y).astype(BF16)
    u_scr[0:_CARRY, :] = u[tb - _CARRY:tb, :]

    @pl.when(t == pl.num_programs(1) - 1)
    def _():
        buf_ref[0] = u[tb - (CONV_W - 1):tb, :]


def _conv_prompt(pc, conv_w, batch, seq, tb):
    d = conv_w.shape[1]
    n_t = seq // tb
    blk = lambda kk: pl.BlockSpec((tb, d), lambda b, t, kk=kk: (b * n_t + t, kk))
    return pl.pallas_call(
        functools.partial(_conv_prompt_body, tb),
        grid=(batch, n_t),
        in_specs=[blk(0), blk(1), blk(2), pl.BlockSpec(conv_w.shape, lambda b, t: (0, 0))],
        out_specs=[pl.BlockSpec((tb, d), lambda b, t: (b * n_t + t, 0)),
                   pl.BlockSpec((1, CONV_W - 1, d), lambda b, t: (b, 0, 0))],
        out_shape=[jax.ShapeDtypeStruct((batch * seq, d), BF16),
                   jax.ShapeDtypeStruct((batch, CONV_W - 1, d), F32)],
        scratch_shapes=[pltpu.VMEM((_CARRY + tb, d), F32)],
        compiler_params=_params(2),
        name="conv_prompt",
    )(pc, pc, pc, conv_w)


def _conv_sample_body(d, b_ref, c_ref, h_ref, w_ref, st_ref, z_ref, buf_ref):
    u = c_ref[...].astype(F32) * h_ref[...].astype(F32)
    w = w_ref[...]
    buf0 = st_ref[:, 0:d]
    buf1 = st_ref[:, d:2 * d]
    y = w[0:1, :] * buf0
    y = y + w[1:2, :] * buf1
    y = y + w[2:3, :] * u
    z_ref[...] = (b_ref[...].astype(F32) * y).astype(BF16)
    buf_ref[:, 0:d] = buf1
    buf_ref[:, d:2 * d] = u


def _conv_sample(pc, conv_w, state2d, row0):
    nb = state2d.shape[0]
    d = conv_w.shape[1]
    rb0 = row0 // nb
    blk = lambda kk: pl.BlockSpec((nb, d), lambda i, kk=kk: (rb0, kk))
    return pl.pallas_call(
        functools.partial(_conv_sample_body, d),
        grid=(1,),
        in_specs=[blk(0), blk(1), blk(2), pl.BlockSpec(conv_w.shape, lambda i: (0, 0)),
                  pl.BlockSpec(state2d.shape, lambda i: (0, 0))],
        out_specs=[pl.BlockSpec((nb, d), lambda i: (0, 0)),
                   pl.BlockSpec(state2d.shape, lambda i: (0, 0))],
        out_shape=[jax.ShapeDtypeStruct((nb, d), BF16),
                   jax.ShapeDtypeStruct(state2d.shape, F32)],
        compiler_params=_params(1),
        name="conv_sample",
    )(pc, pc, pc, conv_w, state2d)


def _outproj_body(n_blocks, x_ref, z_ref, w_ref, o_ref, w16_ref):
    i = pl.program_id(0)

    @pl.when(i == 0)
    def _():
        w16_ref[...] = w_ref[...].astype(BF16)

    @pl.when(i < n_blocks)
    def _():
        o_ref[...] = x_ref[...] + jnp.dot(z_ref[...], w16_ref[...], preferred_element_type=F32)

    @pl.when(i >= n_blocks)
    def _():
        o_ref[...] = jnp.zeros_like(o_ref)


def _outproj_prompt(x, z, w, n_rows, tm):
    d = x.shape[1]
    n_blocks = n_rows // tm
    row_blk = lambda i: (jnp.minimum(i, n_blocks - 1), 0)
    return pl.pallas_call(
        functools.partial(_outproj_body, n_blocks),
        grid=(n_blocks + 1,),
        in_specs=[pl.BlockSpec((tm, d), row_blk),
                  pl.BlockSpec((tm, d), row_blk),
                  pl.BlockSpec((d, d), lambda i: (0, 0))],
        out_specs=pl.BlockSpec((tm, d), lambda i: (i, 0)),
        out_shape=jax.ShapeDtypeStruct(((n_blocks + 1) * tm, d), F32),
        scratch_shapes=[pltpu.VMEM((d, d), BF16)],
        compiler_params=_params(1),
        name="outproj",
    )(x, z, w)


def _outproj_sample_body(x_ref, z_ref, w_ref, buf_ref, o_ref):
    del buf_ref
    o_ref[...] = x_ref[...] + jnp.dot(z_ref[...], w_ref[...].astype(BF16), preferred_element_type=F32)


def _outproj_sample(x, x_row0, z, w, buf, row0):
    nb, d = z.shape
    return pl.pallas_call(
        _outproj_sample_body,
        grid=(1,),
        in_specs=[pl.BlockSpec((nb, d), lambda i: (x_row0 // nb, 0)),
                  pl.BlockSpec((nb, d), lambda i: (0, 0)),
                  pl.BlockSpec((d, d), lambda i: (0, 0)),
                  pl.BlockSpec(memory_space=pl.ANY)],
        out_specs=pl.BlockSpec((nb, d), lambda i: (row0 // nb, 0)),
        out_shape=jax.ShapeDtypeStruct(buf.shape, buf.dtype),
        input_output_aliases={3: 0},
        compiler_params=_params(1),
        name="outproj_sample",
    )(x, z, w, buf)


def _swiglu_partial(xn, w1, w3, w2):
    h1 = jnp.dot(xn, w1.astype(BF16), preferred_element_type=F32)
    h3 = jnp.dot(xn, w3.astype(BF16), preferred_element_type=F32)
    h = (jax.nn.silu(h1) * h3).astype(BF16)
    return jnp.dot(h, w2.astype(BF16), preferred_element_type=F32)


def _cache_weight_chunk(w1_hbm, w3_hbm, w2_hbm, f, n_f, tf, stages, caches, wsem):
    n_slots = stages[0].shape[0]

    def copies(ff, ws):
        cols = pl.ds(pl.multiple_of(ff * tf, tf), tf)
        srcs = (w1_hbm.at[:, cols], w3_hbm.at[:, cols], w2_hbm.at[cols, :])
        return [pltpu.make_async_copy(src, st.at[ws], wsem.at[ws, m]) for m, (src, st) in enumerate(zip(srcs, stages))]

    if n_slots == 1:
        ws = 0
        for c in copies(f, 0):
            c.start()
    else:
        ws = lax.rem(f, 2)

        @pl.when(f == 0)
        def _():
            for c in copies(0, 0):
                c.start()

        @pl.when(f + 1 < n_f)
        def _():
            for c in copies(f + 1, 1 - ws):
                c.start()

    for c in copies(f, ws):
        c.wait()
    for st, ca in zip(stages, caches):
        ca[f] = st[ws].astype(BF16)


def _ffn_body(n_f, tf, x_ref, g_ref, w1_hbm, w3_hbm, w2_hbm, o_ref, xn_ref, acc_ref, c1, c3, c2, s1, s3, s2, wsem):
    i = pl.program_id(0)
    f = pl.program_id(1)

    @pl.when(f == 0)
    def _():
        xn_ref[...] = _rmsnorm_bf16(x_ref[...], g_ref[...])
        acc_ref[...] = jnp.zeros_like(acc_ref)

    @pl.when(i == 0)
    def _():
        _cache_weight_chunk(w1_hbm, w3_hbm, w2_hbm, f, n_f, tf, (s1, s3, s2), (c1, c3, c2), wsem)

    acc_ref[...] += _swiglu_partial(xn_ref[...], c1[f], c3[f], c2[f])

    @pl.when(f == n_f - 1)
    def _():
        o_ref[...] = x_ref[...] + acc_ref[...]


def _ffn(x, n, g, w1, w3, w2, tm, tf):
    d = x.shape[1]
    dff = w1.shape[1]
    n_f = dff // tf
    return pl.pallas_call(
        functools.partial(_ffn_body, n_f, tf),
        grid=(n // tm, n_f),
        in_specs=[pl.BlockSpec((tm, d), lambda i, f: (i, 0)),
                  pl.BlockSpec((1, d), lambda i, f: (0, 0)),
                  pl.BlockSpec(memory_space=pl.ANY),
                  pl.BlockSpec(memory_space=pl.ANY),
                  pl.BlockSpec(memory_space=pl.ANY)],
        out_specs=pl.BlockSpec((tm, d), lambda i, f: (i, 0)),
        out_shape=jax.ShapeDtypeStruct((n, d), F32),
        scratch_shapes=[pltpu.VMEM((tm, d), BF16), pltpu.VMEM((tm, d), F32),
                        pltpu.VMEM((n_f, d, tf), BF16), pltpu.VMEM((n_f, d, tf), BF16),
                        pltpu.VMEM((n_f, tf, d), BF16),
                        pltpu.VMEM((1, d, tf), F32), pltpu.VMEM((1, d, tf), F32), pltpu.VMEM((1, tf, d), F32),
                        pltpu.SemaphoreType.DMA((1, 3))],
        compiler_params=_params(2),
        name="ffn",
    )(x, g.reshape(1, d), w1, w3, w2)


ROUTE_ROWS = 8


def _router_body(n_experts, x_ref, g_ref, rw_ref, route_ref, cnt_ref, cnt_scr):
    i = pl.program_id(0)
    tm = x_ref.shape[0]

    @pl.when(i == 0)
    def _():
        cnt_scr[...] = jnp.zeros_like(cnt_scr)

    xn = _rmsnorm_bf16(x_ref[...], g_ref[...])
    logits = jnp.dot(xn, rw_ref[...].astype(BF16), preferred_element_type=F32)
    lane = lax.broadcasted_iota(I32, (tm, LANES), 1).astype(F32)
    neg = jnp.float32(-jnp.inf)
    logits = jnp.where(lane < n_experts, logits, neg)
    m1 = jnp.max(logits, axis=-1, keepdims=True)
    i1 = jnp.min(jnp.where(logits == m1, lane, float(LANES)), axis=-1, keepdims=True)
    i1 = jnp.minimum(i1, float(n_experts - 1))
    rest = jnp.where(lane == i1, neg, logits)
    m2 = jnp.max(rest, axis=-1, keepdims=True)
    i2 = jnp.min(jnp.where(rest == m2, lane, float(LANES)), axis=-1, keepdims=True)
    i2 = jnp.minimum(i2, float(n_experts - 1))
    e2 = jnp.exp(m2 - m1)
    den = 1.0 + e2
    g1 = 1.0 / den
    g2 = e2 / den

    sel1 = lane == i1
    sel2 = lane == i2
    onehot = jnp.logical_or(sel1, sel2)
    row = lax.broadcasted_iota(I32, (tm, tm), 0)
    col = lax.broadcasted_iota(I32, (tm, tm), 1)
    before = (row > col).astype(BF16)
    rank = jnp.dot(before, onehot.astype(BF16), preferred_element_type=F32) + cnt_scr[...]
    r1 = jnp.sum(jnp.where(sel1, rank, 0.0), axis=-1, keepdims=True)
    r2 = jnp.sum(jnp.where(sel2, rank, 0.0), axis=-1, keepdims=True)
    cnt_scr[...] += jnp.sum(onehot.astype(F32), axis=0, keepdims=True)

    out = jnp.zeros((tm, LANES), F32)
    for slot, val in enumerate((i1, i2, g1, g2, r1, r2)):
        out = jnp.where(lane == slot, val, out)
    route_ref[...] = out.T[0:ROUTE_ROWS, :]
    cnt_ref[...] = cnt_scr[...]


def _router(x, n, g, rw_pad, n_experts, tm):
    d = x.shape[1]
    return pl.pallas_call(
        functools.partial(_router_body, n_experts),
        grid=(n // tm,),
        in_specs=[pl.BlockSpec((tm, d), lambda i: (i, 0)),
                  pl.BlockSpec((1, d), lambda i: (0, 0)),
                  pl.BlockSpec((d, LANES), lambda i: (0, 0))],
        out_specs=[pl.BlockSpec((ROUTE_ROWS, tm), lambda i: (0, i)),
                   pl.BlockSpec((1, LANES), lambda i: (0, 0))],
        out_shape=[jax.ShapeDtypeStruct((ROUTE_ROWS, n), F32),
                   jax.ShapeDtypeStruct((1, LANES), F32)],
        scratch_shapes=[pltpu.VMEM((1, LANES), F32)],
        compiler_params=_params(1),
        name="router",
    )(x, g.reshape(1, d), rw_pad)


SUBLANES = 8
MOE_TM = 448


def _start_row_group(src_hbm, idx_ref, g, dst, sem):
    for k in range(SUBLANES):
        pltpu.make_async_copy(src_hbm.at[pl.ds(idx_ref[0, 0, g * SUBLANES + k], 1)],
                              dst.at[g, pl.ds(k, 1)], sem).start()


def _start_row_gather(src_hbm, idx_ref, n_rows, dst, sem):
    def body(g, carry):
        _start_row_group(src_hbm, idx_ref, g, dst, sem)
        return carry

    lax.fori_loop(0, n_rows // SUBLANES, body, 0)


def _wait_row_gather(src_hbm, n_rows, dst, sem):
    pltpu.make_async_copy(src_hbm.at[pl.ds(0, n_rows)], dst.reshape(n_rows, dst.shape[-1]), sem).wait()


def _moe_body(n_f, tf, te_ref, nu_ref, idx_cur, idx_nxt, x_hbm, gate_ref, g_ref, w1_hbm, w3_hbm, w2_hbm,
              y_ref, xbuf, sem, xn_ref, acc_ref, c1, c3, c2, s1, s3, s2, wsem):
    i = pl.program_id(0)
    f = pl.program_id(1)
    n_tiles = pl.num_programs(0)
    tm = xbuf.shape[1] * SUBLANES
    rows_per_step = tm // n_f
    slot = lax.rem(i, 2)
    n_used = nu_ref[0]
    used = i < n_used
    e = te_ref[i]
    first_of_expert = jnp.logical_or(i == 0, e != te_ref[jnp.maximum(i - 1, 0)])

    @pl.when(f == 0)
    def _():
        @pl.when(i == 0)
        def _():
            _start_row_gather(x_hbm, idx_cur, tm, xbuf.at[0], sem.at[0])

        @pl.when(i <= n_used)
        def _():
            _wait_row_gather(x_hbm, tm, xbuf.at[slot], sem.at[slot])

        xn_ref[...] = _rmsnorm_bf16(xbuf[slot].reshape(tm, xbuf.shape[-1]), g_ref[...])
        acc_ref[...] = jnp.zeros_like(acc_ref)

    @pl.when(jnp.logical_and(used, first_of_expert))
    def _():
        _cache_weight_chunk(w1_hbm.at[e], w3_hbm.at[e], w2_hbm.at[e], f, n_f, tf, (s1, s3, s2), (c1, c3, c2), wsem)

    @pl.when(used)
    def _():
        for g in range(rows_per_step // SUBLANES):
            _start_row_group(x_hbm, idx_nxt, f * (rows_per_step // SUBLANES) + g, xbuf.at[1 - slot], sem.at[1 - slot])
        acc_ref[...] += _swiglu_partial(xn_ref[...], c1[f], c3[f], c2[f])

    @pl.when(f == n_f - 1)
    def _():
        r_id = lax.broadcasted_iota(I32, (tm, tm), 0)
        c_id = lax.broadcasted_iota(I32, (tm, tm), 1)
        gate_col = jnp.sum(jnp.where(r_id == c_id, gate_ref[0], 0.0), axis=1, keepdims=True)
        y_ref[...] = acc_ref[...] * gate_col

        @pl.when(jnp.logical_and(used, i == n_tiles - 1))
        def _():
            _wait_row_gather(x_hbm, tm, xbuf.at[1 - slot], sem.at[1 - slot])


def _moe(x, g, w1, w3, w2, tile_expert, n_used, inv3, gate_sorted, tm, tf):
    n_tiles = inv3.shape[0]
    d = x.shape[1]
    dff = w1.shape[2]
    n_f = dff // tf
    assert tm % (n_f * SUBLANES) == 0, "each hidden chunk fetches whole 8-row tiles of the next row tile"
    grid_spec = pltpu.PrefetchScalarGridSpec(
        num_scalar_prefetch=2,
        grid=(n_tiles, n_f),
        in_specs=[pl.BlockSpec((1, 1, tm), lambda i, f, te, nu: (i, 0, 0), memory_space=pltpu.SMEM),
                  pl.BlockSpec((1, 1, tm), lambda i, f, te, nu: (jnp.minimum(i + 1, n_tiles - 1), 0, 0),
                               memory_space=pltpu.SMEM),
                  pl.BlockSpec(memory_space=pl.ANY),
                  pl.BlockSpec((1, 1, tm), lambda i, f, te, nu: (i, 0, 0)),
                  pl.BlockSpec((1, d), lambda i, f, te, nu: (0, 0)),
                  pl.BlockSpec(memory_space=pl.ANY),
                  pl.BlockSpec(memory_space=pl.ANY),
                  pl.BlockSpec(memory_space=pl.ANY)],
        out_specs=pl.BlockSpec((tm, d), lambda i, f, te, nu: (i, 0)),
        scratch_shapes=[pltpu.VMEM((2, tm // SUBLANES, SUBLANES, d), F32),
                        pltpu.SemaphoreType.DMA((2,)),
                        pltpu.VMEM((tm, d), BF16),
                        pltpu.VMEM((tm, d), F32),
                        pltpu.VMEM((n_f, d, tf), BF16),
                        pltpu.VMEM((n_f, d, tf), BF16),
                        pltpu.VMEM((n_f, tf, d), BF16),
                        pltpu.VMEM((2, d, tf), F32),
                        pltpu.VMEM((2, d, tf), F32),
                        pltpu.VMEM((2, tf, d), F32),
                        pltpu.SemaphoreType.DMA((2, 3))],
    )
    return pl.pallas_call(
        functools.partial(_moe_body, n_f, tf),
        grid_spec=grid_spec,
        out_shape=jax.ShapeDtypeStruct((n_tiles * tm, d), F32),
        compiler_params=_params(2),
        name="moe",
    )(tile_expert, n_used, inv3, inv3, x, gate_sorted, g.reshape(1, d), w1, w3, w2)


def _combine_body(pos_cur, pos_nxt, x_ref, g_ref, y_hbm, o_ref, ybuf, sem):
    i = pl.program_id(0)
    n_tiles = pl.num_programs(0)
    rows = ybuf.shape[1] * SUBLANES
    slot = lax.rem(i, 2)

    @pl.when(i == 0)
    def _():
        _start_row_gather(y_hbm, pos_cur, rows, ybuf.at[0], sem.at[0])

    @pl.when(i + 1 < n_tiles)
    def _():
        _start_row_gather(y_hbm, pos_nxt, rows, ybuf.at[1 - slot], sem.at[1 - slot])

    _wait_row_gather(y_hbm, rows, ybuf.at[slot], sem.at[slot])
    tc = rows // TOP_K
    y = ybuf[slot].reshape(rows, ybuf.shape[-1])
    moe = y[0:tc, :] + y[tc:rows, :]
    x = x_ref[...] + moe
    ms = jnp.mean(x * x, axis=-1, keepdims=True)
    o_ref[...] = x * lax.rsqrt(ms + EPS) * g_ref[...]


def _combine(x, g, y_sorted, pos3, row0, n_rows, tc):
    d = x.shape[1]
    n_tiles = n_rows // tc
    rb0 = row0 // tc
    grid_spec = pltpu.PrefetchScalarGridSpec(
        num_scalar_prefetch=0,
        grid=(n_tiles,),
        in_specs=[pl.BlockSpec((1, 1, TOP_K * tc), lambda i: (i, 0, 0), memory_space=pltpu.SMEM),
                  pl.BlockSpec((1, 1, TOP_K * tc), lambda i: (jnp.minimum(i + 1, n_tiles - 1), 0, 0),
                               memory_space=pltpu.SMEM),
                  pl.BlockSpec((tc, d), lambda i: (rb0 + i, 0)),
                  pl.BlockSpec((1, d), lambda i: (0, 0)),
                  pl.BlockSpec(memory_space=pl.ANY)],
        out_specs=pl.BlockSpec((tc, d), lambda i: (i, 0)),
        scratch_shapes=[pltpu.VMEM((2, TOP_K * tc // SUBLANES, SUBLANES, d), F32), pltpu.SemaphoreType.DMA((2,))],
    )
    return pl.pallas_call(
        _combine_body,
        grid_spec=grid_spec,
        out_shape=jax.ShapeDtypeStruct((n_rows, d), F32),
        compiler_params=_params(1),
        name="combine",
    )(pos3, pos3, x, g.reshape(1, d), y_sorted)


def _lookup(table, idx):
    out = jnp.zeros(idx.shape, table.dtype)
    for e in range(table.shape[0]):
        out = jnp.where(idx == e, table[e], out)
    return out


def _expert_layer(x, g_ffn, g_final, router_w, w1, w3, w2, n_prompt, n_dec, tm_router, tm_moe, tf, tc):
    n = n_prompt + n_dec
    n_experts = router_w.shape[-1]
    rw_pad = jnp.pad(router_w, ((0, 0), (0, LANES - n_experts)))
    route, cnt = _router(x, n, g_ffn, rw_pad, n_experts, tm_router)
    ids = route[0:TOP_K].astype(I32)
    gates = route[TOP_K:2 * TOP_K]
    ranks = route[2 * TOP_K:3 * TOP_K].astype(I32)
    counts = cnt[0, :n_experts].astype(I32)

    n_tiles = (TOP_K * n + n_experts * (tm_moe - 1)) // tm_moe
    padded = ((counts + tm_moe - 1) // tm_moe) * tm_moe
    ends = jnp.cumsum(padded)
    starts = ends - padded
    cstart = jnp.cumsum(counts) - counts
    pos = _lookup(starts, ids) + ranks
    keys = ids * n + jnp.arange(n, dtype=I32)[None, :]
    order = jnp.argsort(keys.reshape(-1)).astype(I32)
    slot_pos = jnp.arange(n_tiles * tm_moe, dtype=I32)
    slot_e = jnp.minimum(jnp.sum(slot_pos[None, :] >= ends[:, None], axis=0), n_experts - 1).astype(I32)
    slot_rank = slot_pos - _lookup(starts, slot_e)
    slot_valid = slot_rank < _lookup(counts, slot_e)
    slot_asg = order[jnp.clip(_lookup(cstart, slot_e) + slot_rank, 0, TOP_K * n - 1)]
    inv = jnp.where(slot_valid, slot_asg % n, 0).astype(I32)
    gate_sorted = jnp.where(slot_valid, gates.reshape(-1)[slot_asg], 0.0)
    tile_expert = slot_e[::tm_moe]
    n_used = (ends[-1] // tm_moe).astype(I32).reshape(1)

    y_sorted = _moe(x, g_ffn, w1, w3, w2, tile_expert, n_used, inv.reshape(n_tiles, 1, tm_moe),
                    gate_sorted.reshape(n_tiles, 1, tm_moe), tm_moe, tf)

    def tile_positions(row0, n_rows, t):
        p = pos[:, row0:row0 + n_rows].reshape(TOP_K, n_rows // t, t)
        return jnp.swapaxes(p, 0, 1).reshape(n_rows // t, 1, TOP_K * t)

    y_prompt = _combine(x, g_final, y_sorted, tile_positions(0, n_prompt, tc), 0, n_prompt, tc)
    y_sample = _combine(x, g_final, y_sorted, tile_positions(n_prompt, n_dec, n_dec), n_prompt, n_dec, n_dec)
    return y_prompt, y_sample


def kernel(x_prompt, x_sample, state_hgrn, state_conv, norm_mix, norm_ffn, norm_final, hgrn_w_in, hgrn_lb, hgrn_g_norm, hgrn_w_out, conv_w_in, conv_w, conv_w_out, ffn_w1, ffn_w3, ffn_w2, moe_router, moe_w1, moe_w3, moe_w2):
    batch, seq, d = x_prompt.shape
    n_dec = x_sample.shape[0]
    n_prompt = batch * seq
    n = n_prompt + n_dec
    n_experts = moe_router.shape[-1]
    assert x_sample.shape[1] == 1 and d % HEAD == 0 and seq % CHUNK == 0
    assert norm_mix.shape[0] == 2, "one HGRN2 layer followed by one short-conv layer"

    tm = _pick_tile(n, (688, 384, 128, 16))
    tm_proj = _pick_tile(n, (1376, 688, 384, 128, 16))
    tm_prompt = _pick_tile(n_prompt, (1024, 512, 256, 128, 16))
    tb = _pick_tile(seq, (256, 128, 64, 32))
    tb_conv = _pick_tile(seq, (512, 256, 128, 64, 32))
    tg = _pick_tile(n_dec, (16,))
    tf = _pick_tile(ffn_w1.shape[-1], (512, 256, 128))
    tm_moe = _pick_tile(n_prompt, (512, 128, 16))
    tm_moe = MOE_TM if n_prompt % 128 == 0 and ffn_w1.shape[-1] % (7 * 512) == 0 else tm_moe
    tc = _pick_tile(n_prompt, (512, 128, 16))
    tm_router = 3 * LANES if n % (3 * LANES) == 0 else n
    assert n_prompt % tg == 0 and n_prompt % n_dec == 0 and n_prompt % tc == 0

    xp = x_prompt.reshape(n_prompt, d)
    xs = x_sample.reshape(n_dec, d)

    p16, p32 = _proj(xp, norm_mix[0], hgrn_w_in[0], tm_prompt, f32_col=1)
    p16_s, p32_s = _proj(xs, norm_mix[0], hgrn_w_in[0], n_dec, f32_col=1)
    o_p, s_prompt = _hgrn_prompt(p16, p32, hgrn_lb, hgrn_g_norm[0], 0, batch, seq, tb)
    o_s, s_sample = _hgrn_sample(p16_s, p32_s, hgrn_lb, hgrn_g_norm[0], state_hgrn, 0, 0, tg)
    x = _outproj_prompt(xp, o_p, hgrn_w_out[0], n_prompt, tm_prompt)
    x = _outproj_sample(xs, 0, o_s, hgrn_w_out[0], x, n_prompt)
    x = _ffn(x, n, norm_ffn[0], ffn_w1[0], ffn_w3[0], ffn_w2[0], tm, tf)

    (pc,) = _proj(x, norm_mix[1], conv_w_in[0], tm_proj)
    z_p, c_prompt = _conv_prompt(pc, conv_w[0], batch, seq, tb_conv)
    z_s, c_sample = _conv_sample(pc, conv_w[0], state_conv.reshape(n_dec, (CONV_W - 1) * d), n_prompt)
    x1 = _outproj_prompt(x, z_p, conv_w_out[0], n_prompt, tm_prompt)
    x1 = _outproj_sample(x, n_prompt, z_s, conv_w_out[0], x1, n_prompt)

    y_prompt, y_sample = _expert_layer(x1, norm_ffn[1], norm_final, moe_router[0], moe_w1[0], moe_w3[0], moe_w2[0],
                                       n_prompt, n_dec, tm_router, tm_moe, tf, tc)

    return (y_prompt.reshape(batch, seq, d),
            y_sample.reshape(n_dec, 1, d),
            s_prompt,
            s_sample,
            c_prompt.reshape(batch, 1, CONV_W - 1, d),
            c_sample.reshape(n_dec, 1, CONV_W - 1, d))
```

```python
import functools

import jax
import jax.numpy as jnp
from jax import lax
from jax.experimental import pallas as pl
from jax.experimental.pallas import tpu as pltpu

F32 = jnp.float32
BF16 = jnp.bfloat16
I32 = jnp.int32

EPS = 1e-6
HEAD = 128
CHUNK = 32
CONV_W = 3
TOP_K = 2
LANES = 128
V7X_VMEM_LIMIT = 56 * 1024 * 1024

ARB = "arbitrary"


def _params(n_axes):
    return pltpu.CompilerParams(dimension_semantics=(ARB,) * n_axes, vmem_limit_bytes=V7X_VMEM_LIMIT)


def _pick_tile(n, candidates):
    for c in candidates:
        if n % c == 0:
            return c
    raise ValueError(f"no tile in {candidates} divides {n}")


def _rmsnorm_bf16(x, g):
    ms = jnp.mean(x * x, axis=-1, keepdims=True)
    return (x * lax.rsqrt(ms + EPS) * g).astype(BF16)


def _proj_body(f32_col, x_ref, g_ref, w_hbm, *refs):
    o16_ref = refs[0]
    xn_ref, w16_ref, stage_ref, wsem = refs[-4:]
    i = pl.program_id(0)
    j = pl.program_id(1)
    d = x_ref.shape[1]

    @pl.when(j == 0)
    def _():
        xn_ref[...] = _rmsnorm_bf16(x_ref[...], g_ref[...])

    @pl.when(i == 0)
    def _():
        cp = pltpu.make_async_copy(w_hbm.at[:, pl.ds(pl.multiple_of(j * d, d), d)], stage_ref, wsem)
        cp.start()
        cp.wait()
        w16_ref[j] = stage_ref[...].astype(BF16)

    def result():
        return jnp.dot(xn_ref[...], w16_ref[j], preferred_element_type=F32)

    if f32_col is None:
        o16_ref[...] = result().astype(BF16)
    else:
        o32_ref = refs[1]

        @pl.when(j == f32_col)
        def _():
            o32_ref[...] = result()

        @pl.when(j != f32_col)
        def _():
            o16_ref[...] = result().astype(BF16)


def _proj(x, g, w, tm, f32_col=None):
    n, d = x.shape
    n_col = w.shape[1] // d
    if f32_col is None:
        col16 = lambda i, j: (i, j)
        n16 = n_col
    else:
        assert 0 < f32_col < n_col
        col16 = lambda i, j: (i, jnp.where(j >= f32_col, j - 1, j))
        n16 = n_col - 1
    out_specs = [pl.BlockSpec((tm, d), col16)]
    out_shape = [jax.ShapeDtypeStruct((n, n16 * d), BF16)]
    if f32_col is not None:
        out_specs.append(pl.BlockSpec((tm, d), lambda i, j: (i, 0)))
        out_shape.append(jax.ShapeDtypeStruct((n, d), F32))
    return pl.pallas_call(
        functools.partial(_proj_body, f32_col),
        grid=(n // tm, n_col),
        in_specs=[pl.BlockSpec((tm, d), lambda i, j: (i, 0)),
                  pl.BlockSpec((1, d), lambda i, j: (0, 0)),
                  pl.BlockSpec(memory_space=pl.ANY)],
        out_specs=out_specs,
        out_shape=out_shape,
        scratch_shapes=[pltpu.VMEM((tm, d), BF16), pltpu.VMEM((n_col, d, d), BF16),
                        pltpu.VMEM((d, d), F32), pltpu.SemaphoreType.DMA(())],
        compiler_params=_params(2),
        name="proj",
    )(x, g.reshape(1, d), w)


def _forget_lower_bound(lb_ref, layer):
    lbw = lb_ref[...]
    e = jnp.exp(lbw - jnp.max(lbw, axis=0, keepdims=True))
    sm = e / jnp.sum(e, axis=0, keepdims=True)
    return jnp.sum(sm[:layer + 1], axis=0, keepdims=True)


def _head_rmsnorm(o):
    return o * lax.rsqrt(jnp.mean(o * o, axis=-1, keepdims=True) + EPS)


def _sigmoid(x):
    return 0.5 * (jnp.tanh(0.5 * x) + 1.0)


def _silu(x):
    h = 0.5 * x
    return h + h * jnp.tanh(h)


def _split3_bf16(x):
    hi = x.astype(BF16)
    r1 = x - hi.astype(F32)
    mid = r1.astype(BF16)
    lo = (r1 - mid.astype(F32)).astype(BF16)
    return hi, mid, lo


def _hgrn_prompt_body(layer, n_heads, tb, q_ref, f_ref, v_ref, gt_ref, lb_ref, gn_ref,
                      o_ref, s_ref, st_scr, qin_scr, kin_scr, kdec_scr, g_scr, sprev_scr):
    t = pl.program_id(1)
    nc = tb // CHUNK
    d = n_heads * HEAD

    @pl.when(t == 0)
    def _():
        st_scr[...] = jnp.zeros_like(st_scr)

    lb = _forget_lower_bound(lb_ref, layer)
    gn = gn_ref[...]
    row = lax.broadcasted_iota(I32, (tb, tb), 0)
    col = lax.broadcasted_iota(I32, (tb, tb), 1)
    same_chunk = (row // CHUNK) == (col // CHUNK)
    causal = jnp.logical_and(same_chunk, row >= col)

    f = lb + (1.0 - lb) * _sigmoid(f_ref[...])
    logf = jnp.log2(f)
    k = 1.0 - f
    tri = causal.astype(BF16)
    G = None
    for part in _split3_bf16(logf):
        term = jnp.dot(tri, part, preferred_element_type=F32)
        G = term if G is None else G + term
    g_scr[...] = G
    g_last = [g_scr[(c + 1) * CHUNK - 1:(c + 1) * CHUNK, :] for c in range(nc)]
    decay = [jnp.exp2(g) for g in g_last]
    decay_rows = jnp.concatenate([jnp.broadcast_to(dc, (CHUNK, d)) for dc in decay], axis=0)
    k_in = k * jnp.exp2(-G)
    qin_scr[...] = (_silu(q_ref[...].astype(F32)) * jnp.exp2(G)).astype(BF16)
    kin_scr[...] = k_in.astype(BF16)
    kdec_scr[...] = (k_in * decay_rows).astype(BF16)

    row_chunk = lax.broadcasted_iota(I32, (tb, HEAD), 0) // CHUNK
    zero16 = jnp.zeros((), BF16)
    grp = next(g for g in (4, 2, 1) if nc % g == 0)

    for h in range(n_heads):
        hs = slice(h * HEAD, (h + 1) * HEAD)
        qh = qin_scr[:, hs]
        kd = kdec_scr[:, hs]
        vh = v_ref[:, hs]
        a = lax.dot_general(qh, kin_scr[:, hs], (((1,), (1,)), ((), ())), preferred_element_type=F32)
        a = jnp.where(causal, a, 0.0).astype(BF16)
        o = jnp.dot(a, vh, preferred_element_type=F32)
        ut = jnp.concatenate(
            [lax.dot_general(vh[c * CHUNK:(c + 1) * CHUNK], kd[c * CHUNK:(c + 1) * CHUNK],
                             (((0,), (0,)), ((), ())), preferred_element_type=F32) for c in range(nc)], axis=0)
        st = st_scr[h]
        for c in range(nc):
            sprev_scr[h, :, c * HEAD:(c + 1) * HEAD] = st.astype(BF16)
            st = st * decay[c][:, hs] + ut[c * HEAD:(c + 1) * HEAD, :]
        st_scr[h] = st
        inter = []
        for g0 in range(0, nc, grp):
            rows = slice(g0 * CHUNK, (g0 + grp) * CHUNK)
            q_blocks = jnp.concatenate(
                [jnp.where(row_chunk[rows] == g0 + j, qh[rows], zero16) for j in range(grp)], axis=1)
            inter.append(lax.dot_general(q_blocks, sprev_scr[h, :, g0 * HEAD:(g0 + grp) * HEAD],
                                         (((1,), (1,)), ((), ())), preferred_element_type=F32))
        o = o + jnp.concatenate(inter, axis=0)
        o = _head_rmsnorm(o) * gn[:, hs] * _silu(gt_ref[:, hs].astype(F32))
        o_ref[:, hs] = o.astype(BF16)

    @pl.when(t == pl.num_programs(1) - 1)
    def _():
        for h in range(n_heads):
            s_ref[0, 0, h] = st_scr[h].T


def _hgrn_prompt(p16, p32, hgrn_lb, g_norm, layer, batch, seq, tb):
    d = g_norm.shape[0]
    n_heads = d // HEAD
    n_t = seq // tb
    blk = lambda kk: pl.BlockSpec((tb, d), lambda b, t, kk=kk: (b * n_t + t, kk))
    return pl.pallas_call(
        functools.partial(_hgrn_prompt_body, layer, n_heads, tb),
        grid=(batch, n_t),
        in_specs=[blk(0), blk(0), blk(1), blk(2),
                  pl.BlockSpec(hgrn_lb.shape, lambda b, t: (0, 0)),
                  pl.BlockSpec((1, d), lambda b, t: (0, 0))],
        out_specs=[pl.BlockSpec((tb, d), lambda b, t: (b * n_t + t, 0)),
                   pl.BlockSpec((1, 1, n_heads, HEAD, HEAD), lambda b, t: (b, 0, 0, 0, 0))],
        out_shape=[jax.ShapeDtypeStruct((batch * seq, d), BF16),
                   jax.ShapeDtypeStruct((batch, 1, n_heads, HEAD, HEAD), F32)],
        scratch_shapes=[pltpu.VMEM((n_heads, HEAD, HEAD), F32),
                        pltpu.VMEM((tb, d), BF16), pltpu.VMEM((tb, d), BF16), pltpu.VMEM((tb, d), BF16),
                        pltpu.VMEM((tb, d), F32),
                        pltpu.VMEM((n_heads, HEAD, (tb // CHUNK) * HEAD), BF16)],
        compiler_params=_params(2),
        name="hgrn_prompt",
    )(p16, p32, p16, p16, hgrn_lb, g_norm.reshape(1, d))


def _hgrn_sample_body(layer, tg, q_ref, f_ref, v_ref, gt_ref, lb_ref, gn_ref, s0_ref,
                      o_ref, s_ref, o_scr):
    lb_h = _forget_lower_bound(lb_ref, layer)
    gn_h = gn_ref[...]

    q = jax.nn.silu(q_ref[...].astype(F32))
    f = lb_h + (1.0 - lb_h) * jax.nn.sigmoid(f_ref[...])
    decay = f
    k = 1.0 - f
    v = v_ref[...].astype(F32)
    q_t, d_t, k_t = q.T, decay.T, k.T
    for j in range(tg):
        s = d_t[:, j:j + 1] * s0_ref[j, 0, 0] + k_t[:, j:j + 1] * v[j:j + 1, :]
        s_ref[j, 0, 0] = s
        o_scr[j:j + 1, :] = jnp.sum(q_t[:, j:j + 1] * s, axis=0, keepdims=True)
    o = _head_rmsnorm(o_scr[...]) * gn_h * jax.nn.silu(gt_ref[...].astype(F32))
    o_ref[...] = o.astype(BF16)


def _hgrn_sample(p16, p32, hgrn_lb, g_norm, state, layer, row0, tg):
    nb = state.shape[0]
    d = g_norm.shape[0]
    n_heads = d // HEAD
    rb0 = row0 // tg
    blk = lambda kk: pl.BlockSpec((tg, HEAD), lambda g, h, kk=kk: (rb0 + g, kk * n_heads + h))
    st_spec = pl.BlockSpec((tg, 1, 1, HEAD, HEAD), lambda g, h: (g, 0, h, 0, 0))
    return pl.pallas_call(
        functools.partial(_hgrn_sample_body, layer, tg),
        grid=(nb // tg, n_heads),
        in_specs=[blk(0), blk(0), blk(1), blk(2),
                  pl.BlockSpec((hgrn_lb.shape[0], HEAD), lambda g, h: (0, h)),
                  pl.BlockSpec((1, HEAD), lambda g, h: (0, h)),
                  st_spec],
        out_specs=[pl.BlockSpec((tg, HEAD), lambda g, h: (g, h)), st_spec],
        out_shape=[jax.ShapeDtypeStruct((nb, d), BF16),
                   jax.ShapeDtypeStruct(state.shape, F32)],
        scratch_shapes=[pltpu.VMEM((tg, HEAD), F32)],
        compiler_params=_params(2),
        name="hgrn_sample",
    )(p16, p32, p16, p16, hgrn_lb, g_norm.reshape(1, d), state)


_CARRY = 8


def _conv_prompt_body(tb, b_ref, c_ref, h_ref, w_ref, z_ref, buf_ref, u_scr):
    t = pl.program_id(1)

    @pl.when(t == 0)
    def _():
        u_scr[0:_CARRY, :] = jnp.zeros((_CARRY, u_scr.shape[1]), F32)

    u = c_ref[...].astype(F32) * h_ref[...].astype(F32)
    u_scr[_CARRY:_CARRY + tb, :] = u
    w = w_ref[...]
    y = w[0:1, :] * u_scr[_CARRY - 2:_CARRY - 2 + tb, :]
    y = y + w[1:2, :] * u_scr[_CARRY - 1:_CARRY - 1 + tb, :]
    y = y + w[2:3, :] * u
    z_ref[...] = (b_ref[...].astype(F32) * y).astype(BF16)
    u_scr[0:_CARRY, :] = u[tb - _CARRY:tb, :]

    @pl.when(t == pl.num_programs(1) - 1)
    def _():
        buf_ref[0] = u[tb - (CONV_W - 1):tb, :]


def _conv_prompt(pc, conv_w, batch, seq, tb):
    d = conv_w.shape[1]
    n_t = seq // tb
    blk = lambda kk: pl.BlockSpec((tb, d), lambda b, t, kk=kk: (b * n_t + t, kk))
    return pl.pallas_call(
        functools.partial(_conv_prompt_body, tb),
        grid=(batch, n_t),
        in_specs=[blk(0), blk(1), blk(2), pl.BlockSpec(conv_w.shape, lambda b, t: (0, 0))],
        out_specs=[pl.BlockSpec((tb, d), lambda b, t: (b * n_t + t, 0)),
                   pl.BlockSpec((1, CONV_W - 1, d), lambda b, t: (b, 0, 0))],
        out_shape=[jax.ShapeDtypeStruct((batch * seq, d), BF16),
                   jax.ShapeDtypeStruct((batch, CONV_W - 1, d), F32)],
        scratch_shapes=[pltpu.VMEM((_CARRY + tb, d), F32)],
        compiler_params=_params(2),
        name="conv_prompt",
    )(pc, pc, pc, conv_w)


def _conv_sample_body(d, b_ref, c_ref, h_ref, w_ref, st_ref, z_ref, buf_ref):
    u = c_ref[...].astype(F32) * h_ref[...].astype(F32)
    w = w_ref[...]
    buf0 = st_ref[:, 0:d]
    buf1 = st_ref[:, d:2 * d]
    y = w[0:1, :] * buf0
    y = y + w[1:2, :] * buf1
    y = y + w[2:3, :] * u
    z_ref[...] = (b_ref[...].astype(F32) * y).astype(BF16)
    buf_ref[:, 0:d] = buf1
    buf_ref[:, d:2 * d] = u


def _conv_sample(pc, conv_w, state2d, row0):
    nb = state2d.shape[0]
    d = conv_w.shape[1]
    rb0 = row0 // nb
    blk = lambda kk: pl.BlockSpec((nb, d), lambda i, kk=kk: (rb0, kk))
    return pl.pallas_call(
        functools.partial(_conv_sample_body, d),
        grid=(1,),
        in_specs=[blk(0), blk(1), blk(2), pl.BlockSpec(conv_w.shape, lambda i: (0, 0)),
                  pl.BlockSpec(state2d.shape, lambda i: (0, 0))],
        out_specs=[pl.BlockSpec((nb, d), lambda i: (0, 0)),
                   pl.BlockSpec(state2d.shape, lambda i: (0, 0))],
        out_shape=[jax.ShapeDtypeStruct((nb, d), BF16),
                   jax.ShapeDtypeStruct(state2d.shape, F32)],
        compiler_params=_params(1),
        name="conv_sample",
    )(pc, pc, pc, conv_w, state2d)


def _outproj_body(n_blocks, x_ref, z_ref, w_ref, o_ref, w16_ref):
    i = pl.program_id(0)

    @pl.when(i == 0)
    def _():
        w16_ref[...] = w_ref[...].astype(BF16)

    @pl.when(i < n_blocks)
    def _():
        o_ref[...] = x_ref[...] + jnp.dot(z_ref[...], w16_ref[...], preferred_element_type=F32)

    @pl.when(i >= n_blocks)
    def _():
        o_ref[...] = jnp.zeros_like(o_ref)


def _outproj_prompt(x, z, w, n_rows, tm):
    d = x.shape[1]
    n_blocks = n_rows // tm
    row_blk = lambda i: (jnp.minimum(i, n_blocks - 1), 0)
    return pl.pallas_call(
        functools.partial(_outproj_body, n_blocks),
        grid=(n_blocks + 1,),
        in_specs=[pl.BlockSpec((tm, d), row_blk),
                  pl.BlockSpec((tm, d), row_blk),
                  pl.BlockSpec((d, d), lambda i: (0, 0))],
        out_specs=pl.BlockSpec((tm, d), lambda i: (i, 0)),
        out_shape=jax.ShapeDtypeStruct(((n_blocks + 1) * tm, d), F32),
        scratch_shapes=[pltpu.VMEM((d, d), BF16)],
        compiler_params=_params(1),
        name="outproj",
    )(x, z, w)


def _outproj_sample_body(x_ref, z_ref, w_ref, buf_ref, o_ref):
    del buf_ref
    o_ref[...] = x_ref[...] + jnp.dot(z_ref[...], w_ref[...].astype(BF16), preferred_element_type=F32)


def _outproj_sample(x, x_row0, z, w, buf, row0):
    nb, d = z.shape
    return pl.pallas_call(
        _outproj_sample_body,
        grid=(1,),
        in_specs=[pl.BlockSpec((nb, d), lambda i: (x_row0 // nb, 0)),
                  pl.BlockSpec((nb, d), lambda i: (0, 0)),
                  pl.BlockSpec((d, d), lambda i: (0, 0)),
                  pl.BlockSpec(memory_space=pl.ANY)],
        out_specs=pl.BlockSpec((nb, d), lambda i: (row0 // nb, 0)),
        out_shape=jax.ShapeDtypeStruct(buf.shape, buf.dtype),
        input_output_aliases={3: 0},
        compiler_params=_params(1),
        name="outproj_sample",
    )(x, z, w, buf)


def _swiglu_partial(xn, w1, w3, w2):
    h1 = jnp.dot(xn, w1.astype(BF16), preferred_element_type=F32)
    h3 = jnp.dot(xn, w3.astype(BF16), preferred_element_type=F32)
    h = (jax.nn.silu(h1) * h3).astype(BF16)
    return jnp.dot(h, w2.astype(BF16), preferred_element_type=F32)


def _cache_weight_chunk(w1_hbm, w3_hbm, w2_hbm, f, n_f, tf, stages, caches, wsem):
    n_slots = stages[0].shape[0]

    def copies(ff, ws):
        cols = pl.ds(pl.multiple_of(ff * tf, tf), tf)
        srcs = (w1_hbm.at[:, cols], w3_hbm.at[:, cols], w2_hbm.at[cols, :])
        return [pltpu.make_async_copy(src, st.at[ws], wsem.at[ws, m]) for m, (src, st) in enumerate(zip(srcs, stages))]

    if n_slots == 1:
        ws = 0
        for c in copies(f, 0):
            c.start()
    else:
        ws = lax.rem(f, 2)

        @pl.when(f == 0)
        def _():
            for c in copies(0, 0):
                c.start()

        @pl.when(f + 1 < n_f)
        def _():
            for c in copies(f + 1, 1 - ws):
                c.start()

    for c in copies(f, ws):
        c.wait()
    for st, ca in zip(stages, caches):
        ca[f] = st[ws].astype(BF16)


def _ffn_body(n_f, tf, x_ref, g_ref, w1_hbm, w3_hbm, w2_hbm, o_ref, xn_ref, acc_ref, c1, c3, c2, s1, s3, s2, wsem):
    i = pl.program_id(0)
    f = pl.program_id(1)

    @pl.when(f == 0)
    def _():
        xn_ref[...] = _rmsnorm_bf16(x_ref[...], g_ref[...])
        acc_ref[...] = jnp.zeros_like(acc_ref)

    @pl.when(i == 0)
    def _():
        _cache_weight_chunk(w1_hbm, w3_hbm, w2_hbm, f, n_f, tf, (s1, s3, s2), (c1, c3, c2), wsem)

    acc_ref[...] += _swiglu_partial(xn_ref[...], c1[f], c3[f], c2[f])

    @pl.when(f == n_f - 1)
    def _():
        o_ref[...] = x_ref[...] + acc_ref[...]


def _ffn(x, n, g, w1, w3, w2, tm, tf):
    d = x.shape[1]
    dff = w1.shape[1]
    n_f = dff // tf
    return pl.pallas_call(
        functools.partial(_ffn_body, n_f, tf),
        grid=(n // tm, n_f),
        in_specs=[pl.BlockSpec((tm, d), lambda i, f: (i, 0)),
                  pl.BlockSpec((1, d), lambda i, f: (0, 0)),
                  pl.BlockSpec(memory_space=pl.ANY),
                  pl.BlockSpec(memory_space=pl.ANY),
                  pl.BlockSpec(memory_space=pl.ANY)],
        out_specs=pl.BlockSpec((tm, d), lambda i, f: (i, 0)),
        out_shape=jax.ShapeDtypeStruct((n, d), F32),
        scratch_shapes=[pltpu.VMEM((tm, d), BF16), pltpu.VMEM((tm, d), F32),
                        pltpu.VMEM((n_f, d, tf), BF16), pltpu.VMEM((n_f, d, tf), BF16),
                        pltpu.VMEM((n_f, tf, d), BF16),
                        pltpu.VMEM((1, d, tf), F32), pltpu.VMEM((1, d, tf), F32), pltpu.VMEM((1, tf, d), F32),
                        pltpu.SemaphoreType.DMA((1, 3))],
        compiler_params=_params(2),
        name="ffn",
    )(x, g.reshape(1, d), w1, w3, w2)


ROUTE_ROWS = 8


def _router_body(n_experts, n_tok, x_ref, g_ref, rw_ref, route_ref, cnt_ref, cnt_scr):
    i = pl.program_id(0)
    tm = x_ref.shape[0]

    @pl.when(i == 0)
    def _():
        cnt_scr[...] = jnp.zeros_like(cnt_scr)

    xn = _rmsnorm_bf16(x_ref[...], g_ref[...])
    logits = jnp.dot(xn, rw_ref[...].astype(BF16), preferred_element_type=F32)
    lane = lax.broadcasted_iota(I32, (tm, LANES), 1).astype(F32)
    neg = jnp.float32(-jnp.inf)
    logits = jnp.where(lane < n_experts, logits, neg)
    m1 = jnp.max(logits, axis=-1, keepdims=True)
    i1 = jnp.min(jnp.where(logits == m1, lane, float(LANES)), axis=-1, keepdims=True)
    i1 = jnp.minimum(i1, float(n_experts - 1))
    rest = jnp.where(lane == i1, neg, logits)
    m2 = jnp.max(rest, axis=-1, keepdims=True)
    i2 = jnp.min(jnp.where(rest == m2, lane, float(LANES)), axis=-1, keepdims=True)
    i2 = jnp.minimum(i2, float(n_experts - 1))
    e2 = jnp.exp(m2 - m1)
    den = 1.0 + e2
    g1 = 1.0 / den
    g2 = e2 / den

    sel1 = lane == i1
    sel2 = lane == i2
    is_token = i * tm + lax.broadcasted_iota(I32, (tm, LANES), 0) < n_tok
    onehot = jnp.logical_and(jnp.logical_or(sel1, sel2), is_token)
    row = lax.broadcasted_iota(I32, (tm, tm), 0)
    col = lax.broadcasted_iota(I32, (tm, tm), 1)
    before = (row > col).astype(BF16)
    rank = jnp.dot(before, onehot.astype(BF16), preferred_element_type=F32) + cnt_scr[...]
    r1 = jnp.sum(jnp.where(sel1, rank, 0.0), axis=-1, keepdims=True)
    r2 = jnp.sum(jnp.where(sel2, rank, 0.0), axis=-1, keepdims=True)
    cnt_scr[...] += jnp.sum(onehot.astype(F32), axis=0, keepdims=True)

    out = jnp.zeros((tm, LANES), F32)
    for slot, val in enumerate((i1, i2, g1, g2, r1, r2)):
        out = jnp.where(lane == slot, val, out)
    route_ref[...] = out.T[0:ROUTE_ROWS, :]
    cnt_ref[...] = cnt_scr[...]


def _router(x, n_tok, g, rw_pad, n_experts, tm):
    n, d = x.shape
    return pl.pallas_call(
        functools.partial(_router_body, n_experts, n_tok),
        grid=(n // tm,),
        in_specs=[pl.BlockSpec((tm, d), lambda i: (i, 0)),
                  pl.BlockSpec((1, d), lambda i: (0, 0)),
                  pl.BlockSpec((d, LANES), lambda i: (0, 0))],
        out_specs=[pl.BlockSpec((ROUTE_ROWS, tm), lambda i: (0, i)),
                   pl.BlockSpec((1, LANES), lambda i: (0, 0))],
        out_shape=[jax.ShapeDtypeStruct((ROUTE_ROWS, n), F32),
                   jax.ShapeDtypeStruct((1, LANES), F32)],
        scratch_shapes=[pltpu.VMEM((1, LANES), F32)],
        compiler_params=_params(1),
        name="router",
    )(x, g.reshape(1, d), rw_pad)


SUBLANES = 8
MOE_TM = 448


def _start_row_group(src_hbm, idx_ref, g, dst, sem):
    for k in range(SUBLANES):
        pltpu.make_async_copy(src_hbm.at[pl.ds(idx_ref[0, 0, g * SUBLANES + k], 1)],
                              dst.at[g, pl.ds(k, 1)], sem).start()


def _start_row_gather(src_hbm, idx_ref, n_rows, dst, sem, unrolled=False):
    if unrolled:
        for g in range(n_rows // SUBLANES):
            _start_row_group(src_hbm, idx_ref, g, dst, sem)
        return

    def body(g, carry):
        _start_row_group(src_hbm, idx_ref, g, dst, sem)
        return carry

    lax.fori_loop(0, n_rows // SUBLANES, body, 0)


def _wait_row_gather(src_hbm, n_rows, dst, sem):
    pltpu.make_async_copy(src_hbm.at[pl.ds(0, n_rows)], dst.reshape(n_rows, dst.shape[-1]), sem).wait()


def _moe_body(n_f, tf, te_ref, nu_ref, idx_cur, idx_nxt, x_hbm, gate_ref, g_ref, w1_hbm, w3_hbm, w2_hbm,
              y_ref, xbuf, sem, xn_ref, acc_ref, c1, c3, c2, s1, s3, s2, wsem):
    i = pl.program_id(0)
    f = pl.program_id(1)
    n_tiles = pl.num_programs(0)
    tm = xbuf.shape[1] * SUBLANES
    rows_per_step = tm // n_f
    slot = lax.rem(i, 2)
    n_used = nu_ref[0]
    used = i < n_used
    e = te_ref[i]
    first_of_expert = jnp.logical_or(i == 0, e != te_ref[jnp.maximum(i - 1, 0)])

    @pl.when(f == 0)
    def _():
        @pl.when(i == 0)
        def _():
            _start_row_gather(x_hbm, idx_cur, tm, xbuf.at[0], sem.at[0])

        @pl.when(i <= n_used)
        def _():
            _wait_row_gather(x_hbm, tm, xbuf.at[slot], sem.at[slot])

        xn_ref[...] = _rmsnorm_bf16(xbuf[slot].reshape(tm, xbuf.shape[-1]), g_ref[...])
        acc_ref[...] = jnp.zeros_like(acc_ref)

    @pl.when(jnp.logical_and(used, first_of_expert))
    def _():
        _cache_weight_chunk(w1_hbm.at[e], w3_hbm.at[e], w2_hbm.at[e], f, n_f, tf, (s1, s3, s2), (c1, c3, c2), wsem)

    @pl.when(used)
    def _():
        for g in range(rows_per_step // SUBLANES):
            _start_row_group(x_hbm, idx_nxt, f * (rows_per_step // SUBLANES) + g, xbuf.at[1 - slot], sem.at[1 - slot])
        acc_ref[...] += _swiglu_partial(xn_ref[...], c1[f], c3[f], c2[f])

    @pl.when(f == n_f - 1)
    def _():
        r_id = lax.broadcasted_iota(I32, (tm, tm), 0)
        c_id = lax.broadcasted_iota(I32, (tm, tm), 1)
        gate_col = jnp.sum(jnp.where(r_id == c_id, gate_ref[0], 0.0), axis=1, keepdims=True)
        y_ref[...] = acc_ref[...] * gate_col

        @pl.when(jnp.logical_and(used, i == n_tiles - 1))
        def _():
            _wait_row_gather(x_hbm, tm, xbuf.at[1 - slot], sem.at[1 - slot])


def _moe(x, g, w1, w3, w2, tile_expert, n_used, inv3, gate_sorted, tm, tf):
    n_tiles = inv3.shape[0]
    d = x.shape[1]
    dff = w1.shape[2]
    n_f = dff // tf
    assert tm % (n_f * SUBLANES) == 0, "each hidden chunk fetches whole 8-row tiles of the next row tile"
    grid_spec = pltpu.PrefetchScalarGridSpec(
        num_scalar_prefetch=2,
        grid=(n_tiles, n_f),
        in_specs=[pl.BlockSpec((1, 1, tm), lambda i, f, te, nu: (i, 0, 0), memory_space=pltpu.SMEM),
                  pl.BlockSpec((1, 1, tm), lambda i, f, te, nu: (jnp.minimum(i + 1, n_tiles - 1), 0, 0),
                               memory_space=pltpu.SMEM),
                  pl.BlockSpec(memory_space=pl.ANY),
                  pl.BlockSpec((1, 1, tm), lambda i, f, te, nu: (i, 0, 0)),
                  pl.BlockSpec((1, d), lambda i, f, te, nu: (0, 0)),
                  pl.BlockSpec(memory_space=pl.ANY),
                  pl.BlockSpec(memory_space=pl.ANY),
                  pl.BlockSpec(memory_space=pl.ANY)],
        out_specs=pl.BlockSpec((tm, d), lambda i, f, te, nu: (i, 0)),
        scratch_shapes=[pltpu.VMEM((2, tm // SUBLANES, SUBLANES, d), F32),
                        pltpu.SemaphoreType.DMA((2,)),
                        pltpu.VMEM((tm, d), BF16),
                        pltpu.VMEM((tm, d), F32),
                        pltpu.VMEM((n_f, d, tf), BF16),
                        pltpu.VMEM((n_f, d, tf), BF16),
                        pltpu.VMEM((n_f, tf, d), BF16),
                        pltpu.VMEM((2, d, tf), F32),
                        pltpu.VMEM((2, d, tf), F32),
                        pltpu.VMEM((2, tf, d), F32),
                        pltpu.SemaphoreType.DMA((2, 3))],
    )
    return pl.pallas_call(
        functools.partial(_moe_body, n_f, tf),
        grid_spec=grid_spec,
        out_shape=jax.ShapeDtypeStruct((n_tiles * tm, d), F32),
        compiler_params=_params(2),
        name="moe",
    )(tile_expert, n_used, inv3, inv3, x, gate_sorted, g.reshape(1, d), w1, w3, w2)


def _combine_body(pos_cur, pos_nxt, x_ref, g_ref, y_hbm, o_ref, ybuf, sem):
    i = pl.program_id(0)
    n_tiles = pl.num_programs(0)
    rows = ybuf.shape[1] * SUBLANES
    slot = lax.rem(i, 2)

    @pl.when(i == 0)
    def _():
        _start_row_gather(y_hbm, pos_cur, rows, ybuf.at[0], sem.at[0])

    @pl.when(i + 1 < n_tiles)
    def _():
        _start_row_gather(y_hbm, pos_nxt, rows, ybuf.at[1 - slot], sem.at[1 - slot], unrolled=True)

    _wait_row_gather(y_hbm, rows, ybuf.at[slot], sem.at[slot])
    tc = rows // TOP_K
    y = ybuf[slot].reshape(rows, ybuf.shape[-1])
    moe = y[0:tc, :] + y[tc:rows, :]
    x = x_ref[...] + moe
    ms = jnp.mean(x * x, axis=-1, keepdims=True)
    o_ref[...] = x * lax.rsqrt(ms + EPS) * g_ref[...]


def _combine(x, g, y_sorted, pos3, row0, n_rows, tc):
    d = x.shape[1]
    n_tiles = n_rows // tc
    rb0 = row0 // tc
    grid_spec = pltpu.PrefetchScalarGridSpec(
        num_scalar_prefetch=0,
        grid=(n_tiles,),
        in_specs=[pl.BlockSpec((1, 1, TOP_K * tc), lambda i: (i, 0, 0), memory_space=pltpu.SMEM),
                  pl.BlockSpec((1, 1, TOP_K * tc), lambda i: (jnp.minimum(i + 1, n_tiles - 1), 0, 0),
                               memory_space=pltpu.SMEM),
                  pl.BlockSpec((tc, d), lambda i: (rb0 + i, 0)),
                  pl.BlockSpec((1, d), lambda i: (0, 0)),
                  pl.BlockSpec(memory_space=pl.ANY)],
        out_specs=pl.BlockSpec((tc, d), lambda i: (i, 0)),
        scratch_shapes=[pltpu.VMEM((2, TOP_K * tc // SUBLANES, SUBLANES, d), F32), pltpu.SemaphoreType.DMA((2,))],
    )
    return pl.pallas_call(
        _combine_body,
        grid_spec=grid_spec,
        out_shape=jax.ShapeDtypeStruct((n_rows, d), F32),
        compiler_params=_params(1),
        name="combine",
    )(pos3, pos3, x, g.reshape(1, d), y_sorted)


def _lookup(table, idx):
    out = jnp.zeros(idx.shape, table.dtype)
    for e in range(table.shape[0]):
        out = jnp.where(idx == e, table[e], out)
    return out


def _expert_layer(x, g_ffn, g_final, router_w, w1, w3, w2, n_prompt, n_dec, tm_router, tm_moe, tf, tc):
    n = n_prompt + n_dec
    n_experts = router_w.shape[-1]
    rw_pad = jnp.pad(router_w, ((0, 0), (0, LANES - n_experts)))
    route, cnt = _router(x, n, g_ffn, rw_pad, n_experts, tm_router)
    ids = route[0:TOP_K, :n].astype(I32)
    gates = route[TOP_K:2 * TOP_K, :n]
    ranks = route[2 * TOP_K:3 * TOP_K, :n].astype(I32)
    counts = cnt[0, :n_experts].astype(I32)

    n_tiles = (TOP_K * n + n_experts * (tm_moe - 1)) // tm_moe
    padded = ((counts + tm_moe - 1) // tm_moe) * tm_moe
    ends = jnp.cumsum(padded)
    starts = ends - padded
    cstart = jnp.cumsum(counts) - counts
    pos = _lookup(starts, ids) + ranks
    keys = ids * n + jnp.arange(n, dtype=I32)[None, :]
    order = jnp.argsort(keys.reshape(-1)).astype(I32)
    slot_pos = jnp.arange(n_tiles * tm_moe, dtype=I32)
    slot_e = jnp.minimum(jnp.sum(slot_pos[None, :] >= ends[:, None], axis=0), n_experts - 1).astype(I32)
    slot_rank = slot_pos - _lookup(starts, slot_e)
    slot_valid = slot_rank < _lookup(counts, slot_e)
    slot_asg = order[jnp.clip(_lookup(cstart, slot_e) + slot_rank, 0, TOP_K * n - 1)]
    inv = jnp.where(slot_valid, slot_asg % n, 0).astype(I32)
    gate_sorted = jnp.where(slot_valid, gates.reshape(-1)[slot_asg], 0.0)
    tile_expert = slot_e[::tm_moe]
    n_used = (ends[-1] // tm_moe).astype(I32).reshape(1)

    y_sorted = _moe(x, g_ffn, w1, w3, w2, tile_expert, n_used, inv.reshape(n_tiles, 1, tm_moe),
                    gate_sorted.reshape(n_tiles, 1, tm_moe), tm_moe, tf)

    def tile_positions(row0, n_rows, t):
        p = pos[:, row0:row0 + n_rows].reshape(TOP_K, n_rows // t, t)
        return jnp.swapaxes(p, 0, 1).reshape(n_rows // t, 1, TOP_K * t)

    y_prompt = _combine(x, g_final, y_sorted, tile_positions(0, n_prompt, tc), 0, n_prompt, tc)
    y_sample = _combine(x, g_final, y_sorted, tile_positions(n_prompt, n_dec, n_dec), n_prompt, n_dec, n_dec)
    return y_prompt, y_sample


def kernel(x_prompt, x_sample, state_hgrn, state_conv, norm_mix, norm_ffn, norm_final, hgrn_w_in, hgrn_lb, hgrn_g_norm, hgrn_w_out, conv_w_in, conv_w, conv_w_out, ffn_w1, ffn_w3, ffn_w2, moe_router, moe_w1, moe_w3, moe_w2):
    batch, seq, d = x_prompt.shape
    n_dec = x_sample.shape[0]
    n_prompt = batch * seq
    n = n_prompt + n_dec
    n_experts = moe_router.shape[-1]
    assert x_sample.shape[1] == 1 and d % HEAD == 0 and seq % CHUNK == 0
    assert norm_mix.shape[0] == 2, "one HGRN2 layer followed by one short-conv layer"

    tm = _pick_tile(n, (688, 384, 128, 16))
    tm_proj = _pick_tile(n, (1376, 688, 384, 128, 16))
    tm_prompt = _pick_tile(n_prompt, (1024, 512, 256, 128, 16))
    tb = _pick_tile(seq, (256, 128, 64, 32))
    tb_conv = _pick_tile(seq, (512, 256, 128, 64, 32))
    tg = _pick_tile(n_dec, (16,))
    tf = _pick_tile(ffn_w1.shape[-1], (512, 256, 128))
    tm_moe = _pick_tile(n_prompt, (512, 128, 16))
    tm_moe = MOE_TM if n_prompt % 128 == 0 and ffn_w1.shape[-1] % (7 * 512) == 0 else tm_moe
    tc = _pick_tile(n_prompt, (512, 128, 16))
    tm_router = tm_prompt if tm_prompt % LANES == 0 else n_prompt + tm_prompt
    assert n_prompt % tg == 0 and n_prompt % n_dec == 0 and n_prompt % tc == 0

    xp = x_prompt.reshape(n_prompt, d)
    xs = x_sample.reshape(n_dec, d)

    p16, p32 = _proj(xp, norm_mix[0], hgrn_w_in[0], tm_prompt, f32_col=1)
    p16_s, p32_s = _proj(xs, norm_mix[0], hgrn_w_in[0], n_dec, f32_col=1)
    o_p, s_prompt = _hgrn_prompt(p16, p32, hgrn_lb, hgrn_g_norm[0], 0, batch, seq, tb)
    o_s, s_sample = _hgrn_sample(p16_s, p32_s, hgrn_lb, hgrn_g_norm[0], state_hgrn, 0, 0, tg)
    x = _outproj_prompt(xp, o_p, hgrn_w_out[0], n_prompt, tm_prompt)
    x = _outproj_sample(xs, 0, o_s, hgrn_w_out[0], x, n_prompt)
    x = _ffn(x, n, norm_ffn[0], ffn_w1[0], ffn_w3[0], ffn_w2[0], tm, tf)

    (pc,) = _proj(x, norm_mix[1], conv_w_in[0], tm_proj)
    z_p, c_prompt = _conv_prompt(pc, conv_w[0], batch, seq, tb_conv)
    z_s, c_sample = _conv_sample(pc, conv_w[0], state_conv.reshape(n_dec, (CONV_W - 1) * d), n_prompt)
    x1 = _outproj_prompt(x, z_p, conv_w_out[0], n_prompt, tm_prompt)
    x1 = _outproj_sample(x, n_prompt, z_s, conv_w_out[0], x1, n_prompt)

    y_prompt, y_sample = _expert_layer(x1, norm_ffn[1], norm_final, moe_router[0], moe_w1[0], moe_w3[0], moe_w2[0],
                                       n_prompt, n_dec, tm_router, tm_moe, tf, tc)

    return (y_prompt.reshape(batch, seq, d),
            y_sample.reshape(n_dec, 1, d),
            s_prompt,
            s_sample,
            c_prompt.reshape(batch, 1, CONV_W - 1, d),
            c_sample.reshape(n_dec, 1, CONV_W - 1, d))
```

```python
import functools

import jax
import jax.numpy as jnp
from jax import lax
from jax.experimental import pallas as pl
from jax.experimental.pallas import tpu as pltpu

F32 = jnp.float32
BF16 = jnp.bfloat16
I32 = jnp.int32

EPS = 1e-6
HEAD = 128
CHUNK = 32
CONV_W = 3
TOP_K = 2
LANES = 128
V7X_VMEM_LIMIT = 56 * 1024 * 1024

ARB = "arbitrary"


def _params(n_axes):
    return pltpu.CompilerParams(dimension_semantics=(ARB,) * n_axes, vmem_limit_bytes=V7X_VMEM_LIMIT)


def _pick_tile(n, candidates):
    for c in candidates:
        if n % c == 0:
            return c
    raise ValueError(f"no tile in {candidates} divides {n}")


def _rmsnorm_bf16(x, g):
    ms = jnp.mean(x * x, axis=-1, keepdims=True)
    return (x * lax.rsqrt(ms + EPS) * g).astype(BF16)


def _proj_body(f32_col, x_ref, g_ref, w_hbm, *refs):
    o16_ref = refs[0]
    xn_ref, w16_ref, stage_ref, wsem = refs[-4:]
    i = pl.program_id(0)
    j = pl.program_id(1)
    d = x_ref.shape[1]

    @pl.when(j == 0)
    def _():
        xn_ref[...] = _rmsnorm_bf16(x_ref[...], g_ref[...])

    @pl.when(i == 0)
    def _():
        cp = pltpu.make_async_copy(w_hbm.at[:, pl.ds(pl.multiple_of(j * d, d), d)], stage_ref, wsem)
        cp.start()
        cp.wait()
        w16_ref[j] = stage_ref[...].astype(BF16)

    def result():
        return jnp.dot(xn_ref[...], w16_ref[j], preferred_element_type=F32)

    if f32_col is None:
        o16_ref[...] = result().astype(BF16)
    else:
        o32_ref = refs[1]

        @pl.when(j == f32_col)
        def _():
            o32_ref[...] = result()

        @pl.when(j != f32_col)
        def _():
            o16_ref[...] = result().astype(BF16)


def _proj(x, g, w, tm, f32_col=None):
    n, d = x.shape
    n_col = w.shape[1] // d
    if f32_col is None:
        col16 = lambda i, j: (i, j)
        n16 = n_col
    else:
        assert 0 < f32_col < n_col
        col16 = lambda i, j: (i, jnp.where(j >= f32_col, j - 1, j))
        n16 = n_col - 1
    out_specs = [pl.BlockSpec((tm, d), col16)]
    out_shape = [jax.ShapeDtypeStruct((n, n16 * d), BF16)]
    if f32_col is not None:
        out_specs.append(pl.BlockSpec((tm, d), lambda i, j: (i, 0)))
        out_shape.append(jax.ShapeDtypeStruct((n, d), F32))
    return pl.pallas_call(
        functools.partial(_proj_body, f32_col),
        grid=(n // tm, n_col),
        in_specs=[pl.BlockSpec((tm, d), lambda i, j: (i, 0)),
                  pl.BlockSpec((1, d), lambda i, j: (0, 0)),
                  pl.BlockSpec(memory_space=pl.ANY)],
        out_specs=out_specs,
        out_shape=out_shape,
        scratch_shapes=[pltpu.VMEM((tm, d), BF16), pltpu.VMEM((n_col, d, d), BF16),
                        pltpu.VMEM((d, d), F32), pltpu.SemaphoreType.DMA(())],
        compiler_params=_params(2),
        name="proj",
    )(x, g.reshape(1, d), w)


def _forget_lower_bound(lb_ref, layer):
    lbw = lb_ref[...]
    e = jnp.exp(lbw - jnp.max(lbw, axis=0, keepdims=True))
    sm = e / jnp.sum(e, axis=0, keepdims=True)
    return jnp.sum(sm[:layer + 1], axis=0, keepdims=True)


def _head_rmsnorm(o):
    return o * lax.rsqrt(jnp.mean(o * o, axis=-1, keepdims=True) + EPS)


def _sigmoid(x):
    return 0.5 * (jnp.tanh(0.5 * x) + 1.0)


def _silu(x):
    h = 0.5 * x
    return h + h * jnp.tanh(h)


def _split3_bf16(x):
    hi = x.astype(BF16)
    r1 = x - hi.astype(F32)
    mid = r1.astype(BF16)
    lo = (r1 - mid.astype(F32)).astype(BF16)
    return hi, mid, lo


def _hgrn_prompt_body(layer, n_heads, tb, q_ref, f_ref, v_ref, gt_ref, lb_ref, gn_ref,
                      o_ref, s_ref, st_scr, qin_scr, kin_scr, kdec_scr, g_scr, sprev_scr):
    t = pl.program_id(1)
    nc = tb // CHUNK
    d = n_heads * HEAD

    @pl.when(t == 0)
    def _():
        st_scr[...] = jnp.zeros_like(st_scr)

    lb = _forget_lower_bound(lb_ref, layer)
    gn = gn_ref[...]
    row = lax.broadcasted_iota(I32, (tb, tb), 0)
    col = lax.broadcasted_iota(I32, (tb, tb), 1)
    same_chunk = (row // CHUNK) == (col // CHUNK)
    causal = jnp.logical_and(same_chunk, row >= col)

    f = lb + (1.0 - lb) * _sigmoid(f_ref[...])
    logf = jnp.log2(f)
    k = 1.0 - f
    tri = causal.astype(BF16)
    G = None
    for part in _split3_bf16(logf):
        term = jnp.dot(tri, part, preferred_element_type=F32)
        G = term if G is None else G + term
    g_scr[...] = G
    g_last = [g_scr[(c + 1) * CHUNK - 1:(c + 1) * CHUNK, :] for c in range(nc)]
    decay = [jnp.exp2(g) for g in g_last]
    decay_rows = jnp.concatenate([jnp.broadcast_to(dc, (CHUNK, d)) for dc in decay], axis=0)
    k_in = k * jnp.exp2(-G)
    qin_scr[...] = (_silu(q_ref[...].astype(F32)) * jnp.exp2(G)).astype(BF16)
    kin_scr[...] = k_in.astype(BF16)
    kdec_scr[...] = (k_in * decay_rows).astype(BF16)

    row_chunk = lax.broadcasted_iota(I32, (tb, HEAD), 0) // CHUNK
    zero16 = jnp.zeros((), BF16)
    grp = next(g for g in (4, 2, 1) if nc % g == 0)

    for h in range(n_heads):
        hs = slice(h * HEAD, (h + 1) * HEAD)
        qh = qin_scr[:, hs]
        kd = kdec_scr[:, hs]
        vh = v_ref[:, hs]
        a = lax.dot_general(qh, kin_scr[:, hs], (((1,), (1,)), ((), ())), preferred_element_type=F32)
        a = jnp.where(causal, a, 0.0).astype(BF16)
        o = jnp.dot(a, vh, preferred_element_type=F32)
        ut = jnp.concatenate(
            [lax.dot_general(vh[c * CHUNK:(c + 1) * CHUNK], kd[c * CHUNK:(c + 1) * CHUNK],
                             (((0,), (0,)), ((), ())), preferred_element_type=F32) for c in range(nc)], axis=0)
        st = st_scr[h]
        for c in range(nc):
            sprev_scr[h, :, c * HEAD:(c + 1) * HEAD] = st.astype(BF16)
            st = st * decay[c][:, hs] + ut[c * HEAD:(c + 1) * HEAD, :]
        st_scr[h] = st
        inter = []
        for g0 in range(0, nc, grp):
            rows = slice(g0 * CHUNK, (g0 + grp) * CHUNK)
            q_blocks = jnp.concatenate(
                [jnp.where(row_chunk[rows] == g0 + j, qh[rows], zero16) for j in range(grp)], axis=1)
            inter.append(lax.dot_general(q_blocks, sprev_scr[h, :, g0 * HEAD:(g0 + grp) * HEAD],
                                         (((1,), (1,)), ((), ())), preferred_element_type=F32))
        o = o + jnp.concatenate(inter, axis=0)
        o = _head_rmsnorm(o) * gn[:, hs] * _silu(gt_ref[:, hs].astype(F32))
        o_ref[:, hs] = o.astype(BF16)

    @pl.when(t == pl.num_programs(1) - 1)
    def _():
        for h in range(n_heads):
            s_ref[0, 0, h] = st_scr[h].T


def _hgrn_prompt(p16, p32, hgrn_lb, g_norm, layer, batch, seq, tb):
    d = g_norm.shape[0]
    n_heads = d // HEAD
    n_t = seq // tb
    blk = lambda kk: pl.BlockSpec((tb, d), lambda b, t, kk=kk: (b * n_t + t, kk))
    return pl.pallas_call(
        functools.partial(_hgrn_prompt_body, layer, n_heads, tb),
        grid=(batch, n_t),
        in_specs=[blk(0), blk(0), blk(1), blk(2),
                  pl.BlockSpec(hgrn_lb.shape, lambda b, t: (0, 0)),
                  pl.BlockSpec((1, d), lambda b, t: (0, 0))],
        out_specs=[pl.BlockSpec((tb, d), lambda b, t: (b * n_t + t, 0)),
                   pl.BlockSpec((1, 1, n_heads, HEAD, HEAD), lambda b, t: (b, 0, 0, 0, 0))],
        out_shape=[jax.ShapeDtypeStruct((batch * seq, d), BF16),
                   jax.ShapeDtypeStruct((batch, 1, n_heads, HEAD, HEAD), F32)],
        scratch_shapes=[pltpu.VMEM((n_heads, HEAD, HEAD), F32),
                        pltpu.VMEM((tb, d), BF16), pltpu.VMEM((tb, d), BF16), pltpu.VMEM((tb, d), BF16),
                        pltpu.VMEM((tb, d), F32),
                        pltpu.VMEM((n_heads, HEAD, (tb // CHUNK) * HEAD), BF16)],
        compiler_params=_params(2),
        name="hgrn_prompt",
    )(p16, p32, p16, p16, hgrn_lb, g_norm.reshape(1, d))


def _hgrn_sample_body(layer, tg, q_ref, f_ref, v_ref, gt_ref, lb_ref, gn_ref, s0_ref,
                      o_ref, s_ref, o_scr):
    lb_h = _forget_lower_bound(lb_ref, layer)
    gn_h = gn_ref[...]

    q = jax.nn.silu(q_ref[...].astype(F32))
    f = lb_h + (1.0 - lb_h) * jax.nn.sigmoid(f_ref[...])
    decay = f
    k = 1.0 - f
    v = v_ref[...].astype(F32)
    q_t, d_t, k_t = q.T, decay.T, k.T
    for j in range(tg):
        s = d_t[:, j:j + 1] * s0_ref[j, 0, 0] + k_t[:, j:j + 1] * v[j:j + 1, :]
        s_ref[j, 0, 0] = s
        o_scr[j:j + 1, :] = jnp.sum(q_t[:, j:j + 1] * s, axis=0, keepdims=True)
    o = _head_rmsnorm(o_scr[...]) * gn_h * jax.nn.silu(gt_ref[...].astype(F32))
    o_ref[...] = o.astype(BF16)


def _hgrn_sample(p16, p32, hgrn_lb, g_norm, state, layer, row0, tg):
    nb = state.shape[0]
    d = g_norm.shape[0]
    n_heads = d // HEAD
    rb0 = row0 // tg
    blk = lambda kk: pl.BlockSpec((tg, HEAD), lambda g, h, kk=kk: (rb0 + g, kk * n_heads + h))
    st_spec = pl.BlockSpec((tg, 1, 1, HEAD, HEAD), lambda g, h: (g, 0, h, 0, 0))
    return pl.pallas_call(
        functools.partial(_hgrn_sample_body, layer, tg),
        grid=(nb // tg, n_heads),
        in_specs=[blk(0), blk(0), blk(1), blk(2),
                  pl.BlockSpec((hgrn_lb.shape[0], HEAD), lambda g, h: (0, h)),
                  pl.BlockSpec((1, HEAD), lambda g, h: (0, h)),
                  st_spec],
        out_specs=[pl.BlockSpec((tg, HEAD), lambda g, h: (g, h)), st_spec],
        out_shape=[jax.ShapeDtypeStruct((nb, d), BF16),
                   jax.ShapeDtypeStruct(state.shape, F32)],
        scratch_shapes=[pltpu.VMEM((tg, HEAD), F32)],
        compiler_params=_params(2),
        name="hgrn_sample",
    )(p16, p32, p16, p16, hgrn_lb, g_norm.reshape(1, d), state)


_CARRY = 8


def _conv_prompt_body(tb, b_ref, c_ref, h_ref, w_ref, z_ref, buf_ref, u_scr):
    t = pl.program_id(1)

    @pl.when(t == 0)
    def _():
        u_scr[0:_CARRY, :] = jnp.zeros((_CARRY, u_scr.shape[1]), F32)

    u = c_ref[...].astype(F32) * h_ref[...].astype(F32)
    u_scr[_CARRY:_CARRY + tb, :] = u
    w = w_ref[...]
    y = w[0:1, :] * u_scr[_CARRY - 2:_CARRY - 2 + tb, :]
    y = y + w[1:2, :] * u_scr[_CARRY - 1:_CARRY - 1 + tb, :]
    y = y + w[2:3, :] * u
    z_ref[...] = (b_ref[...].astype(F32) * y).astype(BF16)
    u_scr[0:_CARRY, :] = u[tb - _CARRY:tb, :]

    @pl.when(t == pl.num_programs(1) - 1)
    def _():
        buf_ref[0] = u[tb - (CONV_W - 1):tb, :]


def _conv_prompt(pc, conv_w, batch, seq, tb):
    d = conv_w.shape[1]
    n_t = seq // tb
    blk = lambda kk: pl.BlockSpec((tb, d), lambda b, t, kk=kk: (b * n_t + t, kk))
    return pl.pallas_call(
        functools.partial(_conv_prompt_body, tb),
        grid=(batch, n_t),
        in_specs=[blk(0), blk(1), blk(2), pl.BlockSpec(conv_w.shape, lambda b, t: (0, 0))],
        out_specs=[pl.BlockSpec((tb, d), lambda b, t: (b * n_t + t, 0)),
                   pl.BlockSpec((1, CONV_W - 1, d), lambda b, t: (b, 0, 0))],
        out_shape=[jax.ShapeDtypeStruct((batch * seq, d), BF16),
                   jax.ShapeDtypeStruct((batch, CONV_W - 1, d), F32)],
        scratch_shapes=[pltpu.VMEM((_CARRY + tb, d), F32)],
        compiler_params=_params(2),
        name="conv_prompt",
    )(pc, pc, pc, conv_w)


def _conv_sample_body(d, b_ref, c_ref, h_ref, w_ref, st_ref, z_ref, buf_ref):
    u = c_ref[...].astype(F32) * h_ref[...].astype(F32)
    w = w_ref[...]
    buf0 = st_ref[:, 0:d]
    buf1 = st_ref[:, d:2 * d]
    y = w[0:1, :] * buf0
    y = y + w[1:2, :] * buf1
    y = y + w[2:3, :] * u
    z_ref[...] = (b_ref[...].astype(F32) * y).astype(BF16)
    buf_ref[:, 0:d] = buf1
    buf_ref[:, d:2 * d] = u


def _conv_sample(pc, conv_w, state2d, row0):
    nb = state2d.shape[0]
    d = conv_w.shape[1]
    rb0 = row0 // nb
    blk = lambda kk: pl.BlockSpec((nb, d), lambda i, kk=kk: (rb0, kk))
    return pl.pallas_call(
        functools.partial(_conv_sample_body, d),
        grid=(1,),
        in_specs=[blk(0), blk(1), blk(2), pl.BlockSpec(conv_w.shape, lambda i: (0, 0)),
                  pl.BlockSpec(state2d.shape, lambda i: (0, 0))],
        out_specs=[pl.BlockSpec((nb, d), lambda i: (0, 0)),
                   pl.BlockSpec(state2d.shape, lambda i: (0, 0))],
        out_shape=[jax.ShapeDtypeStruct((nb, d), BF16),
                   jax.ShapeDtypeStruct(state2d.shape, F32)],
        compiler_params=_params(1),
        name="conv_sample",
    )(pc, pc, pc, conv_w, state2d)


def _outproj_body(n_blocks, x_ref, z_ref, w_ref, o_ref, w16_ref):
    i = pl.program_id(0)

    @pl.when(i == 0)
    def _():
        w16_ref[...] = w_ref[...].astype(BF16)

    @pl.when(i < n_blocks)
    def _():
        o_ref[...] = x_ref[...] + jnp.dot(z_ref[...], w16_ref[...], preferred_element_type=F32)

    @pl.when(i >= n_blocks)
    def _():
        o_ref[...] = jnp.zeros_like(o_ref)


def _outproj_prompt(x, z, w, n_rows, tm):
    d = x.shape[1]
    n_blocks = n_rows // tm
    row_blk = lambda i: (jnp.minimum(i, n_blocks - 1), 0)
    return pl.pallas_call(
        functools.partial(_outproj_body, n_blocks),
        grid=(n_blocks + 1,),
        in_specs=[pl.BlockSpec((tm, d), row_blk),
                  pl.BlockSpec((tm, d), row_blk),
                  pl.BlockSpec((d, d), lambda i: (0, 0))],
        out_specs=pl.BlockSpec((tm, d), lambda i: (i, 0)),
        out_shape=jax.ShapeDtypeStruct(((n_blocks + 1) * tm, d), F32),
        scratch_shapes=[pltpu.VMEM((d, d), BF16)],
        compiler_params=_params(1),
        name="outproj",
    )(x, z, w)


def _outproj_sample_body(x_ref, z_ref, w_ref, buf_ref, o_ref):
    del buf_ref
    o_ref[...] = x_ref[...] + jnp.dot(z_ref[...], w_ref[...].astype(BF16), preferred_element_type=F32)


def _outproj_sample(x, x_row0, z, w, buf, row0):
    nb, d = z.shape
    return pl.pallas_call(
        _outproj_sample_body,
        grid=(1,),
        in_specs=[pl.BlockSpec((nb, d), lambda i: (x_row0 // nb, 0)),
                  pl.BlockSpec((nb, d), lambda i: (0, 0)),
                  pl.BlockSpec((d, d), lambda i: (0, 0)),
                  pl.BlockSpec(memory_space=pl.ANY)],
        out_specs=pl.BlockSpec((nb, d), lambda i: (row0 // nb, 0)),
        out_shape=jax.ShapeDtypeStruct(buf.shape, buf.dtype),
        input_output_aliases={3: 0},
        compiler_params=_params(1),
        name="outproj_sample",
    )(x, z, w, buf)


def _swiglu_partial(xn, w1, w3, w2):
    h1 = jnp.dot(xn, w1.astype(BF16), preferred_element_type=F32)
    h3 = jnp.dot(xn, w3.astype(BF16), preferred_element_type=F32)
    h = (jax.nn.silu(h1) * h3).astype(BF16)
    return jnp.dot(h, w2.astype(BF16), preferred_element_type=F32)


def _cache_weight_chunk(w1_hbm, w3_hbm, w2_hbm, f, n_f, tf, stages, caches, wsem):
    n_slots = stages[0].shape[0]

    def copies(ff, ws):
        cols = pl.ds(pl.multiple_of(ff * tf, tf), tf)
        srcs = (w1_hbm.at[:, cols], w3_hbm.at[:, cols], w2_hbm.at[cols, :])
        return [pltpu.make_async_copy(src, st.at[ws], wsem.at[ws, m]) for m, (src, st) in enumerate(zip(srcs, stages))]

    if n_slots == 1:
        ws = 0
        for c in copies(f, 0):
            c.start()
    else:
        ws = lax.rem(f, 2)

        @pl.when(f == 0)
        def _():
            for c in copies(0, 0):
                c.start()

        @pl.when(f + 1 < n_f)
        def _():
            for c in copies(f + 1, 1 - ws):
                c.start()

    for c in copies(f, ws):
        c.wait()
    for st, ca in zip(stages, caches):
        ca[f] = st[ws].astype(BF16)


def _ffn_body(n_f, tf, x_ref, g_ref, w1_hbm, w3_hbm, w2_hbm, o_ref, xn_ref, acc_ref, c1, c3, c2, s1, s3, s2, wsem):
    i = pl.program_id(0)
    xn_ref[...] = _rmsnorm_bf16(x_ref[...], g_ref[...])
    acc_ref[...] = jnp.zeros_like(acc_ref)

    def chunk(f, carry):
        @pl.when(i == 0)
        def _():
            _cache_weight_chunk(w1_hbm, w3_hbm, w2_hbm, f, n_f, tf, (s1, s3, s2), (c1, c3, c2), wsem)

        acc_ref[...] += _swiglu_partial(xn_ref[...], c1[f], c3[f], c2[f])
        return carry

    lax.fori_loop(0, n_f, chunk, 0)
    o_ref[...] = x_ref[...] + acc_ref[...]


def _ffn(x, n, g, w1, w3, w2, tm, tf):
    d = x.shape[1]
    dff = w1.shape[1]
    n_f = dff // tf
    return pl.pallas_call(
        functools.partial(_ffn_body, n_f, tf),
        grid=(n // tm,),
        in_specs=[pl.BlockSpec((tm, d), lambda i: (i, 0)),
                  pl.BlockSpec((1, d), lambda i: (0, 0)),
                  pl.BlockSpec(memory_space=pl.ANY),
                  pl.BlockSpec(memory_space=pl.ANY),
                  pl.BlockSpec(memory_space=pl.ANY)],
        out_specs=pl.BlockSpec((tm, d), lambda i: (i, 0)),
        out_shape=jax.ShapeDtypeStruct((n, d), F32),
        scratch_shapes=[pltpu.VMEM((tm, d), BF16), pltpu.VMEM((tm, d), F32),
                        pltpu.VMEM((n_f, d, tf), BF16), pltpu.VMEM((n_f, d, tf), BF16),
                        pltpu.VMEM((n_f, tf, d), BF16),
                        pltpu.VMEM((1, d, tf), F32), pltpu.VMEM((1, d, tf), F32), pltpu.VMEM((1, tf, d), F32),
                        pltpu.SemaphoreType.DMA((1, 3))],
        compiler_params=_params(1),
        name="ffn",
    )(x, g.reshape(1, d), w1, w3, w2)


ROUTE_ROWS = 8


def _router_body(n_experts, n_tok, x_ref, g_ref, rw_ref, route_ref, cnt_ref, cnt_scr):
    i = pl.program_id(0)
    tm = x_ref.shape[0]

    @pl.when(i == 0)
    def _():
        cnt_scr[...] = jnp.zeros_like(cnt_scr)

    xn = _rmsnorm_bf16(x_ref[...], g_ref[...])
    logits = jnp.dot(xn, rw_ref[...].astype(BF16), preferred_element_type=F32)
    lane = lax.broadcasted_iota(I32, (tm, LANES), 1).astype(F32)
    neg = jnp.float32(-jnp.inf)
    logits = jnp.where(lane < n_experts, logits, neg)
    m1 = jnp.max(logits, axis=-1, keepdims=True)
    i1 = jnp.min(jnp.where(logits == m1, lane, float(LANES)), axis=-1, keepdims=True)
    i1 = jnp.minimum(i1, float(n_experts - 1))
    rest = jnp.where(lane == i1, neg, logits)
    m2 = jnp.max(rest, axis=-1, keepdims=True)
    i2 = jnp.min(jnp.where(rest == m2, lane, float(LANES)), axis=-1, keepdims=True)
    i2 = jnp.minimum(i2, float(n_experts - 1))
    e2 = jnp.exp(m2 - m1)
    den = 1.0 + e2
    g1 = 1.0 / den
    g2 = e2 / den

    sel1 = lane == i1
    sel2 = lane == i2
    is_token = i * tm + lax.broadcasted_iota(I32, (tm, LANES), 0) < n_tok
    onehot = jnp.logical_and(jnp.logical_or(sel1, sel2), is_token)
    row = lax.broadcasted_iota(I32, (tm, tm), 0)
    col = lax.broadcasted_iota(I32, (tm, tm), 1)
    before = (row > col).astype(BF16)
    rank = jnp.dot(before, onehot.astype(BF16), preferred_element_type=F32) + cnt_scr[...]
    r1 = jnp.sum(jnp.where(sel1, rank, 0.0), axis=-1, keepdims=True)
    r2 = jnp.sum(jnp.where(sel2, rank, 0.0), axis=-1, keepdims=True)
    cnt_scr[...] += jnp.sum(onehot.astype(F32), axis=0, keepdims=True)

    out = jnp.zeros((tm, LANES), F32)
    for slot, val in enumerate((i1, i2, g1, g2, r1, r2)):
        out = jnp.where(lane == slot, val, out)
    route_ref[...] = out.T[0:ROUTE_ROWS, :]
    cnt_ref[...] = cnt_scr[...]


def _router(x, n_tok, g, rw_pad, n_experts, tm):
    n, d = x.shape
    return pl.pallas_call(
        functools.partial(_router_body, n_experts, n_tok),
        grid=(n // tm,),
        in_specs=[pl.BlockSpec((tm, d), lambda i: (i, 0)),
                  pl.BlockSpec((1, d), lambda i: (0, 0)),
                  pl.BlockSpec((d, LANES), lambda i: (0, 0))],
        out_specs=[pl.BlockSpec((ROUTE_ROWS, tm), lambda i: (0, i)),
                   pl.BlockSpec((1, LANES), lambda i: (0, 0))],
        out_shape=[jax.ShapeDtypeStruct((ROUTE_ROWS, n), F32),
                   jax.ShapeDtypeStruct((1, LANES), F32)],
        scratch_shapes=[pltpu.VMEM((1, LANES), F32)],
        compiler_params=_params(1),
        name="router",
    )(x, g.reshape(1, d), rw_pad)


SUBLANES = 8
MOE_TM = 672


def _start_row_group(src_hbm, idx_ref, g, dst, sem):
    for k in range(SUBLANES):
        pltpu.make_async_copy(src_hbm.at[pl.ds(idx_ref[0, 0, g * SUBLANES + k], 1)],
                              dst.at[g, pl.ds(k, 1)], sem).start()


def _start_row_gather(src_hbm, idx_ref, n_rows, dst, sem, unrolled=False):
    if unrolled:
        for g in range(n_rows // SUBLANES):
            _start_row_group(src_hbm, idx_ref, g, dst, sem)
        return

    def body(g, carry):
        _start_row_group(src_hbm, idx_ref, g, dst, sem)
        return carry

    lax.fori_loop(0, n_rows // SUBLANES, body, 0)


def _wait_row_gather(src_hbm, n_rows, dst, sem):
    pltpu.make_async_copy(src_hbm.at[pl.ds(0, n_rows)], dst.reshape(n_rows, dst.shape[-1]), sem).wait()


def _moe_body(n_f, tf, te_ref, nu_ref, idx_cur, idx_nxt, x_hbm, gate_ref, g_ref, w1_hbm, w3_hbm, w2_hbm,
              y_ref, xbuf, sem, xn_ref, acc_ref, c1, c3, c2, s1, s3, s2, wsem):
    i = pl.program_id(0)
    n_tiles = pl.num_programs(0)
    tm = xbuf.shape[1] * SUBLANES
    groups_per_chunk = tm // n_f // SUBLANES
    slot = lax.rem(i, 2)
    n_used = nu_ref[0]
    used = i < n_used
    e = te_ref[i]
    first_of_expert = jnp.logical_or(i == 0, e != te_ref[jnp.maximum(i - 1, 0)])

    @pl.when(i == 0)
    def _():
        _start_row_gather(x_hbm, idx_cur, tm, xbuf.at[0], sem.at[0])

    @pl.when(i <= n_used)
    def _():
        _wait_row_gather(x_hbm, tm, xbuf.at[slot], sem.at[slot])

    @pl.when(used)
    def _():
        xn_ref[...] = _rmsnorm_bf16(xbuf[slot].reshape(tm, xbuf.shape[-1]), g_ref[...])
        acc_ref[...] = jnp.zeros_like(acc_ref)

        def chunk(f, carry):
            @pl.when(first_of_expert)
            def _():
                _cache_weight_chunk(w1_hbm.at[e], w3_hbm.at[e], w2_hbm.at[e], f, n_f, tf,
                                    (s1, s3, s2), (c1, c3, c2), wsem)

            for g in range(groups_per_chunk):
                _start_row_group(x_hbm, idx_nxt, f * groups_per_chunk + g, xbuf.at[1 - slot], sem.at[1 - slot])
            acc_ref[...] += _swiglu_partial(xn_ref[...], c1[f], c3[f], c2[f])
            return carry

        lax.fori_loop(0, n_f, chunk, 0)
        r_id = lax.broadcasted_iota(I32, (tm, tm), 0)
        c_id = lax.broadcasted_iota(I32, (tm, tm), 1)
        gate_col = jnp.sum(jnp.where(r_id == c_id, gate_ref[0], 0.0), axis=1, keepdims=True)
        y_ref[...] = acc_ref[...] * gate_col

    @pl.when(jnp.logical_not(used))
    def _():
        y_ref[...] = jnp.zeros_like(y_ref)

    @pl.when(jnp.logical_and(used, i == n_tiles - 1))
    def _():
        _wait_row_gather(x_hbm, tm, xbuf.at[1 - slot], sem.at[1 - slot])


def _moe(x, g, w1, w3, w2, tile_expert, n_used, inv3, gate_sorted, tm, tf):
    n_tiles = inv3.shape[0]
    d = x.shape[1]
    dff = w1.shape[2]
    n_f = dff // tf
    assert tm % (n_f * SUBLANES) == 0, "each hidden chunk fetches whole 8-row tiles of the next row tile"
    grid_spec = pltpu.PrefetchScalarGridSpec(
        num_scalar_prefetch=2,
        grid=(n_tiles,),
        in_specs=[pl.BlockSpec((1, 1, tm), lambda i, te, nu: (i, 0, 0), memory_space=pltpu.SMEM),
                  pl.BlockSpec((1, 1, tm), lambda i, te, nu: (jnp.minimum(i + 1, n_tiles - 1), 0, 0),
                               memory_space=pltpu.SMEM),
                  pl.BlockSpec(memory_space=pl.ANY),
                  pl.BlockSpec((1, 1, tm), lambda i, te, nu: (i, 0, 0)),
                  pl.BlockSpec((1, d), lambda i, te, nu: (0, 0)),
                  pl.BlockSpec(memory_space=pl.ANY),
                  pl.BlockSpec(memory_space=pl.ANY),
                  pl.BlockSpec(memory_space=pl.ANY)],
        out_specs=pl.BlockSpec((tm, d), lambda i, te, nu: (i, 0)),
        scratch_shapes=[pltpu.VMEM((2, tm // SUBLANES, SUBLANES, d), F32),
                        pltpu.SemaphoreType.DMA((2,)),
                        pltpu.VMEM((tm, d), BF16),
                        pltpu.VMEM((tm, d), F32),
                        pltpu.VMEM((n_f, d, tf), BF16),
                        pltpu.VMEM((n_f, d, tf), BF16),
                        pltpu.VMEM((n_f, tf, d), BF16),
                        pltpu.VMEM((2, d, tf), F32),
                        pltpu.VMEM((2, d, tf), F32),
                        pltpu.VMEM((2, tf, d), F32),
                        pltpu.SemaphoreType.DMA((2, 3))],
    )
    return pl.pallas_call(
        functools.partial(_moe_body, n_f, tf),
        grid_spec=grid_spec,
        out_shape=jax.ShapeDtypeStruct((n_tiles * tm, d), F32),
        compiler_params=_params(1),
        name="moe",
    )(tile_expert, n_used, inv3, inv3, x, gate_sorted, g.reshape(1, d), w1, w3, w2)


def _combine_body(pos_cur, pos_nxt, x_ref, g_ref, y_hbm, o_ref, ybuf, sem):
    i = pl.program_id(0)
    n_tiles = pl.num_programs(0)
    rows = ybuf.shape[1] * SUBLANES
    slot = lax.rem(i, 2)

    @pl.when(i == 0)
    def _():
        _start_row_gather(y_hbm, pos_cur, rows, ybuf.at[0], sem.at[0])

    @pl.when(i + 1 < n_tiles)
    def _():
        _start_row_gather(y_hbm, pos_nxt, rows, ybuf.at[1 - slot], sem.at[1 - slot], unrolled=True)

    _wait_row_gather(y_hbm, rows, ybuf.at[slot], sem.at[slot])
    tc = rows // TOP_K
    y = ybuf[slot].reshape(rows, ybuf.shape[-1])
    moe = y[0:tc, :] + y[tc:rows, :]
    x = x_ref[...] + moe
    ms = jnp.mean(x * x, axis=-1, keepdims=True)
    o_ref[...] = x * lax.rsqrt(ms + EPS) * g_ref[...]


def _combine(x, g, y_sorted, pos3, row0, n_rows, tc):
    d = x.shape[1]
    n_tiles = n_rows // tc
    rb0 = row0 // tc
    grid_spec = pltpu.PrefetchScalarGridSpec(
        num_scalar_prefetch=0,
        grid=(n_tiles,),
        in_specs=[pl.BlockSpec((1, 1, TOP_K * tc), lambda i: (i, 0, 0), memory_space=pltpu.SMEM),
                  pl.BlockSpec((1, 1, TOP_K * tc), lambda i: (jnp.minimum(i + 1, n_tiles - 1), 0, 0),
                               memory_space=pltpu.SMEM),
                  pl.BlockSpec((tc, d), lambda i: (rb0 + i, 0)),
                  pl.BlockSpec((1, d), lambda i: (0, 0)),
                  pl.BlockSpec(memory_space=pl.ANY)],
        out_specs=pl.BlockSpec((tc, d), lambda i: (i, 0)),
        scratch_shapes=[pltpu.VMEM((2, TOP_K * tc // SUBLANES, SUBLANES, d), F32), pltpu.SemaphoreType.DMA((2,))],
    )
    return pl.pallas_call(
        _combine_body,
        grid_spec=grid_spec,
        out_shape=jax.ShapeDtypeStruct((n_rows, d), F32),
        compiler_params=_params(1),
        name="combine",
    )(pos3, pos3, x, g.reshape(1, d), y_sorted)


def _lookup(table, idx):
    out = jnp.zeros(idx.shape, table.dtype)
    for e in range(table.shape[0]):
        out = jnp.where(idx == e, table[e], out)
    return out


def _expert_layer(x, g_ffn, g_final, router_w, w1, w3, w2, n_prompt, n_dec, tm_router, tm_moe, tf, tc):
    n = n_prompt + n_dec
    n_experts = router_w.shape[-1]
    rw_pad = jnp.pad(router_w, ((0, 0), (0, LANES - n_experts)))
    route, cnt = _router(x, n, g_ffn, rw_pad, n_experts, tm_router)
    ids = route[0:TOP_K, :n].astype(I32)
    gates = route[TOP_K:2 * TOP_K, :n]
    ranks = route[2 * TOP_K:3 * TOP_K, :n].astype(I32)
    counts = cnt[0, :n_experts].astype(I32)

    n_tiles = (TOP_K * n + n_experts * (tm_moe - 1)) // tm_moe
    padded = ((counts + tm_moe - 1) // tm_moe) * tm_moe
    ends = jnp.cumsum(padded)
    starts = ends - padded
    cstart = jnp.cumsum(counts) - counts
    pos = _lookup(starts, ids) + ranks
    keys = ids * n + jnp.arange(n, dtype=I32)[None, :]
    order = jnp.argsort(keys.reshape(-1)).astype(I32)
    slot_pos = jnp.arange(n_tiles * tm_moe, dtype=I32)
    slot_e = jnp.minimum(jnp.sum(slot_pos[None, :] >= ends[:, None], axis=0), n_experts - 1).astype(I32)
    slot_rank = slot_pos - _lookup(starts, slot_e)
    slot_valid = slot_rank < _lookup(counts, slot_e)
    slot_asg = order[jnp.clip(_lookup(cstart, slot_e) + slot_rank, 0, TOP_K * n - 1)]
    inv = jnp.where(slot_valid, slot_asg % n, 0).astype(I32)
    gate_sorted = jnp.where(slot_valid, gates.reshape(-1)[slot_asg], 0.0)
    tile_expert = slot_e[::tm_moe]
    n_used = (ends[-1] // tm_moe).astype(I32).reshape(1)

    y_sorted = _moe(x, g_ffn, w1, w3, w2, tile_expert, n_used, inv.reshape(n_tiles, 1, tm_moe),
                    gate_sorted.reshape(n_tiles, 1, tm_moe), tm_moe, tf)

    def tile_positions(row0, n_rows, t):
        p = pos[:, row0:row0 + n_rows].reshape(TOP_K, n_rows // t, t)
        return jnp.swapaxes(p, 0, 1).reshape(n_rows // t, 1, TOP_K * t)

    y_prompt = _combine(x, g_final, y_sorted, tile_positions(0, n_prompt, tc), 0, n_prompt, tc)
    y_sample = _combine(x, g_final, y_sorted, tile_positions(n_prompt, n_dec, n_dec), n_prompt, n_dec, n_dec)
    return y_prompt, y_sample


def kernel(x_prompt, x_sample, state_hgrn, state_conv, norm_mix, norm_ffn, norm_final, hgrn_w_in, hgrn_lb, hgrn_g_norm, hgrn_w_out, conv_w_in, conv_w, conv_w_out, ffn_w1, ffn_w3, ffn_w2, moe_router, moe_w1, moe_w3, moe_w2):
    batch, seq, d = x_prompt.shape
    n_dec = x_sample.shape[0]
    n_prompt = batch * seq
    n = n_prompt + n_dec
    n_experts = moe_router.shape[-1]
    assert x_sample.shape[1] == 1 and d % HEAD == 0 and seq % CHUNK == 0
    assert norm_mix.shape[0] == 2, "one HGRN2 layer followed by one short-conv layer"

    tm = _pick_tile(n, (688, 384, 128, 16))
    tm_proj = _pick_tile(n, (1376, 688, 384, 128, 16))
    tm_prompt = _pick_tile(n_prompt, (1024, 512, 256, 128, 16))
    tb = _pick_tile(seq, (256, 128, 64, 32))
    tb_conv = _pick_tile(seq, (512, 256, 128, 64, 32))
    tg = _pick_tile(n_dec, (16,))
    tf = _pick_tile(ffn_w1.shape[-1], (512, 256, 128))
    tm_moe = _pick_tile(n_prompt, (512, 128, 16))
    tm_moe = MOE_TM if n_prompt % 128 == 0 and ffn_w1.shape[-1] % (7 * 512) == 0 else tm_moe
    tc = _pick_tile(n_prompt, (512, 128, 16))
    tm_router = tm_prompt if tm_prompt % LANES == 0 else n_prompt + tm_prompt
    assert n_prompt % tg == 0 and n_prompt % n_dec == 0 and n_prompt % tc == 0

    xp = x_prompt.reshape(n_prompt, d)
    xs = x_sample.reshape(n_dec, d)

    p16, p32 = _proj(xp, norm_mix[0], hgrn_w_in[0], tm_prompt, f32_col=1)
    p16_s, p32_s = _proj(xs, norm_mix[0], hgrn_w_in[0], n_dec, f32_col=1)
    o_p, s_prompt = _hgrn_prompt(p16, p32, hgrn_lb, hgrn_g_norm[0], 0, batch, seq, tb)
    o_s, s_sample = _hgrn_sample(p16_s, p32_s, hgrn_lb, hgrn_g_norm[0], state_hgrn, 0, 0, tg)
    x = _outproj_prompt(xp, o_p, hgrn_w_out[0], n_prompt, tm_prompt)
    x = _outproj_sample(xs, 0, o_s, hgrn_w_out[0], x, n_prompt)
    x = _ffn(x, n, norm_ffn[0], ffn_w1[0], ffn_w3[0], ffn_w2[0], tm, tf)

    (pc,) = _proj(x, norm_mix[1], conv_w_in[0], tm_proj)
    z_p, c_prompt = _conv_prompt(pc, conv_w[0], batch, seq, tb_conv)
    z_s, c_sample = _conv_sample(pc, conv_w[0], state_conv.reshape(n_dec, (CONV_W - 1) * d), n_prompt)
    x1 = _outproj_prompt(x, z_p, conv_w_out[0], n_prompt, tm_prompt)
    x1 = _outproj_sample(x, n_prompt, z_s, conv_w_out[0], x1, n_prompt)

    y_prompt, y_sample = _expert_layer(x1, norm_ffn[1], norm_final, moe_router[0], moe_w1[0], moe_w3[0], moe_w2[0],
                                       n_prompt, n_dec, tm_router, tm_moe, tf, tc)

    return (y_prompt.reshape(batch, seq, d),
            y_sample.reshape(n_dec, 1, d),
            s_prompt,
            s_sample,
            c_prompt.reshape(batch, 1, CONV_W - 1, d),
            c_sample.reshape(n_dec, 1, CONV_W - 1, d))
```

```python
import functools

import jax
import jax.numpy as jnp
from jax import lax
from jax.experimental import pallas as pl
from jax.experimental.pallas import tpu as pltpu

F32 = jnp.float32
BF16 = jnp.bfloat16
I32 = jnp.int32

EPS = 1e-6
HEAD = 128
CHUNK = 32
CONV_W = 3
TOP_K = 2
LANES = 128
V7X_VMEM_LIMIT = 56 * 1024 * 1024

ARB = "arbitrary"


def _params(n_axes):
    return pltpu.CompilerParams(dimension_semantics=(ARB,) * n_axes, vmem_limit_bytes=V7X_VMEM_LIMIT)


def _pick_tile(n, candidates):
    for c in candidates:
        if n % c == 0:
            return c
    raise ValueError(f"no tile in {candidates} divides {n}")


def _rmsnorm_bf16(x, g):
    ms = jnp.mean(x * x, axis=-1, keepdims=True)
    return (x * lax.rsqrt(ms + EPS) * g).astype(BF16)


def _proj_body(n_col, f32_col, x_ref, g_ref, w_hbm, *refs):
    o16_ref = refs[0]
    w16_ref, stage_ref, wsem = refs[-3:]
    d = x_ref.shape[1]

    @pl.when(pl.program_id(0) == 0)
    def _():
        for j in range(n_col):
            cp = pltpu.make_async_copy(w_hbm.at[:, j * d:(j + 1) * d], stage_ref, wsem)
            cp.start()
            cp.wait()
            w16_ref[j] = stage_ref[...].astype(BF16)

    xn = _rmsnorm_bf16(x_ref[...], g_ref[...])
    j16 = 0
    for j in range(n_col):
        res = jnp.dot(xn, w16_ref[j], preferred_element_type=F32)
        if j == f32_col:
            refs[1][...] = res
        else:
            o16_ref[:, j16 * d:(j16 + 1) * d] = res.astype(BF16)
            j16 += 1


def _proj(x, g, w, tm, f32_col=None):
    n, d = x.shape
    n_col = w.shape[1] // d
    n16 = n_col - (f32_col is not None)
    out_specs = [pl.BlockSpec((tm, n16 * d), lambda i: (i, 0))]
    out_shape = [jax.ShapeDtypeStruct((n, n16 * d), BF16)]
    if f32_col is not None:
        out_specs.append(pl.BlockSpec((tm, d), lambda i: (i, 0)))
        out_shape.append(jax.ShapeDtypeStruct((n, d), F32))
    return pl.pallas_call(
        functools.partial(_proj_body, n_col, f32_col),
        grid=(n // tm,),
        in_specs=[pl.BlockSpec((tm, d), lambda i: (i, 0)),
                  pl.BlockSpec((1, d), lambda i: (0, 0)),
                  pl.BlockSpec(memory_space=pl.ANY)],
        out_specs=out_specs,
        out_shape=out_shape,
        scratch_shapes=[pltpu.VMEM((n_col, d, d), BF16), pltpu.VMEM((d, d), F32), pltpu.SemaphoreType.DMA(())],
        compiler_params=_params(1),
        name="proj",
    )(x, g.reshape(1, d), w)


def _forget_lower_bound(lb_ref, layer):
    lbw = lb_ref[...]
    e = jnp.exp(lbw - jnp.max(lbw, axis=0, keepdims=True))
    sm = e / jnp.sum(e, axis=0, keepdims=True)
    return jnp.sum(sm[:layer + 1], axis=0, keepdims=True)


def _head_rmsnorm(o):
    return o * lax.rsqrt(jnp.mean(o * o, axis=-1, keepdims=True) + EPS)


def _sigmoid(x):
    return 0.5 * (jnp.tanh(0.5 * x) + 1.0)


def _silu(x):
    h = 0.5 * x
    return h + h * jnp.tanh(h)


def _split3_bf16(x):
    hi = x.astype(BF16)
    r1 = x - hi.astype(F32)
    mid = r1.astype(BF16)
    lo = (r1 - mid.astype(F32)).astype(BF16)
    return hi, mid, lo


def _hgrn_prompt_body(layer, n_heads, tb, q_ref, f_ref, v_ref, gt_ref, lb_ref, gn_ref,
                      o_ref, s_ref, st_scr, qin_scr, kin_scr, kdec_scr, g_scr, sprev_scr):
    t = pl.program_id(1)
    nc = tb // CHUNK
    d = n_heads * HEAD

    @pl.when(t == 0)
    def _():
        st_scr[...] = jnp.zeros_like(st_scr)

    lb = _forget_lower_bound(lb_ref, layer)
    gn = gn_ref[...]
    row = lax.broadcasted_iota(I32, (tb, tb), 0)
    col = lax.broadcasted_iota(I32, (tb, tb), 1)
    same_chunk = (row // CHUNK) == (col // CHUNK)
    causal = jnp.logical_and(same_chunk, row >= col)

    f = lb + (1.0 - lb) * _sigmoid(f_ref[...])
    logf = jnp.log2(f)
    k = 1.0 - f
    tri = causal.astype(BF16)
    G = None
    for part in _split3_bf16(logf):
        term = jnp.dot(tri, part, preferred_element_type=F32)
        G = term if G is None else G + term
    g_scr[...] = G
    g_last = [g_scr[(c + 1) * CHUNK - 1:(c + 1) * CHUNK, :] for c in range(nc)]
    decay = [jnp.exp2(g) for g in g_last]
    decay_rows = jnp.concatenate([jnp.broadcast_to(dc, (CHUNK, d)) for dc in decay], axis=0)
    k_in = k * jnp.exp2(-G)
    qin_scr[...] = (_silu(q_ref[...].astype(F32)) * jnp.exp2(G)).astype(BF16)
    kin_scr[...] = k_in.astype(BF16)
    kdec_scr[...] = (k_in * decay_rows).astype(BF16)

    row_chunk = lax.broadcasted_iota(I32, (tb, HEAD), 0) // CHUNK
    zero16 = jnp.zeros((), BF16)
    grp = next(g for g in (4, 2, 1) if nc % g == 0)

    for h in range(n_heads):
        hs = slice(h * HEAD, (h + 1) * HEAD)
        qh = qin_scr[:, hs]
        kd = kdec_scr[:, hs]
        vh = v_ref[:, hs]
        a = lax.dot_general(qh, kin_scr[:, hs], (((1,), (1,)), ((), ())), preferred_element_type=F32)
        a = jnp.where(causal, a, 0.0).astype(BF16)
        o = jnp.dot(a, vh, preferred_element_type=F32)
        ut = jnp.concatenate(
            [lax.dot_general(vh[c * CHUNK:(c + 1) * CHUNK], kd[c * CHUNK:(c + 1) * CHUNK],
                             (((0,), (0,)), ((), ())), preferred_element_type=F32) for c in range(nc)], axis=0)
        st = st_scr[h]
        for c in range(nc):
            sprev_scr[h, :, c * HEAD:(c + 1) * HEAD] = st.astype(BF16)
            st = st * decay[c][:, hs] + ut[c * HEAD:(c + 1) * HEAD, :]
        st_scr[h] = st
        inter = []
        for g0 in range(0, nc, grp):
            rows = slice(g0 * CHUNK, (g0 + grp) * CHUNK)
            q_blocks = jnp.concatenate(
                [jnp.where(row_chunk[rows] == g0 + j, qh[rows], zero16) for j in range(grp)], axis=1)
            inter.append(lax.dot_general(q_blocks, sprev_scr[h, :, g0 * HEAD:(g0 + grp) * HEAD],
                                         (((1,), (1,)), ((), ())), preferred_element_type=F32))
        o = o + jnp.concatenate(inter, axis=0)
        o = _head_rmsnorm(o) * gn[:, hs] * _silu(gt_ref[:, hs].astype(F32))
        o_ref[:, hs] = o.astype(BF16)

    @pl.when(t == pl.num_programs(1) - 1)
    def _():
        for h in range(n_heads):
            s_ref[0, 0, h] = st_scr[h].T


def _hgrn_prompt(p16, p32, hgrn_lb, g_norm, layer, batch, seq, tb):
    d = g_norm.shape[0]
    n_heads = d // HEAD
    n_t = seq // tb
    blk = lambda kk: pl.BlockSpec((tb, d), lambda b, t, kk=kk: (b * n_t + t, kk))
    return pl.pallas_call(
        functools.partial(_hgrn_prompt_body, layer, n_heads, tb),
        grid=(batch, n_t),
        in_specs=[blk(0), blk(0), blk(1), blk(2),
                  pl.BlockSpec(hgrn_lb.shape, lambda b, t: (0, 0)),
                  pl.BlockSpec((1, d), lambda b, t: (0, 0))],
        out_specs=[pl.BlockSpec((tb, d), lambda b, t: (b * n_t + t, 0)),
                   pl.BlockSpec((1, 1, n_heads, HEAD, HEAD), lambda b, t: (b, 0, 0, 0, 0))],
        out_shape=[jax.ShapeDtypeStruct((batch * seq, d), BF16),
                   jax.ShapeDtypeStruct((batch, 1, n_heads, HEAD, HEAD), F32)],
        scratch_shapes=[pltpu.VMEM((n_heads, HEAD, HEAD), F32),
                        pltpu.VMEM((tb, d), BF16), pltpu.VMEM((tb, d), BF16), pltpu.VMEM((tb, d), BF16),
                        pltpu.VMEM((tb, d), F32),
                        pltpu.VMEM((n_heads, HEAD, (tb // CHUNK) * HEAD), BF16)],
        compiler_params=_params(2),
        name="hgrn_prompt",
    )(p16, p32, p16, p16, hgrn_lb, g_norm.reshape(1, d))


def _hgrn_sample_body(layer, tg, q_ref, f_ref, v_ref, gt_ref, lb_ref, gn_ref, s0_ref,
                      o_ref, s_ref, o_scr):
    lb_h = _forget_lower_bound(lb_ref, layer)
    gn_h = gn_ref[...]

    q = jax.nn.silu(q_ref[...].astype(F32))
    f = lb_h + (1.0 - lb_h) * jax.nn.sigmoid(f_ref[...])
    decay = f
    k = 1.0 - f
    v = v_ref[...].astype(F32)
    q_t, d_t, k_t = q.T, decay.T, k.T
    for j in range(tg):
        s = d_t[:, j:j + 1] * s0_ref[j, 0, 0] + k_t[:, j:j + 1] * v[j:j + 1, :]
        s_ref[j, 0, 0] = s
        o_scr[j:j + 1, :] = jnp.sum(q_t[:, j:j + 1] * s, axis=0, keepdims=True)
    o = _head_rmsnorm(o_scr[...]) * gn_h * jax.nn.silu(gt_ref[...].astype(F32))
    o_ref[...] = o.astype(BF16)


def _hgrn_sample(p16, p32, hgrn_lb, g_norm, state, layer, row0, tg):
    nb = state.shape[0]
    d = g_norm.shape[0]
    n_heads = d // HEAD
    rb0 = row0 // tg
    blk = lambda kk: pl.BlockSpec((tg, HEAD), lambda g, h, kk=kk: (rb0 + g, kk * n_heads + h))
    st_spec = pl.BlockSpec((tg, 1, 1, HEAD, HEAD), lambda g, h: (g, 0, h, 0, 0))
    return pl.pallas_call(
        functools.partial(_hgrn_sample_body, layer, tg),
        grid=(nb // tg, n_heads),
        in_specs=[blk(0), blk(0), blk(1), blk(2),
                  pl.BlockSpec((hgrn_lb.shape[0], HEAD), lambda g, h: (0, h)),
                  pl.BlockSpec((1, HEAD), lambda g, h: (0, h)),
                  st_spec],
        out_specs=[pl.BlockSpec((tg, HEAD), lambda g, h: (g, h)), st_spec],
        out_shape=[jax.ShapeDtypeStruct((nb, d), BF16),
                   jax.ShapeDtypeStruct(state.shape, F32)],
        scratch_shapes=[pltpu.VMEM((tg, HEAD), F32)],
        compiler_params=_params(2),
        name="hgrn_sample",
    )(p16, p32, p16, p16, hgrn_lb, g_norm.reshape(1, d), state)


_CARRY = 8


def _conv_prompt_body(tb, b_ref, c_ref, h_ref, w_ref, z_ref, buf_ref, u_scr):
    t = pl.program_id(1)

    @pl.when(t == 0)
    def _():
        u_scr[0:_CARRY, :] = jnp.zeros((_CARRY, u_scr.shape[1]), F32)

    u = c_ref[...].astype(F32) * h_ref[...].astype(F32)
    u_scr[_CARRY:_CARRY + tb, :] = u
    w = w_ref[...]
    y = w[0:1, :] * u_scr[_CARRY - 2:_CARRY - 2 + tb, :]
    y = y + w[1:2, :] * u_scr[_CARRY - 1:_CARRY - 1 + tb, :]
    y = y + w[2:3, :] * u
    z_ref[...] = (b_ref[...].astype(F32) * y).astype(BF16)
    u_scr[0:_CARRY, :] = u[tb - _CARRY:tb, :]

    @pl.when(t == pl.num_programs(1) - 1)
    def _():
        buf_ref[0] = u[tb - (CONV_W - 1):tb, :]


def _conv_prompt(pc, conv_w, batch, seq, tb):
    d = conv_w.shape[1]
    n_t = seq // tb
    blk = lambda kk: pl.BlockSpec((tb, d), lambda b, t, kk=kk: (b * n_t + t, kk))
    return pl.pallas_call(
        functools.partial(_conv_prompt_body, tb),
        grid=(batch, n_t),
        in_specs=[blk(0), blk(1), blk(2), pl.BlockSpec(conv_w.shape, lambda b, t: (0, 0))],
        out_specs=[pl.BlockSpec((tb, d), lambda b, t: (b * n_t + t, 0)),
                   pl.BlockSpec((1, CONV_W - 1, d), lambda b, t: (b, 0, 0))],
        out_shape=[jax.ShapeDtypeStruct((batch * seq, d), BF16),
                   jax.ShapeDtypeStruct((batch, CONV_W - 1, d), F32)],
        scratch_shapes=[pltpu.VMEM((_CARRY + tb, d), F32)],
        compiler_params=_params(2),
        name="conv_prompt",
    )(pc, pc, pc, conv_w)


def _conv_sample_body(d, b_ref, c_ref, h_ref, w_ref, st_ref, z_ref, buf_ref):
    u = c_ref[...].astype(F32) * h_ref[...].astype(F32)
    w = w_ref[...]
    buf0 = st_ref[:, 0:d]
    buf1 = st_ref[:, d:2 * d]
    y = w[0:1, :] * buf0
    y = y + w[1:2, :] * buf1
    y = y + w[2:3, :] * u
    z_ref[...] = (b_ref[...].astype(F32) * y).astype(BF16)
    buf_ref[:, 0:d] = buf1
    buf_ref[:, d:2 * d] = u


def _conv_sample(pc, conv_w, state2d, row0):
    nb = state2d.shape[0]
    d = conv_w.shape[1]
    rb0 = row0 // nb
    blk = lambda kk: pl.BlockSpec((nb, d), lambda i, kk=kk: (rb0, kk))
    return pl.pallas_call(
        functools.partial(_conv_sample_body, d),
        grid=(1,),
        in_specs=[blk(0), blk(1), blk(2), pl.BlockSpec(conv_w.shape, lambda i: (0, 0)),
                  pl.BlockSpec(state2d.shape, lambda i: (0, 0))],
        out_specs=[pl.BlockSpec((nb, d), lambda i: (0, 0)),
                   pl.BlockSpec(state2d.shape, lambda i: (0, 0))],
        out_shape=[jax.ShapeDtypeStruct((nb, d), BF16),
                   jax.ShapeDtypeStruct(state2d.shape, F32)],
        compiler_params=_params(1),
        name="conv_sample",
    )(pc, pc, pc, conv_w, state2d)


def _outproj_body(n_blocks, x_ref, z_ref, w_ref, o_ref, w16_ref):
    i = pl.program_id(0)

    @pl.when(i == 0)
    def _():
        w16_ref[...] = w_ref[...].astype(BF16)

    @pl.when(i < n_blocks)
    def _():
        o_ref[...] = x_ref[...] + jnp.dot(z_ref[...], w16_ref[...], preferred_element_type=F32)

    @pl.when(i >= n_blocks)
    def _():
        o_ref[...] = jnp.zeros_like(o_ref)


def _outproj_prompt(x, z, w, n_rows, tm):
    d = x.shape[1]
    n_blocks = n_rows // tm
    row_blk = lambda i: (jnp.minimum(i, n_blocks - 1), 0)
    return pl.pallas_call(
        functools.partial(_outproj_body, n_blocks),
        grid=(n_blocks + 1,),
        in_specs=[pl.BlockSpec((tm, d), row_blk),
                  pl.BlockSpec((tm, d), row_blk),
                  pl.BlockSpec((d, d), lambda i: (0, 0))],
        out_specs=pl.BlockSpec((tm, d), lambda i: (i, 0)),
        out_shape=jax.ShapeDtypeStruct(((n_blocks + 1) * tm, d), F32),
        scratch_shapes=[pltpu.VMEM((d, d), BF16)],
        compiler_params=_params(1),
        name="outproj",
    )(x, z, w)


def _outproj_sample_body(x_ref, z_ref, w_ref, buf_ref, o_ref):
    del buf_ref
    o_ref[...] = x_ref[...] + jnp.dot(z_ref[...], w_ref[...].astype(BF16), preferred_element_type=F32)


def _outproj_sample(x, x_row0, z, w, buf, row0):
    nb, d = z.shape
    return pl.pallas_call(
        _outproj_sample_body,
        grid=(1,),
        in_specs=[pl.BlockSpec((nb, d), lambda i: (x_row0 // nb, 0)),
                  pl.BlockSpec((nb, d), lambda i: (0, 0)),
                  pl.BlockSpec((d, d), lambda i: (0, 0)),
                  pl.BlockSpec(memory_space=pl.ANY)],
        out_specs=pl.BlockSpec((nb, d), lambda i: (row0 // nb, 0)),
        out_shape=jax.ShapeDtypeStruct(buf.shape, buf.dtype),
        input_output_aliases={3: 0},
        compiler_params=_params(1),
        name="outproj_sample",
    )(x, z, w, buf)


def _swiglu_partial(xn, w1, w3, w2):
    h1 = jnp.dot(xn, w1.astype(BF16), preferred_element_type=F32)
    h3 = jnp.dot(xn, w3.astype(BF16), preferred_element_type=F32)
    h = (jax.nn.silu(h1) * h3).astype(BF16)
    return jnp.dot(h, w2.astype(BF16), preferred_element_type=F32)


def _cache_weight_chunk(w1_hbm, w3_hbm, w2_hbm, f, n_f, tf, stages, caches, wsem):
    n_slots = stages[0].shape[0]

    def copies(ff, ws):
        cols = pl.ds(pl.multiple_of(ff * tf, tf), tf)
        srcs = (w1_hbm.at[:, cols], w3_hbm.at[:, cols], w2_hbm.at[cols, :])
        return [pltpu.make_async_copy(src, st.at[ws], wsem.at[ws, m]) for m, (src, st) in enumerate(zip(srcs, stages))]

    if n_slots == 1:
        ws = 0
        for c in copies(f, 0):
            c.start()
    else:
        ws = lax.rem(f, 2)

        @pl.when(f == 0)
        def _():
            for c in copies(0, 0):
                c.start()

        @pl.when(f + 1 < n_f)
        def _():
            for c in copies(f + 1, 1 - ws):
                c.start()

    for c in copies(f, ws):
        c.wait()
    for st, ca in zip(stages, caches):
        ca[f] = st[ws].astype(BF16)


def _ffn_body(n_f, tf, x_ref, g_ref, w1_hbm, w3_hbm, w2_hbm, o_ref, xn_ref, acc_ref, c1, c3, c2, s1, s3, s2, wsem):
    i = pl.program_id(0)
    xn_ref[...] = _rmsnorm_bf16(x_ref[...], g_ref[...])
    acc_ref[...] = jnp.zeros_like(acc_ref)

    def chunk(f, carry):
        @pl.when(i == 0)
        def _():
            _cache_weight_chunk(w1_hbm, w3_hbm, w2_hbm, f, n_f, tf, (s1, s3, s2), (c1, c3, c2), wsem)

        acc_ref[...] += _swiglu_partial(xn_ref[...], c1[f], c3[f], c2[f])
        return carry

    lax.fori_loop(0, n_f, chunk, 0)
    o_ref[...] = x_ref[...] + acc_ref[...]


def _ffn(x, n, g, w1, w3, w2, tm, tf):
    d = x.shape[1]
    dff = w1.shape[1]
    n_f = dff // tf
    return pl.pallas_call(
        functools.partial(_ffn_body, n_f, tf),
        grid=(n // tm,),
        in_specs=[pl.BlockSpec((tm, d), lambda i: (i, 0)),
                  pl.BlockSpec((1, d), lambda i: (0, 0)),
                  pl.BlockSpec(memory_space=pl.ANY),
                  pl.BlockSpec(memory_space=pl.ANY),
                  pl.BlockSpec(memory_space=pl.ANY)],
        out_specs=pl.BlockSpec((tm, d), lambda i: (i, 0)),
        out_shape=jax.ShapeDtypeStruct((n, d), F32),
        scratch_shapes=[pltpu.VMEM((tm, d), BF16), pltpu.VMEM((tm, d), F32),
                        pltpu.VMEM((n_f, d, tf), BF16), pltpu.VMEM((n_f, d, tf), BF16),
                        pltpu.VMEM((n_f, tf, d), BF16),
                        pltpu.VMEM((1, d, tf), F32), pltpu.VMEM((1, d, tf), F32), pltpu.VMEM((1, tf, d), F32),
                        pltpu.SemaphoreType.DMA((1, 3))],
        compiler_params=_params(1),
        name="ffn",
    )(x, g.reshape(1, d), w1, w3, w2)


ROUTE_ROWS = 8


def _router_body(n_experts, n_tok, x_ref, g_ref, rw_ref, route_ref, cnt_ref, cnt_scr):
    i = pl.program_id(0)
    tm = x_ref.shape[0]

    @pl.when(i == 0)
    def _():
        cnt_scr[...] = jnp.zeros_like(cnt_scr)

    xn = _rmsnorm_bf16(x_ref[...], g_ref[...])
    logits = jnp.dot(xn, rw_ref[...].astype(BF16), preferred_element_type=F32)
    lane = lax.broadcasted_iota(I32, (tm, LANES), 1).astype(F32)
    neg = jnp.float32(-jnp.inf)
    logits = jnp.where(lane < n_experts, logits, neg)
    m1 = jnp.max(logits, axis=-1, keepdims=True)
    i1 = jnp.min(jnp.where(logits == m1, lane, float(LANES)), axis=-1, keepdims=True)
    i1 = jnp.minimum(i1, float(n_experts - 1))
    rest = jnp.where(lane == i1, neg, logits)
    m2 = jnp.max(rest, axis=-1, keepdims=True)
    i2 = jnp.min(jnp.where(rest == m2, lane, float(LANES)), axis=-1, keepdims=True)
    i2 = jnp.minimum(i2, float(n_experts - 1))
    e2 = jnp.exp(m2 - m1)
    den = 1.0 + e2
    g1 = 1.0 / den
    g2 = e2 / den

    sel1 = lane == i1
    sel2 = lane == i2
    is_token = i * tm + lax.broadcasted_iota(I32, (tm, LANES), 0) < n_tok
    onehot = jnp.logical_and(jnp.logical_or(sel1, sel2), is_token)
    row = lax.broadcasted_iota(I32, (tm, tm), 0)
    col = lax.broadcasted_iota(I32, (tm, tm), 1)
    before = (row > col).astype(BF16)
    rank = jnp.dot(before, onehot.astype(BF16), preferred_element_type=F32) + cnt_scr[...]
    r1 = jnp.sum(jnp.where(sel1, rank, 0.0), axis=-1, keepdims=True)
    r2 = jnp.sum(jnp.where(sel2, rank, 0.0), axis=-1, keepdims=True)
    cnt_scr[...] += jnp.sum(onehot.astype(F32), axis=0, keepdims=True)

    out = jnp.zeros((tm, LANES), F32)
    for slot, val in enumerate((i1, i2, g1, g2, r1, r2)):
        out = jnp.where(lane == slot, val, out)
    route_ref[...] = out.T[0:ROUTE_ROWS, :]
    cnt_ref[...] = cnt_scr[...]


def _router(x, n_tok, g, rw_pad, n_experts, tm):
    n, d = x.shape
    return pl.pallas_call(
        functools.partial(_router_body, n_experts, n_tok),
        grid=(n // tm,),
        in_specs=[pl.BlockSpec((tm, d), lambda i: (i, 0)),
                  pl.BlockSpec((1, d), lambda i: (0, 0)),
                  pl.BlockSpec((d, LANES), lambda i: (0, 0))],
        out_specs=[pl.BlockSpec((ROUTE_ROWS, tm), lambda i: (0, i)),
                   pl.BlockSpec((1, LANES), lambda i: (0, 0))],
        out_shape=[jax.ShapeDtypeStruct((ROUTE_ROWS, n), F32),
                   jax.ShapeDtypeStruct((1, LANES), F32)],
        scratch_shapes=[pltpu.VMEM((1, LANES), F32)],
        compiler_params=_params(1),
        name="router",
    )(x, g.reshape(1, d), rw_pad)


SUBLANES = 8
MOE_TM = 672


def _start_row_group(src_hbm, idx_ref, g, dst, sem):
    for k in range(SUBLANES):
        pltpu.make_async_copy(src_hbm.at[pl.ds(idx_ref[0, 0, g * SUBLANES + k], 1)],
                              dst.at[g, pl.ds(k, 1)], sem).start(priority=k % 2)


def _start_row_gather(src_hbm, idx_ref, n_rows, dst, sem, unrolled=False):
    if unrolled:
        for g in range(n_rows // SUBLANES):
            _start_row_group(src_hbm, idx_ref, g, dst, sem)
        return

    def body(g, carry):
        _start_row_group(src_hbm, idx_ref, g, dst, sem)
        return carry

    lax.fori_loop(0, n_rows // SUBLANES, body, 0)


def _wait_row_gather(src_hbm, n_rows, dst, sem):
    pltpu.make_async_copy(src_hbm.at[pl.ds(0, n_rows)], dst.reshape(n_rows, dst.shape[-1]), sem).wait()


def _moe_body(n_f, tf, te_ref, nu_ref, idx_cur, idx_nxt, x_hbm, gate_ref, g_ref, w1_hbm, w3_hbm, w2_hbm,
              y_ref, xbuf, sem, xn_ref, acc_ref, c1, c3, c2, s1, s3, s2, wsem):
    i = pl.program_id(0)
    n_tiles = pl.num_programs(0)
    tm = xbuf.shape[1] * SUBLANES
    groups_per_chunk = tm // n_f // SUBLANES
    slot = lax.rem(i, 2)
    n_used = nu_ref[0]
    used = i < n_used
    e = te_ref[i]
    first_of_expert = jnp.logical_or(i == 0, e != te_ref[jnp.maximum(i - 1, 0)])

    @pl.when(i == 0)
    def _():
        _start_row_gather(x_hbm, idx_cur, tm, xbuf.at[0], sem.at[0])

    @pl.when(i <= n_used)
    def _():
        _wait_row_gather(x_hbm, tm, xbuf.at[slot], sem.at[slot])

    @pl.when(used)
    def _():
        xn_ref[...] = _rmsnorm_bf16(xbuf[slot].reshape(tm, xbuf.shape[-1]), g_ref[...])
        acc_ref[...] = jnp.zeros_like(acc_ref)

        def chunk(f, carry):
            @pl.when(first_of_expert)
            def _():
                _cache_weight_chunk(w1_hbm.at[e], w3_hbm.at[e], w2_hbm.at[e], f, n_f, tf,
                                    (s1, s3, s2), (c1, c3, c2), wsem)

            for g in range(groups_per_chunk):
                _start_row_group(x_hbm, idx_nxt, f * groups_per_chunk + g, xbuf.at[1 - slot], sem.at[1 - slot])
            acc_ref[...] += _swiglu_partial(xn_ref[...], c1[f], c3[f], c2[f])
            return carry

        lax.fori_loop(0, n_f, chunk, 0)
        r_id = lax.broadcasted_iota(I32, (tm, tm), 0)
        c_id = lax.broadcasted_iota(I32, (tm, tm), 1)
        gate_col = jnp.sum(jnp.where(r_id == c_id, gate_ref[0], 0.0), axis=1, keepdims=True)
        y_ref[...] = acc_ref[...] * gate_col

    @pl.when(jnp.logical_not(used))
    def _():
        y_ref[...] = jnp.zeros_like(y_ref)

    @pl.when(jnp.logical_and(used, i == n_tiles - 1))
    def _():
        _wait_row_gather(x_hbm, tm, xbuf.at[1 - slot], sem.at[1 - slot])


def _moe(x, g, w1, w3, w2, tile_expert, n_used, inv3, gate_sorted, tm, tf):
    n_tiles = inv3.shape[0]
    d = x.shape[1]
    dff = w1.shape[2]
    n_f = dff // tf
    assert tm % (n_f * SUBLANES) == 0, "each hidden chunk fetches whole 8-row tiles of the next row tile"
    grid_spec = pltpu.PrefetchScalarGridSpec(
        num_scalar_prefetch=2,
        grid=(n_tiles,),
        in_specs=[pl.BlockSpec((1, 1, tm), lambda i, te, nu: (i, 0, 0), memory_space=pltpu.SMEM),
                  pl.BlockSpec((1, 1, tm), lambda i, te, nu: (jnp.minimum(i + 1, n_tiles - 1), 0, 0),
                               memory_space=pltpu.SMEM),
                  pl.BlockSpec(memory_space=pl.ANY),
                  pl.BlockSpec((1, 1, tm), lambda i, te, nu: (i, 0, 0)),
                  pl.BlockSpec((1, d), lambda i, te, nu: (0, 0)),
                  pl.BlockSpec(memory_space=pl.ANY),
                  pl.BlockSpec(memory_space=pl.ANY),
                  pl.BlockSpec(memory_space=pl.ANY)],
        out_specs=pl.BlockSpec((tm, d), lambda i, te, nu: (i, 0)),
        scratch_shapes=[pltpu.VMEM((2, tm // SUBLANES, SUBLANES, d), F32),
                        pltpu.SemaphoreType.DMA((2,)),
                        pltpu.VMEM((tm, d), BF16),
                        pltpu.VMEM((tm, d), F32),
                        pltpu.VMEM((n_f, d, tf), BF16),
                        pltpu.VMEM((n_f, d, tf), BF16),
                        pltpu.VMEM((n_f, tf, d), BF16),
                        pltpu.VMEM((2, d, tf), F32),
                        pltpu.VMEM((2, d, tf), F32),
                        pltpu.VMEM((2, tf, d), F32),
                        pltpu.SemaphoreType.DMA((2, 3))],
    )
    return pl.pallas_call(
        functools.partial(_moe_body, n_f, tf),
        grid_spec=grid_spec,
        out_shape=jax.ShapeDtypeStruct((n_tiles * tm, d), F32),
        compiler_params=_params(1),
        name="moe",
    )(tile_expert, n_used, inv3, inv3, x, gate_sorted, g.reshape(1, d), w1, w3, w2)


def _combine_body(pos_cur, pos_nxt, x_ref, g_ref, y_hbm, o_ref, ybuf, sem):
    i = pl.program_id(0)
    n_tiles = pl.num_programs(0)
    rows = ybuf.shape[1] * SUBLANES
    slot = lax.rem(i, 2)

    @pl.when(i == 0)
    def _():
        _start_row_gather(y_hbm, pos_cur, rows, ybuf.at[0], sem.at[0])

    @pl.when(i + 1 < n_tiles)
    def _():
        _start_row_gather(y_hbm, pos_nxt, rows, ybuf.at[1 - slot], sem.at[1 - slot], unrolled=True)

    _wait_row_gather(y_hbm, rows, ybuf.at[slot], sem.at[slot])
    tc = rows // TOP_K
    y = ybuf[slot].reshape(rows, ybuf.shape[-1])
    moe = y[0:tc, :] + y[tc:rows, :]
    x = x_ref[...] + moe
    ms = jnp.mean(x * x, axis=-1, keepdims=True)
    o_ref[...] = x * lax.rsqrt(ms + EPS) * g_ref[...]


def _combine(x, g, y_sorted, pos3, row0, n_rows, tc):
    d = x.shape[1]
    n_tiles = n_rows // tc
    rb0 = row0 // tc
    grid_spec = pltpu.PrefetchScalarGridSpec(
        num_scalar_prefetch=0,
        grid=(n_tiles,),
        in_specs=[pl.BlockSpec((1, 1, TOP_K * tc), lambda i: (i, 0, 0), memory_space=pltpu.SMEM),
                  pl.BlockSpec((1, 1, TOP_K * tc), lambda i: (jnp.minimum(i + 1, n_tiles - 1), 0, 0),
                               memory_space=pltpu.SMEM),
                  pl.BlockSpec((tc, d), lambda i: (rb0 + i, 0)),
                  pl.BlockSpec((1, d), lambda i: (0, 0)),
                  pl.BlockSpec(memory_space=pl.ANY)],
        out_specs=pl.BlockSpec((tc, d), lambda i: (i, 0)),
        scratch_shapes=[pltpu.VMEM((2, TOP_K * tc // SUBLANES, SUBLANES, d), F32), pltpu.SemaphoreType.DMA((2,))],
    )
    return pl.pallas_call(
        _combine_body,
        grid_spec=grid_spec,
        out_shape=jax.ShapeDtypeStruct((n_rows, d), F32),
        compiler_params=_params(1),
        name="combine",
    )(pos3, pos3, x, g.reshape(1, d), y_sorted)


def _lookup(table, idx):
    out = jnp.zeros(idx.shape, table.dtype)
    for e in range(table.shape[0]):
        out = jnp.where(idx == e, table[e], out)
    return out


def _expert_layer(x, g_ffn, g_final, router_w, w1, w3, w2, n_prompt, n_dec, tm_router, tm_moe, tf, tc):
    n = n_prompt + n_dec
    n_experts = router_w.shape[-1]
    rw_pad = jnp.pad(router_w, ((0, 0), (0, LANES - n_experts)))
    route, cnt = _router(x, n, g_ffn, rw_pad, n_experts, tm_router)
    ids = route[0:TOP_K, :n].astype(I32)
    gates = route[TOP_K:2 * TOP_K, :n]
    ranks = route[2 * TOP_K:3 * TOP_K, :n].astype(I32)
    counts = cnt[0, :n_experts].astype(I32)

    n_tiles = (TOP_K * n + n_experts * (tm_moe - 1)) // tm_moe
    padded = ((counts + tm_moe - 1) // tm_moe) * tm_moe
    ends = jnp.cumsum(padded)
    starts = ends - padded
    cstart = jnp.cumsum(counts) - counts
    pos = _lookup(starts, ids) + ranks
    keys = ids * n + jnp.arange(n, dtype=I32)[None, :]
    order = jnp.argsort(keys.reshape(-1)).astype(I32)
    slot_pos = jnp.arange(n_tiles * tm_moe, dtype=I32)
    slot_e = jnp.minimum(jnp.sum(slot_pos[None, :] >= ends[:, None], axis=0), n_experts - 1).astype(I32)
    slot_rank = slot_pos - _lookup(starts, slot_e)
    slot_valid = slot_rank < _lookup(counts, slot_e)
    slot_asg = order[jnp.clip(_lookup(cstart, slot_e) + slot_rank, 0, TOP_K * n - 1)]
    inv = jnp.where(slot_valid, slot_asg % n, 0).astype(I32)
    gate_sorted = jnp.where(slot_valid, gates.reshape(-1)[slot_asg], 0.0)
    tile_expert = slot_e[::tm_moe]
    n_used = (ends[-1] // tm_moe).astype(I32).reshape(1)

    y_sorted = _moe(x, g_ffn, w1, w3, w2, tile_expert, n_used, inv.reshape(n_tiles, 1, tm_moe),
                    gate_sorted.reshape(n_tiles, 1, tm_moe), tm_moe, tf)

    def tile_positions(row0, n_rows, t):
        p = pos[:, row0:row0 + n_rows].reshape(TOP_K, n_rows // t, t)
        return jnp.swapaxes(p, 0, 1).reshape(n_rows // t, 1, TOP_K * t)

    y_prompt = _combine(x, g_final, y_sorted, tile_positions(0, n_prompt, tc), 0, n_prompt, tc)
    y_sample = _combine(x, g_final, y_sorted, tile_positions(n_prompt, n_dec, n_dec), n_prompt, n_dec, n_dec)
    return y_prompt, y_sample


def kernel(x_prompt, x_sample, state_hgrn, state_conv, norm_mix, norm_ffn, norm_final, hgrn_w_in, hgrn_lb, hgrn_g_norm, hgrn_w_out, conv_w_in, conv_w, conv_w_out, ffn_w1, ffn_w3, ffn_w2, moe_router, moe_w1, moe_w3, moe_w2):
    batch, seq, d = x_prompt.shape
    n_dec = x_sample.shape[0]
    n_prompt = batch * seq
    n = n_prompt + n_dec
    n_experts = moe_router.shape[-1]
    assert x_sample.shape[1] == 1 and d % HEAD == 0 and seq % CHUNK == 0
    assert norm_mix.shape[0] == 2, "one HGRN2 layer followed by one short-conv layer"

    tm = _pick_tile(n, (688, 384, 128, 16))
    tm_proj = _pick_tile(n, (1376, 688, 384, 128, 16))
    tm_prompt = _pick_tile(n_prompt, (1024, 512, 256, 128, 16))
    tb = _pick_tile(seq, (256, 128, 64, 32))
    tb_conv = _pick_tile(seq, (512, 256, 128, 64, 32))
    tg = _pick_tile(n_dec, (16,))
    tf = _pick_tile(ffn_w1.shape[-1], (512, 256, 128))
    tm_moe = _pick_tile(n_prompt, (512, 128, 16))
    tm_moe = MOE_TM if n_prompt % 128 == 0 and ffn_w1.shape[-1] % (7 * 512) == 0 else tm_moe
    tc = _pick_tile(n_prompt, (512, 128, 16))
    tm_router = tm_prompt if tm_prompt % LANES == 0 else n_prompt + tm_prompt
    assert n_prompt % tg == 0 and n_prompt % n_dec == 0 and n_prompt % tc == 0

    xp = x_prompt.reshape(n_prompt, d)
    xs = x_sample.reshape(n_dec, d)

    p16, p32 = _proj(xp, norm_mix[0], hgrn_w_in[0], tm_prompt, f32_col=1)
    p16_s, p32_s = _proj(xs, norm_mix[0], hgrn_w_in[0], n_dec, f32_col=1)
    o_p, s_prompt = _hgrn_prompt(p16, p32, hgrn_lb, hgrn_g_norm[0], 0, batch, seq, tb)
    o_s, s_sample = _hgrn_sample(p16_s, p32_s, hgrn_lb, hgrn_g_norm[0], state_hgrn, 0, 0, tg)
    x = _outproj_prompt(xp, o_p, hgrn_w_out[0], n_prompt, tm_prompt)
    x = _outproj_sample(xs, 0, o_s, hgrn_w_out[0], x, n_prompt)
    x = _ffn(x, n, norm_ffn[0], ffn_w1[0], ffn_w3[0], ffn_w2[0], tm, tf)

    (pc,) = _proj(x, norm_mix[1], conv_w_in[0], tm_proj)
    z_p, c_prompt = _conv_prompt(pc, conv_w[0], batch, seq, tb_conv)
    z_s, c_sample = _conv_sample(pc, conv_w[0], state_conv.reshape(n_dec, (CONV_W - 1) * d), n_prompt)
    x1 = _outproj_prompt(x, z_p, conv_w_out[0], n_prompt, tm_prompt)
    x1 = _outproj_sample(x, n_prompt, z_s, conv_w_out[0], x1, n_prompt)

    y_prompt, y_sample = _expert_layer(x1, norm_ffn[1], norm_final, moe_router[0], moe_w1[0], moe_w3[0], moe_w2[0],
                                       n_prompt, n_dec, tm_router, tm_moe, tf, tc)

    return (y_prompt.reshape(batch, seq, d),
            y_sample.reshape(n_dec, 1, d),
            s_prompt,
            s_sample,
            c_prompt.reshape(batch, 1, CONV_W - 1, d),
            c_sample.reshape(n_dec, 1, CONV_W - 1, d))
```

```python
import functools

import jax
import jax.numpy as jnp
from jax import lax
from jax.experimental import pallas as pl
from jax.experimental.pallas import tpu as pltpu

F32 = jnp.float32
BF16 = jnp.bfloat16
I32 = jnp.int32

EPS = 1e-6
HEAD = 128
CHUNK = 32
CONV_W = 3
TOP_K = 2
LANES = 128
V7X_VMEM_LIMIT = 56 * 1024 * 1024

ARB = "arbitrary"


def _params(n_axes):
    return pltpu.CompilerParams(dimension_semantics=(ARB,) * n_axes, vmem_limit_bytes=V7X_VMEM_LIMIT)


def _pick_tile(n, candidates):
    for c in candidates:
        if n % c == 0:
            return c
    raise ValueError(f"no tile in {candidates} divides {n}")


def _rmsnorm_bf16(x, g):
    ms = jnp.mean(x * x, axis=-1, keepdims=True)
    return (x * lax.rsqrt(ms + EPS) * g).astype(BF16)


def _proj_body(n_col, f32_col, x_ref, g_ref, w_hbm, *refs):
    o16_ref = refs[0]
    w16_ref, stage_ref, wsem = refs[-3:]
    d = x_ref.shape[1]

    @pl.when(pl.program_id(0) == 0)
    def _():
        for j in range(n_col):
            cp = pltpu.make_async_copy(w_hbm.at[:, j * d:(j + 1) * d], stage_ref, wsem)
            cp.start()
            cp.wait()
            w16_ref[j] = stage_ref[...].astype(BF16)

    xn = _rmsnorm_bf16(x_ref[...], g_ref[...])
    j16 = 0
    for j in range(n_col):
        res = jnp.dot(xn, w16_ref[j], preferred_element_type=F32)
        if j == f32_col:
            refs[1][...] = res
        else:
            o16_ref[:, j16 * d:(j16 + 1) * d] = res.astype(BF16)
            j16 += 1


def _proj(x, g, w, tm, f32_col=None):
    n, d = x.shape
    n_col = w.shape[1] // d
    n16 = n_col - (f32_col is not None)
    out_specs = [pl.BlockSpec((tm, n16 * d), lambda i: (i, 0))]
    out_shape = [jax.ShapeDtypeStruct((n, n16 * d), BF16)]
    if f32_col is not None:
        out_specs.append(pl.BlockSpec((tm, d), lambda i: (i, 0)))
        out_shape.append(jax.ShapeDtypeStruct((n, d), F32))
    return pl.pallas_call(
        functools.partial(_proj_body, n_col, f32_col),
        grid=(n // tm,),
        in_specs=[pl.BlockSpec((tm, d), lambda i: (i, 0)),
                  pl.BlockSpec((1, d), lambda i: (0, 0)),
                  pl.BlockSpec(memory_space=pl.ANY)],
        out_specs=out_specs,
        out_shape=out_shape,
        scratch_shapes=[pltpu.VMEM((n_col, d, d), BF16), pltpu.VMEM((d, d), F32), pltpu.SemaphoreType.DMA(())],
        compiler_params=_params(1),
        name="proj",
    )(x, g.reshape(1, d), w)


def _forget_lower_bound(lb_ref, layer):
    lbw = lb_ref[...]
    e = jnp.exp(lbw - jnp.max(lbw, axis=0, keepdims=True))
    sm = e / jnp.sum(e, axis=0, keepdims=True)
    return jnp.sum(sm[:layer + 1], axis=0, keepdims=True)


def _head_rmsnorm(o):
    return o * lax.rsqrt(jnp.mean(o * o, axis=-1, keepdims=True) + EPS)


def _sigmoid(x):
    return 0.5 * (jnp.tanh(0.5 * x) + 1.0)


def _silu(x):
    h = 0.5 * x
    return h + h * jnp.tanh(h)


def _split3_bf16(x):
    hi = x.astype(BF16)
    r1 = x - hi.astype(F32)
    mid = r1.astype(BF16)
    lo = (r1 - mid.astype(F32)).astype(BF16)
    return hi, mid, lo


def _hgrn_prompt_body(layer, n_heads, tb, q_ref, f_ref, v_ref, gt_ref, lb_ref, gn_ref,
                      o_ref, s_ref, st_scr, qin_scr, kin_scr, kdec_scr, g_scr, sprev_scr):
    t = pl.program_id(1)
    nc = tb // CHUNK
    d = n_heads * HEAD

    @pl.when(t == 0)
    def _():
        st_scr[...] = jnp.zeros_like(st_scr)

    lb = _forget_lower_bound(lb_ref, layer)
    gn = gn_ref[...]
    row = lax.broadcasted_iota(I32, (tb, tb), 0)
    col = lax.broadcasted_iota(I32, (tb, tb), 1)
    same_chunk = (row // CHUNK) == (col // CHUNK)
    causal = jnp.logical_and(same_chunk, row >= col)

    f = lb + (1.0 - lb) * _sigmoid(f_ref[...])
    logf = jnp.log2(f)
    k = 1.0 - f
    tri = causal.astype(BF16)
    G = None
    for part in _split3_bf16(logf):
        term = jnp.dot(tri, part, preferred_element_type=F32)
        G = term if G is None else G + term
    g_scr[...] = G
    g_last = [g_scr[(c + 1) * CHUNK - 1:(c + 1) * CHUNK, :] for c in range(nc)]
    decay = [jnp.exp2(g) for g in g_last]
    decay_rows = jnp.concatenate([jnp.broadcast_to(dc, (CHUNK, d)) for dc in decay], axis=0)
    k_in = k * jnp.exp2(-G)
    qin_scr[...] = (_silu(q_ref[...].astype(F32)) * jnp.exp2(G)).astype(BF16)
    kin_scr[...] = k_in.astype(BF16)
    kdec_scr[...] = (k_in * decay_rows).astype(BF16)

    row_chunk = lax.broadcasted_iota(I32, (tb, HEAD), 0) // CHUNK
    zero16 = jnp.zeros((), BF16)
    grp = next(g for g in (4, 2, 1) if nc % g == 0)

    for h in range(n_heads):
        hs = slice(h * HEAD, (h + 1) * HEAD)
        qh = qin_scr[:, hs]
        kd = kdec_scr[:, hs]
        vh = v_ref[:, hs]
        a = lax.dot_general(qh, kin_scr[:, hs], (((1,), (1,)), ((), ())), preferred_element_type=F32)
        a = jnp.where(causal, a, 0.0).astype(BF16)
        o = jnp.dot(a, vh, preferred_element_type=F32)
        ut = jnp.concatenate(
            [lax.dot_general(vh[c * CHUNK:(c + 1) * CHUNK], kd[c * CHUNK:(c + 1) * CHUNK],
                             (((0,), (0,)), ((), ())), preferred_element_type=F32) for c in range(nc)], axis=0)
        st = st_scr[h]
        for c in range(nc):
            sprev_scr[h, :, c * HEAD:(c + 1) * HEAD] = st.astype(BF16)
            st = st * decay[c][:, hs] + ut[c * HEAD:(c + 1) * HEAD, :]
        st_scr[h] = st
        inter = []
        for g0 in range(0, nc, grp):
            rows = slice(g0 * CHUNK, (g0 + grp) * CHUNK)
            q_blocks = jnp.concatenate(
                [jnp.where(row_chunk[rows] == g0 + j, qh[rows], zero16) for j in range(grp)], axis=1)
            inter.append(lax.dot_general(q_blocks, sprev_scr[h, :, g0 * HEAD:(g0 + grp) * HEAD],
                                         (((1,), (1,)), ((), ())), preferred_element_type=F32))
        o = o + jnp.concatenate(inter, axis=0)
        o = _head_rmsnorm(o) * gn[:, hs] * _silu(gt_ref[:, hs].astype(F32))
        o_ref[:, hs] = o.astype(BF16)

    @pl.when(t == pl.num_programs(1) - 1)
    def _():
        for h in range(n_heads):
            s_ref[0, 0, h] = st_scr[h].T


def _hgrn_prompt(p16, p32, hgrn_lb, g_norm, layer, batch, seq, tb):
    d = g_norm.shape[0]
    n_heads = d // HEAD
    n_t = seq // tb
    blk = lambda kk: pl.BlockSpec((tb, d), lambda b, t, kk=kk: (b * n_t + t, kk))
    return pl.pallas_call(
        functools.partial(_hgrn_prompt_body, layer, n_heads, tb),
        grid=(batch, n_t),
        in_specs=[blk(0), blk(0), blk(1), blk(2),
                  pl.BlockSpec(hgrn_lb.shape, lambda b, t: (0, 0)),
                  pl.BlockSpec((1, d), lambda b, t: (0, 0))],
        out_specs=[pl.BlockSpec((tb, d), lambda b, t: (b * n_t + t, 0)),
                   pl.BlockSpec((1, 1, n_heads, HEAD, HEAD), lambda b, t: (b, 0, 0, 0, 0))],
        out_shape=[jax.ShapeDtypeStruct((batch * seq, d), BF16),
                   jax.ShapeDtypeStruct((batch, 1, n_heads, HEAD, HEAD), F32)],
        scratch_shapes=[pltpu.VMEM((n_heads, HEAD, HEAD), F32),
                        pltpu.VMEM((tb, d), BF16), pltpu.VMEM((tb, d), BF16), pltpu.VMEM((tb, d), BF16),
                        pltpu.VMEM((tb, d), F32),
                        pltpu.VMEM((n_heads, HEAD, (tb // CHUNK) * HEAD), BF16)],
        compiler_params=_params(2),
        name="hgrn_prompt",
    )(p16, p32, p16, p16, hgrn_lb, g_norm.reshape(1, d))


def _hgrn_sample_body(layer, tg, q_ref, f_ref, v_ref, gt_ref, lb_ref, gn_ref, s0_ref,
                      o_ref, s_ref, o_scr):
    lb_h = _forget_lower_bound(lb_ref, layer)
    gn_h = gn_ref[...]

    q = jax.nn.silu(q_ref[...].astype(F32))
    f = lb_h + (1.0 - lb_h) * jax.nn.sigmoid(f_ref[...])
    decay = f
    k = 1.0 - f
    v = v_ref[...].astype(F32)
    q_t, d_t, k_t = q.T, decay.T, k.T
    for j in range(tg):
        s = d_t[:, j:j + 1] * s0_ref[j, 0, 0] + k_t[:, j:j + 1] * v[j:j + 1, :]
        s_ref[j, 0, 0] = s
        o_scr[j:j + 1, :] = jnp.sum(q_t[:, j:j + 1] * s, axis=0, keepdims=True)
    o = _head_rmsnorm(o_scr[...]) * gn_h * jax.nn.silu(gt_ref[...].astype(F32))
    o_ref[...] = o.astype(BF16)


def _hgrn_sample(p16, p32, hgrn_lb, g_norm, state, layer, row0, tg):
    nb = state.shape[0]
    d = g_norm.shape[0]
    n_heads = d // HEAD
    rb0 = row0 // tg
    blk = lambda kk: pl.BlockSpec((tg, HEAD), lambda g, h, kk=kk: (rb0 + g, kk * n_heads + h))
    st_spec = pl.BlockSpec((tg, 1, 1, HEAD, HEAD), lambda g, h: (g, 0, h, 0, 0))
    return pl.pallas_call(
        functools.partial(_hgrn_sample_body, layer, tg),
        grid=(nb // tg, n_heads),
        in_specs=[blk(0), blk(0), blk(1), blk(2),
                  pl.BlockSpec((hgrn_lb.shape[0], HEAD), lambda g, h: (0, h)),
                  pl.BlockSpec((1, HEAD), lambda g, h: (0, h)),
                  st_spec],
        out_specs=[pl.BlockSpec((tg, HEAD), lambda g, h: (g, h)), st_spec],
        out_shape=[jax.ShapeDtypeStruct((nb, d), BF16),
                   jax.ShapeDtypeStruct(state.shape, F32)],
        scratch_shapes=[pltpu.VMEM((tg, HEAD), F32)],
        compiler_params=_params(2),
        name="hgrn_sample",
    )(p16, p32, p16, p16, hgrn_lb, g_norm.reshape(1, d), state)


_CARRY = 8


def _conv_prompt_body(tb, b_ref, c_ref, h_ref, w_ref, z_ref, buf_ref, u_scr):
    t = pl.program_id(1)

    @pl.when(t == 0)
    def _():
        u_scr[0:_CARRY, :] = jnp.zeros((_CARRY, u_scr.shape[1]), F32)

    u = c_ref[...].astype(F32) * h_ref[...].astype(F32)
    u_scr[_CARRY:_CARRY + tb, :] = u
    w = w_ref[...]
    y = w[0:1, :] * u_scr[_CARRY - 2:_CARRY - 2 + tb, :]
    y = y + w[1:2, :] * u_scr[_CARRY - 1:_CARRY - 1 + tb, :]
    y = y + w[2:3, :] * u
    z_ref[...] = (b_ref[...].astype(F32) * y).astype(BF16)
    u_scr[0:_CARRY, :] = u[tb - _CARRY:tb, :]

    @pl.when(t == pl.num_programs(1) - 1)
    def _():
        buf_ref[0] = u[tb - (CONV_W - 1):tb, :]


def _conv_prompt(pc, conv_w, batch, seq, tb):
    d = conv_w.shape[1]
    n_t = seq // tb
    blk = lambda kk: pl.BlockSpec((tb, d), lambda b, t, kk=kk: (b * n_t + t, kk))
    return pl.pallas_call(
        functools.partial(_conv_prompt_body, tb),
        grid=(batch, n_t),
        in_specs=[blk(0), blk(1), blk(2), pl.BlockSpec(conv_w.shape, lambda b, t: (0, 0))],
        out_specs=[pl.BlockSpec((tb, d), lambda b, t: (b * n_t + t, 0)),
                   pl.BlockSpec((1, CONV_W - 1, d), lambda b, t: (b, 0, 0))],
        out_shape=[jax.ShapeDtypeStruct((batch * seq, d), BF16),
                   jax.ShapeDtypeStruct((batch, CONV_W - 1, d), F32)],
        scratch_shapes=[pltpu.VMEM((_CARRY + tb, d), F32)],
        compiler_params=_params(2),
        name="conv_prompt",
    )(pc, pc, pc, conv_w)


def _conv_sample_body(d, b_ref, c_ref, h_ref, w_ref, st_ref, z_ref, buf_ref):
    u = c_ref[...].astype(F32) * h_ref[...].astype(F32)
    w = w_ref[...]
    buf0 = st_ref[:, 0:d]
    buf1 = st_ref[:, d:2 * d]
    y = w[0:1, :] * buf0
    y = y + w[1:2, :] * buf1
    y = y + w[2:3, :] * u
    z_ref[...] = (b_ref[...].astype(F32) * y).astype(BF16)
    buf_ref[:, 0:d] = buf1
    buf_ref[:, d:2 * d] = u


def _conv_sample(pc, conv_w, state2d, row0):
    nb = state2d.shape[0]
    d = conv_w.shape[1]
    rb0 = row0 // nb
    blk = lambda kk: pl.BlockSpec((nb, d), lambda i, kk=kk: (rb0, kk))
    return pl.pallas_call(
        functools.partial(_conv_sample_body, d),
        grid=(1,),
        in_specs=[blk(0), blk(1), blk(2), pl.BlockSpec(conv_w.shape, lambda i: (0, 0)),
                  pl.BlockSpec(state2d.shape, lambda i: (0, 0))],
        out_specs=[pl.BlockSpec((nb, d), lambda i: (0, 0)),
                   pl.BlockSpec(state2d.shape, lambda i: (0, 0))],
        out_shape=[jax.ShapeDtypeStruct((nb, d), BF16),
                   jax.ShapeDtypeStruct(state2d.shape, F32)],
        compiler_params=_params(1),
        name="conv_sample",
    )(pc, pc, pc, conv_w, state2d)


def _outproj_body(n_blocks, x_ref, z_ref, w_ref, o_ref, w16_ref):
    i = pl.program_id(0)

    @pl.when(i == 0)
    def _():
        w16_ref[...] = w_ref[...].astype(BF16)

    @pl.when(i < n_blocks)
    def _():
        o_ref[...] = x_ref[...] + jnp.dot(z_ref[...], w16_ref[...], preferred_element_type=F32)

    @pl.when(i >= n_blocks)
    def _():
        o_ref[...] = jnp.zeros_like(o_ref)


def _outproj_prompt(x, z, w, n_rows, tm):
    d = x.shape[1]
    n_blocks = n_rows // tm
    row_blk = lambda i: (jnp.minimum(i, n_blocks - 1), 0)
    return pl.pallas_call(
        functools.partial(_outproj_body, n_blocks),
        grid=(n_blocks + 1,),
        in_specs=[pl.BlockSpec((tm, d), row_blk),
                  pl.BlockSpec((tm, d), row_blk),
                  pl.BlockSpec((d, d), lambda i: (0, 0))],
        out_specs=pl.BlockSpec((tm, d), lambda i: (i, 0)),
        out_shape=jax.ShapeDtypeStruct(((n_blocks + 1) * tm, d), F32),
        scratch_shapes=[pltpu.VMEM((d, d), BF16)],
        compiler_params=_params(1),
        name="outproj",
    )(x, z, w)


def _outproj_sample_body(x_ref, z_ref, w_ref, buf_ref, o_ref):
    del buf_ref
    o_ref[...] = x_ref[...] + jnp.dot(z_ref[...], w_ref[...].astype(BF16), preferred_element_type=F32)


def _outproj_sample(x, x_row0, z, w, buf, row0):
    nb, d = z.shape
    return pl.pallas_call(
        _outproj_sample_body,
        grid=(1,),
        in_specs=[pl.BlockSpec((nb, d), lambda i: (x_row0 // nb, 0)),
                  pl.BlockSpec((nb, d), lambda i: (0, 0)),
                  pl.BlockSpec((d, d), lambda i: (0, 0)),
                  pl.BlockSpec(memory_space=pl.ANY)],
        out_specs=pl.BlockSpec((nb, d), lambda i: (row0 // nb, 0)),
        out_shape=jax.ShapeDtypeStruct(buf.shape, buf.dtype),
        input_output_aliases={3: 0},
        compiler_params=_params(1),
        name="outproj_sample",
    )(x, z, w, buf)


def _swiglu_partial(xn, w1, w3, w2):
    h1 = jnp.dot(xn, w1.astype(BF16), preferred_element_type=F32)
    h3 = jnp.dot(xn, w3.astype(BF16), preferred_element_type=F32)
    h = (jax.nn.silu(h1) * h3).astype(BF16)
    return jnp.dot(h, w2.astype(BF16), preferred_element_type=F32)


def _cache_weight_chunk(w1_hbm, w3_hbm, w2_hbm, f, n_f, tf, stages, caches, wsem):
    n_slots = stages[0].shape[0]

    def copies(ff, ws):
        cols = pl.ds(pl.multiple_of(ff * tf, tf), tf)
        srcs = (w1_hbm.at[:, cols], w3_hbm.at[:, cols], w2_hbm.at[cols, :])
        return [pltpu.make_async_copy(src, st.at[ws], wsem.at[ws, m]) for m, (src, st) in enumerate(zip(srcs, stages))]

    if n_slots == 1:
        ws = 0
        for c in copies(f, 0):
            c.start()
    else:
        ws = lax.rem(f, 2)

        @pl.when(f == 0)
        def _():
            for c in copies(0, 0):
                c.start()

        @pl.when(f + 1 < n_f)
        def _():
            for c in copies(f + 1, 1 - ws):
                c.start()

    for c in copies(f, ws):
        c.wait()
    for st, ca in zip(stages, caches):
        ca[f] = st[ws].astype(BF16)


def _ffn_body(n_f, tf, x_ref, g_ref, w1_hbm, w3_hbm, w2_hbm, o_ref, xn_ref, acc_ref, c1, c3, c2, s1, s3, s2, wsem):
    i = pl.program_id(0)
    xn_ref[...] = _rmsnorm_bf16(x_ref[...], g_ref[...])
    acc_ref[...] = jnp.zeros_like(acc_ref)

    def chunk(f, carry):
        @pl.when(i == 0)
        def _():
            _cache_weight_chunk(w1_hbm, w3_hbm, w2_hbm, f, n_f, tf, (s1, s3, s2), (c1, c3, c2), wsem)

        acc_ref[...] += _swiglu_partial(xn_ref[...], c1[f], c3[f], c2[f])
        return carry

    lax.fori_loop(0, n_f, chunk, 0)
    o_ref[...] = x_ref[...] + acc_ref[...]


def _ffn(x, n, g, w1, w3, w2, tm, tf):
    d = x.shape[1]
    dff = w1.shape[1]
    n_f = dff // tf
    return pl.pallas_call(
        functools.partial(_ffn_body, n_f, tf),
        grid=(n // tm,),
        in_specs=[pl.BlockSpec((tm, d), lambda i: (i, 0)),
                  pl.BlockSpec((1, d), lambda i: (0, 0)),
                  pl.BlockSpec(memory_space=pl.ANY),
                  pl.BlockSpec(memory_space=pl.ANY),
                  pl.BlockSpec(memory_space=pl.ANY)],
        out_specs=pl.BlockSpec((tm, d), lambda i: (i, 0)),
        out_shape=jax.ShapeDtypeStruct((n, d), F32),
        scratch_shapes=[pltpu.VMEM((tm, d), BF16), pltpu.VMEM((tm, d), F32),
                        pltpu.VMEM((n_f, d, tf), BF16), pltpu.VMEM((n_f, d, tf), BF16),
                        pltpu.VMEM((n_f, tf, d), BF16),
                        pltpu.VMEM((1, d, tf), F32), pltpu.VMEM((1, d, tf), F32), pltpu.VMEM((1, tf, d), F32),
                        pltpu.SemaphoreType.DMA((1, 3))],
        compiler_params=_params(1),
        name="ffn",
    )(x, g.reshape(1, d), w1, w3, w2)


ROUTE_ROWS = 8


def _router_body(n_experts, n_tok, x_ref, g_ref, rw_ref, route_ref, cnt_ref, cnt_scr):
    i = pl.program_id(0)
    tm = x_ref.shape[0]

    @pl.when(i == 0)
    def _():
        cnt_scr[...] = jnp.zeros_like(cnt_scr)

    xn = _rmsnorm_bf16(x_ref[...], g_ref[...])
    logits = jnp.dot(xn, rw_ref[...].astype(BF16), preferred_element_type=F32)
    lane = lax.broadcasted_iota(I32, (tm, LANES), 1).astype(F32)
    neg = jnp.float32(-jnp.inf)
    logits = jnp.where(lane < n_experts, logits, neg)
    m1 = jnp.max(logits, axis=-1, keepdims=True)
    i1 = jnp.min(jnp.where(logits == m1, lane, float(LANES)), axis=-1, keepdims=True)
    i1 = jnp.minimum(i1, float(n_experts - 1))
    rest = jnp.where(lane == i1, neg, logits)
    m2 = jnp.max(rest, axis=-1, keepdims=True)
    i2 = jnp.min(jnp.where(rest == m2, lane, float(LANES)), axis=-1, keepdims=True)
    i2 = jnp.minimum(i2, float(n_experts - 1))
    e2 = jnp.exp(m2 - m1)
    den = 1.0 + e2
    g1 = 1.0 / den
    g2 = e2 / den

    sel1 = lane == i1
    sel2 = lane == i2
    is_token = i * tm + lax.broadcasted_iota(I32, (tm, LANES), 0) < n_tok
    onehot = jnp.logical_and(jnp.logical_or(sel1, sel2), is_token)
    row = lax.broadcasted_iota(I32, (tm, tm), 0)
    col = lax.broadcasted_iota(I32, (tm, tm), 1)
    before = (row > col).astype(BF16)
    rank = jnp.dot(before, onehot.astype(BF16), preferred_element_type=F32) + cnt_scr[...]
    r1 = jnp.sum(jnp.where(sel1, rank, 0.0), axis=-1, keepdims=True)
    r2 = jnp.sum(jnp.where(sel2, rank, 0.0), axis=-1, keepdims=True)
    cnt_scr[...] += jnp.sum(onehot.astype(F32), axis=0, keepdims=True)

    out = jnp.zeros((tm, LANES), F32)
    for slot, val in enumerate((i1, i2, g1, g2, r1, r2)):
        out = jnp.where(lane == slot, val, out)
    route_ref[...] = out.T[0:ROUTE_ROWS, :]
    cnt_ref[...] = cnt_scr[...]


def _router(x, n_tok, g, rw_pad, n_experts, tm):
    n, d = x.shape
    return pl.pallas_call(
        functools.partial(_router_body, n_experts, n_tok),
        grid=(n // tm,),
        in_specs=[pl.BlockSpec((tm, d), lambda i: (i, 0)),
                  pl.BlockSpec((1, d), lambda i: (0, 0)),
                  pl.BlockSpec((d, LANES), lambda i: (0, 0))],
        out_specs=[pl.BlockSpec((ROUTE_ROWS, tm), lambda i: (0, i)),
                   pl.BlockSpec((1, LANES), lambda i: (0, 0))],
        out_shape=[jax.ShapeDtypeStruct((ROUTE_ROWS, n), F32),
                   jax.ShapeDtypeStruct((1, LANES), F32)],
        scratch_shapes=[pltpu.VMEM((1, LANES), F32)],
        compiler_params=_params(1),
        name="router",
    )(x, g.reshape(1, d), rw_pad)


SUBLANES = 8
MOE_TM = 672


def _start_row_group(src_hbm, idx_ref, g, dst, sem):
    for k in range(SUBLANES):
        pltpu.make_async_copy(src_hbm.at[pl.ds(idx_ref[0, 0, g * SUBLANES + k], 1)],
                              dst.at[g, pl.ds(k, 1)], sem).start(priority=k % 2)


def _start_row_gather(src_hbm, idx_ref, n_rows, dst, sem, unrolled=False):
    if unrolled:
        for g in range(n_rows // SUBLANES):
            _start_row_group(src_hbm, idx_ref, g, dst, sem)
        return

    def body(g, carry):
        _start_row_group(src_hbm, idx_ref, g, dst, sem)
        return carry

    lax.fori_loop(0, n_rows // SUBLANES, body, 0)


def _wait_row_gather(src_hbm, n_rows, dst, sem):
    pltpu.make_async_copy(src_hbm.at[pl.ds(0, n_rows)], dst.reshape(n_rows, dst.shape[-1]), sem).wait()


def _moe_body(n_f, tf, te_ref, nu_ref, hf_ref, idx_cur, idx_nxt, x_hbm, gate_ref, g_ref, w1_hbm, w3_hbm, w2_hbm,
              y_ref, xbuf, sem, xn_ref, acc_ref, c1, c3, c2, s1, s3, s2, wsem):
    i = pl.program_id(0)
    n_tiles = pl.num_programs(0)
    tm = xbuf.shape[1] * SUBLANES
    groups_per_chunk = tm // n_f // SUBLANES
    slot = lax.rem(i, 2)
    n_used = nu_ref[0]
    used = i < n_used
    e = te_ref[i]
    first_of_expert = jnp.logical_or(i == 0, e != te_ref[jnp.maximum(i - 1, 0)])
    half_filled = hf_ref[i] == 1

    @pl.when(i == 0)
    def _():
        _start_row_gather(x_hbm, idx_cur, tm, xbuf.at[0], sem.at[0])

    @pl.when(i <= n_used)
    def _():
        _wait_row_gather(x_hbm, tm, xbuf.at[slot], sem.at[slot])

    @pl.when(used)
    def _():
        xn_ref[...] = _rmsnorm_bf16(xbuf[slot].reshape(tm, xbuf.shape[-1]), g_ref[...])
        acc_ref[...] = jnp.zeros_like(acc_ref)

        def chunk(f, carry):
            @pl.when(first_of_expert)
            def _():
                _cache_weight_chunk(w1_hbm.at[e], w3_hbm.at[e], w2_hbm.at[e], f, n_f, tf,
                                    (s1, s3, s2), (c1, c3, c2), wsem)

            def compute(rows):
                for g in range(groups_per_chunk):
                    _start_row_group(x_hbm, idx_nxt, f * groups_per_chunk + g, xbuf.at[1 - slot], sem.at[1 - slot])
                acc_ref[0:rows, :] += _swiglu_partial(xn_ref[0:rows, :], c1[f], c3[f], c2[f])

            @pl.when(half_filled)
            def _():
                compute(tm // 2)

            @pl.when(jnp.logical_not(half_filled))
            def _():
                compute(tm)

            return carry

        lax.fori_loop(0, n_f, chunk, 0)
        r_id = lax.broadcasted_iota(I32, (tm, tm), 0)
        c_id = lax.broadcasted_iota(I32, (tm, tm), 1)
        gate_col = jnp.sum(jnp.where(r_id == c_id, gate_ref[0], 0.0), axis=1, keepdims=True)
        y_ref[...] = acc_ref[...] * gate_col

    @pl.when(jnp.logical_not(used))
    def _():
        y_ref[...] = jnp.zeros_like(y_ref)

    @pl.when(jnp.logical_and(used, i == n_tiles - 1))
    def _():
        _wait_row_gather(x_hbm, tm, xbuf.at[1 - slot], sem.at[1 - slot])


def _moe(x, g, w1, w3, w2, tile_expert, n_used, half_filled, inv3, gate_sorted, tm, tf):
    n_tiles = inv3.shape[0]
    d = x.shape[1]
    dff = w1.shape[2]
    n_f = dff // tf
    assert tm % (n_f * SUBLANES) == 0, "each hidden chunk fetches whole 8-row tiles of the next row tile"
    assert tm % 32 == 0, "half a row tile must be whole bf16 sublane tiles"
    grid_spec = pltpu.PrefetchScalarGridSpec(
        num_scalar_prefetch=3,
        grid=(n_tiles,),
        in_specs=[pl.BlockSpec((1, 1, tm), lambda i, te, nu, hf: (i, 0, 0), memory_space=pltpu.SMEM),
                  pl.BlockSpec((1, 1, tm), lambda i, te, nu, hf: (jnp.minimum(i + 1, n_tiles - 1), 0, 0),
                               memory_space=pltpu.SMEM),
                  pl.BlockSpec(memory_space=pl.ANY),
                  pl.BlockSpec((1, 1, tm), lambda i, te, nu, hf: (i, 0, 0)),
                  pl.BlockSpec((1, d), lambda i, te, nu, hf: (0, 0)),
                  pl.BlockSpec(memory_space=pl.ANY),
                  pl.BlockSpec(memory_space=pl.ANY),
                  pl.BlockSpec(memory_space=pl.ANY)],
        out_specs=pl.BlockSpec((tm, d), lambda i, te, nu, hf: (i, 0)),
        scratch_shapes=[pltpu.VMEM((2, tm // SUBLANES, SUBLANES, d), F32),
                        pltpu.SemaphoreType.DMA((2,)),
                        pltpu.VMEM((tm, d), BF16),
                        pltpu.VMEM((tm, d), F32),
                        pltpu.VMEM((n_f, d, tf), BF16),
                        pltpu.VMEM((n_f, d, tf), BF16),
                        pltpu.VMEM((n_f, tf, d), BF16),
                        pltpu.VMEM((2, d, tf), F32),
                        pltpu.VMEM((2, d, tf), F32),
                        pltpu.VMEM((2, tf, d), F32),
                        pltpu.SemaphoreType.DMA((2, 3))],
    )
    return pl.pallas_call(
        functools.partial(_moe_body, n_f, tf),
        grid_spec=grid_spec,
        out_shape=jax.ShapeDtypeStruct((n_tiles * tm, d), F32),
        compiler_params=_params(1),
        name="moe",
    )(tile_expert, n_used, half_filled, inv3, inv3, x, gate_sorted, g.reshape(1, d), w1, w3, w2)


def _combine_body(pos_cur, pos_nxt, x_ref, g_ref, y_hbm, o_ref, ybuf, sem):
    i = pl.program_id(0)
    n_tiles = pl.num_programs(0)
    rows = ybuf.shape[1] * SUBLANES
    slot = lax.rem(i, 2)

    @pl.when(i == 0)
    def _():
        _start_row_gather(y_hbm, pos_cur, rows, ybuf.at[0], sem.at[0])

    @pl.when(i + 1 < n_tiles)
    def _():
        _start_row_gather(y_hbm, pos_nxt, rows, ybuf.at[1 - slot], sem.at[1 - slot], unrolled=True)

    _wait_row_gather(y_hbm, rows, ybuf.at[slot], sem.at[slot])
    tc = rows // TOP_K
    y = ybuf[slot].reshape(rows, ybuf.shape[-1])
    moe = y[0:tc, :] + y[tc:rows, :]
    x = x_ref[...] + moe
    ms = jnp.mean(x * x, axis=-1, keepdims=True)
    o_ref[...] = x * lax.rsqrt(ms + EPS) * g_ref[...]


def _combine(x, g, y_sorted, pos3, row0, n_rows, tc):
    d = x.shape[1]
    n_tiles = n_rows // tc
    rb0 = row0 // tc
    grid_spec = pltpu.PrefetchScalarGridSpec(
        num_scalar_prefetch=0,
        grid=(n_tiles,),
        in_specs=[pl.BlockSpec((1, 1, TOP_K * tc), lambda i: (i, 0, 0), memory_space=pltpu.SMEM),
                  pl.BlockSpec((1, 1, TOP_K * tc), lambda i: (jnp.minimum(i + 1, n_tiles - 1), 0, 0),
                               memory_space=pltpu.SMEM),
                  pl.BlockSpec((tc, d), lambda i: (rb0 + i, 0)),
                  pl.BlockSpec((1, d), lambda i: (0, 0)),
                  pl.BlockSpec(memory_space=pl.ANY)],
        out_specs=pl.BlockSpec((tc, d), lambda i: (i, 0)),
        scratch_shapes=[pltpu.VMEM((2, TOP_K * tc // SUBLANES, SUBLANES, d), F32), pltpu.SemaphoreType.DMA((2,))],
    )
    return pl.pallas_call(
        _combine_body,
        grid_spec=grid_spec,
        out_shape=jax.ShapeDtypeStruct((n_rows, d), F32),
        compiler_params=_params(1),
        name="combine",
    )(pos3, pos3, x, g.reshape(1, d), y_sorted)


def _lookup(table, idx):
    out = jnp.zeros(idx.shape, table.dtype)
    for e in range(table.shape[0]):
        out = jnp.where(idx == e, table[e], out)
    return out


def _expert_layer(x, g_ffn, g_final, router_w, w1, w3, w2, n_prompt, n_dec, tm_router, tm_moe, tf, tc):
    n = n_prompt + n_dec
    n_experts = router_w.shape[-1]
    rw_pad = jnp.pad(router_w, ((0, 0), (0, LANES - n_experts)))
    route, cnt = _router(x, n, g_ffn, rw_pad, n_experts, tm_router)
    ids = route[0:TOP_K, :n].astype(I32)
    gates = route[TOP_K:2 * TOP_K, :n]
    ranks = route[2 * TOP_K:3 * TOP_K, :n].astype(I32)
    counts = cnt[0, :n_experts].astype(I32)

    n_tiles = (TOP_K * n + n_experts * (tm_moe - 1)) // tm_moe
    padded = ((counts + tm_moe - 1) // tm_moe) * tm_moe
    ends = jnp.cumsum(padded)
    starts = ends - padded
    cstart = jnp.cumsum(counts) - counts
    pos = _lookup(starts, ids) + ranks
    keys = ids * n + jnp.arange(n, dtype=I32)[None, :]
    order = jnp.argsort(keys.reshape(-1)).astype(I32)
    slot_pos = jnp.arange(n_tiles * tm_moe, dtype=I32)
    slot_e = jnp.minimum(jnp.sum(slot_pos[None, :] >= ends[:, None], axis=0), n_experts - 1).astype(I32)
    slot_rank = slot_pos - _lookup(starts, slot_e)
    slot_valid = slot_rank < _lookup(counts, slot_e)
    slot_asg = order[jnp.clip(_lookup(cstart, slot_e) + slot_rank, 0, TOP_K * n - 1)]
    inv = jnp.where(slot_valid, slot_asg % n, 0).astype(I32)
    gate_sorted = jnp.where(slot_valid, gates.reshape(-1)[slot_asg], 0.0)
    tile_expert = slot_e[::tm_moe]
    n_used = (ends[-1] // tm_moe).astype(I32).reshape(1)

    tile_start = jnp.arange(n_tiles, dtype=I32) * tm_moe
    tile_rows = _lookup(counts, tile_expert) - (tile_start - _lookup(starts, tile_expert))
    half_filled = (tile_rows <= tm_moe // 2).astype(I32)

    y_sorted = _moe(x, g_ffn, w1, w3, w2, tile_expert, n_used, half_filled, inv.reshape(n_tiles, 1, tm_moe),
                    gate_sorted.reshape(n_tiles, 1, tm_moe), tm_moe, tf)

    def tile_positions(row0, n_rows, t):
        p = pos[:, row0:row0 + n_rows].reshape(TOP_K, n_rows // t, t)
        return jnp.swapaxes(p, 0, 1).reshape(n_rows // t, 1, TOP_K * t)

    y_prompt = _combine(x, g_final, y_sorted, tile_positions(0, n_prompt, tc), 0, n_prompt, tc)
    y_sample = _combine(x, g_final, y_sorted, tile_positions(n_prompt, n_dec, n_dec), n_prompt, n_dec, n_dec)
    return y_prompt, y_sample


def kernel(x_prompt, x_sample, state_hgrn, state_conv, norm_mix, norm_ffn, norm_final, hgrn_w_in, hgrn_lb, hgrn_g_norm, hgrn_w_out, conv_w_in, conv_w, conv_w_out, ffn_w1, ffn_w3, ffn_w2, moe_router, moe_w1, moe_w3, moe_w2):
    batch, seq, d = x_prompt.shape
    n_dec = x_sample.shape[0]
    n_prompt = batch * seq
    n = n_prompt + n_dec
    n_experts = moe_router.shape[-1]
    assert x_sample.shape[1] == 1 and d % HEAD == 0 and seq % CHUNK == 0
    assert norm_mix.shape[0] == 2, "one HGRN2 layer followed by one short-conv layer"

    tm = _pick_tile(n, (688, 384, 128, 16))
    tm_proj = _pick_tile(n, (1376, 688, 384, 128, 16))
    tm_prompt = _pick_tile(n_prompt, (1024, 512, 256, 128, 16))
    tb = _pick_tile(seq, (256, 128, 64, 32))
    tb_conv = _pick_tile(seq, (512, 256, 128, 64, 32))
    tg = _pick_tile(n_dec, (16,))
    tf = _pick_tile(ffn_w1.shape[-1], (512, 256, 128))
    tm_moe = _pick_tile(n_prompt, (512, 128, 16))
    tm_moe = MOE_TM if n_prompt % 128 == 0 and ffn_w1.shape[-1] % (7 * 512) == 0 else tm_moe
    tc = _pick_tile(n_prompt, (512, 128, 16))
    tm_router = tm_prompt if tm_prompt % LANES == 0 else n_prompt + tm_prompt
    assert n_prompt % tg == 0 and n_prompt % n_dec == 0 and n_prompt % tc == 0

    xp = x_prompt.reshape(n_prompt, d)
    xs = x_sample.reshape(n_dec, d)

    p16, p32 = _proj(xp, norm_mix[0], hgrn_w_in[0], tm_prompt, f32_col=1)
    p16_s, p32_s = _proj(xs, norm_mix[0], hgrn_w_in[0], n_dec, f32_col=1)
    o_p, s_prompt = _hgrn_prompt(p16, p32, hgrn_lb, hgrn_g_norm[0], 0, batch, seq, tb)
    o_s, s_sample = _hgrn_sample(p16_s, p32_s, hgrn_lb, hgrn_g_norm[0], state_hgrn, 0, 0, tg)
    x = _outproj_prompt(xp, o_p, hgrn_w_out[0], n_prompt, tm_prompt)
    x = _outproj_sample(xs, 0, o_s, hgrn_w_out[0], x, n_prompt)
    x = _ffn(x, n, norm_ffn[0], ffn_w1[0], ffn_w3[0], ffn_w2[0], tm, tf)

    (pc,) = _proj(x, norm_mix[1], conv_w_in[0], tm_proj)
    z_p, c_prompt = _conv_prompt(pc, conv_w[0], batch, seq, tb_conv)
    z_s, c_sample = _conv_sample(pc, conv_w[0], state_conv.reshape(n_dec, (CONV_W - 1) * d), n_prompt)
    x1 = _outproj_prompt(x, z_p, conv_w_out[0], n_prompt, tm_prompt)
    x1 = _outproj_sample(x, n_prompt, z_s, conv_w_out[0], x1, n_prompt)

    y_prompt, y_sample = _expert_layer(x1, norm_ffn[1], norm_final, moe_router[0], moe_w1[0], moe_w3[0], moe_w2[0],
                                       n_prompt, n_dec, tm_router, tm_moe, tf, tc)

    return (y_prompt.reshape(batch, seq, d),
            y_sample.reshape(n_dec, 1, d),
            s_prompt,
            s_sample,
            c_prompt.reshape(batch, 1, CONV_W - 1, d),
            c_sample.reshape(n_dec, 1, CONV_W - 1, d))
```

```python
import functools

import jax
import jax.numpy as jnp
from jax import lax
from jax.experimental import pallas as pl
from jax.experimental.pallas import tpu as pltpu

F32 = jnp.float32
BF16 = jnp.bfloat16
I32 = jnp.int32

EPS = 1e-6
HEAD = 128
CHUNK = 32
CONV_W = 3
TOP_K = 2
LANES = 128
V7X_VMEM_LIMIT = 56 * 1024 * 1024

ARB = "arbitrary"


def _params(n_axes):
    return pltpu.CompilerParams(dimension_semantics=(ARB,) * n_axes, vmem_limit_bytes=V7X_VMEM_LIMIT)


def _pick_tile(n, candidates):
    for c in candidates:
        if n % c == 0:
            return c
    raise ValueError(f"no tile in {candidates} divides {n}")


def _rmsnorm_bf16(x, g):
    ms = jnp.mean(x * x, axis=-1, keepdims=True)
    return (x * lax.rsqrt(ms + EPS) * g).astype(BF16)


def _proj_body(n_col, f32_col, x_ref, g_ref, w_hbm, *refs):
    o16_ref = refs[0]
    w16_ref, stage_ref, wsem = refs[-3:]
    d = x_ref.shape[1]

    @pl.when(pl.program_id(0) == 0)
    def _():
        for j in range(n_col):
            cp = pltpu.make_async_copy(w_hbm.at[:, j * d:(j + 1) * d], stage_ref, wsem)
            cp.start()
            cp.wait()
            w16_ref[j] = stage_ref[...].astype(BF16)

    xn = _rmsnorm_bf16(x_ref[...], g_ref[...])
    j16 = 0
    for j in range(n_col):
        res = jnp.dot(xn, w16_ref[j], preferred_element_type=F32)
        if j == f32_col:
            refs[1][...] = res
        else:
            o16_ref[:, j16 * d:(j16 + 1) * d] = res.astype(BF16)
            j16 += 1


def _proj(x, g, w, tm, f32_col=None):
    n, d = x.shape
    n_col = w.shape[1] // d
    n16 = n_col - (f32_col is not None)
    out_specs = [pl.BlockSpec((tm, n16 * d), lambda i: (i, 0))]
    out_shape = [jax.ShapeDtypeStruct((n, n16 * d), BF16)]
    if f32_col is not None:
        out_specs.append(pl.BlockSpec((tm, d), lambda i: (i, 0)))
        out_shape.append(jax.ShapeDtypeStruct((n, d), F32))
    return pl.pallas_call(
        functools.partial(_proj_body, n_col, f32_col),
        grid=(n // tm,),
        in_specs=[pl.BlockSpec((tm, d), lambda i: (i, 0)),
                  pl.BlockSpec((1, d), lambda i: (0, 0)),
                  pl.BlockSpec(memory_space=pl.ANY)],
        out_specs=out_specs,
        out_shape=out_shape,
        scratch_shapes=[pltpu.VMEM((n_col, d, d), BF16), pltpu.VMEM((d, d), F32), pltpu.SemaphoreType.DMA(())],
        compiler_params=_params(1),
        name="proj",
    )(x, g.reshape(1, d), w)


def _forget_lower_bound(lb_ref, layer):
    lbw = lb_ref[...]
    e = jnp.exp(lbw - jnp.max(lbw, axis=0, keepdims=True))
    sm = e / jnp.sum(e, axis=0, keepdims=True)
    return jnp.sum(sm[:layer + 1], axis=0, keepdims=True)


def _head_rmsnorm(o):
    return o * lax.rsqrt(jnp.mean(o * o, axis=-1, keepdims=True) + EPS)


def _sigmoid(x):
    return 0.5 * (jnp.tanh(0.5 * x) + 1.0)


def _silu(x):
    h = 0.5 * x
    return h + h * jnp.tanh(h)


def _split3_bf16(x):
    hi = x.astype(BF16)
    r1 = x - hi.astype(F32)
    mid = r1.astype(BF16)
    lo = (r1 - mid.astype(F32)).astype(BF16)
    return hi, mid, lo


def _hgrn_prompt_body(layer, n_heads, tb, q_ref, f_ref, v_ref, gt_ref, lb_ref, gn_ref,
                      o_ref, s_ref, st_scr, qin_scr, kin_scr, kdec_scr, g_scr, sprev_scr):
    t = pl.program_id(1)
    nc = tb // CHUNK
    d = n_heads * HEAD

    @pl.when(t == 0)
    def _():
        st_scr[...] = jnp.zeros_like(st_scr)

    lb = _forget_lower_bound(lb_ref, layer)
    gn = gn_ref[...]
    row = lax.broadcasted_iota(I32, (tb, tb), 0)
    col = lax.broadcasted_iota(I32, (tb, tb), 1)
    same_chunk = (row // CHUNK) == (col // CHUNK)
    causal = jnp.logical_and(same_chunk, row >= col)

    f = lb + (1.0 - lb) * _sigmoid(f_ref[...])
    logf = jnp.log2(f)
    k = 1.0 - f
    tri = causal.astype(BF16)
    G = None
    for part in _split3_bf16(logf):
        term = jnp.dot(tri, part, preferred_element_type=F32)
        G = term if G is None else G + term
    g_scr[...] = G
    g_last = [g_scr[(c + 1) * CHUNK - 1:(c + 1) * CHUNK, :] for c in range(nc)]
    decay = [jnp.exp2(g) for g in g_last]
    decay_rows = jnp.concatenate([jnp.broadcast_to(dc, (CHUNK, d)) for dc in decay], axis=0)
    k_in = k * jnp.exp2(-G)
    qin_scr[...] = (_silu(q_ref[...].astype(F32)) * jnp.exp2(G)).astype(BF16)
    kin_scr[...] = k_in.astype(BF16)
    kdec_scr[...] = (k_in * decay_rows).astype(BF16)

    row_chunk = lax.broadcasted_iota(I32, (tb, HEAD), 0) // CHUNK
    zero16 = jnp.zeros((), BF16)
    grp = next(g for g in (4, 2, 1) if nc % g == 0)

    for h in range(n_heads):
        hs = slice(h * HEAD, (h + 1) * HEAD)
        qh = qin_scr[:, hs]
        kd = kdec_scr[:, hs]
        vh = v_ref[:, hs]
        a = lax.dot_general(qh, kin_scr[:, hs], (((1,), (1,)), ((), ())), preferred_element_type=F32)
        a = jnp.where(causal, a, 0.0).astype(BF16)
        o = jnp.dot(a, vh, preferred_element_type=F32)
        ut = jnp.concatenate(
            [lax.dot_general(vh[c * CHUNK:(c + 1) * CHUNK], kd[c * CHUNK:(c + 1) * CHUNK],
                             (((0,), (0,)), ((), ())), preferred_element_type=F32) for c in range(nc)], axis=0)
        st = st_scr[h]
        for c in range(nc):
            sprev_scr[h, :, c * HEAD:(c + 1) * HEAD] = st.astype(BF16)
            st = st * decay[c][:, hs] + ut[c * HEAD:(c + 1) * HEAD, :]
        st_scr[h] = st
        inter = []
        for g0 in range(0, nc, grp):
            rows = slice(g0 * CHUNK, (g0 + grp) * CHUNK)
            q_blocks = jnp.concatenate(
                [jnp.where(row_chunk[rows] == g0 + j, qh[rows], zero16) for j in range(grp)], axis=1)
            inter.append(lax.dot_general(q_blocks, sprev_scr[h, :, g0 * HEAD:(g0 + grp) * HEAD],
                                         (((1,), (1,)), ((), ())), preferred_element_type=F32))
        o = o + jnp.concatenate(inter, axis=0)
        o = _head_rmsnorm(o) * gn[:, hs] * _silu(gt_ref[:, hs].astype(F32))
        o_ref[:, hs] = o.astype(BF16)

    @pl.when(t == pl.num_programs(1) - 1)
    def _():
        for h in range(n_heads):
            s_ref[0, 0, h] = st_scr[h].T


def _hgrn_prompt(p16, p32, hgrn_lb, g_norm, layer, batch, seq, tb):
    d = g_norm.shape[0]
    n_heads = d // HEAD
    n_t = seq // tb
    blk = lambda kk: pl.BlockSpec((tb, d), lambda b, t, kk=kk: (b * n_t + t, kk))
    return pl.pallas_call(
        functools.partial(_hgrn_prompt_body, layer, n_heads, tb),
        grid=(batch, n_t),
        in_specs=[blk(0), blk(0), blk(1), blk(2),
                  pl.BlockSpec(hgrn_lb.shape, lambda b, t: (0, 0)),
                  pl.BlockSpec((1, d), lambda b, t: (0, 0))],
        out_specs=[pl.BlockSpec((tb, d), lambda b, t: (b * n_t + t, 0)),
                   pl.BlockSpec((1, 1, n_heads, HEAD, HEAD), lambda b, t: (b, 0, 0, 0, 0))],
        out_shape=[jax.ShapeDtypeStruct((batch * seq, d), BF16),
                   jax.ShapeDtypeStruct((batch, 1, n_heads, HEAD, HEAD), F32)],
        scratch_shapes=[pltpu.VMEM((n_heads, HEAD, HEAD), F32),
                        pltpu.VMEM((tb, d), BF16), pltpu.VMEM((tb, d), BF16), pltpu.VMEM((tb, d), BF16),
                        pltpu.VMEM((tb, d), F32),
                        pltpu.VMEM((n_heads, HEAD, (tb // CHUNK) * HEAD), BF16)],
        compiler_params=_params(2),
        name="hgrn_prompt",
    )(p16, p32, p16, p16, hgrn_lb, g_norm.reshape(1, d))


def _hgrn_sample_body(layer, tg, q_ref, f_ref, v_ref, gt_ref, lb_ref, gn_ref, s0_ref,
                      o_ref, s_ref, o_scr):
    lb_h = _forget_lower_bound(lb_ref, layer)
    gn_h = gn_ref[...]

    q = jax.nn.silu(q_ref[...].astype(F32))
    f = lb_h + (1.0 - lb_h) * jax.nn.sigmoid(f_ref[...])
    decay = f
    k = 1.0 - f
    v = v_ref[...].astype(F32)
    q_t, d_t, k_t = q.T, decay.T, k.T
    for j in range(tg):
        s = d_t[:, j:j + 1] * s0_ref[j, 0, 0] + k_t[:, j:j + 1] * v[j:j + 1, :]
        s_ref[j, 0, 0] = s
        o_scr[j:j + 1, :] = jnp.sum(q_t[:, j:j + 1] * s, axis=0, keepdims=True)
    o = _head_rmsnorm(o_scr[...]) * gn_h * jax.nn.silu(gt_ref[...].astype(F32))
    o_ref[...] = o.astype(BF16)


def _hgrn_sample(p16, p32, hgrn_lb, g_norm, state, layer, row0, tg):
    nb = state.shape[0]
    d = g_norm.shape[0]
    n_heads = d // HEAD
    rb0 = row0 // tg
    blk = lambda kk: pl.BlockSpec((tg, HEAD), lambda g, h, kk=kk: (rb0 + g, kk * n_heads + h))
    st_spec = pl.BlockSpec((tg, 1, 1, HEAD, HEAD), lambda g, h: (g, 0, h, 0, 0))
    return pl.pallas_call(
        functools.partial(_hgrn_sample_body, layer, tg),
        grid=(nb // tg, n_heads),
        in_specs=[blk(0), blk(0), blk(1), blk(2),
                  pl.BlockSpec((hgrn_lb.shape[0], HEAD), lambda g, h: (0, h)),
                  pl.BlockSpec((1, HEAD), lambda g, h: (0, h)),
                  st_spec],
        out_specs=[pl.BlockSpec((tg, HEAD), lambda g, h: (g, h)), st_spec],
        out_shape=[jax.ShapeDtypeStruct((nb, d), BF16),
                   jax.ShapeDtypeStruct(state.shape, F32)],
        scratch_shapes=[pltpu.VMEM((tg, HEAD), F32)],
        compiler_params=_params(2),
        name="hgrn_sample",
    )(p16, p32, p16, p16, hgrn_lb, g_norm.reshape(1, d), state)


_CARRY = 8


def _conv_prompt_body(tb, b_ref, c_ref, h_ref, w_ref, z_ref, buf_ref, u_scr):
    t = pl.program_id(1)

    @pl.when(t == 0)
    def _():
        u_scr[0:_CARRY, :] = jnp.zeros((_CARRY, u_scr.shape[1]), F32)

    u = c_ref[...].astype(F32) * h_ref[...].astype(F32)
    u_scr[_CARRY:_CARRY + tb, :] = u
    w = w_ref[...]
    y = w[0:1, :] * u_scr[_CARRY - 2:_CARRY - 2 + tb, :]
    y = y + w[1:2, :] * u_scr[_CARRY - 1:_CARRY - 1 + tb, :]
    y = y + w[2:3, :] * u
    z_ref[...] = (b_ref[...].astype(F32) * y).astype(BF16)
    u_scr[0:_CARRY, :] = u[tb - _CARRY:tb, :]

    @pl.when(t == pl.num_programs(1) - 1)
    def _():
        buf_ref[0] = u[tb - (CONV_W - 1):tb, :]


def _conv_prompt(pc, conv_w, batch, seq, tb):
    d = conv_w.shape[1]
    n_t = seq // tb
    blk = lambda kk: pl.BlockSpec((tb, d), lambda b, t, kk=kk: (b * n_t + t, kk))
    return pl.pallas_call(
        functools.partial(_conv_prompt_body, tb),
        grid=(batch, n_t),
        in_specs=[blk(0), blk(1), blk(2), pl.BlockSpec(conv_w.shape, lambda b, t: (0, 0))],
        out_specs=[pl.BlockSpec((tb, d), lambda b, t: (b * n_t + t, 0)),
                   pl.BlockSpec((1, CONV_W - 1, d), lambda b, t: (b, 0, 0))],
        out_shape=[jax.ShapeDtypeStruct((batch * seq, d), BF16),
                   jax.ShapeDtypeStruct((batch, CONV_W - 1, d), F32)],
        scratch_shapes=[pltpu.VMEM((_CARRY + tb, d), F32)],
        compiler_params=_params(2),
        name="conv_prompt",
    )(pc, pc, pc, conv_w)


def _conv_sample_body(d, b_ref, c_ref, h_ref, w_ref, st_ref, z_ref, buf_ref):
    u = c_ref[...].astype(F32) * h_ref[...].astype(F32)
    w = w_ref[...]
    buf0 = st_ref[:, 0:d]
    buf1 = st_ref[:, d:2 * d]
    y = w[0:1, :] * buf0
    y = y + w[1:2, :] * buf1
    y = y + w[2:3, :] * u
    z_ref[...] = (b_ref[...].astype(F32) * y).astype(BF16)
    buf_ref[:, 0:d] = buf1
    buf_ref[:, d:2 * d] = u


def _conv_sample(pc, conv_w, state2d, row0):
    nb = state2d.shape[0]
    d = conv_w.shape[1]
    rb0 = row0 // nb
    blk = lambda kk: pl.BlockSpec((nb, d), lambda i, kk=kk: (rb0, kk))
    return pl.pallas_call(
        functools.partial(_conv_sample_body, d),
        grid=(1,),
        in_specs=[blk(0), blk(1), blk(2), pl.BlockSpec(conv_w.shape, lambda i: (0, 0)),
                  pl.BlockSpec(state2d.shape, lambda i: (0, 0))],
        out_specs=[pl.BlockSpec((nb, d), lambda i: (0, 0)),
                   pl.BlockSpec(state2d.shape, lambda i: (0, 0))],
        out_shape=[jax.ShapeDtypeStruct((nb, d), BF16),
                   jax.ShapeDtypeStruct(state2d.shape, F32)],
        compiler_params=_params(1),
        name="conv_sample",
    )(pc, pc, pc, conv_w, state2d)


def _outproj_body(n_blocks, x_ref, z_ref, w_ref, o_ref, w16_ref):
    i = pl.program_id(0)

    @pl.when(i == 0)
    def _():
        w16_ref[...] = w_ref[...].astype(BF16)

    @pl.when(i < n_blocks)
    def _():
        o_ref[...] = x_ref[...] + jnp.dot(z_ref[...], w16_ref[...], preferred_element_type=F32)

    @pl.when(i >= n_blocks)
    def _():
        o_ref[...] = jnp.zeros_like(o_ref)


def _outproj_route_body(n_blocks, n_experts, x_ref, z_ref, w_ref, g_ref, rw_ref,
                        o_ref, route_ref, cnt_ref, w16_ref, cnt_scr):
    i = pl.program_id(0)

    @pl.when(i == 0)
    def _():
        w16_ref[...] = w_ref[...].astype(BF16)
        cnt_scr[...] = jnp.zeros_like(cnt_scr)

    @pl.when(i < n_blocks)
    def _():
        res = x_ref[...] + jnp.dot(z_ref[...], w16_ref[...], preferred_element_type=F32)
        o_ref[...] = res
        route_ref[...], cnt_scr[...] = _route_rows(res, g_ref[...], rw_ref[...], cnt_scr[...], n_experts)

    @pl.when(i >= n_blocks)
    def _():
        o_ref[...] = jnp.zeros_like(o_ref)
        route_ref[...] = jnp.zeros_like(route_ref)

    cnt_ref[...] = cnt_scr[...]


def _outproj_prompt(x, z, w, n_rows, tm, router=None):
    d = x.shape[1]
    n_blocks = n_rows // tm
    row_blk = lambda i: (jnp.minimum(i, n_blocks - 1), 0)
    in_specs = [pl.BlockSpec((tm, d), row_blk),
                pl.BlockSpec((tm, d), row_blk),
                pl.BlockSpec((d, d), lambda i: (0, 0))]
    out_rows = (n_blocks + 1) * tm
    if router is None:
        return pl.pallas_call(
            functools.partial(_outproj_body, n_blocks),
            grid=(n_blocks + 1,),
            in_specs=in_specs,
            out_specs=pl.BlockSpec((tm, d), lambda i: (i, 0)),
            out_shape=jax.ShapeDtypeStruct((out_rows, d), F32),
            scratch_shapes=[pltpu.VMEM((d, d), BF16)],
            compiler_params=_params(1),
            name="outproj",
        )(x, z, w)
    g, rw_pad, n_experts = router
    return pl.pallas_call(
        functools.partial(_outproj_route_body, n_blocks, n_experts),
        grid=(n_blocks + 1,),
        in_specs=in_specs + [pl.BlockSpec((1, d), lambda i: (0, 0)),
                             pl.BlockSpec((d, LANES), lambda i: (0, 0))],
        out_specs=[pl.BlockSpec((tm, d), lambda i: (i, 0)),
                   pl.BlockSpec((ROUTE_ROWS, tm), lambda i: (0, i)),
                   pl.BlockSpec((1, LANES), lambda i: (0, 0))],
        out_shape=[jax.ShapeDtypeStruct((out_rows, d), F32),
                   jax.ShapeDtypeStruct((ROUTE_ROWS, out_rows), F32),
                   jax.ShapeDtypeStruct((1, LANES), F32)],
        scratch_shapes=[pltpu.VMEM((d, d), BF16), pltpu.VMEM((1, LANES), F32)],
        compiler_params=_params(1),
        name="outproj_route",
    )(x, z, w, g.reshape(1, d), rw_pad)


def _outproj_sample_body(x_ref, z_ref, w_ref, buf_ref, o_ref):
    del buf_ref
    o_ref[...] = x_ref[...] + jnp.dot(z_ref[...], w_ref[...].astype(BF16), preferred_element_type=F32)


def _outproj_sample_route_body(n_experts, x_ref, z_ref, w_ref, g_ref, rw_ref, cnt_in_ref, buf_ref, route_in_ref,
                               o_ref, route_ref, cnt_ref):
    del buf_ref, route_in_ref
    res = x_ref[...] + jnp.dot(z_ref[...], w_ref[...].astype(BF16), preferred_element_type=F32)
    o_ref[...] = res
    route_ref[...], cnt_ref[...] = _route_rows(res, g_ref[...], rw_ref[...], cnt_in_ref[...], n_experts)


def _outproj_sample(x, x_row0, z, w, buf, row0, router=None):
    nb, d = z.shape
    in_specs = [pl.BlockSpec((nb, d), lambda i: (x_row0 // nb, 0)),
                pl.BlockSpec((nb, d), lambda i: (0, 0)),
                pl.BlockSpec((d, d), lambda i: (0, 0))]
    if router is None:
        return pl.pallas_call(
            _outproj_sample_body,
            grid=(1,),
            in_specs=in_specs + [pl.BlockSpec(memory_space=pl.ANY)],
            out_specs=pl.BlockSpec((nb, d), lambda i: (row0 // nb, 0)),
            out_shape=jax.ShapeDtypeStruct(buf.shape, buf.dtype),
            input_output_aliases={3: 0},
            compiler_params=_params(1),
            name="outproj_sample",
        )(x, z, w, buf)
    g, rw_pad, n_experts, route, counts = router
    return pl.pallas_call(
        functools.partial(_outproj_sample_route_body, n_experts),
        grid=(1,),
        in_specs=in_specs + [pl.BlockSpec((1, d), lambda i: (0, 0)),
                             pl.BlockSpec((d, LANES), lambda i: (0, 0)),
                             pl.BlockSpec((1, LANES), lambda i: (0, 0)),
                             pl.BlockSpec(memory_space=pl.ANY),
                             pl.BlockSpec(memory_space=pl.ANY)],
        out_specs=[pl.BlockSpec((nb, d), lambda i: (row0 // nb, 0)),
                   pl.BlockSpec((ROUTE_ROWS, nb), lambda i: (0, row0 // nb)),
                   pl.BlockSpec((1, LANES), lambda i: (0, 0))],
        out_shape=[jax.ShapeDtypeStruct(buf.shape, buf.dtype),
                   jax.ShapeDtypeStruct(route.shape, route.dtype),
                   jax.ShapeDtypeStruct((1, LANES), F32)],
        input_output_aliases={6: 0, 7: 1},
        compiler_params=_params(1),
        name="outproj_sample_route",
    )(x, z, w, g.reshape(1, d), rw_pad, counts, buf, route)


def _swiglu_partial(xn, w1, w3, w2):
    h1 = jnp.dot(xn, w1.astype(BF16), preferred_element_type=F32)
    h3 = jnp.dot(xn, w3.astype(BF16), preferred_element_type=F32)
    h = (jax.nn.silu(h1) * h3).astype(BF16)
    return jnp.dot(h, w2.astype(BF16), preferred_element_type=F32)


def _cache_weight_chunk(w1_hbm, w3_hbm, w2_hbm, f, n_f, tf, stages, caches, wsem):
    n_slots = stages[0].shape[0]

    def copies(ff, ws):
        cols = pl.ds(pl.multiple_of(ff * tf, tf), tf)
        srcs = (w1_hbm.at[:, cols], w3_hbm.at[:, cols], w2_hbm.at[cols, :])
        return [pltpu.make_async_copy(src, st.at[ws], wsem.at[ws, m]) for m, (src, st) in enumerate(zip(srcs, stages))]

    if n_slots == 1:
        ws = 0
        for c in copies(f, 0):
            c.start()
    else:
        ws = lax.rem(f, 2)

        @pl.when(f == 0)
        def _():
            for c in copies(0, 0):
                c.start()

        @pl.when(f + 1 < n_f)
        def _():
            for c in copies(f + 1, 1 - ws):
                c.start()

    for c in copies(f, ws):
        c.wait()
    for st, ca in zip(stages, caches):
        ca[f] = st[ws].astype(BF16)


def _ffn_body(n_f, tf, x_ref, g_ref, w1_hbm, w3_hbm, w2_hbm, o_ref, xn_ref, acc_ref, c1, c3, c2, s1, s3, s2, wsem):
    i = pl.program_id(0)
    xn_ref[...] = _rmsnorm_bf16(x_ref[...], g_ref[...])
    acc_ref[...] = jnp.zeros_like(acc_ref)

    def chunk(f, carry):
        @pl.when(i == 0)
        def _():
            _cache_weight_chunk(w1_hbm, w3_hbm, w2_hbm, f, n_f, tf, (s1, s3, s2), (c1, c3, c2), wsem)

        acc_ref[...] += _swiglu_partial(xn_ref[...], c1[f], c3[f], c2[f])
        return carry

    lax.fori_loop(0, n_f, chunk, 0)
    o_ref[...] = x_ref[...] + acc_ref[...]


def _ffn(x, n, g, w1, w3, w2, tm, tf):
    d = x.shape[1]
    dff = w1.shape[1]
    n_f = dff // tf
    return pl.pallas_call(
        functools.partial(_ffn_body, n_f, tf),
        grid=(n // tm,),
        in_specs=[pl.BlockSpec((tm, d), lambda i: (i, 0)),
                  pl.BlockSpec((1, d), lambda i: (0, 0)),
                  pl.BlockSpec(memory_space=pl.ANY),
                  pl.BlockSpec(memory_space=pl.ANY),
                  pl.BlockSpec(memory_space=pl.ANY)],
        out_specs=pl.BlockSpec((tm, d), lambda i: (i, 0)),
        out_shape=jax.ShapeDtypeStruct((n, d), F32),
        scratch_shapes=[pltpu.VMEM((tm, d), BF16), pltpu.VMEM((tm, d), F32),
                        pltpu.VMEM((n_f, d, tf), BF16), pltpu.VMEM((n_f, d, tf), BF16),
                        pltpu.VMEM((n_f, tf, d), BF16),
                        pltpu.VMEM((1, d, tf), F32), pltpu.VMEM((1, d, tf), F32), pltpu.VMEM((1, tf, d), F32),
                        pltpu.SemaphoreType.DMA((1, 3))],
        compiler_params=_params(1),
        name="ffn",
    )(x, g.reshape(1, d), w1, w3, w2)


ROUTE_ROWS = 8


def _route_rows(x, g, rw, counts, n_experts):
    rows = x.shape[0]
    xn = _rmsnorm_bf16(x, g)
    logits = jnp.dot(xn, rw.astype(BF16), preferred_element_type=F32)
    lane = lax.broadcasted_iota(I32, (rows, LANES), 1).astype(F32)
    neg = jnp.float32(-jnp.inf)
    logits = jnp.where(lane < n_experts, logits, neg)
    m1 = jnp.max(logits, axis=-1, keepdims=True)
    i1 = jnp.min(jnp.where(logits == m1, lane, float(LANES)), axis=-1, keepdims=True)
    i1 = jnp.minimum(i1, float(n_experts - 1))
    rest = jnp.where(lane == i1, neg, logits)
    m2 = jnp.max(rest, axis=-1, keepdims=True)
    i2 = jnp.min(jnp.where(rest == m2, lane, float(LANES)), axis=-1, keepdims=True)
    i2 = jnp.minimum(i2, float(n_experts - 1))
    e2 = jnp.exp(m2 - m1)
    den = 1.0 + e2
    g1 = 1.0 / den
    g2 = e2 / den

    sel1 = lane == i1
    sel2 = lane == i2
    onehot = jnp.logical_or(sel1, sel2)
    row = lax.broadcasted_iota(I32, (rows, rows), 0)
    col = lax.broadcasted_iota(I32, (rows, rows), 1)
    before = (row > col).astype(BF16)
    rank = jnp.dot(before, onehot.astype(BF16), preferred_element_type=F32) + counts
    r1 = jnp.sum(jnp.where(sel1, rank, 0.0), axis=-1, keepdims=True)
    r2 = jnp.sum(jnp.where(sel2, rank, 0.0), axis=-1, keepdims=True)

    out = jnp.zeros((rows, LANES), F32)
    for slot, val in enumerate((i1, i2, g1, g2, r1, r2)):
        out = jnp.where(lane == slot, val, out)
    return out.T[0:ROUTE_ROWS, :], counts + jnp.sum(onehot.astype(F32), axis=0, keepdims=True)


SUBLANES = 8
MOE_TM = 672


def _start_row_group(src_hbm, idx_ref, g, dst, sem):
    for k in range(SUBLANES):
        pltpu.make_async_copy(src_hbm.at[pl.ds(idx_ref[0, 0, g * SUBLANES + k], 1)],
                              dst.at[g, pl.ds(k, 1)], sem).start(priority=k % 2)


def _start_row_gather(src_hbm, idx_ref, n_rows, dst, sem, unrolled=False):
    if unrolled:
        for g in range(n_rows // SUBLANES):
            _start_row_group(src_hbm, idx_ref, g, dst, sem)
        return

    def body(g, carry):
        _start_row_group(src_hbm, idx_ref, g, dst, sem)
        return carry

    lax.fori_loop(0, n_rows // SUBLANES, body, 0)


def _wait_row_gather(src_hbm, n_rows, dst, sem):
    pltpu.make_async_copy(src_hbm.at[pl.ds(0, n_rows)], dst.reshape(n_rows, dst.shape[-1]), sem).wait()


def _moe_body(n_f, tf, te_ref, nu_ref, hf_ref, idx_cur, idx_nxt, x_hbm, gate_ref, g_ref, w1_hbm, w3_hbm, w2_hbm,
              y_ref, xbuf, sem, xn_ref, acc_ref, c1, c3, c2, s1, s3, s2, wsem):
    i = pl.program_id(0)
    n_tiles = pl.num_programs(0)
    tm = xbuf.shape[1] * SUBLANES
    groups_per_chunk = tm // n_f // SUBLANES
    slot = lax.rem(i, 2)
    n_used = nu_ref[0]
    used = i < n_used
    e = te_ref[i]
    first_of_expert = jnp.logical_or(i == 0, e != te_ref[jnp.maximum(i - 1, 0)])
    half_filled = hf_ref[i] == 1

    @pl.when(i == 0)
    def _():
        _start_row_gather(x_hbm, idx_cur, tm, xbuf.at[0], sem.at[0])

    @pl.when(i <= n_used)
    def _():
        _wait_row_gather(x_hbm, tm, xbuf.at[slot], sem.at[slot])

    @pl.when(used)
    def _():
        xn_ref[...] = _rmsnorm_bf16(xbuf[slot].reshape(tm, xbuf.shape[-1]), g_ref[...])
        acc_ref[...] = jnp.zeros_like(acc_ref)

        def chunk(f, carry):
            @pl.when(first_of_expert)
            def _():
                _cache_weight_chunk(w1_hbm.at[e], w3_hbm.at[e], w2_hbm.at[e], f, n_f, tf,
                                    (s1, s3, s2), (c1, c3, c2), wsem)

            def compute(rows):
                for g in range(groups_per_chunk):
                    _start_row_group(x_hbm, idx_nxt, f * groups_per_chunk + g, xbuf.at[1 - slot], sem.at[1 - slot])
                acc_ref[0:rows, :] += _swiglu_partial(xn_ref[0:rows, :], c1[f], c3[f], c2[f])

            @pl.when(half_filled)
            def _():
                compute(tm // 2)

            @pl.when(jnp.logical_not(half_filled))
            def _():
                compute(tm)

            return carry

        lax.fori_loop(0, n_f, chunk, 0)
        r_id = lax.broadcasted_iota(I32, (tm, tm), 0)
        c_id = lax.broadcasted_iota(I32, (tm, tm), 1)
        gate_col = jnp.sum(jnp.where(r_id == c_id, gate_ref[0], 0.0), axis=1, keepdims=True)
        y_ref[...] = acc_ref[...] * gate_col

    @pl.when(jnp.logical_not(used))
    def _():
        y_ref[...] = jnp.zeros_like(y_ref)

    @pl.when(jnp.logical_and(used, i == n_tiles - 1))
    def _():
        _wait_row_gather(x_hbm, tm, xbuf.at[1 - slot], sem.at[1 - slot])


def _moe(x, g, w1, w3, w2, tile_expert, n_used, half_filled, inv3, gate_sorted, tm, tf):
    n_tiles = inv3.shape[0]
    d = x.shape[1]
    dff = w1.shape[2]
    n_f = dff // tf
    assert tm % (n_f * SUBLANES) == 0, "each hidden chunk fetches whole 8-row tiles of the next row tile"
    assert tm % 32 == 0, "half a row tile must be whole bf16 sublane tiles"
    grid_spec = pltpu.PrefetchScalarGridSpec(
        num_scalar_prefetch=3,
        grid=(n_tiles,),
        in_specs=[pl.BlockSpec((1, 1, tm), lambda i, te, nu, hf: (i, 0, 0), memory_space=pltpu.SMEM),
                  pl.BlockSpec((1, 1, tm), lambda i, te, nu, hf: (jnp.minimum(i + 1, n_tiles - 1), 0, 0),
                               memory_space=pltpu.SMEM),
                  pl.BlockSpec(memory_space=pl.ANY),
                  pl.BlockSpec((1, 1, tm), lambda i, te, nu, hf: (i, 0, 0)),
                  pl.BlockSpec((1, d), lambda i, te, nu, hf: (0, 0)),
                  pl.BlockSpec(memory_space=pl.ANY),
                  pl.BlockSpec(memory_space=pl.ANY),
                  pl.BlockSpec(memory_space=pl.ANY)],
        out_specs=pl.BlockSpec((tm, d), lambda i, te, nu, hf: (i, 0)),
        scratch_shapes=[pltpu.VMEM((2, tm // SUBLANES, SUBLANES, d), F32),
                        pltpu.SemaphoreType.DMA((2,)),
                        pltpu.VMEM((tm, d), BF16),
                        pltpu.VMEM((tm, d), F32),
                        pltpu.VMEM((n_f, d, tf), BF16),
                        pltpu.VMEM((n_f, d, tf), BF16),
                        pltpu.VMEM((n_f, tf, d), BF16),
                        pltpu.VMEM((2, d, tf), F32),
                        pltpu.VMEM((2, d, tf), F32),
                        pltpu.VMEM((2, tf, d), F32),
                        pltpu.SemaphoreType.DMA((2, 3))],
    )
    return pl.pallas_call(
        functools.partial(_moe_body, n_f, tf),
        grid_spec=grid_spec,
        out_shape=jax.ShapeDtypeStruct((n_tiles * tm, d), F32),
        compiler_params=_params(1),
        name="moe",
    )(tile_expert, n_used, half_filled, inv3, inv3, x, gate_sorted, g.reshape(1, d), w1, w3, w2)


def _combine_body(pos_cur, pos_nxt, x_ref, g_ref, y_hbm, o_ref, ybuf, sem):
    i = pl.program_id(0)
    n_tiles = pl.num_programs(0)
    rows = ybuf.shape[1] * SUBLANES
    slot = lax.rem(i, 2)

    @pl.when(i == 0)
    def _():
        _start_row_gather(y_hbm, pos_cur, rows, ybuf.at[0], sem.at[0])

    @pl.when(i + 1 < n_tiles)
    def _():
        _start_row_gather(y_hbm, pos_nxt, rows, ybuf.at[1 - slot], sem.at[1 - slot], unrolled=True)

    _wait_row_gather(y_hbm, rows, ybuf.at[slot], sem.at[slot])
    tc = rows // TOP_K
    y = ybuf[slot].reshape(rows, ybuf.shape[-1])
    moe = y[0:tc, :] + y[tc:rows, :]
    x = x_ref[...] + moe
    ms = jnp.mean(x * x, axis=-1, keepdims=True)
    o_ref[...] = x * lax.rsqrt(ms + EPS) * g_ref[...]


def _combine(x, g, y_sorted, pos3, row0, n_rows, tc):
    d = x.shape[1]
    n_tiles = n_rows // tc
    rb0 = row0 // tc
    grid_spec = pltpu.PrefetchScalarGridSpec(
        num_scalar_prefetch=0,
        grid=(n_tiles,),
        in_specs=[pl.BlockSpec((1, 1, TOP_K * tc), lambda i: (i, 0, 0), memory_space=pltpu.SMEM),
                  pl.BlockSpec((1, 1, TOP_K * tc), lambda i: (jnp.minimum(i + 1, n_tiles - 1), 0, 0),
                               memory_space=pltpu.SMEM),
                  pl.BlockSpec((tc, d), lambda i: (rb0 + i, 0)),
                  pl.BlockSpec((1, d), lambda i: (0, 0)),
                  pl.BlockSpec(memory_space=pl.ANY)],
        out_specs=pl.BlockSpec((tc, d), lambda i: (i, 0)),
        scratch_shapes=[pltpu.VMEM((2, TOP_K * tc // SUBLANES, SUBLANES, d), F32), pltpu.SemaphoreType.DMA((2,))],
    )
    return pl.pallas_call(
        _combine_body,
        grid_spec=grid_spec,
        out_shape=jax.ShapeDtypeStruct((n_rows, d), F32),
        compiler_params=_params(1),
        name="combine",
    )(pos3, pos3, x, g.reshape(1, d), y_sorted)


def _lookup(table, idx):
    out = jnp.zeros(idx.shape, table.dtype)
    for e in range(table.shape[0]):
        out = jnp.where(idx == e, table[e], out)
    return out


def _expert_layer(x, route, cnt, n_experts, g_ffn, g_final, w1, w3, w2, n_prompt, n_dec, tm_moe, tf, tc):
    n = n_prompt + n_dec
    ids = route[0:TOP_K, :n].astype(I32)
    gates = route[TOP_K:2 * TOP_K, :n]
    ranks = route[2 * TOP_K:3 * TOP_K, :n].astype(I32)
    counts = cnt[0, :n_experts].astype(I32)

    n_tiles = (TOP_K * n + n_experts * (tm_moe - 1)) // tm_moe
    padded = ((counts + tm_moe - 1) // tm_moe) * tm_moe
    ends = jnp.cumsum(padded)
    starts = ends - padded
    cstart = jnp.cumsum(counts) - counts
    pos = _lookup(starts, ids) + ranks
    keys = ids * n + jnp.arange(n, dtype=I32)[None, :]
    order = jnp.argsort(keys.reshape(-1)).astype(I32)
    slot_pos = jnp.arange(n_tiles * tm_moe, dtype=I32)
    slot_e = jnp.minimum(jnp.sum(slot_pos[None, :] >= ends[:, None], axis=0), n_experts - 1).astype(I32)
    slot_rank = slot_pos - _lookup(starts, slot_e)
    slot_valid = slot_rank < _lookup(counts, slot_e)
    slot_asg = order[jnp.clip(_lookup(cstart, slot_e) + slot_rank, 0, TOP_K * n - 1)]
    inv = jnp.where(slot_valid, slot_asg % n, 0).astype(I32)
    gate_sorted = jnp.where(slot_valid, gates.reshape(-1)[slot_asg], 0.0)
    tile_expert = slot_e[::tm_moe]
    n_used = (ends[-1] // tm_moe).astype(I32).reshape(1)

    tile_start = jnp.arange(n_tiles, dtype=I32) * tm_moe
    tile_rows = _lookup(counts, tile_expert) - (tile_start - _lookup(starts, tile_expert))
    half_filled = (tile_rows <= tm_moe // 2).astype(I32)

    y_sorted = _moe(x, g_ffn, w1, w3, w2, tile_expert, n_used, half_filled, inv.reshape(n_tiles, 1, tm_moe),
                    gate_sorted.reshape(n_tiles, 1, tm_moe), tm_moe, tf)

    def tile_positions(row0, n_rows, t):
        p = pos[:, row0:row0 + n_rows].reshape(TOP_K, n_rows // t, t)
        return jnp.swapaxes(p, 0, 1).reshape(n_rows // t, 1, TOP_K * t)

    y_prompt = _combine(x, g_final, y_sorted, tile_positions(0, n_prompt, tc), 0, n_prompt, tc)
    y_sample = _combine(x, g_final, y_sorted, tile_positions(n_prompt, n_dec, n_dec), n_prompt, n_dec, n_dec)
    return y_prompt, y_sample


def kernel(x_prompt, x_sample, state_hgrn, state_conv, norm_mix, norm_ffn, norm_final, hgrn_w_in, hgrn_lb, hgrn_g_norm, hgrn_w_out, conv_w_in, conv_w, conv_w_out, ffn_w1, ffn_w3, ffn_w2, moe_router, moe_w1, moe_w3, moe_w2):
    batch, seq, d = x_prompt.shape
    n_dec = x_sample.shape[0]
    n_prompt = batch * seq
    n = n_prompt + n_dec
    n_experts = moe_router.shape[-1]
    assert x_sample.shape[1] == 1 and d % HEAD == 0 and seq % CHUNK == 0
    assert norm_mix.shape[0] == 2, "one HGRN2 layer followed by one short-conv layer"

    tm = _pick_tile(n, (688, 384, 128, 16))
    tm_proj = _pick_tile(n, (1376, 688, 384, 128, 16))
    tm_prompt = _pick_tile(n_prompt, (1024, 512, 256, 128, 16))
    tb = _pick_tile(seq, (256, 128, 64, 32))
    tb_conv = _pick_tile(seq, (512, 256, 128, 64, 32))
    tg = _pick_tile(n_dec, (16,))
    tf = _pick_tile(ffn_w1.shape[-1], (512, 256, 128))
    tm_moe = _pick_tile(n_prompt, (512, 128, 16))
    tm_moe = MOE_TM if n_prompt % 128 == 0 and ffn_w1.shape[-1] % (7 * 512) == 0 else tm_moe
    tc = _pick_tile(n_prompt, (512, 128, 16))
    assert n_prompt % tg == 0 and n_prompt % n_dec == 0 and n_prompt % tc == 0

    xp = x_prompt.reshape(n_prompt, d)
    xs = x_sample.reshape(n_dec, d)

    p16, p32 = _proj(xp, norm_mix[0], hgrn_w_in[0], tm_prompt, f32_col=1)
    p16_s, p32_s = _proj(xs, norm_mix[0], hgrn_w_in[0], n_dec, f32_col=1)
    o_p, s_prompt = _hgrn_prompt(p16, p32, hgrn_lb, hgrn_g_norm[0], 0, batch, seq, tb)
    o_s, s_sample = _hgrn_sample(p16_s, p32_s, hgrn_lb, hgrn_g_norm[0], state_hgrn, 0, 0, tg)
    x = _outproj_prompt(xp, o_p, hgrn_w_out[0], n_prompt, tm_prompt)
    x = _outproj_sample(xs, 0, o_s, hgrn_w_out[0], x, n_prompt)
    x = _ffn(x, n, norm_ffn[0], ffn_w1[0], ffn_w3[0], ffn_w2[0], tm, tf)

    (pc,) = _proj(x, norm_mix[1], conv_w_in[0], tm_proj)
    z_p, c_prompt = _conv_prompt(pc, conv_w[0], batch, seq, tb_conv)
    z_s, c_sample = _conv_sample(pc, conv_w[0], state_conv.reshape(n_dec, (CONV_W - 1) * d), n_prompt)
    rw_pad = jnp.pad(moe_router[0], ((0, 0), (0, LANES - n_experts)))
    x1, route, cnt = _outproj_prompt(x, z_p, conv_w_out[0], n_prompt, tm_prompt, router=(norm_ffn[1], rw_pad, n_experts))
    x1, route, cnt = _outproj_sample(x, n_prompt, z_s, conv_w_out[0], x1, n_prompt,
                                     router=(norm_ffn[1], rw_pad, n_experts, route, cnt))

    y_prompt, y_sample = _expert_layer(x1, route, cnt, n_experts, norm_ffn[1], norm_final, moe_w1[0], moe_w3[0],
                                       moe_w2[0], n_prompt, n_dec, tm_moe, tf, tc)

    return (y_prompt.reshape(batch, seq, d),
            y_sample.reshape(n_dec, 1, d),
            s_prompt,
            s_sample,
            c_prompt.reshape(batch, 1, CONV_W - 1, d),
            c_sample.reshape(n_dec, 1, CONV_W - 1, d))
```

```python
import functools

import jax
import jax.numpy as jnp
from jax import lax
from jax.experimental import pallas as pl
from jax.experimental.pallas import tpu as pltpu

F32 = jnp.float32
BF16 = jnp.bfloat16
I32 = jnp.int32

EPS = 1e-6
HEAD = 128
CHUNK = 32
CONV_W = 3
TOP_K = 2
LANES = 128
V7X_VMEM_LIMIT = 56 * 1024 * 1024

ARB = "arbitrary"


def _params(n_axes):
    return pltpu.CompilerParams(dimension_semantics=(ARB,) * n_axes, vmem_limit_bytes=V7X_VMEM_LIMIT)


def _pick_tile(n, candidates):
    for c in candidates:
        if n % c == 0:
            return c
    raise ValueError(f"no tile in {candidates} divides {n}")


def _rmsnorm_bf16(x, g):
    ms = jnp.mean(x * x, axis=-1, keepdims=True)
    return (x * lax.rsqrt(ms + EPS) * g).astype(BF16)


def _proj_body(n_col, f32_col, x_ref, g_ref, w_hbm, *refs):
    o16_ref = refs[0]
    w16_ref, stage_ref, wsem = refs[-3:]
    d = x_ref.shape[1]

    @pl.when(pl.program_id(0) == 0)
    def _():
        for j in range(n_col):
            cp = pltpu.make_async_copy(w_hbm.at[:, j * d:(j + 1) * d], stage_ref, wsem)
            cp.start()
            cp.wait()
            w16_ref[j] = stage_ref[...].astype(BF16)

    xn = _rmsnorm_bf16(x_ref[...], g_ref[...])
    j16 = 0
    for j in range(n_col):
        res = jnp.dot(xn, w16_ref[j], preferred_element_type=F32)
        if j == f32_col:
            refs[1][...] = res
        else:
            o16_ref[:, j16 * d:(j16 + 1) * d] = res.astype(BF16)
            j16 += 1


def _proj(x, g, w, tm, f32_col=None):
    n, d = x.shape
    n_col = w.shape[1] // d
    n16 = n_col - (f32_col is not None)
    out_specs = [pl.BlockSpec((tm, n16 * d), lambda i: (i, 0))]
    out_shape = [jax.ShapeDtypeStruct((n, n16 * d), BF16)]
    if f32_col is not None:
        out_specs.append(pl.BlockSpec((tm, d), lambda i: (i, 0)))
        out_shape.append(jax.ShapeDtypeStruct((n, d), F32))
    return pl.pallas_call(
        functools.partial(_proj_body, n_col, f32_col),
        grid=(n // tm,),
        in_specs=[pl.BlockSpec((tm, d), lambda i: (i, 0)),
                  pl.BlockSpec((1, d), lambda i: (0, 0)),
                  pl.BlockSpec(memory_space=pl.ANY)],
        out_specs=out_specs,
        out_shape=out_shape,
        scratch_shapes=[pltpu.VMEM((n_col, d, d), BF16), pltpu.VMEM((d, d), F32), pltpu.SemaphoreType.DMA(())],
        compiler_params=_params(1),
        name="proj",
    )(x, g.reshape(1, d), w)


def _forget_lower_bound(lb_ref, layer):
    lbw = lb_ref[...]
    e = jnp.exp(lbw - jnp.max(lbw, axis=0, keepdims=True))
    sm = e / jnp.sum(e, axis=0, keepdims=True)
    return jnp.sum(sm[:layer + 1], axis=0, keepdims=True)


def _head_rmsnorm(o):
    return o * lax.rsqrt(jnp.mean(o * o, axis=-1, keepdims=True) + EPS)


def _sigmoid(x):
    return 0.5 * (jnp.tanh(0.5 * x) + 1.0)


def _silu(x):
    h = 0.5 * x
    return h + h * jnp.tanh(h)


def _split3_bf16(x):
    hi = x.astype(BF16)
    r1 = x - hi.astype(F32)
    mid = r1.astype(BF16)
    lo = (r1 - mid.astype(F32)).astype(BF16)
    return hi, mid, lo


def _hgrn_prompt_body(layer, n_heads, tb, q_ref, f_ref, v_ref, gt_ref, lb_ref, gn_ref,
                      o_ref, s_ref, st_scr, qin_scr, kin_scr, kdec_scr, g_scr, sprev_scr):
    t = pl.program_id(1)
    nc = tb // CHUNK
    d = n_heads * HEAD

    @pl.when(t == 0)
    def _():
        st_scr[...] = jnp.zeros_like(st_scr)

    lb = _forget_lower_bound(lb_ref, layer)
    gn = gn_ref[...]
    row = lax.broadcasted_iota(I32, (tb, tb), 0)
    col = lax.broadcasted_iota(I32, (tb, tb), 1)
    same_chunk = (row // CHUNK) == (col // CHUNK)
    causal = jnp.logical_and(same_chunk, row >= col)

    f = lb + (1.0 - lb) * _sigmoid(f_ref[...])
    logf = jnp.log2(f)
    k = 1.0 - f
    tri = causal.astype(BF16)
    G = None
    for part in _split3_bf16(logf):
        term = jnp.dot(tri, part, preferred_element_type=F32)
        G = term if G is None else G + term
    g_scr[...] = G
    g_last = [g_scr[(c + 1) * CHUNK - 1:(c + 1) * CHUNK, :] for c in range(nc)]
    decay = [jnp.exp2(g) for g in g_last]
    decay_rows = jnp.concatenate([jnp.broadcast_to(dc, (CHUNK, d)) for dc in decay], axis=0)
    k_in = k * jnp.exp2(-G)
    qin_scr[...] = (_silu(q_ref[...].astype(F32)) * jnp.exp2(G)).astype(BF16)
    kin_scr[...] = k_in.astype(BF16)
    kdec_scr[...] = (k_in * decay_rows).astype(BF16)

    row_chunk = lax.broadcasted_iota(I32, (tb, HEAD), 0) // CHUNK
    zero16 = jnp.zeros((), BF16)
    grp = next(g for g in (4, 2, 1) if nc % g == 0)

    for h in range(n_heads):
        hs = slice(h * HEAD, (h + 1) * HEAD)
        qh = qin_scr[:, hs]
        kd = kdec_scr[:, hs]
        vh = v_ref[:, hs]
        a = lax.dot_general(qh, kin_scr[:, hs], (((1,), (1,)), ((), ())), preferred_element_type=F32)
        a = jnp.where(causal, a, 0.0).astype(BF16)
        o = jnp.dot(a, vh, preferred_element_type=F32)
        ut = jnp.concatenate(
            [lax.dot_general(vh[c * CHUNK:(c + 1) * CHUNK], kd[c * CHUNK:(c + 1) * CHUNK],
                             (((0,), (0,)), ((), ())), preferred_element_type=F32) for c in range(nc)], axis=0)
        st = st_scr[h]
        for c in range(nc):
            sprev_scr[h, :, c * HEAD:(c + 1) * HEAD] = st.astype(BF16)
            st = st * decay[c][:, hs] + ut[c * HEAD:(c + 1) * HEAD, :]
        st_scr[h] = st
        inter = []
        for g0 in range(0, nc, grp):
            rows = slice(g0 * CHUNK, (g0 + grp) * CHUNK)
            q_blocks = jnp.concatenate(
                [jnp.where(row_chunk[rows] == g0 + j, qh[rows], zero16) for j in range(grp)], axis=1)
            inter.append(lax.dot_general(q_blocks, sprev_scr[h, :, g0 * HEAD:(g0 + grp) * HEAD],
                                         (((1,), (1,)), ((), ())), preferred_element_type=F32))
        o = o + jnp.concatenate(inter, axis=0)
        o = _head_rmsnorm(o) * gn[:, hs] * _silu(gt_ref[:, hs].astype(F32))
        o_ref[:, hs] = o.astype(BF16)

    @pl.when(t == pl.num_programs(1) - 1)
    def _():
        for h in range(n_heads):
            s_ref[0, 0, h] = st_scr[h].T


def _hgrn_prompt(p16, p32, hgrn_lb, g_norm, layer, batch, seq, tb):
    d = g_norm.shape[0]
    n_heads = d // HEAD
    n_t = seq // tb
    blk = lambda kk: pl.BlockSpec((tb, d), lambda b, t, kk=kk: (b * n_t + t, kk))
    return pl.pallas_call(
        functools.partial(_hgrn_prompt_body, layer, n_heads, tb),
        grid=(batch, n_t),
        in_specs=[blk(0), blk(0), blk(1), blk(2),
                  pl.BlockSpec(hgrn_lb.shape, lambda b, t: (0, 0)),
                  pl.BlockSpec((1, d), lambda b, t: (0, 0))],
        out_specs=[pl.BlockSpec((tb, d), lambda b, t: (b * n_t + t, 0)),
                   pl.BlockSpec((1, 1, n_heads, HEAD, HEAD), lambda b, t: (b, 0, 0, 0, 0))],
        out_shape=[jax.ShapeDtypeStruct((batch * seq, d), BF16),
                   jax.ShapeDtypeStruct((batch, 1, n_heads, HEAD, HEAD), F32)],
        scratch_shapes=[pltpu.VMEM((n_heads, HEAD, HEAD), F32),
                        pltpu.VMEM((tb, d), BF16), pltpu.VMEM((tb, d), BF16), pltpu.VMEM((tb, d), BF16),
                        pltpu.VMEM((tb, d), F32),
                        pltpu.VMEM((n_heads, HEAD, (tb // CHUNK) * HEAD), BF16)],
        compiler_params=_params(2),
        name="hgrn_prompt",
    )(p16, p32, p16, p16, hgrn_lb, g_norm.reshape(1, d))


def _hgrn_sample_body(layer, tg, q_ref, f_ref, v_ref, gt_ref, lb_ref, gn_ref, s0_ref,
                      o_ref, s_ref, o_scr):
    lb_h = _forget_lower_bound(lb_ref, layer)
    gn_h = gn_ref[...]

    q = jax.nn.silu(q_ref[...].astype(F32))
    f = lb_h + (1.0 - lb_h) * jax.nn.sigmoid(f_ref[...])
    decay = f
    k = 1.0 - f
    v = v_ref[...].astype(F32)
    d_t, k_t = decay.T, k.T
    q16 = q.astype(BF16)
    for j in range(tg):
        s = d_t[:, j:j + 1] * s0_ref[j, 0, 0] + k_t[:, j:j + 1] * v[j:j + 1, :]
        s_ref[j, 0, 0] = s
        o_scr[j:j + 1, :] = jnp.dot(q16[j:j + 1, :], s.astype(BF16), preferred_element_type=F32)
    o = _head_rmsnorm(o_scr[...]) * gn_h * jax.nn.silu(gt_ref[...].astype(F32))
    o_ref[...] = o.astype(BF16)


def _hgrn_sample(p16, p32, hgrn_lb, g_norm, state, layer, row0, tg):
    nb = state.shape[0]
    d = g_norm.shape[0]
    n_heads = d // HEAD
    rb0 = row0 // tg
    blk = lambda kk: pl.BlockSpec((tg, HEAD), lambda g, h, kk=kk: (rb0 + g, kk * n_heads + h))
    st_spec = pl.BlockSpec((tg, 1, 1, HEAD, HEAD), lambda g, h: (g, 0, h, 0, 0))
    return pl.pallas_call(
        functools.partial(_hgrn_sample_body, layer, tg),
        grid=(nb // tg, n_heads),
        in_specs=[blk(0), blk(0), blk(1), blk(2),
                  pl.BlockSpec((hgrn_lb.shape[0], HEAD), lambda g, h: (0, h)),
                  pl.BlockSpec((1, HEAD), lambda g, h: (0, h)),
                  st_spec],
        out_specs=[pl.BlockSpec((tg, HEAD), lambda g, h: (g, h)), st_spec],
        out_shape=[jax.ShapeDtypeStruct((nb, d), BF16),
                   jax.ShapeDtypeStruct(state.shape, F32)],
        scratch_shapes=[pltpu.VMEM((tg, HEAD), F32)],
        compiler_params=_params(2),
        name="hgrn_sample",
    )(p16, p32, p16, p16, hgrn_lb, g_norm.reshape(1, d), state)


_CARRY = 8


def _conv_prompt_body(tb, b_ref, c_ref, h_ref, w_ref, z_ref, buf_ref, u_scr):
    t = pl.program_id(1)

    @pl.when(t == 0)
    def _():
        u_scr[0:_CARRY, :] = jnp.zeros((_CARRY, u_scr.shape[1]), F32)

    u = c_ref[...].astype(F32) * h_ref[...].astype(F32)
    u_scr[_CARRY:_CARRY + tb, :] = u
    w = w_ref[...]
    y = w[0:1, :] * u_scr[_CARRY - 2:_CARRY - 2 + tb, :]
    y = y + w[1:2, :] * u_scr[_CARRY - 1:_CARRY - 1 + tb, :]
    y = y + w[2:3, :] * u
    z_ref[...] = (b_ref[...].astype(F32) * y).astype(BF16)
    u_scr[0:_CARRY, :] = u[tb - _CARRY:tb, :]

    @pl.when(t == pl.num_programs(1) - 1)
    def _():
        buf_ref[0] = u[tb - (CONV_W - 1):tb, :]


def _conv_prompt(pc, conv_w, batch, seq, tb):
    d = conv_w.shape[1]
    n_t = seq // tb
    blk = lambda kk: pl.BlockSpec((tb, d), lambda b, t, kk=kk: (b * n_t + t, kk))
    return pl.pallas_call(
        functools.partial(_conv_prompt_body, tb),
        grid=(batch, n_t),
        in_specs=[blk(0), blk(1), blk(2), pl.BlockSpec(conv_w.shape, lambda b, t: (0, 0))],
        out_specs=[pl.BlockSpec((tb, d), lambda b, t: (b * n_t + t, 0)),
                   pl.BlockSpec((1, CONV_W - 1, d), lambda b, t: (b, 0, 0))],
        out_shape=[jax.ShapeDtypeStruct((batch * seq, d), BF16),
                   jax.ShapeDtypeStruct((batch, CONV_W - 1, d), F32)],
        scratch_shapes=[pltpu.VMEM((_CARRY + tb, d), F32)],
        compiler_params=_params(2),
        name="conv_prompt",
    )(pc, pc, pc, conv_w)


def _conv_sample_body(d, b_ref, c_ref, h_ref, w_ref, st_ref, z_ref, buf_ref):
    u = c_ref[...].astype(F32) * h_ref[...].astype(F32)
    w = w_ref[...]
    buf0 = st_ref[:, 0:d]
    buf1 = st_ref[:, d:2 * d]
    y = w[0:1, :] * buf0
    y = y + w[1:2, :] * buf1
    y = y + w[2:3, :] * u
    z_ref[...] = (b_ref[...].astype(F32) * y).astype(BF16)
    buf_ref[:, 0:d] = buf1
    buf_ref[:, d:2 * d] = u


def _conv_sample(pc, conv_w, state2d, row0):
    nb = state2d.shape[0]
    d = conv_w.shape[1]
    rb0 = row0 // nb
    blk = lambda kk: pl.BlockSpec((nb, d), lambda i, kk=kk: (rb0, kk))
    return pl.pallas_call(
        functools.partial(_conv_sample_body, d),
        grid=(1,),
        in_specs=[blk(0), blk(1), blk(2), pl.BlockSpec(conv_w.shape, lambda i: (0, 0)),
                  pl.BlockSpec(state2d.shape, lambda i: (0, 0))],
        out_specs=[pl.BlockSpec((nb, d), lambda i: (0, 0)),
                   pl.BlockSpec(state2d.shape, lambda i: (0, 0))],
        out_shape=[jax.ShapeDtypeStruct((nb, d), BF16),
                   jax.ShapeDtypeStruct(state2d.shape, F32)],
        compiler_params=_params(1),
        name="conv_sample",
    )(pc, pc, pc, conv_w, state2d)


def _outproj_body(n_blocks, x_ref, z_ref, w_ref, o_ref, w16_ref):
    i = pl.program_id(0)

    @pl.when(i == 0)
    def _():
        w16_ref[...] = w_ref[...].astype(BF16)

    @pl.when(i < n_blocks)
    def _():
        o_ref[...] = x_ref[...] + jnp.dot(z_ref[...], w16_ref[...], preferred_element_type=F32)

    @pl.when(i >= n_blocks)
    def _():
        o_ref[...] = jnp.zeros_like(o_ref)


def _outproj_route_body(n_blocks, n_experts, x_ref, z_ref, w_ref, g_ref, rw_ref,
                        o_ref, route_ref, cnt_ref, w16_ref, cnt_scr):
    i = pl.program_id(0)

    @pl.when(i == 0)
    def _():
        w16_ref[...] = w_ref[...].astype(BF16)
        cnt_scr[...] = jnp.zeros_like(cnt_scr)

    @pl.when(i < n_blocks)
    def _():
        res = x_ref[...] + jnp.dot(z_ref[...], w16_ref[...], preferred_element_type=F32)
        o_ref[...] = res
        route_ref[...], cnt_scr[...] = _route_rows(res, g_ref[...], rw_ref[...], cnt_scr[...], n_experts)

    @pl.when(i >= n_blocks)
    def _():
        o_ref[...] = jnp.zeros_like(o_ref)
        route_ref[...] = jnp.zeros_like(route_ref)

    cnt_ref[...] = cnt_scr[...]


def _outproj_prompt(x, z, w, n_rows, tm, router=None):
    d = x.shape[1]
    n_blocks = n_rows // tm
    row_blk = lambda i: (jnp.minimum(i, n_blocks - 1), 0)
    in_specs = [pl.BlockSpec((tm, d), row_blk),
                pl.BlockSpec((tm, d), row_blk),
                pl.BlockSpec((d, d), lambda i: (0, 0))]
    out_rows = (n_blocks + 1) * tm
    if router is None:
        return pl.pallas_call(
            functools.partial(_outproj_body, n_blocks),
            grid=(n_blocks + 1,),
            in_specs=in_specs,
            out_specs=pl.BlockSpec((tm, d), lambda i: (i, 0)),
            out_shape=jax.ShapeDtypeStruct((out_rows, d), F32),
            scratch_shapes=[pltpu.VMEM((d, d), BF16)],
            compiler_params=_params(1),
            name="outproj",
        )(x, z, w)
    g, rw_pad, n_experts = router
    return pl.pallas_call(
        functools.partial(_outproj_route_body, n_blocks, n_experts),
        grid=(n_blocks + 1,),
        in_specs=in_specs + [pl.BlockSpec((1, d), lambda i: (0, 0)),
                             pl.BlockSpec((d, LANES), lambda i: (0, 0))],
        out_specs=[pl.BlockSpec((tm, d), lambda i: (i, 0)),
                   pl.BlockSpec((ROUTE_ROWS, tm), lambda i: (0, i)),
                   pl.BlockSpec((1, LANES), lambda i: (0, 0))],
        out_shape=[jax.ShapeDtypeStruct((out_rows, d), F32),
                   jax.ShapeDtypeStruct((ROUTE_ROWS, out_rows), F32),
                   jax.ShapeDtypeStruct((1, LANES), F32)],
        scratch_shapes=[pltpu.VMEM((d, d), BF16), pltpu.VMEM((1, LANES), F32)],
        compiler_params=_params(1),
        name="outproj_route",
    )(x, z, w, g.reshape(1, d), rw_pad)


def _outproj_sample_body(x_ref, z_ref, w_ref, buf_ref, o_ref):
    del buf_ref
    o_ref[...] = x_ref[...] + jnp.dot(z_ref[...], w_ref[...].astype(BF16), preferred_element_type=F32)


def _outproj_sample_route_body(n_experts, x_ref, z_ref, w_ref, g_ref, rw_ref, cnt_in_ref, buf_ref, route_in_ref,
                               o_ref, route_ref, cnt_ref):
    del buf_ref, route_in_ref
    res = x_ref[...] + jnp.dot(z_ref[...], w_ref[...].astype(BF16), preferred_element_type=F32)
    o_ref[...] = res
    route_ref[...], cnt_ref[...] = _route_rows(res, g_ref[...], rw_ref[...], cnt_in_ref[...], n_experts)


def _outproj_sample(x, x_row0, z, w, buf, row0, router=None):
    nb, d = z.shape
    in_specs = [pl.BlockSpec((nb, d), lambda i: (x_row0 // nb, 0)),
                pl.BlockSpec((nb, d), lambda i: (0, 0)),
                pl.BlockSpec((d, d), lambda i: (0, 0))]
    if router is None:
        return pl.pallas_call(
            _outproj_sample_body,
            grid=(1,),
            in_specs=in_specs + [pl.BlockSpec(memory_space=pl.ANY)],
            out_specs=pl.BlockSpec((nb, d), lambda i: (row0 // nb, 0)),
            out_shape=jax.ShapeDtypeStruct(buf.shape, buf.dtype),
            input_output_aliases={3: 0},
            compiler_params=_params(1),
            name="outproj_sample",
        )(x, z, w, buf)
    g, rw_pad, n_experts, route, counts = router
    return pl.pallas_call(
        functools.partial(_outproj_sample_route_body, n_experts),
        grid=(1,),
        in_specs=in_specs + [pl.BlockSpec((1, d), lambda i: (0, 0)),
                             pl.BlockSpec((d, LANES), lambda i: (0, 0)),
                             pl.BlockSpec((1, LANES), lambda i: (0, 0)),
                             pl.BlockSpec(memory_space=pl.ANY),
                             pl.BlockSpec(memory_space=pl.ANY)],
        out_specs=[pl.BlockSpec((nb, d), lambda i: (row0 // nb, 0)),
                   pl.BlockSpec((ROUTE_ROWS, nb), lambda i: (0, row0 // nb)),
                   pl.BlockSpec((1, LANES), lambda i: (0, 0))],
        out_shape=[jax.ShapeDtypeStruct(buf.shape, buf.dtype),
                   jax.ShapeDtypeStruct(route.shape, route.dtype),
                   jax.ShapeDtypeStruct((1, LANES), F32)],
        input_output_aliases={6: 0, 7: 1},
        compiler_params=_params(1),
        name="outproj_sample_route",
    )(x, z, w, g.reshape(1, d), rw_pad, counts, buf, route)


def _swiglu_partial(xn, w1, w3, w2):
    h1 = jnp.dot(xn, w1, preferred_element_type=F32)
    h3 = jnp.dot(xn, w3, preferred_element_type=F32)
    h = (jax.nn.silu(h1) * h3).astype(BF16)
    return jnp.dot(h, w2, preferred_element_type=F32)


def _cache_weight_chunk(w1_hbm, w3_hbm, w2_hbm, f, n_f, tf, stages, caches, wsem):
    n_slots = stages[0].shape[0]

    def copies(ff, ws):
        cols = pl.ds(pl.multiple_of(ff * tf, tf), tf)
        srcs = (w1_hbm.at[:, cols], w3_hbm.at[:, cols], w2_hbm.at[cols, :])
        return [pltpu.make_async_copy(src, st.at[ws], wsem.at[ws, m]) for m, (src, st) in enumerate(zip(srcs, stages))]

    if n_slots == 1:
        ws = 0
        for c in copies(f, 0):
            c.start()
    else:
        ws = lax.rem(f, 2)

        @pl.when(f == 0)
        def _():
            for c in copies(0, 0):
                c.start()

        @pl.when(f + 1 < n_f)
        def _():
            for c in copies(f + 1, 1 - ws):
                c.start()

    for c in copies(f, ws):
        c.wait()
    for st, ca in zip(stages, caches):
        ca[f] = st[ws].astype(BF16)


def _ffn_body(n_f, tf, x_ref, g_ref, w1_hbm, w3_hbm, w2_hbm, o_ref, xn_ref, acc_ref, c1, c3, c2, s1, s3, s2, wsem):
    i = pl.program_id(0)
    xn_ref[...] = _rmsnorm_bf16(x_ref[...], g_ref[...])
    acc_ref[...] = jnp.zeros_like(acc_ref)

    def chunk(f, carry):
        @pl.when(i == 0)
        def _():
            _cache_weight_chunk(w1_hbm, w3_hbm, w2_hbm, f, n_f, tf, (s1, s3, s2), (c1, c3, c2), wsem)

        acc_ref[...] += _swiglu_partial(xn_ref[...], c1[f], c3[f], c2[f])
        return carry

    lax.fori_loop(0, n_f, chunk, 0)
    o_ref[...] = x_ref[...] + acc_ref[...]


def _ffn(x, n, g, w1, w3, w2, tm, tf):
    d = x.shape[1]
    dff = w1.shape[1]
    n_f = dff // tf
    return pl.pallas_call(
        functools.partial(_ffn_body, n_f, tf),
        grid=(n // tm,),
        in_specs=[pl.BlockSpec((tm, d), lambda i: (i, 0)),
                  pl.BlockSpec((1, d), lambda i: (0, 0)),
                  pl.BlockSpec(memory_space=pl.ANY),
                  pl.BlockSpec(memory_space=pl.ANY),
                  pl.BlockSpec(memory_space=pl.ANY)],
        out_specs=pl.BlockSpec((tm, d), lambda i: (i, 0)),
        out_shape=jax.ShapeDtypeStruct((n, d), F32),
        scratch_shapes=[pltpu.VMEM((tm, d), BF16), pltpu.VMEM((tm, d), F32),
                        pltpu.VMEM((n_f, d, tf), BF16), pltpu.VMEM((n_f, d, tf), BF16),
                        pltpu.VMEM((n_f, tf, d), BF16),
                        pltpu.VMEM((1, d, tf), F32), pltpu.VMEM((1, d, tf), F32), pltpu.VMEM((1, tf, d), F32),
                        pltpu.SemaphoreType.DMA((1, 3))],
        compiler_params=_params(1),
        name="ffn",
    )(x, g.reshape(1, d), w1, w3, w2)


ROUTE_ROWS = 8


def _route_rows(x, g, rw, counts, n_experts):
    rows = x.shape[0]
    xn = _rmsnorm_bf16(x, g)
    logits = jnp.dot(xn, rw.astype(BF16), preferred_element_type=F32)
    lane = lax.broadcasted_iota(I32, (rows, LANES), 1).astype(F32)
    neg = jnp.float32(-jnp.inf)
    logits = jnp.where(lane < n_experts, logits, neg)
    m1 = jnp.max(logits, axis=-1, keepdims=True)
    i1 = jnp.min(jnp.where(logits == m1, lane, float(LANES)), axis=-1, keepdims=True)
    i1 = jnp.minimum(i1, float(n_experts - 1))
    rest = jnp.where(lane == i1, neg, logits)
    m2 = jnp.max(rest, axis=-1, keepdims=True)
    i2 = jnp.min(jnp.where(rest == m2, lane, float(LANES)), axis=-1, keepdims=True)
    i2 = jnp.minimum(i2, float(n_experts - 1))
    e2 = jnp.exp(m2 - m1)
    den = 1.0 + e2
    g1 = 1.0 / den
    g2 = e2 / den

    sel1 = lane == i1
    sel2 = lane == i2
    onehot = jnp.logical_or(sel1, sel2)
    row = lax.broadcasted_iota(I32, (rows, rows), 0)
    col = lax.broadcasted_iota(I32, (rows, rows), 1)
    before = (row > col).astype(BF16)
    rank = jnp.dot(before, onehot.astype(BF16), preferred_element_type=F32) + counts
    r1 = jnp.sum(jnp.where(sel1, rank, 0.0), axis=-1, keepdims=True)
    r2 = jnp.sum(jnp.where(sel2, rank, 0.0), axis=-1, keepdims=True)

    out = jnp.zeros((rows, LANES), F32)
    for slot, val in enumerate((i1, i2, g1, g2, r1, r2)):
        out = jnp.where(lane == slot, val, out)
    return out.T[0:ROUTE_ROWS, :], counts + jnp.sum(onehot.astype(F32), axis=0, keepdims=True)


SUBLANES = 8
MOE_ROWS_PER_CHUNK = 96


def _start_row_group(src_hbm, idx_ref, g, dst, sem):
    for k in range(SUBLANES):
        pltpu.make_async_copy(src_hbm.at[pl.ds(idx_ref[0, 0, g * SUBLANES + k], 1)],
                              dst.at[g, pl.ds(k, 1)], sem).start(priority=k % 2)


def _start_row_gather(src_hbm, idx_ref, n_rows, dst, sem, unrolled=False):
    if unrolled:
        for g in range(n_rows // SUBLANES):
            _start_row_group(src_hbm, idx_ref, g, dst, sem)
        return

    def body(g, carry):
        _start_row_group(src_hbm, idx_ref, g, dst, sem)
        return carry

    lax.fori_loop(0, n_rows // SUBLANES, body, 0)


def _wait_row_gather(src_hbm, n_rows, dst, sem):
    pltpu.make_async_copy(src_hbm.at[pl.ds(0, n_rows)], dst.reshape(n_rows, dst.shape[-1]), sem).wait()


def _moe_body(n_f, tf, te_ref, nu_ref, hf_ref, idx_cur, idx_nxt, x_hbm, gate_ref, g_ref, w1_hbm, w3_hbm, w2_hbm,
              y_ref, xbuf, sem, xn_ref, acc_ref, c1, c3, c2, s1, s3, s2, wsem):
    i = pl.program_id(0)
    n_tiles = pl.num_programs(0)
    tm = xbuf.shape[1] * SUBLANES
    groups_per_chunk = tm // n_f // SUBLANES
    slot = lax.rem(i, 2)
    n_used = nu_ref[0]
    used = i < n_used
    e = te_ref[i]
    first_of_expert = jnp.logical_or(i == 0, e != te_ref[jnp.maximum(i - 1, 0)])
    half_filled = hf_ref[i] == 1

    @pl.when(i == 0)
    def _():
        _start_row_gather(x_hbm, idx_cur, tm, xbuf.at[0], sem.at[0])

    @pl.when(i <= n_used)
    def _():
        _wait_row_gather(x_hbm, tm, xbuf.at[slot], sem.at[slot])

    @pl.when(used)
    def _():
        xn_ref[...] = _rmsnorm_bf16(xbuf[slot].reshape(tm, xbuf.shape[-1]), g_ref[...])
        acc_ref[...] = jnp.zeros_like(acc_ref)

        def chunk(f, carry):
            @pl.when(first_of_expert)
            def _():
                _cache_weight_chunk(w1_hbm.at[e], w3_hbm.at[e], w2_hbm.at[e], f, n_f, tf,
                                    (s1, s3, s2), (c1, c3, c2), wsem)

            def compute(rows):
                for g in range(groups_per_chunk):
                    _start_row_group(x_hbm, idx_nxt, f * groups_per_chunk + g, xbuf.at[1 - slot], sem.at[1 - slot])
                acc_ref[0:rows, :] += _swiglu_partial(xn_ref[0:rows, :], c1[f], c3[f], c2[f])

            @pl.when(half_filled)
            def _():
                compute(tm // 2)

            @pl.when(jnp.logical_not(half_filled))
            def _():
                compute(tm)

            return carry

        lax.fori_loop(0, n_f, chunk, 0)
        r_id = lax.broadcasted_iota(I32, (tm, tm), 0)
        c_id = lax.broadcasted_iota(I32, (tm, tm), 1)
        gate_col = jnp.sum(jnp.where(r_id == c_id, gate_ref[0], 0.0), axis=1, keepdims=True)
        y_ref[...] = acc_ref[...] * gate_col

    @pl.when(jnp.logical_not(used))
    def _():
        y_ref[...] = jnp.zeros_like(y_ref)

    @pl.when(jnp.logical_and(used, i == n_tiles - 1))
    def _():
        _wait_row_gather(x_hbm, tm, xbuf.at[1 - slot], sem.at[1 - slot])


def _moe(x, g, w1, w3, w2, tile_expert, n_used, half_filled, inv3, gate_sorted, tm, tf):
    n_tiles = inv3.shape[0]
    d = x.shape[1]
    dff = w1.shape[2]
    n_f = dff // tf
    assert tm % (n_f * SUBLANES) == 0, "each hidden chunk fetches whole 8-row tiles of the next row tile"
    assert tm % 32 == 0, "half a row tile must be whole bf16 sublane tiles"
    grid_spec = pltpu.PrefetchScalarGridSpec(
        num_scalar_prefetch=3,
        grid=(n_tiles,),
        in_specs=[pl.BlockSpec((1, 1, tm), lambda i, te, nu, hf: (i, 0, 0), memory_space=pltpu.SMEM),
                  pl.BlockSpec((1, 1, tm), lambda i, te, nu, hf: (jnp.minimum(i + 1, n_tiles - 1), 0, 0),
                               memory_space=pltpu.SMEM),
                  pl.BlockSpec(memory_space=pl.ANY),
                  pl.BlockSpec((1, 1, tm), lambda i, te, nu, hf: (i, 0, 0)),
                  pl.BlockSpec((1, d), lambda i, te, nu, hf: (0, 0)),
                  pl.BlockSpec(memory_space=pl.ANY),
                  pl.BlockSpec(memory_space=pl.ANY),
                  pl.BlockSpec(memory_space=pl.ANY)],
        out_specs=pl.BlockSpec((tm, d), lambda i, te, nu, hf: (i, 0)),
        scratch_shapes=[pltpu.VMEM((2, tm // SUBLANES, SUBLANES, d), F32),
                        pltpu.SemaphoreType.DMA((2,)),
                        pltpu.VMEM((tm, d), BF16),
                        pltpu.VMEM((tm, d), F32),
                        pltpu.VMEM((n_f, d, tf), BF16),
                        pltpu.VMEM((n_f, d, tf), BF16),
                        pltpu.VMEM((n_f, tf, d), BF16),
                        pltpu.VMEM((2, d, tf), F32),
                        pltpu.VMEM((2, d, tf), F32),
                        pltpu.VMEM((2, tf, d), F32),
                        pltpu.SemaphoreType.DMA((2, 3))],
    )
    return pl.pallas_call(
        functools.partial(_moe_body, n_f, tf),
        grid_spec=grid_spec,
        out_shape=jax.ShapeDtypeStruct((n_tiles * tm, d), F32),
        compiler_params=_params(1),
        name="moe",
    )(tile_expert, n_used, half_filled, inv3, inv3, x, gate_sorted, g.reshape(1, d), w1, w3, w2)


def _combine_body(pos_cur, pos_nxt, x_ref, g_ref, y_hbm, o_ref, ybuf, sem):
    i = pl.program_id(0)
    n_tiles = pl.num_programs(0)
    rows = ybuf.shape[1] * SUBLANES
    slot = lax.rem(i, 2)

    @pl.when(i == 0)
    def _():
        _start_row_gather(y_hbm, pos_cur, rows, ybuf.at[0], sem.at[0])

    @pl.when(i + 1 < n_tiles)
    def _():
        _start_row_gather(y_hbm, pos_nxt, rows, ybuf.at[1 - slot], sem.at[1 - slot], unrolled=True)

    _wait_row_gather(y_hbm, rows, ybuf.at[slot], sem.at[slot])
    tc = rows // TOP_K
    y = ybuf[slot].reshape(rows, ybuf.shape[-1])
    moe = y[0:tc, :] + y[tc:rows, :]
    x = x_ref[...] + moe
    ms = jnp.mean(x * x, axis=-1, keepdims=True)
    o_ref[...] = x * lax.rsqrt(ms + EPS) * g_ref[...]


def _combine(x, g, y_sorted, pos3, row0, n_rows, tc):
    d = x.shape[1]
    n_tiles = n_rows // tc
    rb0 = row0 // tc
    grid_spec = pltpu.PrefetchScalarGridSpec(
        num_scalar_prefetch=0,
        grid=(n_tiles,),
        in_specs=[pl.BlockSpec((1, 1, TOP_K * tc), lambda i: (i, 0, 0), memory_space=pltpu.SMEM),
                  pl.BlockSpec((1, 1, TOP_K * tc), lambda i: (jnp.minimum(i + 1, n_tiles - 1), 0, 0),
                               memory_space=pltpu.SMEM),
                  pl.BlockSpec((tc, d), lambda i: (rb0 + i, 0)),
                  pl.BlockSpec((1, d), lambda i: (0, 0)),
                  pl.BlockSpec(memory_space=pl.ANY)],
        out_specs=pl.BlockSpec((tc, d), lambda i: (i, 0)),
        scratch_shapes=[pltpu.VMEM((2, TOP_K * tc // SUBLANES, SUBLANES, d), F32), pltpu.SemaphoreType.DMA((2,))],
    )
    return pl.pallas_call(
        _combine_body,
        grid_spec=grid_spec,
        out_shape=jax.ShapeDtypeStruct((n_rows, d), F32),
        compiler_params=_params(1),
        name="combine",
    )(pos3, pos3, x, g.reshape(1, d), y_sorted)


def _lookup(table, idx):
    out = jnp.zeros(idx.shape, table.dtype)
    for e in range(table.shape[0]):
        out = jnp.where(idx == e, table[e], out)
    return out


def _expert_layer(x, route, cnt, n_experts, g_ffn, g_final, w1, w3, w2, n_prompt, n_dec, tm_moe, tf, tc):
    n = n_prompt + n_dec
    ids = route[0:TOP_K, :n].astype(I32)
    gates = route[TOP_K:2 * TOP_K, :n]
    ranks = route[2 * TOP_K:3 * TOP_K, :n].astype(I32)
    counts = cnt[0, :n_experts].astype(I32)

    n_tiles = (TOP_K * n + n_experts * (tm_moe - 1)) // tm_moe
    padded = ((counts + tm_moe - 1) // tm_moe) * tm_moe
    ends = jnp.cumsum(padded)
    starts = ends - padded
    cstart = jnp.cumsum(counts) - counts
    pos = _lookup(starts, ids) + ranks
    keys = ids * n + jnp.arange(n, dtype=I32)[None, :]
    order = jnp.argsort(keys.reshape(-1)).astype(I32)
    slot_pos = jnp.arange(n_tiles * tm_moe, dtype=I32)
    slot_e = jnp.minimum(jnp.sum(slot_pos[None, :] >= ends[:, None], axis=0), n_experts - 1).astype(I32)
    slot_rank = slot_pos - _lookup(starts, slot_e)
    slot_valid = slot_rank < _lookup(counts, slot_e)
    slot_asg = order[jnp.clip(_lookup(cstart, slot_e) + slot_rank, 0, TOP_K * n - 1)]
    inv = jnp.where(slot_valid, slot_asg % n, 0).astype(I32)
    gate_sorted = jnp.where(slot_valid, gates.reshape(-1)[slot_asg], 0.0)
    tile_expert = slot_e[::tm_moe]
    n_used = (ends[-1] // tm_moe).astype(I32).reshape(1)

    tile_start = jnp.arange(n_tiles, dtype=I32) * tm_moe
    tile_rows = _lookup(counts, tile_expert) - (tile_start - _lookup(starts, tile_expert))
    half_filled = (tile_rows <= tm_moe // 2).astype(I32)

    y_sorted = _moe(x, g_ffn, w1, w3, w2, tile_expert, n_used, half_filled, inv.reshape(n_tiles, 1, tm_moe),
                    gate_sorted.reshape(n_tiles, 1, tm_moe), tm_moe, tf)

    def tile_positions(row0, n_rows, t):
        p = pos[:, row0:row0 + n_rows].reshape(TOP_K, n_rows // t, t)
        return jnp.swapaxes(p, 0, 1).reshape(n_rows // t, 1, TOP_K * t)

    y_prompt = _combine(x, g_final, y_sorted, tile_positions(0, n_prompt, tc), 0, n_prompt, tc)
    y_sample = _combine(x, g_final, y_sorted, tile_positions(n_prompt, n_dec, n_dec), n_prompt, n_dec, n_dec)
    return y_prompt, y_sample


def kernel(x_prompt, x_sample, state_hgrn, state_conv, norm_mix, norm_ffn, norm_final, hgrn_w_in, hgrn_lb, hgrn_g_norm, hgrn_w_out, conv_w_in, conv_w, conv_w_out, ffn_w1, ffn_w3, ffn_w2, moe_router, moe_w1, moe_w3, moe_w2):
    batch, seq, d = x_prompt.shape
    n_dec = x_sample.shape[0]
    n_prompt = batch * seq
    n = n_prompt + n_dec
    n_experts = moe_router.shape[-1]
    assert x_sample.shape[1] == 1 and d % HEAD == 0 and seq % CHUNK == 0
    assert norm_mix.shape[0] == 2, "one HGRN2 layer followed by one short-conv layer"

    tm = _pick_tile(n, (688, 384, 128, 16))
    tm_proj = _pick_tile(n, (1376, 688, 384, 128, 16))
    tm_prompt = _pick_tile(n_prompt, (1024, 512, 256, 128, 16))
    tb = _pick_tile(seq, (256, 128, 64, 32))
    tb_conv = _pick_tile(seq, (512, 256, 128, 64, 32))
    tg = _pick_tile(n_dec, (16,))
    tf = _pick_tile(ffn_w1.shape[-1], (512, 256, 128))
    tm_moe = (moe_w1.shape[-1] // tf) * MOE_ROWS_PER_CHUNK
    tc = _pick_tile(n_prompt, (512, 128, 16))
    assert n_prompt % tg == 0 and n_prompt % n_dec == 0 and n_prompt % tc == 0

    xp = x_prompt.reshape(n_prompt, d)
    xs = x_sample.reshape(n_dec, d)

    p16, p32 = _proj(xp, norm_mix[0], hgrn_w_in[0], tm_prompt, f32_col=1)
    p16_s, p32_s = _proj(xs, norm_mix[0], hgrn_w_in[0], n_dec, f32_col=1)
    o_p, s_prompt = _hgrn_prompt(p16, p32, hgrn_lb, hgrn_g_norm[0], 0, batch, seq, tb)
    o_s, s_sample = _hgrn_sample(p16_s, p32_s, hgrn_lb, hgrn_g_norm[0], state_hgrn, 0, 0, tg)
    x = _outproj_prompt(xp, o_p, hgrn_w_out[0], n_prompt, tm_prompt)
    x = _outproj_sample(xs, 0, o_s, hgrn_w_out[0], x, n_prompt)
    x = _ffn(x, n, norm_ffn[0], ffn_w1[0], ffn_w3[0], ffn_w2[0], tm, tf)

    (pc,) = _proj(x, norm_mix[1], conv_w_in[0], tm_proj)
    z_p, c_prompt = _conv_prompt(pc, conv_w[0], batch, seq, tb_conv)
    z_s, c_sample = _conv_sample(pc, conv_w[0], state_conv.reshape(n_dec, (CONV_W - 1) * d), n_prompt)
    rw_pad = jnp.pad(moe_router[0], ((0, 0), (0, LANES - n_experts)))
    x1, route, cnt = _outproj_prompt(x, z_p, conv_w_out[0], n_prompt, tm_prompt, router=(norm_ffn[1], rw_pad, n_experts))
    x1, route, cnt = _outproj_sample(x, n_prompt, z_s, conv_w_out[0], x1, n_prompt,
                                     router=(norm_ffn[1], rw_pad, n_experts, route, cnt))

    y_prompt, y_sample = _expert_layer(x1, route, cnt, n_experts, norm_ffn[1], norm_final, moe_w1[0], moe_w3[0],
                                       moe_w2[0], n_prompt, n_dec, tm_moe, tf, tc)

    return (y_prompt.reshape(batch, seq, d),
            y_sample.reshape(n_dec, 1, d),
            s_prompt,
            s_sample,
            c_prompt.reshape(batch, 1, CONV_W - 1, d),
            c_sample.reshape(n_dec, 1, CONV_W - 1, d))
```

```python
import functools

import jax
import jax.numpy as jnp
from jax import lax
from jax.experimental import pallas as pl
from jax.experimental.pallas import tpu as pltpu

F32 = jnp.float32
BF16 = jnp.bfloat16
I32 = jnp.int32

EPS = 1e-6
HEAD = 128
CHUNK = 32
CONV_W = 3
TOP_K = 2
LANES = 128
V7X_VMEM_LIMIT = 56 * 1024 * 1024

ARB = "arbitrary"


def _params(n_axes):
    return pltpu.CompilerParams(dimension_semantics=(ARB,) * n_axes, vmem_limit_bytes=V7X_VMEM_LIMIT)


def _pick_tile(n, candidates):
    for c in candidates:
        if n % c == 0:
            return c
    raise ValueError(f"no tile in {candidates} divides {n}")


def _rmsnorm_bf16(x, g):
    ms = jnp.mean(x * x, axis=-1, keepdims=True)
    return (x * lax.rsqrt(ms + EPS) * g).astype(BF16)


def _proj_body(n_col, f32_col, x_ref, g_ref, w_hbm, *refs):
    o16_ref = refs[0]
    w16_ref, stage_ref, wsem = refs[-3:]
    d = x_ref.shape[1]

    @pl.when(pl.program_id(0) == 0)
    def _():
        for j in range(n_col):
            cp = pltpu.make_async_copy(w_hbm.at[:, j * d:(j + 1) * d], stage_ref, wsem)
            cp.start()
            cp.wait()
            w16_ref[j] = stage_ref[...].astype(BF16)

    xn = _rmsnorm_bf16(x_ref[...], g_ref[...])
    j16 = 0
    for j in range(n_col):
        res = jnp.dot(xn, w16_ref[j], preferred_element_type=F32)
        if j == f32_col:
            refs[1][...] = res
        else:
            o16_ref[:, j16 * d:(j16 + 1) * d] = res.astype(BF16)
            j16 += 1


def _proj(x, g, w, tm, f32_col=None):
    n, d = x.shape
    n_col = w.shape[1] // d
    n16 = n_col - (f32_col is not None)
    out_specs = [pl.BlockSpec((tm, n16 * d), lambda i: (i, 0))]
    out_shape = [jax.ShapeDtypeStruct((n, n16 * d), BF16)]
    if f32_col is not None:
        out_specs.append(pl.BlockSpec((tm, d), lambda i: (i, 0)))
        out_shape.append(jax.ShapeDtypeStruct((n, d), F32))
    return pl.pallas_call(
        functools.partial(_proj_body, n_col, f32_col),
        grid=(n // tm,),
        in_specs=[pl.BlockSpec((tm, d), lambda i: (i, 0)),
                  pl.BlockSpec((1, d), lambda i: (0, 0)),
                  pl.BlockSpec(memory_space=pl.ANY)],
        out_specs=out_specs,
        out_shape=out_shape,
        scratch_shapes=[pltpu.VMEM((n_col, d, d), BF16), pltpu.VMEM((d, d), F32), pltpu.SemaphoreType.DMA(())],
        compiler_params=_params(1),
        name="proj",
    )(x, g.reshape(1, d), w)


def _forget_lower_bound(lb_ref, layer):
    lbw = lb_ref[...]
    e = jnp.exp(lbw - jnp.max(lbw, axis=0, keepdims=True))
    sm = e / jnp.sum(e, axis=0, keepdims=True)
    return jnp.sum(sm[:layer + 1], axis=0, keepdims=True)


def _head_rmsnorm(o):
    return o * lax.rsqrt(jnp.mean(o * o, axis=-1, keepdims=True) + EPS)


def _sigmoid(x):
    return 0.5 * (jnp.tanh(0.5 * x) + 1.0)


def _silu(x):
    h = 0.5 * x
    return h + h * jnp.tanh(h)


def _split3_bf16(x):
    hi = x.astype(BF16)
    r1 = x - hi.astype(F32)
    mid = r1.astype(BF16)
    lo = (r1 - mid.astype(F32)).astype(BF16)
    return hi, mid, lo


def _hgrn_prompt_body(layer, n_heads, tb, q_ref, f_ref, v_ref, gt_ref, lb_ref, gn_ref,
                      o_ref, s_ref, st_scr, qin_scr, kin_scr, kdec_scr, g_scr, sprev_scr):
    t = pl.program_id(1)
    nc = tb // CHUNK
    d = n_heads * HEAD

    @pl.when(t == 0)
    def _():
        st_scr[...] = jnp.zeros_like(st_scr)

    lb = _forget_lower_bound(lb_ref, layer)
    gn = gn_ref[...]
    row = lax.broadcasted_iota(I32, (tb, tb), 0)
    col = lax.broadcasted_iota(I32, (tb, tb), 1)
    same_chunk = (row // CHUNK) == (col // CHUNK)
    causal = jnp.logical_and(same_chunk, row >= col)

    f = lb + (1.0 - lb) * _sigmoid(f_ref[...])
    logf = jnp.log2(f)
    k = 1.0 - f
    tri = causal.astype(BF16)
    G = None
    for part in _split3_bf16(logf):
        term = jnp.dot(tri, part, preferred_element_type=F32)
        G = term if G is None else G + term
    g_scr[...] = G
    g_last = [g_scr[(c + 1) * CHUNK - 1:(c + 1) * CHUNK, :] for c in range(nc)]
    decay = [jnp.exp2(g) for g in g_last]
    decay_rows = jnp.concatenate([jnp.broadcast_to(dc, (CHUNK, d)) for dc in decay], axis=0)
    k_in = k * jnp.exp2(-G)
    qin_scr[...] = (_silu(q_ref[...].astype(F32)) * jnp.exp2(G)).astype(BF16)
    kin_scr[...] = k_in.astype(BF16)
    kdec_scr[...] = (k_in * decay_rows).astype(BF16)

    row_chunk = lax.broadcasted_iota(I32, (tb, HEAD), 0) // CHUNK
    zero16 = jnp.zeros((), BF16)
    grp = next(g for g in (4, 2, 1) if nc % g == 0)

    for h in range(n_heads):
        hs = slice(h * HEAD, (h + 1) * HEAD)
        qh = qin_scr[:, hs]
        kd = kdec_scr[:, hs]
        vh = v_ref[:, hs]
        a = lax.dot_general(qh, kin_scr[:, hs], (((1,), (1,)), ((), ())), preferred_element_type=F32)
        a = jnp.where(causal, a, 0.0).astype(BF16)
        o = jnp.dot(a, vh, preferred_element_type=F32)
        ut = jnp.concatenate(
            [lax.dot_general(vh[c * CHUNK:(c + 1) * CHUNK], kd[c * CHUNK:(c + 1) * CHUNK],
                             (((0,), (0,)), ((), ())), preferred_element_type=F32) for c in range(nc)], axis=0)
        st = st_scr[h]
        for c in range(nc):
            sprev_scr[h, :, c * HEAD:(c + 1) * HEAD] = st.astype(BF16)
            st = st * decay[c][:, hs] + ut[c * HEAD:(c + 1) * HEAD, :]
        st_scr[h] = st
        inter = []
        for g0 in range(0, nc, grp):
            rows = slice(g0 * CHUNK, (g0 + grp) * CHUNK)
            q_blocks = jnp.concatenate(
                [jnp.where(row_chunk[rows] == g0 + j, qh[rows], zero16) for j in range(grp)], axis=1)
            inter.append(lax.dot_general(q_blocks, sprev_scr[h, :, g0 * HEAD:(g0 + grp) * HEAD],
                                         (((1,), (1,)), ((), ())), preferred_element_type=F32))
        o = o + jnp.concatenate(inter, axis=0)
        o = _head_rmsnorm(o) * gn[:, hs] * _silu(gt_ref[:, hs].astype(F32))
        o_ref[:, hs] = o.astype(BF16)

    @pl.when(t == pl.num_programs(1) - 1)
    def _():
        for h in range(n_heads):
            s_ref[0, 0, h] = st_scr[h].T


def _hgrn_prompt(p16, p32, hgrn_lb, g_norm, layer, batch, seq, tb):
    d = g_norm.shape[0]
    n_heads = d // HEAD
    n_t = seq // tb
    blk = lambda kk: pl.BlockSpec((tb, d), lambda b, t, kk=kk: (b * n_t + t, kk))
    return pl.pallas_call(
        functools.partial(_hgrn_prompt_body, layer, n_heads, tb),
        grid=(batch, n_t),
        in_specs=[blk(0), blk(0), blk(1), blk(2),
                  pl.BlockSpec(hgrn_lb.shape, lambda b, t: (0, 0)),
                  pl.BlockSpec((1, d), lambda b, t: (0, 0))],
        out_specs=[pl.BlockSpec((tb, d), lambda b, t: (b * n_t + t, 0)),
                   pl.BlockSpec((1, 1, n_heads, HEAD, HEAD), lambda b, t: (b, 0, 0, 0, 0))],
        out_shape=[jax.ShapeDtypeStruct((batch * seq, d), BF16),
                   jax.ShapeDtypeStruct((batch, 1, n_heads, HEAD, HEAD), F32)],
        scratch_shapes=[pltpu.VMEM((n_heads, HEAD, HEAD), F32),
                        pltpu.VMEM((tb, d), BF16), pltpu.VMEM((tb, d), BF16), pltpu.VMEM((tb, d), BF16),
                        pltpu.VMEM((tb, d), F32),
                        pltpu.VMEM((n_heads, HEAD, (tb // CHUNK) * HEAD), BF16)],
        compiler_params=_params(2),
        name="hgrn_prompt",
    )(p16, p32, p16, p16, hgrn_lb, g_norm.reshape(1, d))


def _hgrn_sample_body(layer, tg, q_ref, f_ref, v_ref, gt_ref, lb_ref, gn_ref, s0_ref,
                      o_ref, s_ref, o_scr):
    lb_h = _forget_lower_bound(lb_ref, layer)
    gn_h = gn_ref[...]

    q = jax.nn.silu(q_ref[...].astype(F32))
    f = lb_h + (1.0 - lb_h) * jax.nn.sigmoid(f_ref[...])
    decay = f
    k = 1.0 - f
    v = v_ref[...].astype(F32)
    d_t, k_t = decay.T, k.T
    q16 = q.astype(BF16)
    for j in range(tg):
        s = d_t[:, j:j + 1] * s0_ref[j, 0, 0] + k_t[:, j:j + 1] * v[j:j + 1, :]
        s_ref[j, 0, 0] = s
        o_scr[j:j + 1, :] = jnp.dot(q16[j:j + 1, :], s.astype(BF16), preferred_element_type=F32)
    o = _head_rmsnorm(o_scr[...]) * gn_h * jax.nn.silu(gt_ref[...].astype(F32))
    o_ref[...] = o.astype(BF16)


def _hgrn_sample(p16, p32, hgrn_lb, g_norm, state, layer, row0, tg):
    nb = state.shape[0]
    d = g_norm.shape[0]
    n_heads = d // HEAD
    rb0 = row0 // tg
    blk = lambda kk: pl.BlockSpec((tg, HEAD), lambda g, h, kk=kk: (rb0 + g, kk * n_heads + h))
    st_spec = pl.BlockSpec((tg, 1, 1, HEAD, HEAD), lambda g, h: (g, 0, h, 0, 0))
    return pl.pallas_call(
        functools.partial(_hgrn_sample_body, layer, tg),
        grid=(nb // tg, n_heads),
        in_specs=[blk(0), blk(0), blk(1), blk(2),
                  pl.BlockSpec((hgrn_lb.shape[0], HEAD), lambda g, h: (0, h)),
                  pl.BlockSpec((1, HEAD), lambda g, h: (0, h)),
                  st_spec],
        out_specs=[pl.BlockSpec((tg, HEAD), lambda g, h: (g, h)), st_spec],
        out_shape=[jax.ShapeDtypeStruct((nb, d), BF16),
                   jax.ShapeDtypeStruct(state.shape, F32)],
        scratch_shapes=[pltpu.VMEM((tg, HEAD), F32)],
        compiler_params=_params(2),
        name="hgrn_sample",
    )(p16, p32, p16, p16, hgrn_lb, g_norm.reshape(1, d), state)


_CARRY = 8


def _conv_prompt_body(tb, b_ref, c_ref, h_ref, w_ref, z_ref, buf_ref, u_scr):
    t = pl.program_id(1)

    @pl.when(t == 0)
    def _():
        u_scr[0:_CARRY, :] = jnp.zeros((_CARRY, u_scr.shape[1]), F32)

    u = c_ref[...].astype(F32) * h_ref[...].astype(F32)
    u_scr[_CARRY:_CARRY + tb, :] = u
    w = w_ref[...]
    y = w[0:1, :] * u_scr[_CARRY - 2:_CARRY - 2 + tb, :]
    y = y + w[1:2, :] * u_scr[_CARRY - 1:_CARRY - 1 + tb, :]
    y = y + w[2:3, :] * u
    z_ref[...] = (b_ref[...].astype(F32) * y).astype(BF16)
    u_scr[0:_CARRY, :] = u[tb - _CARRY:tb, :]

    @pl.when(t == pl.num_programs(1) - 1)
    def _():
        buf_ref[0] = u[tb - (CONV_W - 1):tb, :]


def _conv_prompt(pc, conv_w, batch, seq, tb):
    d = conv_w.shape[1]
    n_t = seq // tb
    blk = lambda kk: pl.BlockSpec((tb, d), lambda b, t, kk=kk: (b * n_t + t, kk))
    return pl.pallas_call(
        functools.partial(_conv_prompt_body, tb),
        grid=(batch, n_t),
        in_specs=[blk(0), blk(1), blk(2), pl.BlockSpec(conv_w.shape, lambda b, t: (0, 0))],
        out_specs=[pl.BlockSpec((tb, d), lambda b, t: (b * n_t + t, 0)),
                   pl.BlockSpec((1, CONV_W - 1, d), lambda b, t: (b, 0, 0))],
        out_shape=[jax.ShapeDtypeStruct((batch * seq, d), BF16),
                   jax.ShapeDtypeStruct((batch, CONV_W - 1, d), F32)],
        scratch_shapes=[pltpu.VMEM((_CARRY + tb, d), F32)],
        compiler_params=_params(2),
        name="conv_prompt",
    )(pc, pc, pc, conv_w)


def _conv_sample_body(d, b_ref, c_ref, h_ref, w_ref, st_ref, z_ref, buf_ref):
    u = c_ref[...].astype(F32) * h_ref[...].astype(F32)
    w = w_ref[...]
    buf0 = st_ref[:, 0:d]
    buf1 = st_ref[:, d:2 * d]
    y = w[0:1, :] * buf0
    y = y + w[1:2, :] * buf1
    y = y + w[2:3, :] * u
    z_ref[...] = (b_ref[...].astype(F32) * y).astype(BF16)
    buf_ref[:, 0:d] = buf1
    buf_ref[:, d:2 * d] = u


def _conv_sample(pc, conv_w, state2d, row0):
    nb = state2d.shape[0]
    d = conv_w.shape[1]
    rb0 = row0 // nb
    blk = lambda kk: pl.BlockSpec((nb, d), lambda i, kk=kk: (rb0, kk))
    return pl.pallas_call(
        functools.partial(_conv_sample_body, d),
        grid=(1,),
        in_specs=[blk(0), blk(1), blk(2), pl.BlockSpec(conv_w.shape, lambda i: (0, 0)),
                  pl.BlockSpec(state2d.shape, lambda i: (0, 0))],
        out_specs=[pl.BlockSpec((nb, d), lambda i: (0, 0)),
                   pl.BlockSpec(state2d.shape, lambda i: (0, 0))],
        out_shape=[jax.ShapeDtypeStruct((nb, d), BF16),
                   jax.ShapeDtypeStruct(state2d.shape, F32)],
        compiler_params=_params(1),
        name="conv_sample",
    )(pc, pc, pc, conv_w, state2d)


def _outproj_body(n_blocks, x_ref, z_ref, w_ref, o_ref, w16_ref):
    i = pl.program_id(0)

    @pl.when(i == 0)
    def _():
        w16_ref[...] = w_ref[...].astype(BF16)

    @pl.when(i < n_blocks)
    def _():
        o_ref[...] = x_ref[...] + jnp.dot(z_ref[...], w16_ref[...], preferred_element_type=F32)

    @pl.when(i >= n_blocks)
    def _():
        o_ref[...] = jnp.zeros_like(o_ref)


def _outproj_route_body(n_blocks, n_experts, x_ref, z_ref, w_ref, g_ref, rw_ref,
                        o_ref, route_ref, cnt_ref, w16_ref, cnt_scr):
    i = pl.program_id(0)

    @pl.when(i == 0)
    def _():
        w16_ref[...] = w_ref[...].astype(BF16)
        cnt_scr[...] = jnp.zeros_like(cnt_scr)

    @pl.when(i < n_blocks)
    def _():
        res = x_ref[...] + jnp.dot(z_ref[...], w16_ref[...], preferred_element_type=F32)
        o_ref[...] = res
        route_ref[...], cnt_scr[...] = _route_rows(res, g_ref[...], rw_ref[...], cnt_scr[...], n_experts)

    @pl.when(i >= n_blocks)
    def _():
        o_ref[...] = jnp.zeros_like(o_ref)
        route_ref[...] = jnp.zeros_like(route_ref)

    cnt_ref[...] = cnt_scr[...]


def _outproj_prompt(x, z, w, n_rows, tm, router=None):
    d = x.shape[1]
    n_blocks = n_rows // tm
    row_blk = lambda i: (jnp.minimum(i, n_blocks - 1), 0)
    in_specs = [pl.BlockSpec((tm, d), row_blk),
                pl.BlockSpec((tm, d), row_blk),
                pl.BlockSpec((d, d), lambda i: (0, 0))]
    out_rows = (n_blocks + 1) * tm
    if router is None:
        return pl.pallas_call(
            functools.partial(_outproj_body, n_blocks),
            grid=(n_blocks + 1,),
            in_specs=in_specs,
            out_specs=pl.BlockSpec((tm, d), lambda i: (i, 0)),
            out_shape=jax.ShapeDtypeStruct((out_rows, d), F32),
            scratch_shapes=[pltpu.VMEM((d, d), BF16)],
            compiler_params=_params(1),
            name="outproj",
        )(x, z, w)
    g, rw_pad, n_experts = router
    return pl.pallas_call(
        functools.partial(_outproj_route_body, n_blocks, n_experts),
        grid=(n_blocks + 1,),
        in_specs=in_specs + [pl.BlockSpec((1, d), lambda i: (0, 0)),
                             pl.BlockSpec((d, LANES), lambda i: (0, 0))],
        out_specs=[pl.BlockSpec((tm, d), lambda i: (i, 0)),
                   pl.BlockSpec((ROUTE_ROWS, tm), lambda i: (0, i)),
                   pl.BlockSpec((1, LANES), lambda i: (0, 0))],
        out_shape=[jax.ShapeDtypeStruct((out_rows, d), F32),
                   jax.ShapeDtypeStruct((ROUTE_ROWS, out_rows), F32),
                   jax.ShapeDtypeStruct((1, LANES), F32)],
        scratch_shapes=[pltpu.VMEM((d, d), BF16), pltpu.VMEM((1, LANES), F32)],
        compiler_params=_params(1),
        name="outproj_route",
    )(x, z, w, g.reshape(1, d), rw_pad)


def _outproj_sample_body(x_ref, z_ref, w_ref, buf_ref, o_ref):
    del buf_ref
    o_ref[...] = x_ref[...] + jnp.dot(z_ref[...], w_ref[...].astype(BF16), preferred_element_type=F32)


def _outproj_sample_route_body(n_experts, x_ref, z_ref, w_ref, g_ref, rw_ref, cnt_in_ref, buf_ref, route_in_ref,
                               o_ref, route_ref, cnt_ref):
    del buf_ref, route_in_ref
    res = x_ref[...] + jnp.dot(z_ref[...], w_ref[...].astype(BF16), preferred_element_type=F32)
    o_ref[...] = res
    route_ref[...], cnt_ref[...] = _route_rows(res, g_ref[...], rw_ref[...], cnt_in_ref[...], n_experts)


def _outproj_sample(x, x_row0, z, w, buf, row0, router=None):
    nb, d = z.shape
    in_specs = [pl.BlockSpec((nb, d), lambda i: (x_row0 // nb, 0)),
                pl.BlockSpec((nb, d), lambda i: (0, 0)),
                pl.BlockSpec((d, d), lambda i: (0, 0))]
    if router is None:
        return pl.pallas_call(
            _outproj_sample_body,
            grid=(1,),
            in_specs=in_specs + [pl.BlockSpec(memory_space=pl.ANY)],
            out_specs=pl.BlockSpec((nb, d), lambda i: (row0 // nb, 0)),
            out_shape=jax.ShapeDtypeStruct(buf.shape, buf.dtype),
            input_output_aliases={3: 0},
            compiler_params=_params(1),
            name="outproj_sample",
        )(x, z, w, buf)
    g, rw_pad, n_experts, route, counts = router
    return pl.pallas_call(
        functools.partial(_outproj_sample_route_body, n_experts),
        grid=(1,),
        in_specs=in_specs + [pl.BlockSpec((1, d), lambda i: (0, 0)),
                             pl.BlockSpec((d, LANES), lambda i: (0, 0)),
                             pl.BlockSpec((1, LANES), lambda i: (0, 0)),
                             pl.BlockSpec(memory_space=pl.ANY),
                             pl.BlockSpec(memory_space=pl.ANY)],
        out_specs=[pl.BlockSpec((nb, d), lambda i: (row0 // nb, 0)),
                   pl.BlockSpec((ROUTE_ROWS, nb), lambda i: (0, row0 // nb)),
                   pl.BlockSpec((1, LANES), lambda i: (0, 0))],
        out_shape=[jax.ShapeDtypeStruct(buf.shape, buf.dtype),
                   jax.ShapeDtypeStruct(route.shape, route.dtype),
                   jax.ShapeDtypeStruct((1, LANES), F32)],
        input_output_aliases={6: 0, 7: 1},
        compiler_params=_params(1),
        name="outproj_sample_route",
    )(x, z, w, g.reshape(1, d), rw_pad, counts, buf, route)


def _swiglu_partial(xn, w1, w3, w2):
    h1 = jnp.dot(xn, w1, preferred_element_type=F32)
    h3 = jnp.dot(xn, w3, preferred_element_type=F32)
    h = (jax.nn.silu(h1) * h3).astype(BF16)
    return jnp.dot(h, w2, preferred_element_type=F32)


def _cache_weight_chunk(w1_hbm, w3_hbm, w2_hbm, f, n_f, tf, stages, caches, wsem):
    n_slots = stages[0].shape[0]

    def copies(ff, ws):
        cols = pl.ds(pl.multiple_of(ff * tf, tf), tf)
        srcs = (w1_hbm.at[:, cols], w3_hbm.at[:, cols], w2_hbm.at[cols, :])
        return [pltpu.make_async_copy(src, st.at[ws], wsem.at[ws, m]) for m, (src, st) in enumerate(zip(srcs, stages))]

    if n_slots == 1:
        ws = 0
        for c in copies(f, 0):
            c.start()
    else:
        ws = lax.rem(f, 2)

        @pl.when(f == 0)
        def _():
            for c in copies(0, 0):
                c.start()

        @pl.when(f + 1 < n_f)
        def _():
            for c in copies(f + 1, 1 - ws):
                c.start()

    for c in copies(f, ws):
        c.wait()
    for st, ca in zip(stages, caches):
        ca[f] = st[ws].astype(BF16)


def _ffn_body(n_f, tf, x_ref, g_ref, w1_hbm, w3_hbm, w2_hbm, o_ref, xn_ref, acc_ref, c1, c3, c2, s1, s3, s2, wsem):
    i = pl.program_id(0)
    xn_ref[...] = _rmsnorm_bf16(x_ref[...], g_ref[...])
    acc_ref[...] = jnp.zeros_like(acc_ref)

    def chunk(f, carry):
        @pl.when(i == 0)
        def _():
            _cache_weight_chunk(w1_hbm, w3_hbm, w2_hbm, f, n_f, tf, (s1, s3, s2), (c1, c3, c2), wsem)

        acc_ref[...] += _swiglu_partial(xn_ref[...], c1[f], c3[f], c2[f])
        return carry

    lax.fori_loop(0, n_f, chunk, 0)
    o_ref[...] = x_ref[...] + acc_ref[...]


def _ffn(x, n, g, w1, w3, w2, tm, tf):
    d = x.shape[1]
    dff = w1.shape[1]
    n_f = dff // tf
    return pl.pallas_call(
        functools.partial(_ffn_body, n_f, tf),
        grid=(n // tm,),
        in_specs=[pl.BlockSpec((tm, d), lambda i: (i, 0)),
                  pl.BlockSpec((1, d), lambda i: (0, 0)),
                  pl.BlockSpec(memory_space=pl.ANY),
                  pl.BlockSpec(memory_space=pl.ANY),
                  pl.BlockSpec(memory_space=pl.ANY)],
        out_specs=pl.BlockSpec((tm, d), lambda i: (i, 0)),
        out_shape=jax.ShapeDtypeStruct((n, d), F32),
        scratch_shapes=[pltpu.VMEM((tm, d), BF16), pltpu.VMEM((tm, d), F32),
                        pltpu.VMEM((n_f, d, tf), BF16), pltpu.VMEM((n_f, d, tf), BF16),
                        pltpu.VMEM((n_f, tf, d), BF16),
                        pltpu.VMEM((1, d, tf), F32), pltpu.VMEM((1, d, tf), F32), pltpu.VMEM((1, tf, d), F32),
                        pltpu.SemaphoreType.DMA((1, 3))],
        compiler_params=_params(1),
        name="ffn",
    )(x, g.reshape(1, d), w1, w3, w2)


ROUTE_ROWS = 8


def _route_rows(x, g, rw, counts, n_experts):
    rows = x.shape[0]
    xn = _rmsnorm_bf16(x, g)
    logits = jnp.dot(xn, rw.astype(BF16), preferred_element_type=F32)
    lane = lax.broadcasted_iota(I32, (rows, LANES), 1).astype(F32)
    neg = jnp.float32(-jnp.inf)
    logits = jnp.where(lane < n_experts, logits, neg)
    m1 = jnp.max(logits, axis=-1, keepdims=True)
    i1 = jnp.min(jnp.where(logits == m1, lane, float(LANES)), axis=-1, keepdims=True)
    i1 = jnp.minimum(i1, float(n_experts - 1))
    rest = jnp.where(lane == i1, neg, logits)
    m2 = jnp.max(rest, axis=-1, keepdims=True)
    i2 = jnp.min(jnp.where(rest == m2, lane, float(LANES)), axis=-1, keepdims=True)
    i2 = jnp.minimum(i2, float(n_experts - 1))
    e2 = jnp.exp(m2 - m1)
    den = 1.0 + e2
    g1 = 1.0 / den
    g2 = e2 / den

    sel1 = lane == i1
    sel2 = lane == i2
    onehot = jnp.logical_or(sel1, sel2)
    row = lax.broadcasted_iota(I32, (rows, rows), 0)
    col = lax.broadcasted_iota(I32, (rows, rows), 1)
    before = (row > col).astype(BF16)
    rank = jnp.dot(before, onehot.astype(BF16), preferred_element_type=F32) + counts
    r1 = jnp.sum(jnp.where(sel1, rank, 0.0), axis=-1, keepdims=True)
    r2 = jnp.sum(jnp.where(sel2, rank, 0.0), axis=-1, keepdims=True)

    out = jnp.zeros((rows, LANES), F32)
    for slot, val in enumerate((i1, i2, g1, g2, r1, r2)):
        out = jnp.where(lane == slot, val, out)
    return out.T[0:ROUTE_ROWS, :], counts + jnp.sum(onehot.astype(F32), axis=0, keepdims=True)


SUBLANES = 8
MOE_ROWS_PER_CHUNK = 96


def _start_row_group(src_hbm, idx_ref, g, dst, sem):
    for k in range(SUBLANES):
        pltpu.make_async_copy(src_hbm.at[pl.ds(idx_ref[0, 0, g * SUBLANES + k], 1)],
                              dst.at[g, pl.ds(k, 1)], sem).start(priority=k % 2)


def _start_row_gather(src_hbm, idx_ref, n_rows, dst, sem, unrolled=False):
    if unrolled:
        for g in range(n_rows // SUBLANES):
            _start_row_group(src_hbm, idx_ref, g, dst, sem)
        return

    def body(g, carry):
        _start_row_group(src_hbm, idx_ref, g, dst, sem)
        return carry

    lax.fori_loop(0, n_rows // SUBLANES, body, 0)


def _wait_row_gather(src_hbm, n_rows, dst, sem):
    pltpu.make_async_copy(src_hbm.at[pl.ds(0, n_rows)], dst.reshape(n_rows, dst.shape[-1]), sem).wait()


def _moe_body(n_f, tf, te_ref, nu_ref, hf_ref, idx_cur, idx_nxt, x_hbm, gate_ref, g_ref, w1_hbm, w3_hbm, w2_hbm,
              y_ref, xbuf, sem, xn_ref, acc_ref, c1, c3, c2, s1, s3, s2, wsem):
    i = pl.program_id(0)
    n_tiles = pl.num_programs(0)
    tm = xbuf.shape[1] * SUBLANES
    groups_per_chunk = tm // n_f // SUBLANES
    slot = lax.rem(i, 2)
    n_used = nu_ref[0]
    used = i < n_used
    e = te_ref[i]
    first_of_expert = jnp.logical_or(i == 0, e != te_ref[jnp.maximum(i - 1, 0)])
    half_filled = hf_ref[i] == 1

    @pl.when(i == 0)
    def _():
        _start_row_gather(x_hbm, idx_cur, tm, xbuf.at[0], sem.at[0])

    @pl.when(i <= n_used)
    def _():
        _wait_row_gather(x_hbm, tm, xbuf.at[slot], sem.at[slot])

    @pl.when(used)
    def _():
        xn_ref[...] = _rmsnorm_bf16(xbuf[slot].reshape(tm, xbuf.shape[-1]), g_ref[...])
        acc_ref[...] = jnp.zeros_like(acc_ref)

        def chunk(f, carry):
            @pl.when(first_of_expert)
            def _():
                _cache_weight_chunk(w1_hbm.at[e], w3_hbm.at[e], w2_hbm.at[e], f, n_f, tf,
                                    (s1, s3, s2), (c1, c3, c2), wsem)

            def compute(rows):
                for g in range(groups_per_chunk):
                    _start_row_group(x_hbm, idx_nxt, f * groups_per_chunk + g, xbuf.at[1 - slot], sem.at[1 - slot])
                acc_ref[0:rows, :] += _swiglu_partial(xn_ref[0:rows, :], c1[f], c3[f], c2[f])

            @pl.when(half_filled)
            def _():
                compute(tm // 2)

            @pl.when(jnp.logical_not(half_filled))
            def _():
                compute(tm)

            return carry

        lax.fori_loop(0, n_f, chunk, 0)
        r_id = lax.broadcasted_iota(I32, (tm, tm), 0)
        c_id = lax.broadcasted_iota(I32, (tm, tm), 1)
        gate_col = jnp.sum(jnp.where(r_id == c_id, gate_ref[0], 0.0), axis=1, keepdims=True)
        y_ref[...] = acc_ref[...] * gate_col

    @pl.when(jnp.logical_not(used))
    def _():
        y_ref[...] = jnp.zeros_like(y_ref)

    @pl.when(jnp.logical_and(used, i == n_tiles - 1))
    def _():
        _wait_row_gather(x_hbm, tm, xbuf.at[1 - slot], sem.at[1 - slot])


def _moe(x, g, w1, w3, w2, tile_expert, n_used, half_filled, inv3, gate_sorted, tm, tf):
    n_tiles = inv3.shape[0]
    d = x.shape[1]
    dff = w1.shape[2]
    n_f = dff // tf
    assert tm % (n_f * SUBLANES) == 0, "each hidden chunk fetches whole 8-row tiles of the next row tile"
    assert tm % 32 == 0, "half a row tile must be whole bf16 sublane tiles"
    grid_spec = pltpu.PrefetchScalarGridSpec(
        num_scalar_prefetch=3,
        grid=(n_tiles,),
        in_specs=[pl.BlockSpec((1, 1, tm), lambda i, te, nu, hf: (i, 0, 0), memory_space=pltpu.SMEM),
                  pl.BlockSpec((1, 1, tm), lambda i, te, nu, hf: (jnp.minimum(i + 1, n_tiles - 1), 0, 0),
                               memory_space=pltpu.SMEM),
                  pl.BlockSpec(memory_space=pl.ANY),
                  pl.BlockSpec((1, 1, tm), lambda i, te, nu, hf: (i, 0, 0)),
                  pl.BlockSpec((1, d), lambda i, te, nu, hf: (0, 0)),
                  pl.BlockSpec(memory_space=pl.ANY),
                  pl.BlockSpec(memory_space=pl.ANY),
                  pl.BlockSpec(memory_space=pl.ANY)],
        out_specs=pl.BlockSpec((tm, d), lambda i, te, nu, hf: (i, 0)),
        scratch_shapes=[pltpu.VMEM((2, tm // SUBLANES, SUBLANES, d), F32),
                        pltpu.SemaphoreType.DMA((2,)),
                        pltpu.VMEM((tm, d), BF16),
                        pltpu.VMEM((tm, d), F32),
                        pltpu.VMEM((n_f, d, tf), BF16),
                        pltpu.VMEM((n_f, d, tf), BF16),
                        pltpu.VMEM((n_f, tf, d), BF16),
                        pltpu.VMEM((2, d, tf), F32),
                        pltpu.VMEM((2, d, tf), F32),
                        pltpu.VMEM((2, tf, d), F32),
                        pltpu.SemaphoreType.DMA((2, 3))],
    )
    return pl.pallas_call(
        functools.partial(_moe_body, n_f, tf),
        grid_spec=grid_spec,
        out_shape=jax.ShapeDtypeStruct((n_tiles * tm, d), F32),
        compiler_params=_params(1),
        name="moe",
    )(tile_expert, n_used, half_filled, inv3, inv3, x, gate_sorted, g.reshape(1, d), w1, w3, w2)


def _combine_body(pos_cur, pos_nxt, x_ref, g_ref, y_hbm, o_ref, ybuf, sem):
    i = pl.program_id(0)
    n_tiles = pl.num_programs(0)
    rows = ybuf.shape[1] * SUBLANES
    slot = lax.rem(i, 2)

    @pl.when(i == 0)
    def _():
        _start_row_gather(y_hbm, pos_cur, rows, ybuf.at[0], sem.at[0])

    @pl.when(i + 1 < n_tiles)
    def _():
        _start_row_gather(y_hbm, pos_nxt, rows, ybuf.at[1 - slot], sem.at[1 - slot], unrolled=True)

    _wait_row_gather(y_hbm, rows, ybuf.at[slot], sem.at[slot])
    tc = rows // TOP_K
    y = ybuf[slot].reshape(rows, ybuf.shape[-1])
    moe = y[0:tc, :] + y[tc:rows, :]
    x = x_ref[...] + moe
    ms = jnp.mean(x * x, axis=-1, keepdims=True)
    o_ref[...] = x * lax.rsqrt(ms + EPS) * g_ref[...]


def _combine(x, g, y_sorted, pos3, row0, n_rows, tc):
    d = x.shape[1]
    n_tiles = n_rows // tc
    rb0 = row0 // tc
    grid_spec = pltpu.PrefetchScalarGridSpec(
        num_scalar_prefetch=0,
        grid=(n_tiles,),
        in_specs=[pl.BlockSpec((1, 1, TOP_K * tc), lambda i: (i, 0, 0), memory_space=pltpu.SMEM),
                  pl.BlockSpec((1, 1, TOP_K * tc), lambda i: (jnp.minimum(i + 1, n_tiles - 1), 0, 0),
                               memory_space=pltpu.SMEM),
                  pl.BlockSpec((tc, d), lambda i: (rb0 + i, 0)),
                  pl.BlockSpec((1, d), lambda i: (0, 0)),
                  pl.BlockSpec(memory_space=pl.ANY)],
        out_specs=pl.BlockSpec((tc, d), lambda i: (i, 0)),
        scratch_shapes=[pltpu.VMEM((2, TOP_K * tc // SUBLANES, SUBLANES, d), F32), pltpu.SemaphoreType.DMA((2,))],
    )
    return pl.pallas_call(
        _combine_body,
        grid_spec=grid_spec,
        out_shape=jax.ShapeDtypeStruct((n_rows, d), F32),
        compiler_params=_params(1),
        name="combine",
    )(pos3, pos3, x, g.reshape(1, d), y_sorted)


def _lookup(table, idx):
    out = jnp.zeros(idx.shape, table.dtype)
    for e in range(table.shape[0]):
        out = jnp.where(idx == e, table[e], out)
    return out


def _expert_layer(x, route, cnt, n_experts, g_ffn, g_final, w1, w3, w2, n_prompt, n_dec, tm_moe, tf, tc):
    n = n_prompt + n_dec
    ids = route[0:TOP_K, :n].astype(I32)
    gates = route[TOP_K:2 * TOP_K, :n]
    ranks = route[2 * TOP_K:3 * TOP_K, :n].astype(I32)
    counts = cnt[0, :n_experts].astype(I32)

    n_tiles = (TOP_K * n + n_experts * (tm_moe - 1)) // tm_moe
    padded = ((counts + tm_moe - 1) // tm_moe) * tm_moe
    ends = jnp.cumsum(padded)
    starts = ends - padded
    cstart = jnp.cumsum(counts) - counts
    pos = _lookup(starts, ids) + ranks
    keys = ids * n + jnp.arange(n, dtype=I32)[None, :]
    order = jnp.argsort(keys.reshape(-1)).astype(I32)
    slot_pos = jnp.arange(n_tiles * tm_moe, dtype=I32)
    slot_e = jnp.minimum(jnp.sum(slot_pos[None, :] >= ends[:, None], axis=0), n_experts - 1).astype(I32)
    slot_rank = slot_pos - _lookup(starts, slot_e)
    slot_valid = slot_rank < _lookup(counts, slot_e)
    slot_asg = order[jnp.clip(_lookup(cstart, slot_e) + slot_rank, 0, TOP_K * n - 1)]
    inv = jnp.where(slot_valid, slot_asg % n, 0).astype(I32)
    gate_sorted = jnp.where(slot_valid, gates.reshape(-1)[slot_asg], 0.0)
    tile_expert = slot_e[::tm_moe]
    n_used = (ends[-1] // tm_moe).astype(I32).reshape(1)

    tile_start = jnp.arange(n_tiles, dtype=I32) * tm_moe
    tile_rows = _lookup(counts, tile_expert) - (tile_start - _lookup(starts, tile_expert))
    half_filled = (tile_rows <= tm_moe // 2).astype(I32)

    y_sorted = _moe(x, g_ffn, w1, w3, w2, tile_expert, n_used, half_filled, inv.reshape(n_tiles, 1, tm_moe),
                    gate_sorted.reshape(n_tiles, 1, tm_moe), tm_moe, tf)

    def tile_positions(row0, n_rows, t):
        p = pos[:, row0:row0 + n_rows].reshape(TOP_K, n_rows // t, t)
        return jnp.swapaxes(p, 0, 1).reshape(n_rows // t, 1, TOP_K * t)

    y_prompt = _combine(x, g_final, y_sorted, tile_positions(0, n_prompt, tc), 0, n_prompt, tc)
    y_sample = _combine(x, g_final, y_sorted, tile_positions(n_prompt, n_dec, n_dec), n_prompt, n_dec, n_dec)
    return y_prompt, y_sample


def kernel(x_prompt, x_sample, state_hgrn, state_conv, norm_mix, norm_ffn, norm_final, hgrn_w_in, hgrn_lb, hgrn_g_norm, hgrn_w_out, conv_w_in, conv_w, conv_w_out, ffn_w1, ffn_w3, ffn_w2, moe_router, moe_w1, moe_w3, moe_w2):
    batch, seq, d = x_prompt.shape
    n_dec = x_sample.shape[0]
    n_prompt = batch * seq
    n = n_prompt + n_dec
    n_experts = moe_router.shape[-1]
    assert x_sample.shape[1] == 1 and d % HEAD == 0 and seq % CHUNK == 0
    assert norm_mix.shape[0] == 2, "one HGRN2 layer followed by one short-conv layer"

    tm = _pick_tile(n, (688, 384, 128, 16))
    tm_proj = _pick_tile(n, (1376, 688, 384, 128, 16))
    tm_prompt = _pick_tile(n_prompt, (1024, 512, 256, 128, 16))
    tb = _pick_tile(seq, (256, 128, 64, 32))
    tb_conv = _pick_tile(seq, (1024, 512, 256, 128, 64, 32))
    tg = _pick_tile(n_dec, (16,))
    tf = _pick_tile(ffn_w1.shape[-1], (512, 256, 128))
    tm_moe = (moe_w1.shape[-1] // tf) * MOE_ROWS_PER_CHUNK
    tc = _pick_tile(n_prompt, (512, 128, 16))
    assert n_prompt % tg == 0 and n_prompt % n_dec == 0 and n_prompt % tc == 0

    xp = x_prompt.reshape(n_prompt, d)
    xs = x_sample.reshape(n_dec, d)

    p16, p32 = _proj(xp, norm_mix[0], hgrn_w_in[0], tm_prompt, f32_col=1)
    p16_s, p32_s = _proj(xs, norm_mix[0], hgrn_w_in[0], n_dec, f32_col=1)
    o_p, s_prompt = _hgrn_prompt(p16, p32, hgrn_lb, hgrn_g_norm[0], 0, batch, seq, tb)
    o_s, s_sample = _hgrn_sample(p16_s, p32_s, hgrn_lb, hgrn_g_norm[0], state_hgrn, 0, 0, tg)
    x = _outproj_prompt(xp, o_p, hgrn_w_out[0], n_prompt, tm_prompt)
    x = _outproj_sample(xs, 0, o_s, hgrn_w_out[0], x, n_prompt)
    x = _ffn(x, n, norm_ffn[0], ffn_w1[0], ffn_w3[0], ffn_w2[0], tm, tf)

    (pc,) = _proj(x, norm_mix[1], conv_w_in[0], tm_proj)
    z_p, c_prompt = _conv_prompt(pc, conv_w[0], batch, seq, tb_conv)
    z_s, c_sample = _conv_sample(pc, conv_w[0], state_conv.reshape(n_dec, (CONV_W - 1) * d), n_prompt)
    rw_pad = jnp.pad(moe_router[0], ((0, 0), (0, LANES - n_experts)))
    x1, route, cnt = _outproj_prompt(x, z_p, conv_w_out[0], n_prompt, tm_prompt, router=(norm_ffn[1], rw_pad, n_experts))
    x1, route, cnt = _outproj_sample(x, n_prompt, z_s, conv_w_out[0], x1, n_prompt,
                                     router=(norm_ffn[1], rw_pad, n_experts, route, cnt))

    y_prompt, y_sample = _expert_layer(x1, route, cnt, n_experts, norm_ffn[1], norm_final, moe_w1[0], moe_w3[0],
                                       moe_w2[0], n_prompt, n_dec, tm_moe, tf, tc)

    return (y_prompt.reshape(batch, seq, d),
            y_sample.reshape(n_dec, 1, d),
            s_prompt,
            s_sample,
            c_prompt.reshape(batch, 1, CONV_W - 1, d),
            c_sample.reshape(n_dec, 1, CONV_W - 1, d))
```

```python
import functools

import jax
import jax.numpy as jnp
from jax import lax
from jax.experimental import pallas as pl
from jax.experimental.pallas import tpu as pltpu

F32 = jnp.float32
BF16 = jnp.bfloat16
I32 = jnp.int32

EPS = 1e-6
HEAD = 128
CHUNK = 32
CONV_W = 3
TOP_K = 2
LANES = 128
V7X_VMEM_LIMIT = 56 * 1024 * 1024

ARB = "arbitrary"


def _params(n_axes):
    return pltpu.CompilerParams(dimension_semantics=(ARB,) * n_axes, vmem_limit_bytes=V7X_VMEM_LIMIT)


def _pick_tile(n, candidates):
    for c in candidates:
        if n % c == 0:
            return c
    raise ValueError(f"no tile in {candidates} divides {n}")


def _rmsnorm_bf16(x, g):
    ms = jnp.mean(x * x, axis=-1, keepdims=True)
    return (x * lax.rsqrt(ms + EPS) * g).astype(BF16)


def _proj_body(n_col, f32_col, x_ref, g_ref, w_hbm, *refs):
    o16_ref = refs[0]
    w16_ref, stage_ref, wsem = refs[-3:]
    d = x_ref.shape[1]

    @pl.when(pl.program_id(0) == 0)
    def _():
        for j in range(n_col):
            cp = pltpu.make_async_copy(w_hbm.at[:, j * d:(j + 1) * d], stage_ref, wsem)
            cp.start()
            cp.wait()
            w16_ref[j] = stage_ref[...].astype(BF16)

    xn = _rmsnorm_bf16(x_ref[...], g_ref[...])
    j16 = 0
    for j in range(n_col):
        res = jnp.dot(xn, w16_ref[j], preferred_element_type=F32)
        if j == f32_col:
            refs[1][...] = res
        else:
            o16_ref[:, j16 * d:(j16 + 1) * d] = res.astype(BF16)
            j16 += 1


def _proj(x, g, w, tm, f32_col=None):
    n, d = x.shape
    n_col = w.shape[1] // d
    n16 = n_col - (f32_col is not None)
    out_specs = [pl.BlockSpec((tm, n16 * d), lambda i: (i, 0))]
    out_shape = [jax.ShapeDtypeStruct((n, n16 * d), BF16)]
    if f32_col is not None:
        out_specs.append(pl.BlockSpec((tm, d), lambda i: (i, 0)))
        out_shape.append(jax.ShapeDtypeStruct((n, d), F32))
    return pl.pallas_call(
        functools.partial(_proj_body, n_col, f32_col),
        grid=(n // tm,),
        in_specs=[pl.BlockSpec((tm, d), lambda i: (i, 0)),
                  pl.BlockSpec((1, d), lambda i: (0, 0)),
                  pl.BlockSpec(memory_space=pl.ANY)],
        out_specs=out_specs,
        out_shape=out_shape,
        scratch_shapes=[pltpu.VMEM((n_col, d, d), BF16), pltpu.VMEM((d, d), F32), pltpu.SemaphoreType.DMA(())],
        compiler_params=_params(1),
        name="proj",
    )(x, g.reshape(1, d), w)


def _forget_lower_bound(lb_ref, layer):
    lbw = lb_ref[...]
    e = jnp.exp(lbw - jnp.max(lbw, axis=0, keepdims=True))
    sm = e / jnp.sum(e, axis=0, keepdims=True)
    return jnp.sum(sm[:layer + 1], axis=0, keepdims=True)


def _head_rmsnorm(o):
    return o * lax.rsqrt(jnp.mean(o * o, axis=-1, keepdims=True) + EPS)


def _sigmoid(x):
    return 0.5 * (jnp.tanh(0.5 * x) + 1.0)


def _silu(x):
    h = 0.5 * x
    return h + h * jnp.tanh(h)


def _split3_bf16(x):
    hi = x.astype(BF16)
    r1 = x - hi.astype(F32)
    mid = r1.astype(BF16)
    lo = (r1 - mid.astype(F32)).astype(BF16)
    return hi, mid, lo


def _hgrn_prompt_body(layer, n_heads, tb, q_ref, f_ref, v_ref, gt_ref, lb_ref, gn_ref,
                      o_ref, s_ref, st_scr, qin_scr, kin_scr, kdec_scr, g_scr, sprev_scr):
    t = pl.program_id(1)
    nc = tb // CHUNK
    d = n_heads * HEAD

    @pl.when(t == 0)
    def _():
        st_scr[...] = jnp.zeros_like(st_scr)

    lb = _forget_lower_bound(lb_ref, layer)
    gn = gn_ref[...]
    row = lax.broadcasted_iota(I32, (tb, tb), 0)
    col = lax.broadcasted_iota(I32, (tb, tb), 1)
    same_chunk = (row // CHUNK) == (col // CHUNK)
    causal = jnp.logical_and(same_chunk, row >= col)

    f = lb + (1.0 - lb) * _sigmoid(f_ref[...])
    logf = jnp.log2(f)
    k = 1.0 - f
    tri = causal.astype(BF16)
    G = None
    for part in _split3_bf16(logf):
        term = jnp.dot(tri, part, preferred_element_type=F32)
        G = term if G is None else G + term
    g_scr[...] = G
    g_last = [g_scr[(c + 1) * CHUNK - 1:(c + 1) * CHUNK, :] for c in range(nc)]
    decay = [jnp.exp2(g) for g in g_last]
    decay_rows = jnp.concatenate([jnp.broadcast_to(dc, (CHUNK, d)) for dc in decay], axis=0)
    k_in = k * jnp.exp2(-G)
    qin_scr[...] = (_silu(q_ref[...].astype(F32)) * jnp.exp2(G)).astype(BF16)
    kin_scr[...] = k_in.astype(BF16)
    kdec_scr[...] = (k_in * decay_rows).astype(BF16)

    row_chunk = lax.broadcasted_iota(I32, (tb, HEAD), 0) // CHUNK
    zero16 = jnp.zeros((), BF16)
    grp = next(g for g in (4, 2, 1) if nc % g == 0)

    for h in range(n_heads):
        hs = slice(h * HEAD, (h + 1) * HEAD)
        qh = qin_scr[:, hs]
        kd = kdec_scr[:, hs]
        vh = v_ref[:, hs]
        a = lax.dot_general(qh, kin_scr[:, hs], (((1,), (1,)), ((), ())), preferred_element_type=F32)
        a = jnp.where(causal, a, 0.0).astype(BF16)
        o = jnp.dot(a, vh, preferred_element_type=F32)
        ut = jnp.concatenate(
            [lax.dot_general(vh[c * CHUNK:(c + 1) * CHUNK], kd[c * CHUNK:(c + 1) * CHUNK],
                             (((0,), (0,)), ((), ())), preferred_element_type=F32) for c in range(nc)], axis=0)
        st = st_scr[h]
        for c in range(nc):
            sprev_scr[h, :, c * HEAD:(c + 1) * HEAD] = st.astype(BF16)
            st = st * decay[c][:, hs] + ut[c * HEAD:(c + 1) * HEAD, :]
        st_scr[h] = st
        inter = []
        for g0 in range(0, nc, grp):
            rows = slice(g0 * CHUNK, (g0 + grp) * CHUNK)
            q_blocks = jnp.concatenate(
                [jnp.where(row_chunk[rows] == g0 + j, qh[rows], zero16) for j in range(grp)], axis=1)
            inter.append(lax.dot_general(q_blocks, sprev_scr[h, :, g0 * HEAD:(g0 + grp) * HEAD],
                                         (((1,), (1,)), ((), ())), preferred_element_type=F32))
        o = o + jnp.concatenate(inter, axis=0)
        o = _head_rmsnorm(o) * gn[:, hs] * _silu(gt_ref[:, hs].astype(F32))
        o_ref[:, hs] = o.astype(BF16)

    @pl.when(t == pl.num_programs(1) - 1)
    def _():
        for h in range(n_heads):
            s_ref[0, 0, h] = st_scr[h].T


def _hgrn_prompt(p16, p32, hgrn_lb, g_norm, layer, batch, seq, tb):
    d = g_norm.shape[0]
    n_heads = d // HEAD
    n_t = seq // tb
    blk = lambda kk: pl.BlockSpec((tb, d), lambda b, t, kk=kk: (b * n_t + t, kk))
    return pl.pallas_call(
        functools.partial(_hgrn_prompt_body, layer, n_heads, tb),
        grid=(batch, n_t),
        in_specs=[blk(0), blk(0), blk(1), blk(2),
                  pl.BlockSpec(hgrn_lb.shape, lambda b, t: (0, 0)),
                  pl.BlockSpec((1, d), lambda b, t: (0, 0))],
        out_specs=[pl.BlockSpec((tb, d), lambda b, t: (b * n_t + t, 0)),
                   pl.BlockSpec((1, 1, n_heads, HEAD, HEAD), lambda b, t: (b, 0, 0, 0, 0))],
        out_shape=[jax.ShapeDtypeStruct((batch * seq, d), BF16),
                   jax.ShapeDtypeStruct((batch, 1, n_heads, HEAD, HEAD), F32)],
        scratch_shapes=[pltpu.VMEM((n_heads, HEAD, HEAD), F32),
                        pltpu.VMEM((tb, d), BF16), pltpu.VMEM((tb, d), BF16), pltpu.VMEM((tb, d), BF16),
                        pltpu.VMEM((tb, d), F32),
                        pltpu.VMEM((n_heads, HEAD, (tb // CHUNK) * HEAD), BF16)],
        compiler_params=_params(2),
        name="hgrn_prompt",
    )(p16, p32, p16, p16, hgrn_lb, g_norm.reshape(1, d))


def _hgrn_sample_body(layer, tg, q_ref, f_ref, v_ref, gt_ref, lb_ref, gn_ref, s0_ref,
                      o_ref, s_ref, o_scr):
    lb_h = _forget_lower_bound(lb_ref, layer)
    gn_h = gn_ref[...]

    q = jax.nn.silu(q_ref[...].astype(F32))
    f = lb_h + (1.0 - lb_h) * jax.nn.sigmoid(f_ref[...])
    decay = f
    k = 1.0 - f
    v = v_ref[...].astype(F32)
    d_t, k_t = decay.T, k.T
    q16 = q.astype(BF16)
    for j in range(tg):
        s = d_t[:, j:j + 1] * s0_ref[j, 0, 0] + k_t[:, j:j + 1] * v[j:j + 1, :]
        s_ref[j, 0, 0] = s
        o_scr[j:j + 1, :] = jnp.dot(q16[j:j + 1, :], s.astype(BF16), preferred_element_type=F32)
    o = _head_rmsnorm(o_scr[...]) * gn_h * jax.nn.silu(gt_ref[...].astype(F32))
    o_ref[...] = o.astype(BF16)


def _hgrn_sample(p16, p32, hgrn_lb, g_norm, state, layer, row0, tg):
    nb = state.shape[0]
    d = g_norm.shape[0]
    n_heads = d // HEAD
    rb0 = row0 // tg
    blk = lambda kk: pl.BlockSpec((tg, HEAD), lambda g, h, kk=kk: (rb0 + g, kk * n_heads + h))
    st_spec = pl.BlockSpec((tg, 1, 1, HEAD, HEAD), lambda g, h: (g, 0, h, 0, 0))
    return pl.pallas_call(
        functools.partial(_hgrn_sample_body, layer, tg),
        grid=(nb // tg, n_heads),
        in_specs=[blk(0), blk(0), blk(1), blk(2),
                  pl.BlockSpec((hgrn_lb.shape[0], HEAD), lambda g, h: (0, h)),
                  pl.BlockSpec((1, HEAD), lambda g, h: (0, h)),
                  st_spec],
        out_specs=[pl.BlockSpec((tg, HEAD), lambda g, h: (g, h)), st_spec],
        out_shape=[jax.ShapeDtypeStruct((nb, d), BF16),
                   jax.ShapeDtypeStruct(state.shape, F32)],
        scratch_shapes=[pltpu.VMEM((tg, HEAD), F32)],
        compiler_params=_params(2),
        name="hgrn_sample",
    )(p16, p32, p16, p16, hgrn_lb, g_norm.reshape(1, d), state)


_CARRY = 8


def _conv_prompt_body(tb, b_ref, c_ref, h_ref, w_ref, z_ref, buf_ref, u_scr):
    t = pl.program_id(1)

    @pl.when(t == 0)
    def _():
        u_scr[0:_CARRY, :] = jnp.zeros((_CARRY, u_scr.shape[1]), F32)

    u = c_ref[...].astype(F32) * h_ref[...].astype(F32)
    u_scr[_CARRY:_CARRY + tb, :] = u
    w = w_ref[...]
    y = w[0:1, :] * u_scr[_CARRY - 2:_CARRY - 2 + tb, :]
    y = y + w[1:2, :] * u_scr[_CARRY - 1:_CARRY - 1 + tb, :]
    y = y + w[2:3, :] * u
    z_ref[...] = (b_ref[...].astype(F32) * y).astype(BF16)
    u_scr[0:_CARRY, :] = u[tb - _CARRY:tb, :]

    @pl.when(t == pl.num_programs(1) - 1)
    def _():
        buf_ref[0] = u[tb - (CONV_W - 1):tb, :]


def _conv_prompt(pc, conv_w, batch, seq, tb):
    d = conv_w.shape[1]
    n_t = seq // tb
    blk = lambda kk: pl.BlockSpec((tb, d), lambda b, t, kk=kk: (b * n_t + t, kk))
    return pl.pallas_call(
        functools.partial(_conv_prompt_body, tb),
        grid=(batch, n_t),
        in_specs=[blk(0), blk(1), blk(2), pl.BlockSpec(conv_w.shape, lambda b, t: (0, 0))],
        out_specs=[pl.BlockSpec((tb, d), lambda b, t: (b * n_t + t, 0)),
                   pl.BlockSpec((1, CONV_W - 1, d), lambda b, t: (b, 0, 0))],
        out_shape=[jax.ShapeDtypeStruct((batch * seq, d), BF16),
                   jax.ShapeDtypeStruct((batch, CONV_W - 1, d), F32)],
        scratch_shapes=[pltpu.VMEM((_CARRY + tb, d), F32)],
        compiler_params=_params(2),
        name="conv_prompt",
    )(pc, pc, pc, conv_w)


def _conv_sample_body(d, b_ref, c_ref, h_ref, w_ref, st_ref, z_ref, buf_ref):
    u = c_ref[...].astype(F32) * h_ref[...].astype(F32)
    w = w_ref[...]
    buf0 = st_ref[:, 0:d]
    buf1 = st_ref[:, d:2 * d]
    y = w[0:1, :] * buf0
    y = y + w[1:2, :] * buf1
    y = y + w[2:3, :] * u
    z_ref[...] = (b_ref[...].astype(F32) * y).astype(BF16)
    buf_ref[:, 0:d] = buf1
    buf_ref[:, d:2 * d] = u


def _conv_sample(pc, conv_w, state2d, row0):
    nb = state2d.shape[0]
    d = conv_w.shape[1]
    rb0 = row0 // nb
    blk = lambda kk: pl.BlockSpec((nb, d), lambda i, kk=kk: (rb0, kk))
    return pl.pallas_call(
        functools.partial(_conv_sample_body, d),
        grid=(1,),
        in_specs=[blk(0), blk(1), blk(2), pl.BlockSpec(conv_w.shape, lambda i: (0, 0)),
                  pl.BlockSpec(state2d.shape, lambda i: (0, 0))],
        out_specs=[pl.BlockSpec((nb, d), lambda i: (0, 0)),
                   pl.BlockSpec(state2d.shape, lambda i: (0, 0))],
        out_shape=[jax.ShapeDtypeStruct((nb, d), BF16),
                   jax.ShapeDtypeStruct(state2d.shape, F32)],
        compiler_params=_params(1),
        name="conv_sample",
    )(pc, pc, pc, conv_w, state2d)


def _outproj_body(n_blocks, x_ref, z_ref, w_ref, o_ref, w16_ref):
    i = pl.program_id(0)

    @pl.when(i == 0)
    def _():
        w16_ref[...] = w_ref[...].astype(BF16)

    @pl.when(i < n_blocks)
    def _():
        o_ref[...] = x_ref[...] + jnp.dot(z_ref[...], w16_ref[...], preferred_element_type=F32)

    @pl.when(i >= n_blocks)
    def _():
        o_ref[...] = jnp.zeros_like(o_ref)


def _outproj_route_body(n_blocks, n_experts, x_ref, z_ref, w_ref, g_ref, rw_ref,
                        o_ref, route_ref, cnt_ref, w16_ref, cnt_scr):
    i = pl.program_id(0)

    @pl.when(i == 0)
    def _():
        w16_ref[...] = w_ref[...].astype(BF16)
        cnt_scr[...] = jnp.zeros_like(cnt_scr)

    @pl.when(i < n_blocks)
    def _():
        res = x_ref[...] + jnp.dot(z_ref[...], w16_ref[...], preferred_element_type=F32)
        o_ref[...] = res
        route_ref[...], cnt_scr[...] = _route_rows(res, g_ref[...], rw_ref[...], cnt_scr[...], n_experts)

    @pl.when(i >= n_blocks)
    def _():
        o_ref[...] = jnp.zeros_like(o_ref)
        route_ref[...] = jnp.zeros_like(route_ref)

    cnt_ref[...] = cnt_scr[...]


def _outproj_prompt(x, z, w, n_rows, tm, router=None):
    d = x.shape[1]
    n_blocks = n_rows // tm
    row_blk = lambda i: (jnp.minimum(i, n_blocks - 1), 0)
    in_specs = [pl.BlockSpec((tm, d), row_blk),
                pl.BlockSpec((tm, d), row_blk),
                pl.BlockSpec((d, d), lambda i: (0, 0))]
    out_rows = (n_blocks + 1) * tm
    if router is None:
        return pl.pallas_call(
            functools.partial(_outproj_body, n_blocks),
            grid=(n_blocks + 1,),
            in_specs=in_specs,
            out_specs=pl.BlockSpec((tm, d), lambda i: (i, 0)),
            out_shape=jax.ShapeDtypeStruct((out_rows, d), F32),
            scratch_shapes=[pltpu.VMEM((d, d), BF16)],
            compiler_params=_params(1),
            name="outproj",
        )(x, z, w)
    g, rw_pad, n_experts = router
    return pl.pallas_call(
        functools.partial(_outproj_route_body, n_blocks, n_experts),
        grid=(n_blocks + 1,),
        in_specs=in_specs + [pl.BlockSpec((1, d), lambda i: (0, 0)),
                             pl.BlockSpec((d, LANES), lambda i: (0, 0))],
        out_specs=[pl.BlockSpec((tm, d), lambda i: (i, 0)),
                   pl.BlockSpec((ROUTE_ROWS, tm), lambda i: (0, i)),
                   pl.BlockSpec((1, LANES), lambda i: (0, 0))],
        out_shape=[jax.ShapeDtypeStruct((out_rows, d), F32),
                   jax.ShapeDtypeStruct((ROUTE_ROWS, out_rows), F32),
                   jax.ShapeDtypeStruct((1, LANES), F32)],
        scratch_shapes=[pltpu.VMEM((d, d), BF16), pltpu.VMEM((1, LANES), F32)],
        compiler_params=_params(1),
        name="outproj_route",
    )(x, z, w, g.reshape(1, d), rw_pad)


def _outproj_sample_body(x_ref, z_ref, w_ref, buf_ref, o_ref):
    del buf_ref
    o_ref[...] = x_ref[...] + jnp.dot(z_ref[...], w_ref[...].astype(BF16), preferred_element_type=F32)


def _outproj_sample_route_body(n_experts, x_ref, z_ref, w_ref, g_ref, rw_ref, cnt_in_ref, buf_ref, route_in_ref,
                               o_ref, route_ref, cnt_ref):
    del buf_ref, route_in_ref
    res = x_ref[...] + jnp.dot(z_ref[...], w_ref[...].astype(BF16), preferred_element_type=F32)
    o_ref[...] = res
    route_ref[...], cnt_ref[...] = _route_rows(res, g_ref[...], rw_ref[...], cnt_in_ref[...], n_experts)


def _outproj_sample(x, x_row0, z, w, buf, row0, router=None):
    nb, d = z.shape
    in_specs = [pl.BlockSpec((nb, d), lambda i: (x_row0 // nb, 0)),
                pl.BlockSpec((nb, d), lambda i: (0, 0)),
                pl.BlockSpec((d, d), lambda i: (0, 0))]
    if router is None:
        return pl.pallas_call(
            _outproj_sample_body,
            grid=(1,),
            in_specs=in_specs + [pl.BlockSpec(memory_space=pl.ANY)],
            out_specs=pl.BlockSpec((nb, d), lambda i: (row0 // nb, 0)),
            out_shape=jax.ShapeDtypeStruct(buf.shape, buf.dtype),
            input_output_aliases={3: 0},
            compiler_params=_params(1),
            name="outproj_sample",
        )(x, z, w, buf)
    g, rw_pad, n_experts, route, counts = router
    return pl.pallas_call(
        functools.partial(_outproj_sample_route_body, n_experts),
        grid=(1,),
        in_specs=in_specs + [pl.BlockSpec((1, d), lambda i: (0, 0)),
                             pl.BlockSpec((d, LANES), lambda i: (0, 0)),
                             pl.BlockSpec((1, LANES), lambda i: (0, 0)),
                             pl.BlockSpec(memory_space=pl.ANY),
                             pl.BlockSpec(memory_space=pl.ANY)],
        out_specs=[pl.BlockSpec((nb, d), lambda i: (row0 // nb, 0)),
                   pl.BlockSpec((ROUTE_ROWS, nb), lambda i: (0, row0 // nb)),
                   pl.BlockSpec((1, LANES), lambda i: (0, 0))],
        out_shape=[jax.ShapeDtypeStruct(buf.shape, buf.dtype),
                   jax.ShapeDtypeStruct(route.shape, route.dtype),
                   jax.ShapeDtypeStruct((1, LANES), F32)],
        input_output_aliases={6: 0, 7: 1},
        compiler_params=_params(1),
        name="outproj_sample_route",
    )(x, z, w, g.reshape(1, d), rw_pad, counts, buf, route)


def _swiglu_partial(xn, w1, w3, w2):
    h1 = jnp.dot(xn, w1, preferred_element_type=F32)
    h3 = jnp.dot(xn, w3, preferred_element_type=F32)
    h = (_silu(h1) * h3).astype(BF16)
    return jnp.dot(h, w2, preferred_element_type=F32)


def _cache_weight_chunk(w1_hbm, w3_hbm, w2_hbm, f, n_f, tf, stages, caches, wsem):
    n_slots = stages[0].shape[0]

    def copies(ff, ws):
        cols = pl.ds(pl.multiple_of(ff * tf, tf), tf)
        srcs = (w1_hbm.at[:, cols], w3_hbm.at[:, cols], w2_hbm.at[cols, :])
        return [pltpu.make_async_copy(src, st.at[ws], wsem.at[ws, m]) for m, (src, st) in enumerate(zip(srcs, stages))]

    if n_slots == 1:
        ws = 0
        for c in copies(f, 0):
            c.start()
    else:
        ws = lax.rem(f, 2)

        @pl.when(f == 0)
        def _():
            for c in copies(0, 0):
                c.start()

        @pl.when(f + 1 < n_f)
        def _():
            for c in copies(f + 1, 1 - ws):
                c.start()

    for c in copies(f, ws):
        c.wait()
    for st, ca in zip(stages, caches):
        ca[f] = st[ws].astype(BF16)


def _ffn_body(n_f, tf, x_ref, g_ref, w1_hbm, w3_hbm, w2_hbm, o_ref, xn_ref, acc_ref, c1, c3, c2, s1, s3, s2, wsem):
    i = pl.program_id(0)
    xn_ref[...] = _rmsnorm_bf16(x_ref[...], g_ref[...])
    acc_ref[...] = jnp.zeros_like(acc_ref)

    def chunk(f, carry):
        @pl.when(i == 0)
        def _():
            _cache_weight_chunk(w1_hbm, w3_hbm, w2_hbm, f, n_f, tf, (s1, s3, s2), (c1, c3, c2), wsem)

        acc_ref[...] += _swiglu_partial(xn_ref[...], c1[f], c3[f], c2[f])
        return carry

    lax.fori_loop(0, n_f, chunk, 0)
    o_ref[...] = x_ref[...] + acc_ref[...]


def _ffn(x, n, g, w1, w3, w2, tm, tf):
    d = x.shape[1]
    dff = w1.shape[1]
    n_f = dff // tf
    return pl.pallas_call(
        functools.partial(_ffn_body, n_f, tf),
        grid=(n // tm,),
        in_specs=[pl.BlockSpec((tm, d), lambda i: (i, 0)),
                  pl.BlockSpec((1, d), lambda i: (0, 0)),
                  pl.BlockSpec(memory_space=pl.ANY),
                  pl.BlockSpec(memory_space=pl.ANY),
                  pl.BlockSpec(memory_space=pl.ANY)],
        out_specs=pl.BlockSpec((tm, d), lambda i: (i, 0)),
        out_shape=jax.ShapeDtypeStruct((n, d), F32),
        scratch_shapes=[pltpu.VMEM((tm, d), BF16), pltpu.VMEM((tm, d), F32),
                        pltpu.VMEM((n_f, d, tf), BF16), pltpu.VMEM((n_f, d, tf), BF16),
                        pltpu.VMEM((n_f, tf, d), BF16),
                        pltpu.VMEM((1, d, tf), F32), pltpu.VMEM((1, d, tf), F32), pltpu.VMEM((1, tf, d), F32),
                        pltpu.SemaphoreType.DMA((1, 3))],
        compiler_params=_params(1),
        name="ffn",
    )(x, g.reshape(1, d), w1, w3, w2)


ROUTE_ROWS = 8


def _route_rows(x, g, rw, counts, n_experts):
    rows = x.shape[0]
    xn = _rmsnorm_bf16(x, g)
    logits = jnp.dot(xn, rw.astype(BF16), preferred_element_type=F32)
    lane = lax.broadcasted_iota(I32, (rows, LANES), 1).astype(F32)
    neg = jnp.float32(-jnp.inf)
    logits = jnp.where(lane < n_experts, logits, neg)
    m1 = jnp.max(logits, axis=-1, keepdims=True)
    i1 = jnp.min(jnp.where(logits == m1, lane, float(LANES)), axis=-1, keepdims=True)
    i1 = jnp.minimum(i1, float(n_experts - 1))
    rest = jnp.where(lane == i1, neg, logits)
    m2 = jnp.max(rest, axis=-1, keepdims=True)
    i2 = jnp.min(jnp.where(rest == m2, lane, float(LANES)), axis=-1, keepdims=True)
    i2 = jnp.minimum(i2, float(n_experts - 1))
    e2 = jnp.exp(m2 - m1)
    den = 1.0 + e2
    g1 = 1.0 / den
    g2 = e2 / den

    sel1 = lane == i1
    sel2 = lane == i2
    onehot = jnp.logical_or(sel1, sel2)
    row = lax.broadcasted_iota(I32, (rows, rows), 0)
    col = lax.broadcasted_iota(I32, (rows, rows), 1)
    before = (row > col).astype(BF16)
    rank = jnp.dot(before, onehot.astype(BF16), preferred_element_type=F32) + counts
    r1 = jnp.sum(jnp.where(sel1, rank, 0.0), axis=-1, keepdims=True)
    r2 = jnp.sum(jnp.where(sel2, rank, 0.0), axis=-1, keepdims=True)

    out = jnp.zeros((rows, LANES), F32)
    for slot, val in enumerate((i1, i2, g1, g2, r1, r2)):
        out = jnp.where(lane == slot, val, out)
    return out.T[0:ROUTE_ROWS, :], counts + jnp.sum(onehot.astype(F32), axis=0, keepdims=True)


SUBLANES = 8
MOE_ROWS_PER_CHUNK = 96


def _start_row_group(src_hbm, idx_ref, g, dst, sem):
    for k in range(SUBLANES):
        pltpu.make_async_copy(src_hbm.at[pl.ds(idx_ref[0, 0, g * SUBLANES + k], 1)],
                              dst.at[g, pl.ds(k, 1)], sem).start(priority=k % 2)


def _start_row_gather(src_hbm, idx_ref, n_rows, dst, sem, unrolled=False):
    if unrolled:
        for g in range(n_rows // SUBLANES):
            _start_row_group(src_hbm, idx_ref, g, dst, sem)
        return

    def body(g, carry):
        _start_row_group(src_hbm, idx_ref, g, dst, sem)
        return carry

    lax.fori_loop(0, n_rows // SUBLANES, body, 0)


def _wait_row_gather(src_hbm, n_rows, dst, sem):
    pltpu.make_async_copy(src_hbm.at[pl.ds(0, n_rows)], dst.reshape(n_rows, dst.shape[-1]), sem).wait()


def _moe_body(n_f, tf, te_ref, nu_ref, hf_ref, idx_cur, idx_nxt, x_hbm, gate_ref, g_ref, w1_hbm, w3_hbm, w2_hbm,
              y_ref, xbuf, sem, xn_ref, acc_ref, c1, c3, c2, s1, s3, s2, wsem):
    i = pl.program_id(0)
    n_tiles = pl.num_programs(0)
    tm = xbuf.shape[1] * SUBLANES
    groups_per_chunk = tm // n_f // SUBLANES
    slot = lax.rem(i, 2)
    n_used = nu_ref[0]
    used = i < n_used
    e = te_ref[i]
    first_of_expert = jnp.logical_or(i == 0, e != te_ref[jnp.maximum(i - 1, 0)])
    half_filled = hf_ref[i] == 1

    @pl.when(i == 0)
    def _():
        _start_row_gather(x_hbm, idx_cur, tm, xbuf.at[0], sem.at[0])

    @pl.when(i <= n_used)
    def _():
        _wait_row_gather(x_hbm, tm, xbuf.at[slot], sem.at[slot])

    @pl.when(used)
    def _():
        xn_ref[...] = _rmsnorm_bf16(xbuf[slot].reshape(tm, xbuf.shape[-1]), g_ref[...])
        acc_ref[...] = jnp.zeros_like(acc_ref)

        def chunk(f, carry):
            @pl.when(first_of_expert)
            def _():
                _cache_weight_chunk(w1_hbm.at[e], w3_hbm.at[e], w2_hbm.at[e], f, n_f, tf,
                                    (s1, s3, s2), (c1, c3, c2), wsem)

            def compute(rows):
                for g in range(groups_per_chunk):
                    _start_row_group(x_hbm, idx_nxt, f * groups_per_chunk + g, xbuf.at[1 - slot], sem.at[1 - slot])
                acc_ref[0:rows, :] += _swiglu_partial(xn_ref[0:rows, :], c1[f], c3[f], c2[f])

            @pl.when(half_filled)
            def _():
                compute(tm // 2)

            @pl.when(jnp.logical_not(half_filled))
            def _():
                compute(tm)

            return carry

        lax.fori_loop(0, n_f, chunk, 0)
        r_id = lax.broadcasted_iota(I32, (tm, tm), 0)
        c_id = lax.broadcasted_iota(I32, (tm, tm), 1)
        gate_col = jnp.sum(jnp.where(r_id == c_id, gate_ref[0], 0.0), axis=1, keepdims=True)
        y_ref[...] = acc_ref[...] * gate_col

    @pl.when(jnp.logical_not(used))
    def _():
        y_ref[...] = jnp.zeros_like(y_ref)

    @pl.when(jnp.logical_and(used, i == n_tiles - 1))
    def _():
        _wait_row_gather(x_hbm, tm, xbuf.at[1 - slot], sem.at[1 - slot])


def _moe(x, g, w1, w3, w2, tile_expert, n_used, half_filled, inv3, gate_sorted, tm, tf):
    n_tiles = inv3.shape[0]
    d = x.shape[1]
    dff = w1.shape[2]
    n_f = dff // tf
    assert tm % (n_f * SUBLANES) == 0, "each hidden chunk fetches whole 8-row tiles of the next row tile"
    assert tm % 32 == 0, "half a row tile must be whole bf16 sublane tiles"
    grid_spec = pltpu.PrefetchScalarGridSpec(
        num_scalar_prefetch=3,
        grid=(n_tiles,),
        in_specs=[pl.BlockSpec((1, 1, tm), lambda i, te, nu, hf: (i, 0, 0), memory_space=pltpu.SMEM),
                  pl.BlockSpec((1, 1, tm), lambda i, te, nu, hf: (jnp.minimum(i + 1, n_tiles - 1), 0, 0),
                               memory_space=pltpu.SMEM),
                  pl.BlockSpec(memory_space=pl.ANY),
                  pl.BlockSpec((1, 1, tm), lambda i, te, nu, hf: (i, 0, 0)),
                  pl.BlockSpec((1, d), lambda i, te, nu, hf: (0, 0)),
                  pl.BlockSpec(memory_space=pl.ANY),
                  pl.BlockSpec(memory_space=pl.ANY),
                  pl.BlockSpec(memory_space=pl.ANY)],
        out_specs=pl.BlockSpec((tm, d), lambda i, te, nu, hf: (i, 0)),
        scratch_shapes=[pltpu.VMEM((2, tm // SUBLANES, SUBLANES, d), F32),
                        pltpu.SemaphoreType.DMA((2,)),
                        pltpu.VMEM((tm, d), BF16),
                        pltpu.VMEM((tm, d), F32),
                        pltpu.VMEM((n_f, d, tf), BF16),
                        pltpu.VMEM((n_f, d, tf), BF16),
                        pltpu.VMEM((n_f, tf, d), BF16),
                        pltpu.VMEM((2, d, tf), F32),
                        pltpu.VMEM((2, d, tf), F32),
                        pltpu.VMEM((2, tf, d), F32),
                        pltpu.SemaphoreType.DMA((2, 3))],
    )
    return pl.pallas_call(
        functools.partial(_moe_body, n_f, tf),
        grid_spec=grid_spec,
        out_shape=jax.ShapeDtypeStruct((n_tiles * tm, d), F32),
        compiler_params=_params(1),
        name="moe",
    )(tile_expert, n_used, half_filled, inv3, inv3, x, gate_sorted, g.reshape(1, d), w1, w3, w2)


def _combine_body(pos_cur, pos_nxt, x_ref, g_ref, y_hbm, o_ref, ybuf, sem):
    i = pl.program_id(0)
    n_tiles = pl.num_programs(0)
    rows = ybuf.shape[1] * SUBLANES
    slot = lax.rem(i, 2)

    @pl.when(i == 0)
    def _():
        _start_row_gather(y_hbm, pos_cur, rows, ybuf.at[0], sem.at[0])

    @pl.when(i + 1 < n_tiles)
    def _():
        _start_row_gather(y_hbm, pos_nxt, rows, ybuf.at[1 - slot], sem.at[1 - slot], unrolled=True)

    _wait_row_gather(y_hbm, rows, ybuf.at[slot], sem.at[slot])
    tc = rows // TOP_K
    y = ybuf[slot].reshape(rows, ybuf.shape[-1])
    moe = y[0:tc, :] + y[tc:rows, :]
    x = x_ref[...] + moe
    ms = jnp.mean(x * x, axis=-1, keepdims=True)
    o_ref[...] = x * lax.rsqrt(ms + EPS) * g_ref[...]


def _combine(x, g, y_sorted, pos3, row0, n_rows, tc):
    d = x.shape[1]
    n_tiles = n_rows // tc
    rb0 = row0 // tc
    grid_spec = pltpu.PrefetchScalarGridSpec(
        num_scalar_prefetch=0,
        grid=(n_tiles,),
        in_specs=[pl.BlockSpec((1, 1, TOP_K * tc), lambda i: (i, 0, 0), memory_space=pltpu.SMEM),
                  pl.BlockSpec((1, 1, TOP_K * tc), lambda i: (jnp.minimum(i + 1, n_tiles - 1), 0, 0),
                               memory_space=pltpu.SMEM),
                  pl.BlockSpec((tc, d), lambda i: (rb0 + i, 0)),
                  pl.BlockSpec((1, d), lambda i: (0, 0)),
                  pl.BlockSpec(memory_space=pl.ANY)],
        out_specs=pl.BlockSpec((tc, d), lambda i: (i, 0)),
        scratch_shapes=[pltpu.VMEM((2, TOP_K * tc // SUBLANES, SUBLANES, d), F32), pltpu.SemaphoreType.DMA((2,))],
    )
    return pl.pallas_call(
        _combine_body,
        grid_spec=grid_spec,
        out_shape=jax.ShapeDtypeStruct((n_rows, d), F32),
        compiler_params=_params(1),
        name="combine",
    )(pos3, pos3, x, g.reshape(1, d), y_sorted)


def _lookup(table, idx):
    out = jnp.zeros(idx.shape, table.dtype)
    for e in range(table.shape[0]):
        out = jnp.where(idx == e, table[e], out)
    return out


def _expert_layer(x, route, cnt, n_experts, g_ffn, g_final, w1, w3, w2, n_prompt, n_dec, tm_moe, tf, tc):
    n = n_prompt + n_dec
    ids = route[0:TOP_K, :n].astype(I32)
    gates = route[TOP_K:2 * TOP_K, :n]
    ranks = route[2 * TOP_K:3 * TOP_K, :n].astype(I32)
    counts = cnt[0, :n_experts].astype(I32)

    n_tiles = (TOP_K * n + n_experts * (tm_moe - 1)) // tm_moe
    padded = ((counts + tm_moe - 1) // tm_moe) * tm_moe
    ends = jnp.cumsum(padded)
    starts = ends - padded
    cstart = jnp.cumsum(counts) - counts
    pos = _lookup(starts, ids) + ranks
    keys = ids * n + jnp.arange(n, dtype=I32)[None, :]
    order = jnp.argsort(keys.reshape(-1)).astype(I32)
    slot_pos = jnp.arange(n_tiles * tm_moe, dtype=I32)
    slot_e = jnp.minimum(jnp.sum(slot_pos[None, :] >= ends[:, None], axis=0), n_experts - 1).astype(I32)
    slot_rank = slot_pos - _lookup(starts, slot_e)
    slot_valid = slot_rank < _lookup(counts, slot_e)
    slot_asg = order[jnp.clip(_lookup(cstart, slot_e) + slot_rank, 0, TOP_K * n - 1)]
    inv = jnp.where(slot_valid, slot_asg % n, 0).astype(I32)
    gate_sorted = jnp.where(slot_valid, gates.reshape(-1)[slot_asg], 0.0)
    tile_expert = slot_e[::tm_moe]
    n_used = (ends[-1] // tm_moe).astype(I32).reshape(1)

    tile_start = jnp.arange(n_tiles, dtype=I32) * tm_moe
    tile_rows = _lookup(counts, tile_expert) - (tile_start - _lookup(starts, tile_expert))
    half_filled = (tile_rows <= tm_moe // 2).astype(I32)

    y_sorted = _moe(x, g_ffn, w1, w3, w2, tile_expert, n_used, half_filled, inv.reshape(n_tiles, 1, tm_moe),
                    gate_sorted.reshape(n_tiles, 1, tm_moe), tm_moe, tf)

    def tile_positions(row0, n_rows, t):
        p = pos[:, row0:row0 + n_rows].reshape(TOP_K, n_rows // t, t)
        return jnp.swapaxes(p, 0, 1).reshape(n_rows // t, 1, TOP_K * t)

    y_prompt = _combine(x, g_final, y_sorted, tile_positions(0, n_prompt, tc), 0, n_prompt, tc)
    y_sample = _combine(x, g_final, y_sorted, tile_positions(n_prompt, n_dec, n_dec), n_prompt, n_dec, n_dec)
    return y_prompt, y_sample


def kernel(x_prompt, x_sample, state_hgrn, state_conv, norm_mix, norm_ffn, norm_final, hgrn_w_in, hgrn_lb, hgrn_g_norm, hgrn_w_out, conv_w_in, conv_w, conv_w_out, ffn_w1, ffn_w3, ffn_w2, moe_router, moe_w1, moe_w3, moe_w2):
    batch, seq, d = x_prompt.shape
    n_dec = x_sample.shape[0]
    n_prompt = batch * seq
    n = n_prompt + n_dec
    n_experts = moe_router.shape[-1]
    assert x_sample.shape[1] == 1 and d % HEAD == 0 and seq % CHUNK == 0
    assert norm_mix.shape[0] == 2, "one HGRN2 layer followed by one short-conv layer"

    tm = _pick_tile(n, (688, 384, 128, 16))
    tm_proj = _pick_tile(n, (1376, 688, 384, 128, 16))
    tm_prompt = _pick_tile(n_prompt, (1024, 512, 256, 128, 16))
    tb = _pick_tile(seq, (256, 128, 64, 32))
    tb_conv = _pick_tile(seq, (1024, 512, 256, 128, 64, 32))
    tg = _pick_tile(n_dec, (128, 64, 32, 16))
    tf = _pick_tile(ffn_w1.shape[-1], (512, 256, 128))
    tm_moe = (moe_w1.shape[-1] // tf) * MOE_ROWS_PER_CHUNK
    tc = _pick_tile(n_prompt, (512, 128, 16))
    assert n_prompt % tg == 0 and n_prompt % n_dec == 0 and n_prompt % tc == 0

    xp = x_prompt.reshape(n_prompt, d)
    xs = x_sample.reshape(n_dec, d)

    p16, p32 = _proj(xp, norm_mix[0], hgrn_w_in[0], tm_prompt, f32_col=1)
    p16_s, p32_s = _proj(xs, norm_mix[0], hgrn_w_in[0], n_dec, f32_col=1)
    o_p, s_prompt = _hgrn_prompt(p16, p32, hgrn_lb, hgrn_g_norm[0], 0, batch, seq, tb)
    o_s, s_sample = _hgrn_sample(p16_s, p32_s, hgrn_lb, hgrn_g_norm[0], state_hgrn, 0, 0, tg)
    x = _outproj_prompt(xp, o_p, hgrn_w_out[0], n_prompt, tm_prompt)
    x = _outproj_sample(xs, 0, o_s, hgrn_w_out[0], x, n_prompt)
    x = _ffn(x, n, norm_ffn[0], ffn_w1[0], ffn_w3[0], ffn_w2[0], tm, tf)

    (pc,) = _proj(x, norm_mix[1], conv_w_in[0], tm_proj)
    z_p, c_prompt = _conv_prompt(pc, conv_w[0], batch, seq, tb_conv)
    z_s, c_sample = _conv_sample(pc, conv_w[0], state_conv.reshape(n_dec, (CONV_W - 1) * d), n_prompt)
    rw_pad = jnp.pad(moe_router[0], ((0, 0), (0, LANES - n_experts)))
    x1, route, cnt = _outproj_prompt(x, z_p, conv_w_out[0], n_prompt, tm_prompt, router=(norm_ffn[1], rw_pad, n_experts))
    x1, route, cnt = _outproj_sample(x, n_prompt, z_s, conv_w_out[0], x1, n_prompt,
                                     router=(norm_ffn[1], rw_pad, n_experts, route, cnt))

    y_prompt, y_sample = _expert_layer(x1, route, cnt, n_experts, norm_ffn[1], norm_final, moe_w1[0], moe_w3[0],
                                       moe_w2[0], n_prompt, n_dec, tm_moe, tf, tc)

    return (y_prompt.reshape(batch, seq, d),
            y_sample.reshape(n_dec, 1, d),
            s_prompt,
            s_sample,
            c_prompt.reshape(batch, 1, CONV_W - 1, d),
            c_sample.reshape(n_dec, 1, CONV_W - 1, d))
```
